```python
import math
import jax, jax.numpy as jnp
from jax import lax
import numpy as np

D_MODEL = 1024
BATCH = 4
SEQ = 4096
DEPTH = 2

BRANCH_WIDTH = D_MODEL // 2
N_BRANCHES = 3
EPS = 1e-6

A_HEADS = 4
A_DK = BRANCH_WIDTH // A_HEADS
A_DV = BRANCH_WIDTH // A_HEADS
A_CHUNK = 64

B_HEADS = 4
B_DK = BRANCH_WIDTH // B_HEADS
B_DV = BRANCH_WIDTH // B_HEADS
B_CONV = 4
B_CONV_CH = B_HEADS * (2 * B_DK + B_DV)
B_CHUNK = 64

C_Q_HEADS = 8
C_KV_HEADS = 2
C_GROUP = C_Q_HEADS // C_KV_HEADS
C_HEAD_DIM = BRANCH_WIDTH // C_Q_HEADS
WINDOW = 128
C_BLOCK = 128
N_BUCKETS = 32
MAX_DISTANCE = 128

IN_WIDTHS = (
    A_HEADS * A_DK, A_HEADS * A_DK, A_HEADS * A_DV, A_HEADS * A_DV,
    B_CONV_CH, B_HEADS * B_DV, B_HEADS, B_HEADS,
    C_Q_HEADS * C_HEAD_DIM, C_KV_HEADS * C_HEAD_DIM,
    C_KV_HEADS * C_HEAD_DIM, C_Q_HEADS * C_HEAD_DIM,
    N_BRANCHES * D_MODEL,
)
N_IN = sum(IN_WIDTHS)

kernel_name = 'hybrid_hgrn2_gdn_swa_sink_gated_merge'

F32 = jnp.float32


def _rmsnorm(x, w):
    x32 = x.astype(F32)
    y = x32 * lax.rsqrt(jnp.mean(x32 * x32, axis=-1, keepdims=True) + EPS)
    return (y * w.astype(F32)).astype(x.dtype)


def _gated_rmsnorm(o, gate, w):
    y = o * lax.rsqrt(jnp.mean(o * o, axis=-1, keepdims=True) + EPS) * w.astype(F32)
    return y * jax.nn.silu(gate)


def _l2norm(t):
    return t * lax.rsqrt(jnp.sum(t * t, axis=-1, keepdims=True) + EPS)


def _causal_conv(x, w):
    return lax.conv_general_dilated(
        x, w[:, None, :].astype(x.dtype), window_strides=(1,), padding=[(B_CONV - 1, 0)],
        dimension_numbers=('NWC', 'WIO', 'NWC'), feature_group_count=x.shape[-1])


def _t5_bucket(dist):
    max_exact = N_BUCKETS // 2
    d_f = jnp.maximum(dist, 1).astype(F32)
    large = max_exact + (jnp.log(d_f / max_exact) / math.log(MAX_DISTANCE / max_exact)
                         * (N_BUCKETS - max_exact)).astype(jnp.int32)
    large = jnp.minimum(large, N_BUCKETS - 1)
    return jnp.where(dist < max_exact, dist, large)


def _band_dist():
    i = jnp.arange(C_BLOCK)[:, None]
    j = jnp.arange(2 * C_BLOCK)[None, :]
    return i + C_BLOCK - j


def _band_bias(rel_bias):
    bucket = _t5_bucket(jnp.maximum(_band_dist(), 0))
    bias = rel_bias.astype(F32)[bucket]
    return bias.transpose(2, 0, 1).reshape(C_KV_HEADS, C_GROUP, C_BLOCK, 2 * C_BLOCK)


def _hgrn2_branch(q, f, i, g, lb, norm_w):
    Bn, S, _ = q.shape
    n = S // A_CHUNK
    q = jax.nn.silu(q.astype(F32))
    z = f.astype(F32)
    log_f = jnp.logaddexp(jnp.log(lb), jnp.log1p(-lb) + jax.nn.log_sigmoid(z))
    k = (1.0 - lb) * jax.nn.sigmoid(-z)

    def chunks(t, d):
        return t.reshape(Bn, n, A_CHUNK, A_HEADS, d).transpose(1, 0, 3, 2, 4)

    qc, kc, gc = chunks(q, A_DK), chunks(k, A_DK), chunks(log_f, A_DK)
    vc = chunks(i.astype(F32), A_DV)
    causal = jnp.tril(jnp.ones((A_CHUNK, A_CHUNK), dtype=bool))[:, :, None]

    def step(state, inp):
        qb, kb, vb, gb = inp
        b = jnp.cumsum(gb, axis=2)
        diff = b[:, :, :, None, :] - b[:, :, None, :, :]
        decay = jnp.exp(jnp.where(causal, diff, -jnp.inf))
        scores = jnp.einsum('bhtk,bhsk,bhtsk->bhts', qb, kb, decay)
        o = (jnp.einsum('bhts,bhsv->bhtv', scores, vb)
             + jnp.einsum('bhtk,bhkv->bhtv', qb * jnp.exp(b), state))
        b_end = b[:, :, -1:, :]
        state = (state * jnp.exp(b_end[:, :, 0, :, None])
                 + jnp.einsum('bhsk,bhsv->bhkv', kb * jnp.exp(b_end - b), vb))
        return state, o

    state0 = jnp.zeros((Bn, A_HEADS, A_DK, A_DV), F32)
    _, o = lax.scan(step, state0, (qc, kc, vc, gc))
    o = o.transpose(1, 0, 3, 2, 4).reshape(Bn, S, A_HEADS, A_DV)
    gate = g.astype(F32).reshape(Bn, S, A_HEADS, A_DV)
    return _gated_rmsnorm(o, gate, norm_w).reshape(Bn, S, A_HEADS * A_DV)


def _gated_deltanet_branch(qkv, z, beta_logit, a_logit, conv_w, a_log, dt_bias, norm_w):
    Bn, S, _ = qkv.shape
    n = S // B_CHUNK
    C = B_CHUNK
    qkv = jax.nn.silu(_causal_conv(qkv, conv_w)).astype(F32)
    q, k, v = jnp.split(qkv, [B_HEADS * B_DK, 2 * B_HEADS * B_DK], axis=-1)
    q = _l2norm(q.reshape(Bn, S, B_HEADS, B_DK)) * (B_DK ** -0.5)
    k = _l2norm(k.reshape(Bn, S, B_HEADS, B_DK))
    v = v.reshape(Bn, S, B_HEADS, B_DV)
    beta = jax.nn.sigmoid(beta_logit.astype(F32))
    g = -jnp.exp(a_log.astype(F32)) * jax.nn.softplus(a_logit.astype(F32) + dt_bias.astype(F32))

    def chunks4(t):
        return t.reshape(Bn, n, C, B_HEADS, t.shape[-1]).transpose(1, 0, 3, 2, 4)

    def chunks3(t):
        return t.reshape(Bn, n, C, B_HEADS).transpose(1, 0, 3, 2)

    qc, kc, vc = chunks4(q), chunks4(k), chunks4(v)
    bc, gcum = chunks3(beta), jnp.cumsum(chunks3(g), axis=-1)
    incl = jnp.tril(jnp.ones((C, C), dtype=bool))
    strict = jnp.tril(jnp.ones((C, C), dtype=bool), k=-1)
    L = jnp.exp(jnp.where(incl, gcum[..., :, None] - gcum[..., None, :], -jnp.inf))
    kb = kc * bc[..., None]
    A = jnp.where(strict, jnp.einsum('nbhtd,nbhsd->nbhts', kb, kc) * L, 0.0)
    eye = jnp.eye(C, dtype=F32)
    T = lax.linalg.triangular_solve(eye + A, jnp.broadcast_to(eye, A.shape),
                                    left_side=True, lower=True, unit_diagonal=True)
    u = T @ (vc * bc[..., None])
    w = T @ (kb * jnp.exp(gcum)[..., None])
    qk = jnp.where(incl, jnp.einsum('nbhtd,nbhsd->nbhts', qc, kc) * L, 0.0)
    q_dec = qc * jnp.exp(gcum)[..., None]
    k_dec = kc * jnp.exp(gcum[..., -1:] - gcum)[..., None]
    g_end = jnp.exp(gcum[..., -1])

    def step(state, inp):
        qk_c, u_c, w_c, qd_c, kd_c, ge_c = inp
        v_new = u_c - jnp.einsum('bhtk,bhkv->bhtv', w_c, state)
        o = (jnp.einsum('bhtk,bhkv->bhtv', qd_c, state)
             + jnp.einsum('bhts,bhsv->bhtv', qk_c, v_new))
        state = state * ge_c[..., None, None] + jnp.einsum('bhsk,bhsv->bhkv', kd_c, v_new)
        return state, o

    state0 = jnp.zeros((Bn, B_HEADS, B_DK, B_DV), F32)
    _, o = lax.scan(step, state0, (qk, u, w, q_dec, k_dec, g_end))
    o = o.transpose(1, 0, 3, 2, 4).reshape(Bn, S, B_HEADS, B_DV)
    gate = z.astype(F32).reshape(Bn, S, B_HEADS, B_DV)
    return _gated_rmsnorm(o, gate, norm_w).reshape(Bn, S, B_HEADS * B_DV)


def _swa_sink_branch(q, k, v, g, sinks, bias_blk):
    Bn, S, _ = q.shape
    nb = S // C_BLOCK
    qb = q.astype(F32).reshape(Bn, nb, C_BLOCK, C_KV_HEADS, C_GROUP, C_HEAD_DIM)

    def band(t):
        t = t.astype(F32).reshape(Bn, S, C_KV_HEADS, C_HEAD_DIM)
        t = jnp.pad(t, ((0, 0), (C_BLOCK, 0), (0, 0), (0, 0)))
        t = t.reshape(Bn, nb + 1, C_BLOCK, C_KV_HEADS, C_HEAD_DIM)
        return jnp.concatenate([t[:, :-1], t[:, 1:]], axis=2)

    kb, vb = band(k), band(v)
    logits = jnp.einsum('bnqhgd,bnkhd->bnhgqk', qb, kb) * (C_HEAD_DIM ** -0.5) + bias_blk
    dist = _band_dist()
    key_pos = jnp.arange(nb)[:, None] * C_BLOCK + jnp.arange(2 * C_BLOCK)[None, :] - C_BLOCK
    mask = ((dist >= 0) & (dist < WINDOW))[None, :, :] & (key_pos >= 0)[:, None, :]
    logits = jnp.where(mask[None, :, None, None], logits, -jnp.inf)
    sink = jnp.broadcast_to(sinks.astype(F32).reshape(C_KV_HEADS, C_GROUP)[:, :, None, None],
                            logits.shape[:-1] + (1,))
    probs = jax.nn.softmax(jnp.concatenate([logits, sink], axis=-1), axis=-1)[..., :-1]
    o = jnp.einsum('bnhgqk,bnkhd->bnqhgd', probs, vb).reshape(Bn, S, C_Q_HEADS * C_HEAD_DIM)
    return o * jax.nn.silu(g.astype(F32))


def setup_inputs(seed: int = 0) -> dict:
    key = jax.random.key(seed)
    ks = jax.random.split(key, 15)
    nrm = jax.random.normal
    x = nrm(ks[0], (BATCH, SEQ, D_MODEL), F32)
    norm_w = 1.0 + 0.02 * nrm(ks[1], (DEPTH, D_MODEL), F32)
    w_in = nrm(ks[2], (DEPTH, D_MODEL, N_IN), F32) * (D_MODEL ** -0.5)
    conv_w = nrm(ks[3], (DEPTH, B_CONV, B_CONV_CH), F32) * (B_CONV ** -0.5)
    a_log = jnp.log(jax.random.uniform(ks[4], (DEPTH, B_HEADS), F32, 1.0, 16.0))
    dt = jnp.exp(jax.random.uniform(ks[5], (DEPTH, B_HEADS), F32, math.log(1e-3), math.log(0.1)))
    dt_bias = dt + jnp.log(-jnp.expm1(-dt))
    lb_param = 0.1 * nrm(ks[6], (DEPTH, A_HEADS * A_DK), F32)
    norm_a = 1.0 + 0.02 * nrm(ks[7], (DEPTH, A_DV), F32)
    norm_b = 1.0 + 0.02 * nrm(ks[8], (DEPTH, B_DV), F32)
    sinks = 0.5 * nrm(ks[9], (DEPTH, C_Q_HEADS), F32)
    rel_bias = 0.5 * nrm(ks[10], (N_BUCKETS, C_Q_HEADS), F32)
    w_branch = nrm(ks[11], (DEPTH, N_BRANCHES, BRANCH_WIDTH, D_MODEL), F32) * (BRANCH_WIDTH ** -0.5)
    w_out = nrm(ks[12], (DEPTH, D_MODEL, D_MODEL), F32) * (D_MODEL ** -0.5)
    final_norm = 1.0 + 0.02 * nrm(ks[13], (D_MODEL,), F32)
    return {'x': x, 'norm_w': norm_w, 'w_in': w_in, 'conv_w': conv_w, 'a_log': a_log,
            'dt_bias': dt_bias, 'lb_param': lb_param, 'norm_a': norm_a, 'norm_b': norm_b,
            'sinks': sinks, 'rel_bias': rel_bias, 'w_branch': w_branch, 'w_out': w_out,
            'final_norm': final_norm}


def reference(x, norm_w, w_in, conv_w, a_log, dt_bias, lb_param, norm_a, norm_b,
              sinks, rel_bias, w_branch, w_out, final_norm):
    Bn, S, _ = x.shape
    split_points = np.cumsum(IN_WIDTHS)[:-1].tolist()
    lb_all = jnp.cumsum(jax.nn.softmax(lb_param.astype(F32), axis=0), axis=0)
    lb_all = lb_all - lb_all[0:1]
    bias_blk = _band_bias(rel_bias)
    for l in range(DEPTH):
        h = _rmsnorm(x, norm_w[l])
        proj = h @ w_in[l]
        (a_q, a_f, a_i, a_g, b_qkv, b_z, b_beta, b_a,
         c_q, c_k, c_v, c_g, gate_logits) = jnp.split(proj, split_points, axis=-1)
        y_a = _hgrn2_branch(a_q, a_f, a_i, a_g, lb_all[l], norm_a[l])
        y_b = _gated_deltanet_branch(b_qkv, b_z, b_beta, b_a, conv_w[l], a_log[l],
                                     dt_bias[l], norm_b[l])
        y_c = _swa_sink_branch(c_q, c_k, c_v, c_g, sinks[l], bias_blk)
        ys = jnp.stack([y_a, y_b, y_c], axis=2).astype(x.dtype)
        lifted = jnp.einsum('bsnc,ncd->bsnd', ys, w_branch[l])
        gates = jax.nn.sigmoid(gate_logits.reshape(Bn, S, N_BRANCHES, D_MODEL))
        merged = jnp.sum(gates * lifted, axis=2)
        x = x + merged @ w_out[l]
    return _rmsnorm(x, final_norm)
```

```python
import functools
import math

import numpy as np
import jax
import jax.numpy as jnp
from jax import lax
from jax.experimental import pallas as pl
from jax.experimental.pallas import tpu as pltpu

F32 = jnp.float32
BF16 = jnp.bfloat16
HIGHEST = lax.Precision.HIGHEST

D_MODEL = 1024
BRANCH_WIDTH = D_MODEL // 2
N_BRANCHES = 3
EPS = 1e-6
HEADS = 4
HEAD_DIM = BRANCH_WIDTH // HEADS
B_CONV = 4
B_CONV_CH = 3 * BRANCH_WIDTH
C_Q_HEADS = 8
C_KV_HEADS = 2
C_HEAD_DIM = BRANCH_WIDTH // C_Q_HEADS
C_KV_WIDTH = C_KV_HEADS * C_HEAD_DIM
WINDOW = 128
C_BLOCK = 128
N_BUCKETS = 32
MAX_DISTANCE = 128
LANES = 128
SUBLANES = 8
NEG_BIG = -1e30

A_CHUNK = 128
B_CHUNK = 64
PROJ_ROWS = 256
SMALL_PAD = LANES
VMEM_LIMIT = 52 * 1024 * 1024

PA_WIDTH = 4 * BRANCH_WIDTH
PB_WIDTH = B_CONV_CH + BRANCH_WIDTH + SMALL_PAD
PC_WIDTH = 2 * BRANCH_WIDTH + 2 * C_KV_WIDTH


def _dot(a, b):
    return jnp.dot(a, b, preferred_element_type=F32)


def _dot_nt(a, b):
    return lax.dot_general(a, b, (((1,), (1,)), ((), ())), preferred_element_type=F32)


def _dot_tn(a, b):
    return lax.dot_general(a, b, (((0,), (0,)), ((), ())), preferred_element_type=F32)


def _silu(x):
    return x * jax.nn.sigmoid(x)


def _log_sigmoid(x):
    return jnp.minimum(x, 0.0) - jnp.log1p(jnp.exp(-jnp.abs(x)))


def _softplus(x):
    return jnp.maximum(x, 0.0) + jnp.log1p(jnp.exp(-jnp.abs(x)))


def _rms(x, w):
    return x * lax.rsqrt(jnp.mean(x * x, axis=-1, keepdims=True) + EPS) * w


def _inproj_kernel(x_ref, nw_ref, wa_ref, wb_ref, wc_ref, pa_ref, pb_ref, pc_ref):
    h = _rms(x_ref[...], nw_ref[...]).astype(BF16)
    for w_ref, o_ref in ((wa_ref, pa_ref), (wb_ref, pb_ref), (wc_ref, pc_ref)):
        n = w_ref.shape[1]
        for j in range(0, n, BRANCH_WIDTH):
            wd = min(BRANCH_WIDTH, n - j)
            o_ref[:, j:j + wd] = _dot(h, w_ref[:, j:j + wd])


def _inproj(x2, nw, wa, wb, wc):
    t = x2.shape[0]
    tm = PROJ_ROWS
    const = lambda i: (0, 0)
    return pl.pallas_call(
        _inproj_kernel,
        grid=(t // tm,),
        in_specs=[pl.BlockSpec((tm, D_MODEL), lambda i: (i, 0)),
                  pl.BlockSpec((1, D_MODEL), const),
                  pl.BlockSpec(wa.shape, const),
                  pl.BlockSpec(wb.shape, const),
                  pl.BlockSpec(wc.shape, const)],
        out_specs=[pl.BlockSpec((tm, PA_WIDTH), lambda i: (i, 0)),
                   pl.BlockSpec((tm, PB_WIDTH), lambda i: (i, 0)),
                   pl.BlockSpec((tm, PC_WIDTH), lambda i: (i, 0))],
        out_shape=[jax.ShapeDtypeStruct((t, PA_WIDTH), F32),
                   jax.ShapeDtypeStruct((t, PB_WIDTH), F32),
                   jax.ShapeDtypeStruct((t, PC_WIDTH), F32)],
        compiler_params=pltpu.CompilerParams(
            dimension_semantics=("arbitrary",), vmem_limit_bytes=VMEM_LIMIT),
        name="inproj",
    )(x2, nw, wa, wb, wc)


def _merge_kernel(x_ref, nw_ref, wg_ref, ya_ref, yb_ref, yc_ref, wbr_ref, wo_ref, fn_ref,
                  o_ref, *, final):
    x = x_ref[...]
    h = _rms(x, nw_ref[...]).astype(BF16)
    merged = None
    for n, y_ref in enumerate((ya_ref, yb_ref, yc_ref)):
        gate = jax.nn.sigmoid(_dot(h, wg_ref[:, n * D_MODEL:(n + 1) * D_MODEL]))
        term = gate * _dot(y_ref[...], wbr_ref[n])
        merged = term if merged is None else merged + term
    out = x + _dot(merged.astype(BF16), wo_ref[...])
    if final:
        out = _rms(out, fn_ref[...])
    o_ref[...] = out


def _merge(x2, nw, wg, ya, yb, yc, wbr, wo, fn, final):
    t = x2.shape[0]
    tm = PROJ_ROWS
    const = lambda i: (0, 0)
    row = lambda i: (i, 0)
    return pl.pallas_call(
        functools.partial(_merge_kernel, final=final),
        grid=(t // tm,),
        in_specs=[pl.BlockSpec((tm, D_MODEL), row),
                  pl.BlockSpec((1, D_MODEL), const),
                  pl.BlockSpec(wg.shape, const),
                  pl.BlockSpec((tm, BRANCH_WIDTH), row),
                  pl.BlockSpec((tm, BRANCH_WIDTH), row),
                  pl.BlockSpec((tm, BRANCH_WIDTH), row),
                  pl.BlockSpec(wbr.shape, lambda i: (0, 0, 0)),
                  pl.BlockSpec(wo.shape, const),
                  pl.BlockSpec((1, D_MODEL), const)],
        out_specs=pl.BlockSpec((tm, D_MODEL), row),
        out_shape=jax.ShapeDtypeStruct((t, D_MODEL), F32),
        compiler_params=pltpu.CompilerParams(
            dimension_semantics=("arbitrary",), vmem_limit_bytes=VMEM_LIMIT),
        name="merge",
    )(x2, nw, wg, ya, yb, yc, wbr, wo, fn)


def _a_levels():
    return [1 << i for i in range(int(math.log2(A_CHUNK)))]


def _a_masks():
    idx = np.arange(A_CHUNK)
    masks = [np.eye(A_CHUNK, dtype=np.float32)]
    for m in _a_levels():
        blk = idx // (2 * m)
        masks.append((blk[:, None] == blk[None, :]).astype(np.float32))
    return np.stack(masks)


def _hgrn2_kernel(pa_ref, lbp_ref, nw_ref, tril_ref, masks_ref, o_ref,
                  state_ref, b_ref, gp_ref, e_ref, *, layer):
    c = A_CHUNK
    bw = BRANCH_WIDTH

    @pl.when(pl.program_id(1) == 0)
    def _():
        state_ref[...] = jnp.zeros_like(state_ref)
        gp_ref[...] = jnp.zeros_like(gp_ref)

    lbp = lbp_ref[...]
    ex = jnp.exp(lbp - jnp.max(lbp, axis=0, keepdims=True))
    sm = ex / jnp.sum(ex, axis=0, keepdims=True)
    lb = jnp.zeros((1, bw), F32)
    for j in range(1, layer + 1):
        lb = lb + sm[j:j + 1, :]

    q = _silu(pa_ref[:, 0:bw])
    z = pa_ref[:, bw:2 * bw]
    v = pa_ref[:, 2 * bw:3 * bw].astype(BF16)
    t1 = jnp.log(lb)
    t2 = jnp.log1p(-lb) + _log_sigmoid(z)
    lf = jnp.maximum(t1, t2) + jnp.log1p(jnp.exp(-jnp.abs(t1 - t2)))
    k = (1.0 - lb) * jax.nn.sigmoid(-z)

    b = jnp.dot(tril_ref[...], lf, precision=HIGHEST, preferred_element_type=F32)
    b_ref[...] = b
    gp_ref[SUBLANES:SUBLANES + c, :] = lf

    row = lax.broadcasted_iota(jnp.int32, (c, 1), 0)
    q16 = q.astype(BF16)
    k16 = k.astype(BF16)
    acc = [_dot_nt(q16[:, h * HEAD_DIM:(h + 1) * HEAD_DIM], k16[:, h * HEAD_DIM:(h + 1) * HEAD_DIM])
           * masks_ref[0] for h in range(HEADS)]

    for li, m in enumerate(_a_levels()):
        if m == 1:
            nege = jnp.where((row & 1) == 1, lf, 0.0)
        elif m == 2:
            r4 = row & 3
            g_next = gp_ref[SUBLANES + 1:SUBLANES + 1 + c, :]
            g_prev = gp_ref[SUBLANES - 1:SUBLANES - 1 + c, :]
            nege = jnp.where(r4 == 0, g_next, jnp.where(r4 == 1, 0.0, jnp.where(r4 == 2, lf, lf + g_prev)))
        else:
            for j in range(c // (2 * m)):
                lo = j * 2 * m
                anchor = b_ref[lo + m - 1:lo + m, :]
                e_ref[lo:lo + 2 * m, :] = -jnp.abs(b_ref[lo:lo + 2 * m, :] - anchor)
            nege = e_ref[...]
        decay = jnp.exp(nege)
        upper = (row & m) != 0
        qt = jnp.where(upper, q * decay, 0.0).astype(BF16)
        kt = jnp.where(upper, 0.0, k * decay).astype(BF16)
        for h in range(HEADS):
            sl = slice(h * HEAD_DIM, (h + 1) * HEAD_DIM)
            acc[h] = acc[h] + _dot_nt(qt[:, sl], kt[:, sl]) * masks_ref[li + 1]

    b_end = b_ref[c - 1:c, :]
    qd = (q * jnp.exp(b)).astype(BF16)
    kd = (k * jnp.exp(b_end - b)).astype(BF16)
    s_decay = jnp.exp(b_end)
    nw = nw_ref[...]
    for h in range(HEADS):
        sl = slice(h * HEAD_DIM, (h + 1) * HEAD_DIM)
        st = state_ref[h]
        o = _dot(acc[h].astype(BF16), v[:, sl]) + _dot_nt(qd[:, sl], st.astype(BF16))
        state_ref[h] = st * s_decay[:, sl] + _dot_tn(v[:, sl], kd[:, sl])
        y = o * lax.rsqrt(jnp.mean(o * o, axis=-1, keepdims=True) + EPS) * nw
        o_ref[:, sl] = (y * _silu(pa_ref[:, 3 * bw + h * HEAD_DIM:3 * bw + (h + 1) * HEAD_DIM])).astype(BF16)


def _hgrn2(pa, lbp, nw, batch, seq, layer):
    c = A_CHUNK
    nc = seq // c
    tril = jnp.asarray(np.tril(np.ones((c, c), np.float32)))
    masks = jnp.asarray(_a_masks())
    const2 = lambda b, i: (0, 0)
    return pl.pallas_call(
        functools.partial(_hgrn2_kernel, layer=layer),
        grid=(batch, nc),
        in_specs=[pl.BlockSpec((c, PA_WIDTH), lambda b, i: (b * nc + i, 0)),
                  pl.BlockSpec(lbp.shape, const2),
                  pl.BlockSpec((1, HEAD_DIM), const2),
                  pl.BlockSpec((c, c), const2),
                  pl.BlockSpec(masks.shape, lambda b, i: (0, 0, 0))],
        out_specs=pl.BlockSpec((c, BRANCH_WIDTH), lambda b, i: (b * nc + i, 0)),
        out_shape=jax.ShapeDtypeStruct((batch * seq, BRANCH_WIDTH), BF16),
        scratch_shapes=[pltpu.VMEM((HEADS, HEAD_DIM, HEAD_DIM), F32),
                        pltpu.VMEM((c, BRANCH_WIDTH), F32),
                        pltpu.VMEM((c + 2 * SUBLANES, BRANCH_WIDTH), F32),
                        pltpu.VMEM((c, BRANCH_WIDTH), F32)],
        compiler_params=pltpu.CompilerParams(
            dimension_semantics=("arbitrary", "arbitrary"), vmem_limit_bytes=VMEM_LIMIT),
        name="hgrn2",
    )(pa, lbp, nw, tril, masks)


def _split2(x):
    hi = x.astype(BF16)
    lo = (x - hi.astype(F32)).astype(BF16)
    return hi, lo


def _mm3(a, b):
    ah, al = _split2(a)
    bh, bl = _split2(b)
    return _dot(ah, bh) + (_dot(ah, bl) + _dot(al, bh))


def _unit_lower_inverse(a_strict):
    n = a_strict.shape[0]
    eye = (lax.broadcasted_iota(jnp.int32, (n, n), 0) == lax.broadcasted_iota(jnp.int32, (n, n), 1)).astype(F32)
    p = -a_strict
    t = eye + p
    for _ in range(int(math.log2(n)) - 1):
        p = _mm3(p, p)
        t = t + _mm3(t, p)
    return t


def _deltanet_kernel(pb_ref, cw_ref, alog_ref, dtb_ref, nw_ref, tril_ref, o_ref,
                     state_ref, xp_ref):
    c = B_CHUNK
    bw = BRANCH_WIDTH
    sl8 = SUBLANES

    @pl.when(pl.program_id(1) == 0)
    def _():
        state_ref[...] = jnp.zeros_like(state_ref)
        xp_ref[0:sl8, :] = jnp.zeros((sl8, B_CONV_CH), F32)

    xp_ref[sl8:sl8 + c, :] = pb_ref[:, 0:B_CONV_CH]
    conv = None
    for j in range(B_CONV):
        off = sl8 - (B_CONV - 1) + j
        term = xp_ref[off:off + c, :] * cw_ref[j:j + 1, :]
        conv = term if conv is None else conv + term
    xp_ref[0:sl8, :] = xp_ref[c:c + sl8, :]
    qkv = _silu(conv)

    small = pb_ref[:, B_CONV_CH + bw:B_CONV_CH + bw + SMALL_PAD]
    beta_all = jax.nn.sigmoid(small)
    g_all = -jnp.exp(alog_ref[...]) * _softplus(small + dtb_ref[...])
    gcum = jnp.dot(tril_ref[...], g_all, precision=HIGHEST, preferred_element_type=F32)
    gcum_t = gcum.T

    ti = lax.broadcasted_iota(jnp.int32, (c, c), 0)
    si = lax.broadcasted_iota(jnp.int32, (c, c), 1)
    incl = ti >= si
    strict = ti > si
    nw = nw_ref[...]

    for h in range(HEADS):
        sl = slice(h * HEAD_DIM, (h + 1) * HEAD_DIM)
        q = qkv[:, sl]
        k = qkv[:, bw + h * HEAD_DIM:bw + (h + 1) * HEAD_DIM]
        v = qkv[:, 2 * bw + h * HEAD_DIM:2 * bw + (h + 1) * HEAD_DIM]
        q = q * lax.rsqrt(jnp.sum(q * q, axis=-1, keepdims=True) + EPS) * (HEAD_DIM ** -0.5)
        k = k * lax.rsqrt(jnp.sum(k * k, axis=-1, keepdims=True) + EPS)
        beta = beta_all[:, h:h + 1]
        gc = gcum[:, HEADS + h:HEADS + h + 1]
        gr = gcum_t[HEADS + h:HEADS + h + 1, :]
        g_last = gcum[c - 1:c, HEADS + h:HEADS + h + 1]

        decay = jnp.where(incl, jnp.exp(jnp.minimum(gc - gr, 0.0)), 0.0)
        kb = k * beta
        k16 = k.astype(BF16)
        a = jnp.where(strict, _dot_nt(kb.astype(BF16), k16) * decay, 0.0)
        t = _unit_lower_inverse(a).astype(BF16)
        egc = jnp.exp(gc)
        u = _dot(t, (v * beta).astype(BF16))
        w = _dot(t, (kb * egc).astype(BF16))
        qk = (_dot_nt(q.astype(BF16), k16) * decay).astype(BF16)
        qd = (q * egc).astype(BF16)
        kd = (k * jnp.exp(g_last - gc)).astype(BF16)

        st = state_ref[h]
        st16 = st.astype(BF16)
        v_new = u - _dot(w.astype(BF16), st16)
        vn16 = v_new.astype(BF16)
        o = _dot(qd, st16) + _dot(qk, vn16)
        state_ref[h] = st * jnp.exp(g_last) + _dot_tn(kd, vn16)

        y = o * lax.rsqrt(jnp.mean(o * o, axis=-1, keepdims=True) + EPS) * nw
        o_ref[:, sl] = (y * _silu(pb_ref[:, B_CONV_CH + h * HEAD_DIM:B_CONV_CH + (h + 1) * HEAD_DIM])).astype(BF16)


def _deltanet(pb, cw, alog, dtb, nw, batch, seq):
    c = B_CHUNK
    nc = seq // c
    tril = jnp.asarray(np.tril(np.ones((c, c), np.float32)))
    const2 = lambda b, i: (0, 0)
    return pl.pallas_call(
        _deltanet_kernel,
        grid=(batch, nc),
        in_specs=[pl.BlockSpec((c, PB_WIDTH), lambda b, i: (b * nc + i, 0)),
                  pl.BlockSpec(cw.shape, const2),
                  pl.BlockSpec((1, SMALL_PAD), const2),
                  pl.BlockSpec((1, SMALL_PAD), const2),
                  pl.BlockSpec((1, HEAD_DIM), const2),
                  pl.BlockSpec((c, c), const2)],
        out_specs=pl.BlockSpec((c, BRANCH_WIDTH), lambda b, i: (b * nc + i, 0)),
        out_shape=jax.ShapeDtypeStruct((batch * seq, BRANCH_WIDTH), BF16),
        scratch_shapes=[pltpu.VMEM((HEADS, HEAD_DIM, HEAD_DIM), F32),
                        pltpu.VMEM((c + SUBLANES, B_CONV_CH), F32)],
        compiler_params=pltpu.CompilerParams(
            dimension_semantics=("arbitrary", "arbitrary"), vmem_limit_bytes=VMEM_LIMIT),
        name="deltanet",
    )(pb, cw, alog, dtb, nw, tril)


def _band_buckets():
    i = np.arange(C_BLOCK)[:, None]
    j = np.arange(2 * C_BLOCK)[None, :]
    dist = i + C_BLOCK - j
    max_exact = N_BUCKETS // 2
    d_f = np.maximum(dist, 1).astype(np.float32)
    large = max_exact + (np.log(d_f / np.float32(max_exact)) / np.float32(math.log(MAX_DISTANCE / max_exact))
                         * np.float32(N_BUCKETS - max_exact)).astype(np.int32)
    large = np.minimum(large, N_BUCKETS - 1)
    bucket = np.where(dist < max_exact, dist, large)
    return np.where((dist >= 0) & (dist < WINDOW), bucket, -1).astype(np.int32)


def _swa_kernel(rb_ref, sink_ref, cur_ref, prev_ref, bucket_ref, o_ref, bias_ref):
    blk = C_BLOCK
    bw = BRANCH_WIDTH
    n = pl.program_id(1)

    @pl.when((pl.program_id(0) == 0) & (n == 0))
    def _():
        bucket = bucket_ref[...]
        for h in range(C_Q_HEADS):
            acc = jnp.full((blk, 2 * blk), NEG_BIG, F32)
            for bk in range(N_BUCKETS):
                acc = jnp.where(bucket == bk, rb_ref[bk, h], acc)
            bias_ref[h] = acc

    kv_cur = cur_ref[:, bw:bw + 2 * C_KV_WIDTH]
    kv_prev = prev_ref[...]
    kcat = jnp.concatenate([kv_prev[:, 0:C_KV_WIDTH], kv_cur[:, 0:C_KV_WIDTH]], axis=0)
    vcat = jnp.concatenate([kv_prev[:, C_KV_WIDTH:], kv_cur[:, C_KV_WIDTH:]], axis=0)
    lane = lax.broadcasted_iota(jnp.int32, (2 * blk, C_KV_WIDTH), 1)
    key_ok = lax.broadcasted_iota(jnp.int32, (1, 2 * blk), 1) >= jnp.where(n > 0, 0, blk)

    def halves(x, j):
        own = jnp.where((lane >= j * C_HEAD_DIM) & (lane < (j + 1) * C_HEAD_DIM), x, 0.0)
        other = pltpu.roll(own, C_HEAD_DIM, 1)
        pair = (own, other) if j == 0 else (other, own)
        return [t.astype(BF16) for t in pair]

    kz = [halves(kcat, j) for j in range(C_KV_HEADS)]
    vz = [halves(vcat, j) for j in range(C_KV_HEADS)]
    scale = C_HEAD_DIM ** -0.5
    group = C_Q_HEADS // C_KV_HEADS
    for p in range(C_Q_HEADS // 2):
        q2 = cur_ref[:, p * LANES:(p + 1) * LANES].astype(BF16)
        out = None
        for half in range(2):
            h = 2 * p + half
            j = h // group
            logits = _dot_nt(q2, kz[j][half]) * scale + bias_ref[h]
            logits = jnp.where(key_ok, logits, NEG_BIG)
            sink = sink_ref[h]
            mx = jnp.maximum(jnp.max(logits, axis=-1, keepdims=True), sink)
            pr = jnp.exp(logits - mx)
            den = jnp.sum(pr, axis=-1, keepdims=True) + jnp.exp(sink - mx)
            pv = _dot((pr / den).astype(BF16), vz[j][half])
            out = pv if out is None else out + pv
        gate = cur_ref[:, bw + 2 * C_KV_WIDTH + p * LANES:bw + 2 * C_KV_WIDTH + (p + 1) * LANES]
        o_ref[:, p * LANES:(p + 1) * LANES] = (out * _silu(gate)).astype(BF16)


def _swa(pc, rel_bias, sinks, batch, seq):
    blk = C_BLOCK
    nb = seq // blk
    bucket = jnp.asarray(_band_buckets())
    kv_col = BRANCH_WIDTH // (2 * C_KV_WIDTH)
    smem = pl.BlockSpec(memory_space=pltpu.SMEM)
    return pl.pallas_call(
        _swa_kernel,
        grid=(batch, nb),
        in_specs=[smem, smem,
                  pl.BlockSpec((blk, PC_WIDTH), lambda b, i: (b * nb + i, 0)),
                  pl.BlockSpec((blk, 2 * C_KV_WIDTH), lambda b, i: (b * nb + jnp.maximum(i - 1, 0), kv_col)),
                  pl.BlockSpec((blk, 2 * blk), lambda b, i: (0, 0))],
        out_specs=pl.BlockSpec((blk, BRANCH_WIDTH), lambda b, i: (b * nb + i, 0)),
        out_shape=jax.ShapeDtypeStruct((batch * seq, BRANCH_WIDTH), BF16),
        scratch_shapes=[pltpu.VMEM((C_Q_HEADS, blk, 2 * blk), F32)],
        compiler_params=pltpu.CompilerParams(
            dimension_semantics=("arbitrary", "arbitrary"), vmem_limit_bytes=VMEM_LIMIT),
        name="swa",
    )(rel_bias, sinks, pc, pc, bucket)


def _split_w_in(w):
    bw = BRANCH_WIDTH
    widths = (bw, bw, bw, bw, B_CONV_CH, bw, HEADS, HEADS, bw, C_KV_WIDTH, C_KV_WIDTH, bw, N_BRANCHES * D_MODEL)
    offs = np.concatenate([[0], np.cumsum(widths)])
    col = lambda i: w[:, offs[i]:offs[i + 1]]
    wa = w[:, offs[0]:offs[4]]
    wb = jnp.concatenate([col(4), col(5), col(6), col(7),
                          jnp.zeros((D_MODEL, SMALL_PAD - 2 * HEADS), w.dtype)], axis=1)
    wc = w[:, offs[8]:offs[12]]
    wg = col(12)
    return [t.astype(BF16) for t in (wa, wb, wc, wg)]


def _lane_pad(v, offset):
    return jnp.zeros((1, SMALL_PAD), F32).at[0, offset:offset + v.shape[0]].set(v.astype(F32))


def kernel(x, norm_w, w_in, conv_w, a_log, dt_bias, lb_param, norm_a, norm_b, sinks, rel_bias,
           w_branch, w_out, final_norm):
    batch, seq, _ = x.shape
    depth = w_in.shape[0]
    x2 = x.reshape(batch * seq, D_MODEL)
    fn = final_norm.reshape(1, D_MODEL)
    for l in range(depth):
        wa, wb, wc, wg = _split_w_in(w_in[l])
        nw = norm_w[l].reshape(1, D_MODEL)
        pa, pb, pc = _inproj(x2, nw, wa, wb, wc)
        ya = _hgrn2(pa, lb_param, norm_a[l].reshape(1, HEAD_DIM), batch, seq, l)
        yb = _deltanet(pb, conv_w[l], _lane_pad(a_log[l], HEADS), _lane_pad(dt_bias[l], HEADS),
                       norm_b[l].reshape(1, HEAD_DIM), batch, seq)
        yc = _swa(pc, rel_bias, sinks[l], batch, seq)
        x2 = _merge(x2, nw, wg, ya, yb, yc, w_branch[l].astype(BF16), w_out[l].astype(BF16), fn,
                    final=(l == depth - 1))
    return x2.reshape(batch, seq, D_MODEL)
```

```python
import functools
import math

import numpy as np
import jax
import jax.numpy as jnp
from jax import lax
from jax.experimental import pallas as pl
from jax.experimental.pallas import tpu as pltpu

F32 = jnp.float32
BF16 = jnp.bfloat16
HIGHEST = lax.Precision.HIGHEST

D_MODEL = 1024
BRANCH_WIDTH = D_MODEL // 2
N_BRANCHES = 3
EPS = 1e-6
HEADS = 4
HEAD_DIM = BRANCH_WIDTH // HEADS
B_CONV = 4
B_CONV_CH = 3 * BRANCH_WIDTH
C_Q_HEADS = 8
C_KV_HEADS = 2
C_HEAD_DIM = BRANCH_WIDTH // C_Q_HEADS
C_KV_WIDTH = C_KV_HEADS * C_HEAD_DIM
WINDOW = 128
C_BLOCK = 128
N_BUCKETS = 32
MAX_DISTANCE = 128
LANES = 128
SUBLANES = 8
NEG_BIG = -1e30

A_CHUNK = 128
B_CHUNK = 64
B_PAIR = 2 * B_CHUNK
B_TILE = 256
PROJ_ROWS = 256
SMALL_PAD = LANES
VMEM_LIMIT = 52 * 1024 * 1024

PA_WIDTH = 4 * BRANCH_WIDTH
PB_WIDTH = B_CONV_CH + BRANCH_WIDTH + SMALL_PAD
PC_WIDTH = 2 * BRANCH_WIDTH + 2 * C_KV_WIDTH


def _dot(a, b):
    return jnp.dot(a, b, preferred_element_type=F32)


def _dot_nt(a, b):
    return lax.dot_general(a, b, (((1,), (1,)), ((), ())), preferred_element_type=F32)


def _dot_tn(a, b):
    return lax.dot_general(a, b, (((0,), (0,)), ((), ())), preferred_element_type=F32)


def _silu(x):
    return x * jax.nn.sigmoid(x)


def _log_sigmoid(x):
    return jnp.minimum(x, 0.0) - jnp.log1p(jnp.exp(-jnp.abs(x)))


def _softplus(x):
    return jnp.maximum(x, 0.0) + jnp.log1p(jnp.exp(-jnp.abs(x)))


def _rms(x, w):
    return x * lax.rsqrt(jnp.mean(x * x, axis=-1, keepdims=True) + EPS) * w


def _inproj_kernel(x_ref, nw_ref, wa_ref, wb_ref, wc_ref, pa_ref, pb_ref, pc_ref):
    h = _rms(x_ref[...], nw_ref[...]).astype(BF16)
    for w_ref, o_ref in ((wa_ref, pa_ref), (wb_ref, pb_ref), (wc_ref, pc_ref)):
        n = w_ref.shape[1]
        for j in range(0, n, BRANCH_WIDTH):
            wd = min(BRANCH_WIDTH, n - j)
            o_ref[:, j:j + wd] = _dot(h, w_ref[:, j:j + wd])


def _inproj(x2, nw, wa, wb, wc):
    t = x2.shape[0]
    tm = PROJ_ROWS
    const = lambda i: (0, 0)
    return pl.pallas_call(
        _inproj_kernel,
        grid=(t // tm,),
        in_specs=[pl.BlockSpec((tm, D_MODEL), lambda i: (i, 0)),
                  pl.BlockSpec((1, D_MODEL), const),
                  pl.BlockSpec(wa.shape, const),
                  pl.BlockSpec(wb.shape, const),
                  pl.BlockSpec(wc.shape, const)],
        out_specs=[pl.BlockSpec((tm, PA_WIDTH), lambda i: (i, 0)),
                   pl.BlockSpec((tm, PB_WIDTH), lambda i: (i, 0)),
                   pl.BlockSpec((tm, PC_WIDTH), lambda i: (i, 0))],
        out_shape=[jax.ShapeDtypeStruct((t, PA_WIDTH), F32),
                   jax.ShapeDtypeStruct((t, PB_WIDTH), F32),
                   jax.ShapeDtypeStruct((t, PC_WIDTH), F32)],
        compiler_params=pltpu.CompilerParams(
            dimension_semantics=("arbitrary",), vmem_limit_bytes=VMEM_LIMIT),
        name="inproj",
    )(x2, nw, wa, wb, wc)


def _merge_kernel(x_ref, nw_ref, wg_ref, ya_ref, yb_ref, yc_ref, wbr_ref, wo_ref, fn_ref,
                  o_ref, *, final):
    x = x_ref[...]
    h = _rms(x, nw_ref[...]).astype(BF16)
    merged = None
    for n, y_ref in enumerate((ya_ref, yb_ref, yc_ref)):
        gate = jax.nn.sigmoid(_dot(h, wg_ref[:, n * D_MODEL:(n + 1) * D_MODEL]))
        term = gate * _dot(y_ref[...], wbr_ref[n])
        merged = term if merged is None else merged + term
    out = x + _dot(merged.astype(BF16), wo_ref[...])
    if final:
        out = _rms(out, fn_ref[...])
    o_ref[...] = out


def _merge(x2, nw, wg, ya, yb, yc, wbr, wo, fn, final):
    t = x2.shape[0]
    tm = PROJ_ROWS
    const = lambda i: (0, 0)
    row = lambda i: (i, 0)
    return pl.pallas_call(
        functools.partial(_merge_kernel, final=final),
        grid=(t // tm,),
        in_specs=[pl.BlockSpec((tm, D_MODEL), row),
                  pl.BlockSpec((1, D_MODEL), const),
                  pl.BlockSpec(wg.shape, const),
                  pl.BlockSpec((tm, BRANCH_WIDTH), row),
                  pl.BlockSpec((tm, BRANCH_WIDTH), row),
                  pl.BlockSpec((tm, BRANCH_WIDTH), row),
                  pl.BlockSpec(wbr.shape, lambda i: (0, 0, 0)),
                  pl.BlockSpec(wo.shape, const),
                  pl.BlockSpec((1, D_MODEL), const)],
        out_specs=pl.BlockSpec((tm, D_MODEL), row),
        out_shape=jax.ShapeDtypeStruct((t, D_MODEL), F32),
        compiler_params=pltpu.CompilerParams(
            dimension_semantics=("arbitrary",), vmem_limit_bytes=VMEM_LIMIT),
        name="merge",
    )(x2, nw, wg, ya, yb, yc, wbr, wo, fn)


def _a_levels():
    return [1 << i for i in range(int(math.log2(A_CHUNK)))]


def _a_masks():
    idx = np.arange(A_CHUNK)
    masks = [np.eye(A_CHUNK, dtype=np.float32)]
    for m in _a_levels():
        blk = idx // (2 * m)
        masks.append((blk[:, None] == blk[None, :]).astype(np.float32))
    return np.stack(masks)


def _hgrn2_kernel(pa_ref, lbp_ref, nw_ref, tril_ref, masks_ref, o_ref,
                  state_ref, b_ref, gp_ref, e_ref, *, layer):
    c = A_CHUNK
    bw = BRANCH_WIDTH

    @pl.when(pl.program_id(1) == 0)
    def _():
        state_ref[...] = jnp.zeros_like(state_ref)
        gp_ref[...] = jnp.zeros_like(gp_ref)

    lbp = lbp_ref[...]
    ex = jnp.exp(lbp - jnp.max(lbp, axis=0, keepdims=True))
    sm = ex / jnp.sum(ex, axis=0, keepdims=True)
    lb = jnp.zeros((1, bw), F32)
    for j in range(1, layer + 1):
        lb = lb + sm[j:j + 1, :]

    q = _silu(pa_ref[:, 0:bw])
    z = pa_ref[:, bw:2 * bw]
    v = pa_ref[:, 2 * bw:3 * bw].astype(BF16)
    t1 = jnp.log(lb)
    t2 = jnp.log1p(-lb) + _log_sigmoid(z)
    lf = jnp.maximum(t1, t2) + jnp.log1p(jnp.exp(-jnp.abs(t1 - t2)))
    k = (1.0 - lb) * jax.nn.sigmoid(-z)

    b = jnp.dot(tril_ref[...], lf, precision=HIGHEST, preferred_element_type=F32)
    b_ref[...] = b
    gp_ref[SUBLANES:SUBLANES + c, :] = lf

    row = lax.broadcasted_iota(jnp.int32, (c, 1), 0)
    q16 = q.astype(BF16)
    k16 = k.astype(BF16)
    acc = [_dot_nt(q16[:, h * HEAD_DIM:(h + 1) * HEAD_DIM], k16[:, h * HEAD_DIM:(h + 1) * HEAD_DIM])
           * masks_ref[0] for h in range(HEADS)]

    for li, m in enumerate(_a_levels()):
        if m == 1:
            nege = jnp.where((row & 1) == 1, lf, 0.0)
        elif m == 2:
            r4 = row & 3
            g_next = gp_ref[SUBLANES + 1:SUBLANES + 1 + c, :]
            g_prev = gp_ref[SUBLANES - 1:SUBLANES - 1 + c, :]
            nege = jnp.where(r4 == 0, g_next, jnp.where(r4 == 1, 0.0, jnp.where(r4 == 2, lf, lf + g_prev)))
        else:
            for j in range(c // (2 * m)):
                lo = j * 2 * m
                anchor = b_ref[lo + m - 1:lo + m, :]
                e_ref[lo:lo + 2 * m, :] = -jnp.abs(b_ref[lo:lo + 2 * m, :] - anchor)
            nege = e_ref[...]
        decay = jnp.exp(nege)
        upper = (row & m) != 0
        qt = jnp.where(upper, q * decay, 0.0).astype(BF16)
        kt = jnp.where(upper, 0.0, k * decay).astype(BF16)
        for h in range(HEADS):
            sl = slice(h * HEAD_DIM, (h + 1) * HEAD_DIM)
            acc[h] = acc[h] + _dot_nt(qt[:, sl], kt[:, sl]) * masks_ref[li + 1]

    b_end = b_ref[c - 1:c, :]
    qd = (q * jnp.exp(b)).astype(BF16)
    kd = (k * jnp.exp(b_end - b)).astype(BF16)
    s_decay = jnp.exp(b_end)
    nw = nw_ref[...]
    for h in range(HEADS):
        sl = slice(h * HEAD_DIM, (h + 1) * HEAD_DIM)
        st = state_ref[h]
        o = _dot(acc[h].astype(BF16), v[:, sl]) + _dot_nt(qd[:, sl], st.astype(BF16))
        state_ref[h] = st * s_decay[:, sl] + _dot_tn(v[:, sl], kd[:, sl])
        y = o * lax.rsqrt(jnp.mean(o * o, axis=-1, keepdims=True) + EPS) * nw
        o_ref[:, sl] = (y * _silu(pa_ref[:, 3 * bw + h * HEAD_DIM:3 * bw + (h + 1) * HEAD_DIM])).astype(BF16)


def _hgrn2(pa, lbp, nw, batch, seq, layer):
    c = A_CHUNK
    nc = seq // c
    tril = jnp.asarray(np.tril(np.ones((c, c), np.float32)))
    masks = jnp.asarray(_a_masks())
    const2 = lambda b, i: (0, 0)
    return pl.pallas_call(
        functools.partial(_hgrn2_kernel, layer=layer),
        grid=(batch, nc),
        in_specs=[pl.BlockSpec((c, PA_WIDTH), lambda b, i: (b * nc + i, 0)),
                  pl.BlockSpec(lbp.shape, const2),
                  pl.BlockSpec((1, HEAD_DIM), const2),
                  pl.BlockSpec((c, c), const2),
                  pl.BlockSpec(masks.shape, lambda b, i: (0, 0, 0))],
        out_specs=pl.BlockSpec((c, BRANCH_WIDTH), lambda b, i: (b * nc + i, 0)),
        out_shape=jax.ShapeDtypeStruct((batch * seq, BRANCH_WIDTH), BF16),
        scratch_shapes=[pltpu.VMEM((HEADS, HEAD_DIM, HEAD_DIM), F32),
                        pltpu.VMEM((c, BRANCH_WIDTH), F32),
                        pltpu.VMEM((c + 2 * SUBLANES, BRANCH_WIDTH), F32),
                        pltpu.VMEM((c, BRANCH_WIDTH), F32)],
        compiler_params=pltpu.CompilerParams(
            dimension_semantics=("arbitrary", "arbitrary"), vmem_limit_bytes=VMEM_LIMIT),
        name="hgrn2",
    )(pa, lbp, nw, tril, masks)


def _unit_lower_inverses(mats, ti, si):
    eye = (ti == si).astype(F32)
    ts = [eye] * len(mats)
    s = 1
    while s < B_CHUNK:
        shift = int(math.log2(2 * s))
        off = ((ti >> shift) == (si >> shift)) & ((ti & s) != 0) & ((si & s) == 0)
        a_off = [jnp.where(off, a, 0.0) for a in mats]
        if s == 1:
            ts = [t - ao for t, ao in zip(ts, a_off)]
        else:
            t16 = [t.astype(BF16) for t in ts]
            inner = [_dot(ao.astype(BF16), t) for ao, t in zip(a_off, t16)]
            outer = [_dot(t, x.astype(BF16)) for t, x in zip(t16, inner)]
            ts = [t - x for t, x in zip(ts, outer)]
        s *= 2
    return ts


def _dn_prep_kernel(pb_ref, halo_ref, cw_ref, alog_ref, dtb_ref, tril_ref,
                    u_ref, w_ref, qd_ref, kd_ref, qk_ref, ge_ref, xp_ref, *, tiles_per_seq):
    ts = B_TILE
    c = B_CHUNK
    pr = B_PAIR
    bw = BRANCH_WIDTH
    sl8 = SUBLANES

    keep = jnp.where(pl.program_id(0) % tiles_per_seq == 0, 0.0, 1.0)
    xp_ref[0:sl8, :] = halo_ref[...] * keep
    xp_ref[sl8:sl8 + ts, :] = pb_ref[:, 0:B_CONV_CH]
    conv = None
    for j in range(B_CONV):
        off = sl8 - (B_CONV - 1) + j
        term = xp_ref[off:off + ts, :] * cw_ref[j:j + 1, :]
        conv = term if conv is None else conv + term
    qkv = _silu(conv)

    small = pb_ref[:, B_CONV_CH + bw:B_CONV_CH + bw + SMALL_PAD]
    beta_all = jax.nn.sigmoid(small)
    g_all = -jnp.exp(alog_ref[...]) * _softplus(small + dtb_ref[...])
    gcum = jnp.dot(tril_ref[...], g_all, precision=HIGHEST, preferred_element_type=F32)
    gcum_t = gcum.T
    for j in range(ts // c):
        ge_ref[0, j:j + 1, :] = jnp.exp(gcum[j * c + c - 1:j * c + c, :])

    ti = lax.broadcasted_iota(jnp.int32, (pr, pr), 0)
    si = lax.broadcasted_iota(jnp.int32, (pr, pr), 1)
    same = (ti >> int(math.log2(c))) == (si >> int(math.log2(c)))
    incl = same & (ti >= si)
    strict = same & (ti > si)
    first_chunk = lax.broadcasted_iota(jnp.int32, (pr, 1), 0) < c

    probs = [(h, p) for h in range(HEADS) for p in range(ts // pr)]
    k16s, kb16s, q16s, decays, rhs = [], [], [], [], []
    for h, p in probs:
        sl = slice(h * HEAD_DIM, (h + 1) * HEAD_DIM)
        rows = slice(p * pr, (p + 1) * pr)
        q = qkv[rows, sl]
        k = qkv[rows, bw + h * HEAD_DIM:bw + (h + 1) * HEAD_DIM]
        v = qkv[rows, 2 * bw + h * HEAD_DIM:2 * bw + (h + 1) * HEAD_DIM]
        q = q * lax.rsqrt(jnp.sum(q * q, axis=-1, keepdims=True) + EPS) * (HEAD_DIM ** -0.5)
        k = k * lax.rsqrt(jnp.sum(k * k, axis=-1, keepdims=True) + EPS)
        beta = beta_all[rows, h:h + 1]
        gc = gcum[rows, HEADS + h:HEADS + h + 1]
        gr = gcum_t[HEADS + h:HEADS + h + 1, rows]
        g_last = jnp.where(first_chunk, gc[c - 1:c, :], gc[pr - 1:pr, :])
        egc = jnp.exp(gc)
        kb = k * beta
        decays.append(jnp.where(incl, jnp.exp(jnp.minimum(gc - gr, 0.0)), 0.0))
        k16s.append(k.astype(BF16))
        kb16s.append(kb.astype(BF16))
        q16s.append(q.astype(BF16))
        rhs.append(jnp.concatenate([v * beta, kb * egc], axis=1).astype(BF16))
        qd_ref[rows, sl] = (q * egc).astype(BF16)
        kd_ref[rows, sl] = (k * jnp.exp(g_last - gc)).astype(BF16)

    kk = [_dot_nt(kb16, k16) for kb16, k16 in zip(kb16s, k16s)]
    qk = [_dot_nt(q16, k16) for q16, k16 in zip(q16s, k16s)]
    mats = [jnp.where(strict, x * d, 0.0) for x, d in zip(kk, decays)]
    tinv = _unit_lower_inverses(mats, ti, si)
    uw = [_dot(t.astype(BF16), r) for t, r in zip(tinv, rhs)]
    for i, (h, p) in enumerate(probs):
        sl = slice(h * HEAD_DIM, (h + 1) * HEAD_DIM)
        rows = slice(p * pr, (p + 1) * pr)
        u_ref[rows, sl] = uw[i][:, 0:HEAD_DIM]
        w_ref[rows, sl] = uw[i][:, HEAD_DIM:].astype(BF16)
        qk_ref[p, h] = (qk[i] * decays[i]).astype(BF16)


def _dn_scan_kernel(u_ref, w_ref, qd_ref, kd_ref, qk_ref, ge_ref, z_ref, nw_ref, o_ref, state_ref):
    c = B_CHUNK
    batch = u_ref.shape[0]

    @pl.when(pl.program_id(0) == 0)
    def _():
        state_ref[...] = jnp.zeros_like(state_ref)

    nw = nw_ref[...]
    chains = [(b, h) for b in range(batch) for h in range(HEADS)]
    hsl = lambda h: slice(h * HEAD_DIM, (h + 1) * HEAD_DIM)
    states = [state_ref[b * HEADS + h] for b, h in chains]
    o_inter = [[] for _ in chains]
    v_new = [[] for _ in chains]
    for j in range(B_PAIR // c):
        rows = slice(j * c, (j + 1) * c)
        st16 = [st.astype(BF16) for st in states]
        lhs = [jnp.concatenate([w_ref[b, rows, hsl(h)], qd_ref[b, rows, hsl(h)]], axis=0) for b, h in chains]
        prod = [_dot(x, s) for x, s in zip(lhs, st16)]
        vn16 = []
        for i, (b, h) in enumerate(chains):
            vn = (u_ref[b, rows, hsl(h)] - prod[i][0:c, :]).astype(BF16)
            vn16.append(vn)
            v_new[i].append(vn)
            o_inter[i].append(prod[i][c:, :])
        upd = [_dot_tn(kd_ref[b, rows, hsl(h)], vn) for (b, h), vn in zip(chains, vn16)]
        states = [st * ge_ref[b, 0, j:j + 1, HEADS + h:HEADS + h + 1] + x
                  for (b, h), st, x in zip(chains, states, upd)]
    intra = [_dot(qk_ref[b, 0, h], jnp.concatenate(v_new[i], axis=0)) for i, (b, h) in enumerate(chains)]
    for i, (b, h) in enumerate(chains):
        state_ref[b * HEADS + h] = states[i]
        o = jnp.concatenate(o_inter[i], axis=0) + intra[i]
        y = o * lax.rsqrt(jnp.mean(o * o, axis=-1, keepdims=True) + EPS) * nw
        o_ref[b, :, hsl(h)] = (y * _silu(z_ref[b, :, hsl(h)])).astype(BF16)


def _deltanet(pb, cw, alog, dtb, nw, batch, seq):
    c = B_CHUNK
    ts = B_TILE
    pr = B_PAIR
    bw = BRANCH_WIDTH
    t = batch * seq
    n_tiles = t // ts
    tril = jnp.asarray(np.kron(np.eye(ts // c, dtype=np.float32), np.tril(np.ones((c, c), np.float32))))
    const = lambda i: (0, 0)
    row = lambda i: (i, 0)
    halo_blocks = ts // SUBLANES
    u, w, qd, kd, qk, ge = pl.pallas_call(
        functools.partial(_dn_prep_kernel, tiles_per_seq=seq // ts),
        grid=(n_tiles,),
        in_specs=[pl.BlockSpec((ts, PB_WIDTH), row),
                  pl.BlockSpec((SUBLANES, B_CONV_CH), lambda i: (jnp.maximum(i * halo_blocks - 1, 0), 0)),
                  pl.BlockSpec(cw.shape, const),
                  pl.BlockSpec((1, SMALL_PAD), const),
                  pl.BlockSpec((1, SMALL_PAD), const),
                  pl.BlockSpec((ts, ts), const)],
        out_specs=[pl.BlockSpec((ts, bw), row),
                   pl.BlockSpec((ts, bw), row),
                   pl.BlockSpec((ts, bw), row),
                   pl.BlockSpec((ts, bw), row),
                   pl.BlockSpec((ts // pr, HEADS, pr, pr), lambda i: (i, 0, 0, 0)),
                   pl.BlockSpec((1, ts // c, SMALL_PAD), lambda i: (i, 0, 0))],
        out_shape=[jax.ShapeDtypeStruct((t, bw), F32),
                   jax.ShapeDtypeStruct((t, bw), BF16),
                   jax.ShapeDtypeStruct((t, bw), BF16),
                   jax.ShapeDtypeStruct((t, bw), BF16),
                   jax.ShapeDtypeStruct((t // pr, HEADS, pr, pr), BF16),
                   jax.ShapeDtypeStruct((n_tiles, ts // c, SMALL_PAD), F32)],
        scratch_shapes=[pltpu.VMEM((ts + SUBLANES, B_CONV_CH), F32)],
        compiler_params=pltpu.CompilerParams(
            dimension_semantics=("arbitrary",), vmem_limit_bytes=VMEM_LIMIT),
        name="dn_prep",
    )(pb, pb, cw, alog, dtb, tril)

    np_seq = seq // pr
    seq3 = lambda x: x.reshape(batch, seq, x.shape[-1])
    blk3 = pl.BlockSpec((batch, pr, bw), lambda i: (0, i, 0))
    y = pl.pallas_call(
        _dn_scan_kernel,
        grid=(np_seq,),
        in_specs=[blk3, blk3, blk3, blk3,
                  pl.BlockSpec((batch, 1, HEADS, pr, pr), lambda i: (0, i, 0, 0, 0)),
                  pl.BlockSpec((batch, 1, pr // c, SMALL_PAD), lambda i: (0, i, 0, 0)),
                  pl.BlockSpec((batch, pr, bw), lambda i: (0, i, B_CONV_CH // bw)),
                  pl.BlockSpec((1, HEAD_DIM), const)],
        out_specs=blk3,
        out_shape=jax.ShapeDtypeStruct((batch, seq, bw), BF16),
        scratch_shapes=[pltpu.VMEM((batch * HEADS, HEAD_DIM, HEAD_DIM), F32)],
        compiler_params=pltpu.CompilerParams(
            dimension_semantics=("arbitrary",), vmem_limit_bytes=VMEM_LIMIT),
        name="dn_scan",
    )(seq3(u), seq3(w), seq3(qd), seq3(kd),
      qk.reshape(batch, np_seq, HEADS, pr, pr),
      ge.reshape(batch, np_seq, pr // c, SMALL_PAD),
      seq3(pb), nw)
    return y.reshape(t, bw)


def _band_buckets():
    i = np.arange(C_BLOCK)[:, None]
    j = np.arange(2 * C_BLOCK)[None, :]
    dist = i + C_BLOCK - j
    max_exact = N_BUCKETS // 2
    d_f = np.maximum(dist, 1).astype(np.float32)
    large = max_exact + (np.log(d_f / np.float32(max_exact)) / np.float32(math.log(MAX_DISTANCE / max_exact))
                         * np.float32(N_BUCKETS - max_exact)).astype(np.int32)
    large = np.minimum(large, N_BUCKETS - 1)
    bucket = np.where(dist < max_exact, dist, large)
    return np.where((dist >= 0) & (dist < WINDOW), bucket, -1).astype(np.int32)


def _swa_kernel(rb_ref, sink_ref, cur_ref, prev_ref, bucket_ref, o_ref, bias_ref):
    blk = C_BLOCK
    bw = BRANCH_WIDTH
    n = pl.program_id(1)

    @pl.when((pl.program_id(0) == 0) & (n == 0))
    def _():
        bucket = bucket_ref[...]
        for h in range(C_Q_HEADS):
            acc = jnp.full((blk, 2 * blk), NEG_BIG, F32)
            for bk in range(N_BUCKETS):
                acc = jnp.where(bucket == bk, rb_ref[bk, h], acc)
            bias_ref[h] = acc

    kv_cur = cur_ref[:, bw:bw + 2 * C_KV_WIDTH]
    kv_prev = prev_ref[...]
    kcat = jnp.concatenate([kv_prev[:, 0:C_KV_WIDTH], kv_cur[:, 0:C_KV_WIDTH]], axis=0)
    vcat = jnp.concatenate([kv_prev[:, C_KV_WIDTH:], kv_cur[:, C_KV_WIDTH:]], axis=0)
    lane = lax.broadcasted_iota(jnp.int32, (2 * blk, C_KV_WIDTH), 1)
    key_ok = lax.broadcasted_iota(jnp.int32, (1, 2 * blk), 1) >= jnp.where(n > 0, 0, blk)

    def halves(x, j):
        own = jnp.where((lane >= j * C_HEAD_DIM) & (lane < (j + 1) * C_HEAD_DIM), x, 0.0)
        other = pltpu.roll(own, C_HEAD_DIM, 1)
        pair = (own, other) if j == 0 else (other, own)
        return [t.astype(BF16) for t in pair]

    kz = [halves(kcat, j) for j in range(C_KV_HEADS)]
    vz = [halves(vcat, j) for j in range(C_KV_HEADS)]
    scale = C_HEAD_DIM ** -0.5
    group = C_Q_HEADS // C_KV_HEADS
    for p in range(C_Q_HEADS // 2):
        q2 = cur_ref[:, p * LANES:(p + 1) * LANES].astype(BF16)
        out = None
        for half in range(2):
            h = 2 * p + half
            j = h // group
            logits = _dot_nt(q2, kz[j][half]) * scale + bias_ref[h]
            logits = jnp.where(key_ok, logits, NEG_BIG)
            sink = sink_ref[h]
            mx = jnp.maximum(jnp.max(logits, axis=-1, keepdims=True), sink)
            pr = jnp.exp(logits - mx)
            den = jnp.sum(pr, axis=-1, keepdims=True) + jnp.exp(sink - mx)
            pv = _dot((pr / den).astype(BF16), vz[j][half])
            out = pv if out is None else out + pv
        gate = cur_ref[:, bw + 2 * C_KV_WIDTH + p * LANES:bw + 2 * C_KV_WIDTH + (p + 1) * LANES]
        o_ref[:, p * LANES:(p + 1) * LANES] = (out * _silu(gate)).astype(BF16)


def _swa(pc, rel_bias, sinks, batch, seq):
    blk = C_BLOCK
    nb = seq // blk
    bucket = jnp.asarray(_band_buckets())
    kv_col = BRANCH_WIDTH // (2 * C_KV_WIDTH)
    smem = pl.BlockSpec(memory_space=pltpu.SMEM)
    return pl.pallas_call(
        _swa_kernel,
        grid=(batch, nb),
        in_specs=[smem, smem,
                  pl.BlockSpec((blk, PC_WIDTH), lambda b, i: (b * nb + i, 0)),
                  pl.BlockSpec((blk, 2 * C_KV_WIDTH), lambda b, i: (b * nb + jnp.maximum(i - 1, 0), kv_col)),
                  pl.BlockSpec((blk, 2 * blk), lambda b, i: (0, 0))],
        out_specs=pl.BlockSpec((blk, BRANCH_WIDTH), lambda b, i: (b * nb + i, 0)),
        out_shape=jax.ShapeDtypeStruct((batch * seq, BRANCH_WIDTH), BF16),
        scratch_shapes=[pltpu.VMEM((C_Q_HEADS, blk, 2 * blk), F32)],
        compiler_params=pltpu.CompilerParams(
            dimension_semantics=("arbitrary", "arbitrary"), vmem_limit_bytes=VMEM_LIMIT),
        name="swa",
    )(rel_bias, sinks, pc, pc, bucket)


def _split_w_in(w):
    bw = BRANCH_WIDTH
    widths = (bw, bw, bw, bw, B_CONV_CH, bw, HEADS, HEADS, bw, C_KV_WIDTH, C_KV_WIDTH, bw, N_BRANCHES * D_MODEL)
    offs = np.concatenate([[0], np.cumsum(widths)])
    col = lambda i: w[:, offs[i]:offs[i + 1]]
    wa = w[:, offs[0]:offs[4]]
    wb = jnp.concatenate([col(4), col(5), col(6), col(7),
                          jnp.zeros((D_MODEL, SMALL_PAD - 2 * HEADS), w.dtype)], axis=1)
    wc = w[:, offs[8]:offs[12]]
    wg = col(12)
    return [t.astype(BF16) for t in (wa, wb, wc, wg)]


def _lane_pad(v, offset):
    return jnp.zeros((1, SMALL_PAD), F32).at[0, offset:offset + v.shape[0]].set(v.astype(F32))


def kernel(x, norm_w, w_in, conv_w, a_log, dt_bias, lb_param, norm_a, norm_b, sinks, rel_bias,
           w_branch, w_out, final_norm):
    batch, seq, _ = x.shape
    depth = w_in.shape[0]
    x2 = x.reshape(batch * seq, D_MODEL)
    fn = final_norm.reshape(1, D_MODEL)
    for l in range(depth):
        wa, wb, wc, wg = _split_w_in(w_in[l])
        nw = norm_w[l].reshape(1, D_MODEL)
        pa, pb, pc = _inproj(x2, nw, wa, wb, wc)
        ya = _hgrn2(pa, lb_param, norm_a[l].reshape(1, HEAD_DIM), batch, seq, l)
        yb = _deltanet(pb, conv_w[l], _lane_pad(a_log[l], HEADS), _lane_pad(dt_bias[l], HEADS),
                       norm_b[l].reshape(1, HEAD_DIM), batch, seq)
        yc = _swa(pc, rel_bias, sinks[l], batch, seq)
        x2 = _merge(x2, nw, wg, ya, yb, yc, w_branch[l].astype(BF16), w_out[l].astype(BF16), fn,
                    final=(l == depth - 1))
    return x2.reshape(batch, seq, D_MODEL)
```

```python
import functools
import math

import numpy as np
import jax
import jax.numpy as jnp
from jax import lax
from jax.experimental import pallas as pl
from jax.experimental.pallas import tpu as pltpu

F32 = jnp.float32
BF16 = jnp.bfloat16
HIGHEST = lax.Precision.HIGHEST

D_MODEL = 1024
BRANCH_WIDTH = D_MODEL // 2
N_BRANCHES = 3
EPS = 1e-6
HEADS = 4
HEAD_DIM = BRANCH_WIDTH // HEADS
B_CONV = 4
B_CONV_CH = 3 * BRANCH_WIDTH
C_Q_HEADS = 8
C_KV_HEADS = 2
C_HEAD_DIM = BRANCH_WIDTH // C_Q_HEADS
C_KV_WIDTH = C_KV_HEADS * C_HEAD_DIM
WINDOW = 128
C_BLOCK = 128
N_BUCKETS = 32
MAX_DISTANCE = 128
LANES = 128
SUBLANES = 8
NEG_BIG = -1e30
LOG2E = math.log2(math.e)

A_CHUNK = 128
B_CHUNK = 64
B_PAIR = 2 * B_CHUNK
B_TILE = 256
PROJ_ROWS = 256
SMALL_PAD = LANES
VMEM_LIMIT = 52 * 1024 * 1024

PA_WIDTH = 4 * BRANCH_WIDTH
PB_WIDTH = B_CONV_CH + BRANCH_WIDTH + SMALL_PAD
PC_WIDTH = 2 * BRANCH_WIDTH + 2 * C_KV_WIDTH


def _dot(a, b):
    return jnp.dot(a, b, preferred_element_type=F32)


def _dot_nt(a, b):
    return lax.dot_general(a, b, (((1,), (1,)), ((), ())), preferred_element_type=F32)


def _dot_tn(a, b):
    return lax.dot_general(a, b, (((0,), (0,)), ((), ())), preferred_element_type=F32)


def _sigmoid(x):
    return 1.0 / (1.0 + jnp.exp(-x))


def _silu(x):
    return x * _sigmoid(x)


def _softplus(x):
    return jnp.maximum(x, 0.0) + jnp.log(1.0 + jnp.exp(-jnp.abs(x)))


def _rms(x, w):
    return x * lax.rsqrt(jnp.mean(x * x, axis=-1, keepdims=True) + EPS) * w


def _inproj_kernel(x_ref, nw_ref, wa_ref, wb_ref, wc_ref, pa_ref, pb_ref, pc_ref):
    h = _rms(x_ref[...], nw_ref[...]).astype(BF16)
    for w_ref, o_ref in ((wa_ref, pa_ref), (wb_ref, pb_ref), (wc_ref, pc_ref)):
        n = w_ref.shape[1]
        for j in range(0, n, BRANCH_WIDTH):
            wd = min(BRANCH_WIDTH, n - j)
            o_ref[:, j:j + wd] = _dot(h, w_ref[:, j:j + wd])


def _inproj(x2, nw, wa, wb, wc):
    t = x2.shape[0]
    tm = PROJ_ROWS
    const = lambda i: (0, 0)
    return pl.pallas_call(
        _inproj_kernel,
        grid=(t // tm,),
        in_specs=[pl.BlockSpec((tm, D_MODEL), lambda i: (i, 0)),
                  pl.BlockSpec((1, D_MODEL), const),
                  pl.BlockSpec(wa.shape, const),
                  pl.BlockSpec(wb.shape, const),
                  pl.BlockSpec(wc.shape, const)],
        out_specs=[pl.BlockSpec((tm, PA_WIDTH), lambda i: (i, 0)),
                   pl.BlockSpec((tm, PB_WIDTH), lambda i: (i, 0)),
                   pl.BlockSpec((tm, PC_WIDTH), lambda i: (i, 0))],
        out_shape=[jax.ShapeDtypeStruct((t, PA_WIDTH), F32),
                   jax.ShapeDtypeStruct((t, PB_WIDTH), F32),
                   jax.ShapeDtypeStruct((t, PC_WIDTH), F32)],
        compiler_params=pltpu.CompilerParams(
            dimension_semantics=("arbitrary",), vmem_limit_bytes=VMEM_LIMIT),
        name="inproj",
    )(x2, nw, wa, wb, wc)


def _merge_kernel(x_ref, nw_ref, wg_ref, ya_ref, yb_ref, yc_ref, wbr_ref, wo_ref, fn_ref,
                  o_ref, *, final):
    x = x_ref[...]
    h = _rms(x, nw_ref[...]).astype(BF16)
    merged = None
    for n, y_ref in enumerate((ya_ref, yb_ref, yc_ref)):
        gate = _sigmoid(_dot(h, wg_ref[:, n * D_MODEL:(n + 1) * D_MODEL]))
        term = gate * _dot(y_ref[...], wbr_ref[n])
        merged = term if merged is None else merged + term
    out = x + _dot(merged.astype(BF16), wo_ref[...])
    if final:
        out = _rms(out, fn_ref[...])
    o_ref[...] = out


def _merge(x2, nw, wg, ya, yb, yc, wbr, wo, fn, final):
    t = x2.shape[0]
    tm = PROJ_ROWS
    const = lambda i: (0, 0)
    row = lambda i: (i, 0)
    return pl.pallas_call(
        functools.partial(_merge_kernel, final=final),
        grid=(t // tm,),
        in_specs=[pl.BlockSpec((tm, D_MODEL), row),
                  pl.BlockSpec((1, D_MODEL), const),
                  pl.BlockSpec(wg.shape, const),
                  pl.BlockSpec((tm, BRANCH_WIDTH), row),
                  pl.BlockSpec((tm, BRANCH_WIDTH), row),
                  pl.BlockSpec((tm, BRANCH_WIDTH), row),
                  pl.BlockSpec(wbr.shape, lambda i: (0, 0, 0)),
                  pl.BlockSpec(wo.shape, const),
                  pl.BlockSpec((1, D_MODEL), const)],
        out_specs=pl.BlockSpec((tm, D_MODEL), row),
        out_shape=jax.ShapeDtypeStruct((t, D_MODEL), F32),
        compiler_params=pltpu.CompilerParams(
            dimension_semantics=("arbitrary",), vmem_limit_bytes=VMEM_LIMIT),
        name="merge",
    )(x2, nw, wg, ya, yb, yc, wbr, wo, fn)


def _a_levels():
    return [1 << i for i in range(int(math.log2(A_CHUNK)))]


def _a_level_ids():
    idx = np.arange(A_CHUNK)
    ids = np.where(idx[:, None] == idx[None, :], 0, -1).astype(np.int32)
    for i, m in enumerate(_a_levels()):
        blk = idx // (2 * m)
        upper = (idx & m) != 0
        ids[(blk[:, None] == blk[None, :]) & upper[:, None] & ~upper[None, :]] = i + 1
    return ids


def _hgrn2_kernel(pa_ref, lbp_ref, nw_ref, tril_ref, lvl_ref, o_ref,
                  state_ref, b_ref, gp_ref, *, layer):
    c = A_CHUNK
    bw = BRANCH_WIDTH

    @pl.when(pl.program_id(1) == 0)
    def _():
        state_ref[...] = jnp.zeros_like(state_ref)
        gp_ref[...] = jnp.zeros_like(gp_ref)

    lbp = lbp_ref[...]
    ex = jnp.exp(lbp - jnp.max(lbp, axis=0, keepdims=True))
    sm = ex / jnp.sum(ex, axis=0, keepdims=True)
    lb = jnp.zeros((1, bw), F32)
    for j in range(1, layer + 1):
        lb = lb + sm[j:j + 1, :]

    q = _silu(pa_ref[:, 0:bw])
    z = pa_ref[:, bw:2 * bw]
    v = pa_ref[:, 2 * bw:3 * bw].astype(BF16)
    e = jnp.exp(-jnp.abs(z))
    one_e = 1.0 + e
    r = 1.0 / one_e
    log_sig = jnp.minimum(z, 0.0) - jnp.log(one_e)
    sig_neg = jnp.where(z >= 0.0, e * r, r)
    t1 = jnp.log(lb)
    t2 = jnp.log1p(-lb) + log_sig
    lf = jnp.maximum(t1, t2) + jnp.log(1.0 + jnp.exp(-jnp.abs(t1 - t2)))
    k = (1.0 - lb) * sig_neg

    lf2 = lf * LOG2E
    b = jnp.dot(tril_ref[...], lf2, precision=HIGHEST, preferred_element_type=F32)
    b_ref[...] = b
    gp_ref[SUBLANES:SUBLANES + c, :] = lf2

    hsl = [slice(h * HEAD_DIM, (h + 1) * HEAD_DIM) for h in range(HEADS)]
    row = lax.broadcasted_iota(jnp.int32, (c, 1), 0)
    q16 = q.astype(BF16)
    k16 = k.astype(BF16)
    lvl = lvl_ref[...]
    diag = lvl == 0
    acc = [jnp.where(diag, _dot_nt(q16[:, sl], k16[:, sl]), 0.0) for sl in hsl]

    for li, m in enumerate(_a_levels()):
        blocks = range(0, c, 2 * m)
        if m == 1:
            neg = jnp.where((row & 1) == 1, lf2, 0.0)
        elif m == 2:
            r4 = row & 3
            g_next = gp_ref[SUBLANES + 1:SUBLANES + 1 + c, :]
            g_prev = gp_ref[SUBLANES - 1:SUBLANES - 1 + c, :]
            neg = jnp.where(r4 == 0, g_next, jnp.where(r4 == 1, 0.0, jnp.where(r4 == 2, lf2, lf2 + g_prev)))
        elif m < SUBLANES:
            neg = jnp.concatenate(
                [-jnp.abs(b_ref[lo:lo + 2 * m, :] - b_ref[lo + m - 1:lo + m, :]) for lo in blocks], axis=0)
        else:
            pieces = []
            for lo in blocks:
                anchor = b_ref[lo + m - 1:lo + m, :]
                pieces += [anchor - b_ref[lo:lo + m, :], b_ref[lo + m:lo + 2 * m, :] - anchor]
            neg = jnp.concatenate(pieces, axis=0)
        if m < SUBLANES:
            x = jnp.where((row & m) != 0, q, k)
        else:
            x = jnp.concatenate([t for lo in blocks for t in (k[lo:lo + m, :], q[lo + m:lo + 2 * m, :])], axis=0)
        xt = (x * jnp.exp2(neg)).astype(BF16)
        sel = lvl == li + 1
        for h, sl in enumerate(hsl):
            acc[h] = jnp.where(sel, _dot_nt(xt[:, sl], xt[:, sl]), acc[h])

    b_end = b_ref[c - 1:c, :]
    qd = (q * jnp.exp2(b)).astype(BF16)
    kd = (k * jnp.exp2(b_end - b)).astype(BF16)
    s_decay = jnp.exp2(b_end)
    nw = nw_ref[...]
    sts = [state_ref[h] for h in range(HEADS)]
    intra = [_dot(acc[h].astype(BF16), v[:, hsl[h]]) for h in range(HEADS)]
    inter = [_dot_nt(qd[:, hsl[h]], sts[h].astype(BF16)) for h in range(HEADS)]
    upd = [_dot_tn(v[:, hsl[h]], kd[:, hsl[h]]) for h in range(HEADS)]
    for h in range(HEADS):
        sl = hsl[h]
        state_ref[h] = sts[h] * s_decay[:, sl] + upd[h]
        o = intra[h] + inter[h]
        y = o * lax.rsqrt(jnp.mean(o * o, axis=-1, keepdims=True) + EPS) * nw
        o_ref[:, sl] = (y * _silu(pa_ref[:, 3 * bw + h * HEAD_DIM:3 * bw + (h + 1) * HEAD_DIM])).astype(BF16)


def _hgrn2(pa, lbp, nw, batch, seq, layer):
    c = A_CHUNK
    nc = seq // c
    tril = jnp.asarray(np.tril(np.ones((c, c), np.float32)))
    lvl = jnp.asarray(_a_level_ids())
    const2 = lambda b, i: (0, 0)
    return pl.pallas_call(
        functools.partial(_hgrn2_kernel, layer=layer),
        grid=(batch, nc),
        in_specs=[pl.BlockSpec((c, PA_WIDTH), lambda b, i: (b * nc + i, 0)),
                  pl.BlockSpec(lbp.shape, const2),
                  pl.BlockSpec((1, HEAD_DIM), const2),
                  pl.BlockSpec((c, c), const2),
                  pl.BlockSpec((c, c), const2)],
        out_specs=pl.BlockSpec((c, BRANCH_WIDTH), lambda b, i: (b * nc + i, 0)),
        out_shape=jax.ShapeDtypeStruct((batch * seq, BRANCH_WIDTH), BF16),
        scratch_shapes=[pltpu.VMEM((HEADS, HEAD_DIM, HEAD_DIM), F32),
                        pltpu.VMEM((c, BRANCH_WIDTH), F32),
                        pltpu.VMEM((c + 2 * SUBLANES, BRANCH_WIDTH), F32)],
        compiler_params=pltpu.CompilerParams(
            dimension_semantics=("arbitrary", "arbitrary"), vmem_limit_bytes=VMEM_LIMIT),
        name="hgrn2",
    )(pa, lbp, nw, tril, lvl)


def _unit_lower_inverses(mats, ti, si):
    eye = (ti == si).astype(F32)
    ts = [eye] * len(mats)
    s = 1
    while s < B_CHUNK:
        shift = int(math.log2(2 * s))
        off = ((ti >> shift) == (si >> shift)) & ((ti & s) != 0) & ((si & s) == 0)
        a_off = [jnp.where(off, a, 0.0) for a in mats]
        if s == 1:
            ts = [t - ao for t, ao in zip(ts, a_off)]
        else:
            t16 = [t.astype(BF16) for t in ts]
            inner = [_dot(ao.astype(BF16), t) for ao, t in zip(a_off, t16)]
            outer = [_dot(t, x.astype(BF16)) for t, x in zip(t16, inner)]
            ts = [t - x for t, x in zip(ts, outer)]
        s *= 2
    return ts


def _dn_prep_kernel(pb_ref, halo_ref, cw_ref, alog_ref, dtb_ref, tril_ref,
                    u_ref, w_ref, qd_ref, kd_ref, qk_ref, ge_ref, xp_ref, *, tiles_per_seq):
    ts = B_TILE
    c = B_CHUNK
    pr = B_PAIR
    bw = BRANCH_WIDTH
    sl8 = SUBLANES

    keep = jnp.where(pl.program_id(0) % tiles_per_seq == 0, 0.0, 1.0)
    xp_ref[0:sl8, :] = halo_ref[...] * keep
    xp_ref[sl8:sl8 + ts, :] = pb_ref[:, 0:B_CONV_CH]
    conv = None
    for j in range(B_CONV):
        off = sl8 - (B_CONV - 1) + j
        term = xp_ref[off:off + ts, :] * cw_ref[j:j + 1, :]
        conv = term if conv is None else conv + term
    qkv = _silu(conv)

    small = pb_ref[:, B_CONV_CH + bw:B_CONV_CH + bw + SMALL_PAD]
    beta_all = _sigmoid(small)
    g_all = -jnp.exp(alog_ref[...]) * _softplus(small + dtb_ref[...])
    gcum = jnp.dot(tril_ref[...], g_all, precision=HIGHEST, preferred_element_type=F32)
    gcum_t = gcum.T
    for j in range(ts // c):
        ge_ref[0, j:j + 1, :] = jnp.exp(gcum[j * c + c - 1:j * c + c, :])

    ti = lax.broadcasted_iota(jnp.int32, (pr, pr), 0)
    si = lax.broadcasted_iota(jnp.int32, (pr, pr), 1)
    same = (ti >> int(math.log2(c))) == (si >> int(math.log2(c)))
    incl = same & (ti >= si)
    strict = same & (ti > si)
    first_chunk = lax.broadcasted_iota(jnp.int32, (pr, 1), 0) < c

    probs = [(h, p) for h in range(HEADS) for p in range(ts // pr)]
    k16s, kb16s, q16s, decays, rhs = [], [], [], [], []
    for h, p in probs:
        sl = slice(h * HEAD_DIM, (h + 1) * HEAD_DIM)
        rows = slice(p * pr, (p + 1) * pr)
        q = qkv[rows, sl]
        k = qkv[rows, bw + h * HEAD_DIM:bw + (h + 1) * HEAD_DIM]
        v = qkv[rows, 2 * bw + h * HEAD_DIM:2 * bw + (h + 1) * HEAD_DIM]
        q = q * lax.rsqrt(jnp.sum(q * q, axis=-1, keepdims=True) + EPS) * (HEAD_DIM ** -0.5)
        k = k * lax.rsqrt(jnp.sum(k * k, axis=-1, keepdims=True) + EPS)
        beta = beta_all[rows, h:h + 1]
        gc = gcum[rows, HEADS + h:HEADS + h + 1]
        gr = gcum_t[HEADS + h:HEADS + h + 1, rows]
        g_last = jnp.where(first_chunk, gc[c - 1:c, :], gc[pr - 1:pr, :])
        egc = jnp.exp(gc)
        kb = k * beta
        decays.append(jnp.where(incl, jnp.exp(jnp.minimum(gc - gr, 0.0)), 0.0))
        k16s.append(k.astype(BF16))
        kb16s.append(kb.astype(BF16))
        q16s.append(q.astype(BF16))
        rhs.append(jnp.concatenate([v * beta, kb * egc], axis=1).astype(BF16))
        qd_ref[rows, sl] = (q * egc).astype(BF16)
        kd_ref[rows, sl] = (k * jnp.exp(g_last - gc)).astype(BF16)

    kk = [_dot_nt(kb16, k16) for kb16, k16 in zip(kb16s, k16s)]
    qk = [_dot_nt(q16, k16) for q16, k16 in zip(q16s, k16s)]
    mats = [jnp.where(strict, x * d, 0.0) for x, d in zip(kk, decays)]
    tinv = _unit_lower_inverses(mats, ti, si)
    uw = [_dot(t.astype(BF16), r) for t, r in zip(tinv, rhs)]
    for i, (h, p) in enumerate(probs):
        sl = slice(h * HEAD_DIM, (h + 1) * HEAD_DIM)
        rows = slice(p * pr, (p + 1) * pr)
        u_ref[rows, sl] = uw[i][:, 0:HEAD_DIM]
        w_ref[rows, sl] = uw[i][:, HEAD_DIM:].astype(BF16)
        qk_ref[p, h] = (qk[i] * decays[i]).astype(BF16)


def _dn_scan_kernel(u_ref, w_ref, qd_ref, kd_ref, qk_ref, ge_ref, z_ref, nw_ref, o_ref, state_ref):
    c = B_CHUNK
    batch = u_ref.shape[0]

    @pl.when(pl.program_id(0) == 0)
    def _():
        state_ref[...] = jnp.zeros_like(state_ref)

    nw = nw_ref[...]
    chains = [(b, h) for b in range(batch) for h in range(HEADS)]
    hsl = lambda h: slice(h * HEAD_DIM, (h + 1) * HEAD_DIM)
    states = [state_ref[b * HEADS + h] for b, h in chains]
    o_inter = [[] for _ in chains]
    v_new = [[] for _ in chains]
    for j in range(B_PAIR // c):
        rows = slice(j * c, (j + 1) * c)
        st16 = [st.astype(BF16) for st in states]
        lhs = [jnp.concatenate([w_ref[b, rows, hsl(h)], qd_ref[b, rows, hsl(h)]], axis=0) for b, h in chains]
        prod = [_dot(x, s) for x, s in zip(lhs, st16)]
        vn16 = []
        for i, (b, h) in enumerate(chains):
            vn = (u_ref[b, rows, hsl(h)] - prod[i][0:c, :]).astype(BF16)
            vn16.append(vn)
            v_new[i].append(vn)
            o_inter[i].append(prod[i][c:, :])
        upd = [_dot_tn(kd_ref[b, rows, hsl(h)], vn) for (b, h), vn in zip(chains, vn16)]
        states = [st * ge_ref[b, 0, j:j + 1, HEADS + h:HEADS + h + 1] + x
                  for (b, h), st, x in zip(chains, states, upd)]
    intra = [_dot(qk_ref[b, 0, h], jnp.concatenate(v_new[i], axis=0)) for i, (b, h) in enumerate(chains)]
    for i, (b, h) in enumerate(chains):
        state_ref[b * HEADS + h] = states[i]
        o = jnp.concatenate(o_inter[i], axis=0) + intra[i]
        y = o * lax.rsqrt(jnp.mean(o * o, axis=-1, keepdims=True) + EPS) * nw
        o_ref[b, :, hsl(h)] = (y * _silu(z_ref[b, :, hsl(h)])).astype(BF16)


def _deltanet(pb, cw, alog, dtb, nw, batch, seq):
    c = B_CHUNK
    ts = B_TILE
    pr = B_PAIR
    bw = BRANCH_WIDTH
    t = batch * seq
    n_tiles = t // ts
    tril = jnp.asarray(np.kron(np.eye(ts // c, dtype=np.float32), np.tril(np.ones((c, c), np.float32))))
    const = lambda i: (0, 0)
    row = lambda i: (i, 0)
    halo_blocks = ts // SUBLANES
    u, w, qd, kd, qk, ge = pl.pallas_call(
        functools.partial(_dn_prep_kernel, tiles_per_seq=seq // ts),
        grid=(n_tiles,),
        in_specs=[pl.BlockSpec((ts, PB_WIDTH), row),
                  pl.BlockSpec((SUBLANES, B_CONV_CH), lambda i: (jnp.maximum(i * halo_blocks - 1, 0), 0)),
                  pl.BlockSpec(cw.shape, const),
                  pl.BlockSpec((1, SMALL_PAD), const),
                  pl.BlockSpec((1, SMALL_PAD), const),
                  pl.BlockSpec((ts, ts), const)],
        out_specs=[pl.BlockSpec((ts, bw), row),
                   pl.BlockSpec((ts, bw), row),
                   pl.BlockSpec((ts, bw), row),
                   pl.BlockSpec((ts, bw), row),
                   pl.BlockSpec((ts // pr, HEADS, pr, pr), lambda i: (i, 0, 0, 0)),
                   pl.BlockSpec((1, ts // c, SMALL_PAD), lambda i: (i, 0, 0))],
        out_shape=[jax.ShapeDtypeStruct((t, bw), F32),
                   jax.ShapeDtypeStruct((t, bw), BF16),
                   jax.ShapeDtypeStruct((t, bw), BF16),
                   jax.ShapeDtypeStruct((t, bw), BF16),
                   jax.ShapeDtypeStruct((t // pr, HEADS, pr, pr), BF16),
                   jax.ShapeDtypeStruct((n_tiles, ts // c, SMALL_PAD), F32)],
        scratch_shapes=[pltpu.VMEM((ts + SUBLANES, B_CONV_CH), F32)],
        compiler_params=pltpu.CompilerParams(
            dimension_semantics=("arbitrary",), vmem_limit_bytes=VMEM_LIMIT),
        name="dn_prep",
    )(pb, pb, cw, alog, dtb, tril)

    np_seq = seq // pr
    seq3 = lambda x: x.reshape(batch, seq, x.shape[-1])
    blk3 = pl.BlockSpec((batch, pr, bw), lambda i: (0, i, 0))
    y = pl.pallas_call(
        _dn_scan_kernel,
        grid=(np_seq,),
        in_specs=[blk3, blk3, blk3, blk3,
                  pl.BlockSpec((batch, 1, HEADS, pr, pr), lambda i: (0, i, 0, 0, 0)),
                  pl.BlockSpec((batch, 1, pr // c, SMALL_PAD), lambda i: (0, i, 0, 0)),
                  pl.BlockSpec((batch, pr, bw), lambda i: (0, i, B_CONV_CH // bw)),
                  pl.BlockSpec((1, HEAD_DIM), const)],
        out_specs=blk3,
        out_shape=jax.ShapeDtypeStruct((batch, seq, bw), BF16),
        scratch_shapes=[pltpu.VMEM((batch * HEADS, HEAD_DIM, HEAD_DIM), F32)],
        compiler_params=pltpu.CompilerParams(
            dimension_semantics=("arbitrary",), vmem_limit_bytes=VMEM_LIMIT),
        name="dn_scan",
    )(seq3(u), seq3(w), seq3(qd), seq3(kd),
      qk.reshape(batch, np_seq, HEADS, pr, pr),
      ge.reshape(batch, np_seq, pr // c, SMALL_PAD),
      seq3(pb), nw)
    return y.reshape(t, bw)


def _band_buckets():
    i = np.arange(C_BLOCK)[:, None]
    j = np.arange(2 * C_BLOCK)[None, :]
    dist = i + C_BLOCK - j
    max_exact = N_BUCKETS // 2
    d_f = np.maximum(dist, 1).astype(np.float32)
    large = max_exact + (np.log(d_f / np.float32(max_exact)) / np.float32(math.log(MAX_DISTANCE / max_exact))
                         * np.float32(N_BUCKETS - max_exact)).astype(np.int32)
    large = np.minimum(large, N_BUCKETS - 1)
    bucket = np.where(dist < max_exact, dist, large)
    return np.where((dist >= 0) & (dist < WINDOW), bucket, -1).astype(np.int32)


def _swa_kernel(rb_ref, sink_ref, cur_ref, prev_ref, bucket_ref, o_ref, bias_ref):
    blk = C_BLOCK
    bw = BRANCH_WIDTH
    n = pl.program_id(1)

    @pl.when((pl.program_id(0) == 0) & (n == 0))
    def _():
        bucket = bucket_ref[...]
        for h in range(C_Q_HEADS):
            acc = jnp.full((blk, 2 * blk), NEG_BIG, F32)
            for bk in range(N_BUCKETS):
                acc = jnp.where(bucket == bk, rb_ref[bk, h], acc)
            bias_ref[h] = acc

    kv_cur = cur_ref[:, bw:bw + 2 * C_KV_WIDTH]
    kv_prev = prev_ref[...]
    kcat = jnp.concatenate([kv_prev[:, 0:C_KV_WIDTH], kv_cur[:, 0:C_KV_WIDTH]], axis=0)
    vcat = jnp.concatenate([kv_prev[:, C_KV_WIDTH:], kv_cur[:, C_KV_WIDTH:]], axis=0)
    lane = lax.broadcasted_iota(jnp.int32, (2 * blk, C_KV_WIDTH), 1)
    key_ok = lax.broadcasted_iota(jnp.int32, (1, 2 * blk), 1) >= jnp.where(n > 0, 0, blk)

    def halves(x, j):
        own = jnp.where((lane >= j * C_HEAD_DIM) & (lane < (j + 1) * C_HEAD_DIM), x, 0.0)
        other = pltpu.roll(own, C_HEAD_DIM, 1)
        pair = (own, other) if j == 0 else (other, own)
        return [t.astype(BF16) for t in pair]

    kz = [halves(kcat, j) for j in range(C_KV_HEADS)]
    vz = [halves(vcat, j) for j in range(C_KV_HEADS)]
    scale = C_HEAD_DIM ** -0.5
    group = C_Q_HEADS // C_KV_HEADS
    heads = range(C_Q_HEADS)
    q2 = [(cur_ref[:, p * LANES:(p + 1) * LANES] * scale).astype(BF16)
          for p in range(C_Q_HEADS // 2)]
    logits = [_dot_nt(q2[h // 2], kz[h // group][h % 2]) for h in heads]
    logits = [jnp.where(key_ok, lg + bias_ref[h], NEG_BIG) for h, lg in zip(heads, logits)]
    mx = [jnp.maximum(jnp.max(lg, axis=-1, keepdims=True), sink_ref[h]) for h, lg in zip(heads, logits)]
    pr = [jnp.exp(lg - m) for lg, m in zip(logits, mx)]
    den = [jnp.sum(x, axis=-1, keepdims=True) + jnp.exp(sink_ref[h] - m) for h, x, m in zip(heads, pr, mx)]
    pv = [_dot(x.astype(BF16), vz[h // group][h % 2]) for h, x in zip(heads, pr)]
    for p in range(C_Q_HEADS // 2):
        out = pv[2 * p] / den[2 * p] + pv[2 * p + 1] / den[2 * p + 1]
        gate = cur_ref[:, bw + 2 * C_KV_WIDTH + p * LANES:bw + 2 * C_KV_WIDTH + (p + 1) * LANES]
        o_ref[:, p * LANES:(p + 1) * LANES] = (out * _silu(gate)).astype(BF16)


def _swa(pc, rel_bias, sinks, batch, seq):
    blk = C_BLOCK
    nb = seq // blk
    bucket = jnp.asarray(_band_buckets())
    kv_col = BRANCH_WIDTH // (2 * C_KV_WIDTH)
    smem = pl.BlockSpec(memory_space=pltpu.SMEM)
    return pl.pallas_call(
        _swa_kernel,
        grid=(batch, nb),
        in_specs=[smem, smem,
                  pl.BlockSpec((blk, PC_WIDTH), lambda b, i: (b * nb + i, 0)),
                  pl.BlockSpec((blk, 2 * C_KV_WIDTH), lambda b, i: (b * nb + jnp.maximum(i - 1, 0), kv_col)),
                  pl.BlockSpec((blk, 2 * blk), lambda b, i: (0, 0))],
        out_specs=pl.BlockSpec((blk, BRANCH_WIDTH), lambda b, i: (b * nb + i, 0)),
        out_shape=jax.ShapeDtypeStruct((batch * seq, BRANCH_WIDTH), BF16),
        scratch_shapes=[pltpu.VMEM((C_Q_HEADS, blk, 2 * blk), F32)],
        compiler_params=pltpu.CompilerParams(
            dimension_semantics=("arbitrary", "arbitrary"), vmem_limit_bytes=VMEM_LIMIT),
        name="swa",
    )(rel_bias, sinks, pc, pc, bucket)


def _split_w_in(w):
    bw = BRANCH_WIDTH
    widths = (bw, bw, bw, bw, B_CONV_CH, bw, HEADS, HEADS, bw, C_KV_WIDTH, C_KV_WIDTH, bw, N_BRANCHES * D_MODEL)
    offs = np.concatenate([[0], np.cumsum(widths)])
    col = lambda i: w[:, offs[i]:offs[i + 1]]
    wa = w[:, offs[0]:offs[4]]
    wb = jnp.concatenate([col(4), col(5), col(6), col(7),
                          jnp.zeros((D_MODEL, SMALL_PAD - 2 * HEADS), w.dtype)], axis=1)
    wc = w[:, offs[8]:offs[12]]
    wg = col(12)
    return [t.astype(BF16) for t in (wa, wb, wc, wg)]


def _lane_pad(v, offset):
    return jnp.zeros((1, SMALL_PAD), F32).at[0, offset:offset + v.shape[0]].set(v.astype(F32))


def kernel(x, norm_w, w_in, conv_w, a_log, dt_bias, lb_param, norm_a, norm_b, sinks, rel_bias,
           w_branch, w_out, final_norm):
    batch, seq, _ = x.shape
    depth = w_in.shape[0]
    x2 = x.reshape(batch * seq, D_MODEL)
    fn = final_norm.reshape(1, D_MODEL)
    for l in range(depth):
        wa, wb, wc, wg = _split_w_in(w_in[l])
        nw = norm_w[l].reshape(1, D_MODEL)
        pa, pb, pc = _inproj(x2, nw, wa, wb, wc)
        ya = _hgrn2(pa, lb_param, norm_a[l].reshape(1, HEAD_DIM), batch, seq, l)
        yb = _deltanet(pb, conv_w[l], _lane_pad(a_log[l], HEADS), _lane_pad(dt_bias[l], HEADS),
                       norm_b[l].reshape(1, HEAD_DIM), batch, seq)
        yc = _swa(pc, rel_bias, sinks[l], batch, seq)
        x2 = _merge(x2, nw, wg, ya, yb, yc, w_branch[l].astype(BF16), w_out[l].astype(BF16), fn,
                    final=(l == depth - 1))
    return x2.reshape(batch, seq, D_MODEL)
```

```python
import functools
import math

import numpy as np
import jax
import jax.numpy as jnp
from jax import lax
from jax.experimental import pallas as pl
from jax.experimental.pallas import tpu as pltpu

F32 = jnp.float32
BF16 = jnp.bfloat16
HIGHEST = lax.Precision.HIGHEST

D_MODEL = 1024
BRANCH_WIDTH = D_MODEL // 2
N_BRANCHES = 3
EPS = 1e-6
HEADS = 4
HEAD_DIM = BRANCH_WIDTH // HEADS
B_CONV = 4
B_CONV_CH = 3 * BRANCH_WIDTH
C_Q_HEADS = 8
C_KV_HEADS = 2
C_HEAD_DIM = BRANCH_WIDTH // C_Q_HEADS
C_KV_WIDTH = C_KV_HEADS * C_HEAD_DIM
WINDOW = 128
C_BLOCK = 128
C_TILE_BLOCKS = 4
N_BUCKETS = 32
MAX_DISTANCE = 128
LANES = 128
SUBLANES = 8
NEG_BIG = -1e30
LOG2E = math.log2(math.e)

A_CHUNK = 128
A_TILE_CHUNKS = 2
B_CHUNK = 64
B_PAIR = 2 * B_CHUNK
B_TILE = 512
PROJ_ROWS = 256
SMALL_PAD = LANES
VMEM_LIMIT = 52 * 1024 * 1024

PA_WIDTH = 4 * BRANCH_WIDTH
PB_WIDTH = B_CONV_CH + BRANCH_WIDTH + SMALL_PAD
PC_WIDTH = 2 * BRANCH_WIDTH + 2 * C_KV_WIDTH


def _dot(a, b):
    return jnp.dot(a, b, preferred_element_type=F32)


def _dot_nt(a, b):
    return lax.dot_general(a, b, (((1,), (1,)), ((), ())), preferred_element_type=F32)


def _dot_tn(a, b):
    return lax.dot_general(a, b, (((0,), (0,)), ((), ())), preferred_element_type=F32)


def _sigmoid(x):
    return 1.0 / (1.0 + jnp.exp(-x))


def _silu(x):
    return x * _sigmoid(x)


def _softplus(x):
    return jnp.maximum(x, 0.0) + jnp.log(1.0 + jnp.exp(-jnp.abs(x)))


def _rms(x, w):
    return x * lax.rsqrt(jnp.mean(x * x, axis=-1, keepdims=True) + EPS) * w


def _inproj_kernel(x_ref, nw_ref, wa_ref, wb_ref, wc_ref, pa_ref, pb_ref, pc_ref):
    h = _rms(x_ref[...], nw_ref[...]).astype(BF16)
    for w_ref, o_ref in ((wa_ref, pa_ref), (wb_ref, pb_ref), (wc_ref, pc_ref)):
        n = w_ref.shape[1]
        for j in range(0, n, BRANCH_WIDTH):
            wd = min(BRANCH_WIDTH, n - j)
            o_ref[:, j:j + wd] = _dot(h, w_ref[:, j:j + wd])


def _inproj(x2, nw, wa, wb, wc):
    t = x2.shape[0]
    tm = PROJ_ROWS
    const = lambda i: (0, 0)
    return pl.pallas_call(
        _inproj_kernel,
        grid=(t // tm,),
        in_specs=[pl.BlockSpec((tm, D_MODEL), lambda i: (i, 0)),
                  pl.BlockSpec((1, D_MODEL), const),
                  pl.BlockSpec(wa.shape, const),
                  pl.BlockSpec(wb.shape, const),
                  pl.BlockSpec(wc.shape, const)],
        out_specs=[pl.BlockSpec((tm, PA_WIDTH), lambda i: (i, 0)),
                   pl.BlockSpec((tm, PB_WIDTH), lambda i: (i, 0)),
                   pl.BlockSpec((tm, PC_WIDTH), lambda i: (i, 0))],
        out_shape=[jax.ShapeDtypeStruct((t, PA_WIDTH), F32),
                   jax.ShapeDtypeStruct((t, PB_WIDTH), F32),
                   jax.ShapeDtypeStruct((t, PC_WIDTH), F32)],
        compiler_params=pltpu.CompilerParams(
            dimension_semantics=("arbitrary",), vmem_limit_bytes=VMEM_LIMIT),
        name="inproj",
    )(x2, nw, wa, wb, wc)


def _merge_kernel(x_ref, nw_ref, wg_ref, ya_ref, yb_ref, yc_ref, wbr_ref, wo_ref, fn_ref,
                  o_ref, *, final):
    x = x_ref[...]
    h = _rms(x, nw_ref[...]).astype(BF16)
    merged = None
    for n, y_ref in enumerate((ya_ref, yb_ref, yc_ref)):
        gate = _sigmoid(_dot(h, wg_ref[:, n * D_MODEL:(n + 1) * D_MODEL]))
        term = gate * _dot(y_ref[...], wbr_ref[n])
        merged = term if merged is None else merged + term
    out = x + _dot(merged.astype(BF16), wo_ref[...])
    if final:
        out = _rms(out, fn_ref[...])
    o_ref[...] = out


def _merge(x2, nw, wg, ya, yb, yc, wbr, wo, fn, final):
    t = x2.shape[0]
    tm = PROJ_ROWS
    const = lambda i: (0, 0)
    row = lambda i: (i, 0)
    return pl.pallas_call(
        functools.partial(_merge_kernel, final=final),
        grid=(t // tm,),
        in_specs=[pl.BlockSpec((tm, D_MODEL), row),
                  pl.BlockSpec((1, D_MODEL), const),
                  pl.BlockSpec(wg.shape, const),
                  pl.BlockSpec((tm, BRANCH_WIDTH), row),
                  pl.BlockSpec((tm, BRANCH_WIDTH), row),
                  pl.BlockSpec((tm, BRANCH_WIDTH), row),
                  pl.BlockSpec(wbr.shape, lambda i: (0, 0, 0)),
                  pl.BlockSpec(wo.shape, const),
                  pl.BlockSpec((1, D_MODEL), const)],
        out_specs=pl.BlockSpec((tm, D_MODEL), row),
        out_shape=jax.ShapeDtypeStruct((t, D_MODEL), F32),
        compiler_params=pltpu.CompilerParams(
            dimension_semantics=("arbitrary",), vmem_limit_bytes=VMEM_LIMIT),
        name="merge",
    )(x2, nw, wg, ya, yb, yc, wbr, wo, fn)


def _a_levels():
    return [1 << i for i in range(int(math.log2(A_CHUNK)))]


def _a_level_ids():
    idx = np.arange(A_CHUNK)
    ids = np.where(idx[:, None] == idx[None, :], 0, -1).astype(np.int32)
    for i, m in enumerate(_a_levels()):
        blk = idx // (2 * m)
        upper = (idx & m) != 0
        ids[(blk[:, None] == blk[None, :]) & upper[:, None] & ~upper[None, :]] = i + 1
    return ids


def _hgrn2_kernel(pa_ref, lbp_ref, nw_ref, tril_ref, lvl_ref, o_ref,
                  state_ref, b_ref, gp_ref, *, layer):
    c = A_CHUNK
    bw = BRANCH_WIDTH

    @pl.when(pl.program_id(1) == 0)
    def _():
        state_ref[...] = jnp.zeros_like(state_ref)
        gp_ref[...] = jnp.zeros_like(gp_ref)

    lbp = lbp_ref[...]
    ex = jnp.exp(lbp - jnp.max(lbp, axis=0, keepdims=True))
    sm = ex / jnp.sum(ex, axis=0, keepdims=True)
    lb = jnp.zeros((1, bw), F32)
    for j in range(1, layer + 1):
        lb = lb + sm[j:j + 1, :]

    hsl = [slice(h * HEAD_DIM, (h + 1) * HEAD_DIM) for h in range(HEADS)]
    row = lax.broadcasted_iota(jnp.int32, (c, 1), 0)
    lvl = lvl_ref[...]
    t1 = jnp.log(lb)
    t2_lb = jnp.log1p(-lb)

    def intra_chunk(ci):
        rows = slice(ci * c, (ci + 1) * c)
        bc_ref = b_ref.at[ci]
        gc_ref = gp_ref.at[ci]
        q = _silu(pa_ref[rows, 0:bw])
        z = pa_ref[rows, bw:2 * bw]
        v = pa_ref[rows, 2 * bw:3 * bw].astype(BF16)
        e = jnp.exp(-jnp.abs(z))
        one_e = 1.0 + e
        r = 1.0 / one_e
        log_sig = jnp.minimum(z, 0.0) - jnp.log(one_e)
        sig_neg = jnp.where(z >= 0.0, e * r, r)
        t2 = t2_lb + log_sig
        lf = jnp.maximum(t1, t2) + jnp.log(1.0 + jnp.exp(-jnp.abs(t1 - t2)))
        k = (1.0 - lb) * sig_neg

        lf2 = lf * LOG2E
        b = jnp.dot(tril_ref[...], lf2, precision=HIGHEST, preferred_element_type=F32)
        bc_ref[...] = b
        gc_ref[SUBLANES:SUBLANES + c, :] = lf2

        q16 = q.astype(BF16)
        k16 = k.astype(BF16)
        diag = lvl == 0
        acc = [jnp.where(diag, _dot_nt(q16[:, sl], k16[:, sl]), 0.0) for sl in hsl]

        for li, m in enumerate(_a_levels()):
            blocks = range(0, c, 2 * m)
            if m == 1:
                neg = jnp.where((row & 1) == 1, lf2, 0.0)
            elif m == 2:
                r4 = row & 3
                g_next = gc_ref[SUBLANES + 1:SUBLANES + 1 + c, :]
                g_prev = gc_ref[SUBLANES - 1:SUBLANES - 1 + c, :]
                neg = jnp.where(r4 == 0, g_next, jnp.where(r4 == 1, 0.0, jnp.where(r4 == 2, lf2, lf2 + g_prev)))
            elif m < SUBLANES:
                neg = jnp.concatenate(
                    [-jnp.abs(bc_ref[lo:lo + 2 * m, :] - bc_ref[lo + m - 1:lo + m, :]) for lo in blocks], axis=0)
            else:
                pieces = []
                for lo in blocks:
                    anchor = bc_ref[lo + m - 1:lo + m, :]
                    pieces += [anchor - bc_ref[lo:lo + m, :], bc_ref[lo + m:lo + 2 * m, :] - anchor]
                neg = jnp.concatenate(pieces, axis=0)
            if m < SUBLANES:
                x = jnp.where((row & m) != 0, q, k)
            else:
                x = jnp.concatenate(
                    [t for lo in blocks for t in (k[lo:lo + m, :], q[lo + m:lo + 2 * m, :])], axis=0)
            xt = (x * jnp.exp2(neg)).astype(BF16)
            sel = lvl == li + 1
            for h, sl in enumerate(hsl):
                acc[h] = jnp.where(sel, _dot_nt(xt[:, sl], xt[:, sl]), acc[h])

        b_end = bc_ref[c - 1:c, :]
        qd = (q * jnp.exp2(b)).astype(BF16)
        kd = (k * jnp.exp2(b_end - b)).astype(BF16)
        intra = [_dot(acc[h].astype(BF16), v[:, hsl[h]]) for h in range(HEADS)]
        upd = [_dot_tn(v[:, hsl[h]], kd[:, hsl[h]]) for h in range(HEADS)]
        return intra, upd, qd, jnp.exp2(b_end)

    parts = [intra_chunk(ci) for ci in range(A_TILE_CHUNKS)]
    nw = nw_ref[...]
    sts = [state_ref[h] for h in range(HEADS)]
    for ci, (intra, upd, qd, s_decay) in enumerate(parts):
        rows = slice(ci * c, (ci + 1) * c)
        inter = [_dot_nt(qd[:, hsl[h]], sts[h].astype(BF16)) for h in range(HEADS)]
        sts = [sts[h] * s_decay[:, hsl[h]] + upd[h] for h in range(HEADS)]
        for h, sl in enumerate(hsl):
            o = intra[h] + inter[h]
            y = o * lax.rsqrt(jnp.mean(o * o, axis=-1, keepdims=True) + EPS) * nw
            gate = pa_ref[rows, 3 * bw + h * HEAD_DIM:3 * bw + (h + 1) * HEAD_DIM]
            o_ref[rows, sl] = (y * _silu(gate)).astype(BF16)
    for h in range(HEADS):
        state_ref[h] = sts[h]


def _hgrn2(pa, lbp, nw, batch, seq, layer):
    c = A_CHUNK
    tc = A_TILE_CHUNKS
    nt = seq // (c * tc)
    tril = jnp.asarray(np.tril(np.ones((c, c), np.float32)))
    lvl = jnp.asarray(_a_level_ids())
    const2 = lambda b, i: (0, 0)
    return pl.pallas_call(
        functools.partial(_hgrn2_kernel, layer=layer),
        grid=(batch, nt),
        in_specs=[pl.BlockSpec((tc * c, PA_WIDTH), lambda b, i: (b * nt + i, 0)),
                  pl.BlockSpec(lbp.shape, const2),
                  pl.BlockSpec((1, HEAD_DIM), const2),
                  pl.BlockSpec((c, c), const2),
                  pl.BlockSpec((c, c), const2)],
        out_specs=pl.BlockSpec((tc * c, BRANCH_WIDTH), lambda b, i: (b * nt + i, 0)),
        out_shape=jax.ShapeDtypeStruct((batch * seq, BRANCH_WIDTH), BF16),
        scratch_shapes=[pltpu.VMEM((HEADS, HEAD_DIM, HEAD_DIM), F32),
                        pltpu.VMEM((tc, c, BRANCH_WIDTH), F32),
                        pltpu.VMEM((tc, c + 2 * SUBLANES, BRANCH_WIDTH), F32)],
        compiler_params=pltpu.CompilerParams(
            dimension_semantics=("arbitrary", "arbitrary"), vmem_limit_bytes=VMEM_LIMIT),
        name="hgrn2",
    )(pa, lbp, nw, tril, lvl)


def _unit_lower_inverses(mats, ti, si):
    eye = (ti == si).astype(F32)
    ts = [eye] * len(mats)
    s = 1
    while s < B_CHUNK:
        shift = int(math.log2(2 * s))
        off = ((ti >> shift) == (si >> shift)) & ((ti & s) != 0) & ((si & s) == 0)
        a_off = [jnp.where(off, a, 0.0) for a in mats]
        if s == 1:
            ts = [t - ao for t, ao in zip(ts, a_off)]
        else:
            t16 = [t.astype(BF16) for t in ts]
            inner = [_dot(ao.astype(BF16), t) for ao, t in zip(a_off, t16)]
            outer = [_dot(t, x.astype(BF16)) for t, x in zip(t16, inner)]
            ts = [t - x for t, x in zip(ts, outer)]
        s *= 2
    return ts


def _dn_prep_kernel(pb_ref, halo_ref, cw_ref, alog_ref, dtb_ref, tril_ref,
                    u_ref, w_ref, qd_ref, kd_ref, qk_ref, ge_ref, xp_ref, qkv_ref, *, tiles_per_seq):
    ts = B_TILE
    c = B_CHUNK
    pr = B_PAIR
    bw = BRANCH_WIDTH
    sl8 = SUBLANES

    keep = jnp.where(pl.program_id(0) % tiles_per_seq == 0, 0.0, 1.0)
    half = ts // 2
    for s in range(B_CONV_CH // LANES):
        lanes = slice(s * LANES, (s + 1) * LANES)
        xp_ref[s, 0:sl8, :] = halo_ref[:, lanes] * keep
        xp_ref[s, sl8:sl8 + ts, :] = pb_ref[:, lanes]
    for s in range(B_CONV_CH // LANES):
        lanes = slice(s * LANES, (s + 1) * LANES)
        cw = [cw_ref[j:j + 1, lanes] for j in range(B_CONV)]
        ld = {off: xp_ref[s, pl.ds(sl8 + off, half, stride=2), :] for off in range(1 - B_CONV, 2)}
        even = None
        odd = None
        for j in range(B_CONV):
            te = cw[j] * ld[j - (B_CONV - 1)]
            to = cw[j] * ld[j - (B_CONV - 1) + 1]
            even = te if even is None else even + te
            odd = to if odd is None else odd + to
        qkv_ref[s, pl.ds(0, half, stride=2), :] = _silu(even)
        qkv_ref[s, pl.ds(1, half, stride=2), :] = _silu(odd)

    small = pb_ref[:, B_CONV_CH + bw:B_CONV_CH + bw + SMALL_PAD]
    beta_all = _sigmoid(small)
    g_all = -jnp.exp(alog_ref[...]) * _softplus(small + dtb_ref[...])
    gcum = jnp.concatenate(
        [jnp.dot(tril_ref[...], g_all[p * pr:(p + 1) * pr, :], precision=HIGHEST, preferred_element_type=F32)
         for p in range(ts // pr)], axis=0)
    gcum_t = gcum.T
    for j in range(ts // c):
        ge_ref[0, j:j + 1, :] = jnp.exp(gcum[j * c + c - 1:j * c + c, :])

    ti = lax.broadcasted_iota(jnp.int32, (pr, pr), 0)
    si = lax.broadcasted_iota(jnp.int32, (pr, pr), 1)
    same = (ti >> int(math.log2(c))) == (si >> int(math.log2(c)))
    incl = same & (ti >= si)
    strict = same & (ti > si)
    first_chunk = lax.broadcasted_iota(jnp.int32, (pr, 1), 0) < c

    probs = [(h, p) for h in range(HEADS) for p in range(ts // pr)]
    k16s, kb16s, q16s, decays, rhs = [], [], [], [], []
    for h, p in probs:
        sl = slice(h * HEAD_DIM, (h + 1) * HEAD_DIM)
        rows = slice(p * pr, (p + 1) * pr)
        q = qkv_ref[h, rows, :]
        k = qkv_ref[HEADS + h, rows, :]
        v = qkv_ref[2 * HEADS + h, rows, :]
        q = q * lax.rsqrt(jnp.sum(q * q, axis=-1, keepdims=True) + EPS) * (HEAD_DIM ** -0.5)
        k = k * lax.rsqrt(jnp.sum(k * k, axis=-1, keepdims=True) + EPS)
        beta = beta_all[rows, h:h + 1]
        gc = gcum[rows, HEADS + h:HEADS + h + 1]
        gr = gcum_t[HEADS + h:HEADS + h + 1, rows]
        g_last = jnp.where(first_chunk, gc[c - 1:c, :], gc[pr - 1:pr, :])
        egc = jnp.exp(gc)
        kb = k * beta
        decays.append(jnp.where(incl, jnp.exp(jnp.minimum(gc - gr, 0.0)), 0.0))
        k16s.append(k.astype(BF16))
        kb16s.append(kb.astype(BF16))
        q16s.append(q.astype(BF16))
        rhs.append(jnp.concatenate([v * beta, kb * egc], axis=1).astype(BF16))
        qd_ref[rows, sl] = (q * egc).astype(BF16)
        kd_ref[rows, sl] = (k * jnp.exp(g_last - gc)).astype(BF16)

    kk = [_dot_nt(kb16, k16) for kb16, k16 in zip(kb16s, k16s)]
    qk = [_dot_nt(q16, k16) for q16, k16 in zip(q16s, k16s)]
    mats = [jnp.where(strict, x * d, 0.0) for x, d in zip(kk, decays)]
    tinv = _unit_lower_inverses(mats, ti, si)
    uw = [_dot(t.astype(BF16), r) for t, r in zip(tinv, rhs)]
    for i, (h, p) in enumerate(probs):
        sl = slice(h * HEAD_DIM, (h + 1) * HEAD_DIM)
        rows = slice(p * pr, (p + 1) * pr)
        u_ref[rows, sl] = uw[i][:, 0:HEAD_DIM]
        w_ref[rows, sl] = uw[i][:, HEAD_DIM:].astype(BF16)
        qk_ref[p, h] = (qk[i] * decays[i]).astype(BF16)


def _dn_scan_kernel(u_ref, w_ref, qd_ref, kd_ref, qk_ref, ge_ref, z_ref, nw_ref, o_ref, state_ref):
    c = B_CHUNK
    batch = u_ref.shape[0]

    @pl.when(pl.program_id(0) == 0)
    def _():
        state_ref[...] = jnp.zeros_like(state_ref)

    nw = nw_ref[...]
    chains = [(b, h) for b in range(batch) for h in range(HEADS)]
    hsl = lambda h: slice(h * HEAD_DIM, (h + 1) * HEAD_DIM)
    states = [state_ref[b * HEADS + h] for b, h in chains]
    o_inter = [[] for _ in chains]
    v_new = [[] for _ in chains]
    for j in range(B_PAIR // c):
        rows = slice(j * c, (j + 1) * c)
        st16 = [st.astype(BF16) for st in states]
        lhs = [jnp.concatenate([w_ref[b, rows, hsl(h)], qd_ref[b, rows, hsl(h)]], axis=0) for b, h in chains]
        prod = [_dot(x, s) for x, s in zip(lhs, st16)]
        vn16 = []
        for i, (b, h) in enumerate(chains):
            vn = (u_ref[b, rows, hsl(h)] - prod[i][0:c, :]).astype(BF16)
            vn16.append(vn)
            v_new[i].append(vn)
            o_inter[i].append(prod[i][c:, :])
        upd = [_dot_tn(kd_ref[b, rows, hsl(h)], vn) for (b, h), vn in zip(chains, vn16)]
        states = [st * ge_ref[b, 0, j:j + 1, HEADS + h:HEADS + h + 1] + x
                  for (b, h), st, x in zip(chains, states, upd)]
    intra = [_dot(qk_ref[b, 0, h], jnp.concatenate(v_new[i], axis=0)) for i, (b, h) in enumerate(chains)]
    for i, (b, h) in enumerate(chains):
        state_ref[b * HEADS + h] = states[i]
        o = jnp.concatenate(o_inter[i], axis=0) + intra[i]
        y = o * lax.rsqrt(jnp.mean(o * o, axis=-1, keepdims=True) + EPS) * nw
        o_ref[b, :, hsl(h)] = (y * _silu(z_ref[b, :, hsl(h)])).astype(BF16)


def _deltanet(pb, cw, alog, dtb, nw, batch, seq):
    c = B_CHUNK
    ts = B_TILE
    pr = B_PAIR
    bw = BRANCH_WIDTH
    t = batch * seq
    n_tiles = t // ts
    tril = jnp.asarray(np.kron(np.eye(pr // c, dtype=np.float32), np.tril(np.ones((c, c), np.float32))))
    const = lambda i: (0, 0)
    row = lambda i: (i, 0)
    halo_blocks = ts // SUBLANES
    u, w, qd, kd, qk, ge = pl.pallas_call(
        functools.partial(_dn_prep_kernel, tiles_per_seq=seq // ts),
        grid=(n_tiles,),
        in_specs=[pl.BlockSpec((ts, PB_WIDTH), row),
                  pl.BlockSpec((SUBLANES, B_CONV_CH), lambda i: (jnp.maximum(i * halo_blocks - 1, 0), 0)),
                  pl.BlockSpec(cw.shape, const),
                  pl.BlockSpec((1, SMALL_PAD), const),
                  pl.BlockSpec((1, SMALL_PAD), const),
                  pl.BlockSpec((pr, pr), const)],
        out_specs=[pl.BlockSpec((ts, bw), row),
                   pl.BlockSpec((ts, bw), row),
                   pl.BlockSpec((ts, bw), row),
                   pl.BlockSpec((ts, bw), row),
                   pl.BlockSpec((ts // pr, HEADS, pr, pr), lambda i: (i, 0, 0, 0)),
                   pl.BlockSpec((1, ts // c, SMALL_PAD), lambda i: (i, 0, 0))],
        out_shape=[jax.ShapeDtypeStruct((t, bw), F32),
                   jax.ShapeDtypeStruct((t, bw), BF16),
                   jax.ShapeDtypeStruct((t, bw), BF16),
                   jax.ShapeDtypeStruct((t, bw), BF16),
                   jax.ShapeDtypeStruct((t // pr, HEADS, pr, pr), BF16),
                   jax.ShapeDtypeStruct((n_tiles, ts // c, SMALL_PAD), F32)],
        scratch_shapes=[pltpu.VMEM((B_CONV_CH // LANES, ts + SUBLANES, LANES), F32),
                        pltpu.VMEM((B_CONV_CH // LANES, ts, LANES), F32)],
        compiler_params=pltpu.CompilerParams(
            dimension_semantics=("arbitrary",), vmem_limit_bytes=VMEM_LIMIT),
        name="dn_prep",
    )(pb, pb, cw, alog, dtb, tril)

    np_seq = seq // pr
    seq3 = lambda x: x.reshape(batch, seq, x.shape[-1])
    blk3 = pl.BlockSpec((batch, pr, bw), lambda i: (0, i, 0))
    y = pl.pallas_call(
        _dn_scan_kernel,
        grid=(np_seq,),
        in_specs=[blk3, blk3, blk3, blk3,
                  pl.BlockSpec((batch, 1, HEADS, pr, pr), lambda i: (0, i, 0, 0, 0)),
                  pl.BlockSpec((batch, 1, pr // c, SMALL_PAD), lambda i: (0, i, 0, 0)),
                  pl.BlockSpec((batch, pr, bw), lambda i: (0, i, B_CONV_CH // bw)),
                  pl.BlockSpec((1, HEAD_DIM), const)],
        out_specs=blk3,
        out_shape=jax.ShapeDtypeStruct((batch, seq, bw), BF16),
        scratch_shapes=[pltpu.VMEM((batch * HEADS, HEAD_DIM, HEAD_DIM), F32)],
        compiler_params=pltpu.CompilerParams(
            dimension_semantics=("arbitrary",), vmem_limit_bytes=VMEM_LIMIT),
        name="dn_scan",
    )(seq3(u), seq3(w), seq3(qd), seq3(kd),
      qk.reshape(batch, np_seq, HEADS, pr, pr),
      ge.reshape(batch, np_seq, pr // c, SMALL_PAD),
      seq3(pb), nw)
    return y.reshape(t, bw)


def _band_buckets():
    i = np.arange(C_BLOCK)[:, None]
    j = np.arange(2 * C_BLOCK)[None, :]
    dist = i + C_BLOCK - j
    max_exact = N_BUCKETS // 2
    d_f = np.maximum(dist, 1).astype(np.float32)
    large = max_exact + (np.log(d_f / np.float32(max_exact)) / np.float32(math.log(MAX_DISTANCE / max_exact))
                         * np.float32(N_BUCKETS - max_exact)).astype(np.int32)
    large = np.minimum(large, N_BUCKETS - 1)
    bucket = np.where(dist < max_exact, dist, large)
    return np.where((dist >= 0) & (dist < WINDOW), bucket, -1).astype(np.int32)


def _swa_kernel(rb_ref, sink_ref, cur_ref, prev_ref, bucket_ref, o_ref, bias_ref):
    blk = C_BLOCK
    bw = BRANCH_WIDTH
    n = pl.program_id(1)

    @pl.when((pl.program_id(0) == 0) & (n == 0))
    def _():
        bucket = bucket_ref[...]
        for h in range(C_Q_HEADS):
            acc = jnp.full((blk, 2 * blk), NEG_BIG, F32)
            for bk in range(N_BUCKETS):
                acc = jnp.where(bucket == bk, rb_ref[bk, h], acc)
            bias_ref[h] = acc

    kv = jnp.concatenate([prev_ref[...], cur_ref[:, bw:bw + 2 * C_KV_WIDTH]], axis=0)
    rows = kv.shape[0]
    lane = lax.broadcasted_iota(jnp.int32, (rows, C_KV_WIDTH), 1)

    def halves(x, j):
        own = jnp.where((lane >= j * C_HEAD_DIM) & (lane < (j + 1) * C_HEAD_DIM), x, 0.0)
        other = pltpu.roll(own, C_HEAD_DIM, 1)
        pair = (own, other) if j == 0 else (other, own)
        return [t.astype(BF16) for t in pair]

    kz = [halves(kv[:, 0:C_KV_WIDTH], j) for j in range(C_KV_HEADS)]
    vz = [halves(kv[:, C_KV_WIDTH:], j) for j in range(C_KV_HEADS)]
    scale = C_HEAD_DIM ** -0.5
    group = C_Q_HEADS // C_KV_HEADS
    key_ok = lax.broadcasted_iota(jnp.int32, (1, 2 * blk), 1) >= jnp.where(n > 0, 0, blk)

    chains = [(i, h) for i in range(C_TILE_BLOCKS) for h in range(C_Q_HEADS)]
    qrows = lambda i: slice(i * blk, (i + 1) * blk)
    krows = lambda i: slice(i * blk, (i + 2) * blk)
    q2 = [[(cur_ref[qrows(i), p * LANES:(p + 1) * LANES] * scale).astype(BF16)
           for p in range(C_Q_HEADS // 2)] for i in range(C_TILE_BLOCKS)]
    logits = [_dot_nt(q2[i][h // 2], kz[h // group][h % 2][krows(i), :]) + bias_ref[h] for i, h in chains]
    logits = [jnp.where(key_ok, lg, NEG_BIG) if i == 0 else lg for (i, h), lg in zip(chains, logits)]
    mx = [jnp.maximum(jnp.max(lg, axis=-1, keepdims=True), sink_ref[h]) for (i, h), lg in zip(chains, logits)]
    pr = [jnp.exp(lg - m) for lg, m in zip(logits, mx)]
    den = [jnp.sum(x, axis=-1, keepdims=True) + jnp.exp(sink_ref[h] - m) for (i, h), x, m in zip(chains, pr, mx)]
    pv = [_dot(x.astype(BF16), vz[h // group][h % 2][krows(i), :]) for (i, h), x in zip(chains, pr)]
    for i in range(C_TILE_BLOCKS):
        for p in range(C_Q_HEADS // 2):
            a, b = i * C_Q_HEADS + 2 * p, i * C_Q_HEADS + 2 * p + 1
            out = pv[a] / den[a] + pv[b] / den[b]
            gate = cur_ref[qrows(i), bw + 2 * C_KV_WIDTH + p * LANES:bw + 2 * C_KV_WIDTH + (p + 1) * LANES]
            o_ref[qrows(i), p * LANES:(p + 1) * LANES] = (out * _silu(gate)).astype(BF16)


def _swa(pc, rel_bias, sinks, batch, seq):
    blk = C_BLOCK
    nb = seq // blk
    bucket = jnp.asarray(_band_buckets())
    kv_col = BRANCH_WIDTH // (2 * C_KV_WIDTH)
    smem = pl.BlockSpec(memory_space=pltpu.SMEM)
    tb = C_TILE_BLOCKS
    nt = nb // tb
    return pl.pallas_call(
        _swa_kernel,
        grid=(batch, nt),
        in_specs=[smem, smem,
                  pl.BlockSpec((tb * blk, PC_WIDTH), lambda b, i: (b * nt + i, 0)),
                  pl.BlockSpec((blk, 2 * C_KV_WIDTH), lambda b, i: (b * nb + jnp.maximum(i * tb - 1, 0), kv_col)),
                  pl.BlockSpec((blk, 2 * blk), lambda b, i: (0, 0))],
        out_specs=pl.BlockSpec((tb * blk, BRANCH_WIDTH), lambda b, i: (b * nt + i, 0)),
        out_shape=jax.ShapeDtypeStruct((batch * seq, BRANCH_WIDTH), BF16),
        scratch_shapes=[pltpu.VMEM((C_Q_HEADS, blk, 2 * blk), F32)],
        compiler_params=pltpu.CompilerParams(
            dimension_semantics=("arbitrary", "arbitrary"), vmem_limit_bytes=VMEM_LIMIT),
        name="swa",
    )(rel_bias, sinks, pc, pc, bucket)


def _split_w_in(w):
    bw = BRANCH_WIDTH
    widths = (bw, bw, bw, bw, B_CONV_CH, bw, HEADS, HEADS, bw, C_KV_WIDTH, C_KV_WIDTH, bw, N_BRANCHES * D_MODEL)
    offs = np.concatenate([[0], np.cumsum(widths)])
    col = lambda i: w[:, offs[i]:offs[i + 1]]
    wa = w[:, offs[0]:offs[4]]
    wb = jnp.concatenate([col(4), col(5), col(6), col(7),
                          jnp.zeros((D_MODEL, SMALL_PAD - 2 * HEADS), w.dtype)], axis=1)
    wc = w[:, offs[8]:offs[12]]
    wg = col(12)
    return [t.astype(BF16) for t in (wa, wb, wc, wg)]


def _lane_pad(v, offset):
    return jnp.zeros((1, SMALL_PAD), F32).at[0, offset:offset + v.shape[0]].set(v.astype(F32))


def kernel(x, norm_w, w_in, conv_w, a_log, dt_bias, lb_param, norm_a, norm_b, sinks, rel_bias,
           w_branch, w_out, final_norm):
    batch, seq, _ = x.shape
    depth = w_in.shape[0]
    x2 = x.reshape(batch * seq, D_MODEL)
    fn = final_norm.reshape(1, D_MODEL)
    for l in range(depth):
        wa, wb, wc, wg = _split_w_in(w_in[l])
        nw = norm_w[l].reshape(1, D_MODEL)
        pa, pb, pc = _inproj(x2, nw, wa, wb, wc)
        ya = _hgrn2(pa, lb_param, norm_a[l].reshape(1, HEAD_DIM), batch, seq, l)
        yb = _deltanet(pb, conv_w[l], _lane_pad(a_log[l], HEADS), _lane_pad(dt_bias[l], HEADS),
                       norm_b[l].reshape(1, HEAD_DIM), batch, seq)
        yc = _swa(pc, rel_bias, sinks[l], batch, seq)
        x2 = _merge(x2, nw, wg, ya, yb, yc, w_branch[l].astype(BF16), w_out[l].astype(BF16), fn,
                    final=(l == depth - 1))
    return x2.reshape(batch, seq, D_MODEL)
```

```python
import functools
import math

import numpy as np
import jax
import jax.numpy as jnp
from jax import lax
from jax.experimental import pallas as pl
from jax.experimental.pallas import tpu as pltpu

F32 = jnp.float32
BF16 = jnp.bfloat16
HIGHEST = lax.Precision.HIGHEST

D_MODEL = 1024
BRANCH_WIDTH = D_MODEL // 2
N_BRANCHES = 3
EPS = 1e-6
HEADS = 4
HEAD_DIM = BRANCH_WIDTH // HEADS
B_CONV = 4
B_CONV_CH = 3 * BRANCH_WIDTH
C_Q_HEADS = 8
C_KV_HEADS = 2
C_HEAD_DIM = BRANCH_WIDTH // C_Q_HEADS
C_KV_WIDTH = C_KV_HEADS * C_HEAD_DIM
WINDOW = 128
C_BLOCK = 128
C_TILE_BLOCKS = 4
N_BUCKETS = 32
MAX_DISTANCE = 128
LANES = 128
SUBLANES = 8
NEG_BIG = -1e30
LOG2E = math.log2(math.e)

A_CHUNK = 128
A_TILE_CHUNKS = 4
B_CHUNK = 64
B_PAIR = 2 * B_CHUNK
B_TILE = 512
B_SCAN_PAIRS = 2
PROJ_ROWS = 512
SMALL_PAD = LANES
VMEM_LIMIT = 52 * 1024 * 1024

PA_WIDTH = 4 * BRANCH_WIDTH
PB_WIDTH = B_CONV_CH + BRANCH_WIDTH + SMALL_PAD
PC_WIDTH = 2 * BRANCH_WIDTH + 2 * C_KV_WIDTH


def _dot(a, b):
    return jnp.dot(a, b, preferred_element_type=F32)


def _dot_nt(a, b):
    return lax.dot_general(a, b, (((1,), (1,)), ((), ())), preferred_element_type=F32)


def _dot_tn(a, b):
    return lax.dot_general(a, b, (((0,), (0,)), ((), ())), preferred_element_type=F32)


def _sigmoid(x):
    return 1.0 / (1.0 + jnp.exp(-x))


def _silu(x):
    return x * _sigmoid(x)


def _softplus(x):
    return jnp.maximum(x, 0.0) + jnp.log(1.0 + jnp.exp(-jnp.abs(x)))


def _rms(x, w):
    return x * lax.rsqrt(jnp.mean(x * x, axis=-1, keepdims=True) + EPS) * w


def _resident(shape):
    return pl.BlockSpec(shape, lambda i: (0,) * len(shape), pipeline_mode=pl.Buffered(1))


def _inproj_kernel(x_ref, nw_ref, w_ref, pa_ref, pb_ref, pc_ref):
    h = _rms(x_ref[...], nw_ref[...]).astype(BF16)
    base = 0
    for o_ref in (pa_ref, pb_ref, pc_ref):
        n = o_ref.shape[1]
        for j in range(0, n, BRANCH_WIDTH):
            wd = min(BRANCH_WIDTH, n - j)
            o_ref[:, j:j + wd] = _dot(h, w_ref[:, base + j:base + j + wd])
        base += n


def _inproj(x2, nw, w):
    t = x2.shape[0]
    tm = PROJ_ROWS
    return pl.pallas_call(
        _inproj_kernel,
        grid=(t // tm,),
        in_specs=[pl.BlockSpec((tm, D_MODEL), lambda i: (i, 0)),
                  _resident((1, D_MODEL)),
                  _resident(w.shape)],
        out_specs=[pl.BlockSpec((tm, PA_WIDTH), lambda i: (i, 0)),
                   pl.BlockSpec((tm, PB_WIDTH), lambda i: (i, 0)),
                   pl.BlockSpec((tm, PC_WIDTH), lambda i: (i, 0))],
        out_shape=[jax.ShapeDtypeStruct((t, PA_WIDTH), F32),
                   jax.ShapeDtypeStruct((t, PB_WIDTH), F32),
                   jax.ShapeDtypeStruct((t, PC_WIDTH), F32)],
        compiler_params=pltpu.CompilerParams(
            dimension_semantics=("arbitrary",), vmem_limit_bytes=VMEM_LIMIT),
        name="inproj",
    )(x2, nw, w)


def _merge_kernel(x_ref, nw_ref, wg_ref, ya_ref, yb_ref, yc_ref, wbr_ref, wo_ref, fn_ref,
                  o_ref, *, final):
    x = x_ref[...]
    h = _rms(x, nw_ref[...]).astype(BF16)
    merged = None
    for n, y_ref in enumerate((ya_ref, yb_ref, yc_ref)):
        gate = _sigmoid(_dot(h, wg_ref[:, n * D_MODEL:(n + 1) * D_MODEL]))
        term = gate * _dot(y_ref[...], wbr_ref[n])
        merged = term if merged is None else merged + term
    out = x + _dot(merged.astype(BF16), wo_ref[...])
    if final:
        out = _rms(out, fn_ref[...])
    o_ref[...] = out


def _merge(x2, nw, wg, ya, yb, yc, wbr, wo, fn, final):
    t = x2.shape[0]
    tm = PROJ_ROWS
    row = lambda i: (i, 0)
    return pl.pallas_call(
        functools.partial(_merge_kernel, final=final),
        grid=(t // tm,),
        in_specs=[pl.BlockSpec((tm, D_MODEL), row),
                  _resident((1, D_MODEL)),
                  _resident(wg.shape),
                  pl.BlockSpec((tm, BRANCH_WIDTH), row),
                  pl.BlockSpec((tm, BRANCH_WIDTH), row),
                  pl.BlockSpec((tm, BRANCH_WIDTH), row),
                  _resident(wbr.shape),
                  _resident(wo.shape),
                  _resident((1, D_MODEL))],
        out_specs=pl.BlockSpec((tm, D_MODEL), row),
        out_shape=jax.ShapeDtypeStruct((t, D_MODEL), F32),
        compiler_params=pltpu.CompilerParams(
            dimension_semantics=("arbitrary",), vmem_limit_bytes=VMEM_LIMIT),
        name="merge",
    )(x2, nw, wg, ya, yb, yc, wbr, wo, fn)


def _a_levels():
    return [1 << i for i in range(int(math.log2(A_CHUNK)))]


def _a_level_ids():
    idx = np.arange(A_CHUNK)
    ids = np.where(idx[:, None] == idx[None, :], 0, -1).astype(np.int32)
    for i, m in enumerate(_a_levels()):
        blk = idx // (2 * m)
        upper = (idx & m) != 0
        ids[(blk[:, None] == blk[None, :]) & upper[:, None] & ~upper[None, :]] = i + 1
    return ids


def _hgrn2_kernel(pa_ref, lbp_ref, nw_ref, tril_ref, lvl_ref, o_ref,
                  state_ref, b_ref, gp_ref, *, layer):
    c = A_CHUNK
    bw = BRANCH_WIDTH

    @pl.when(pl.program_id(1) == 0)
    def _():
        state_ref[...] = jnp.zeros_like(state_ref)
        gp_ref[...] = jnp.zeros_like(gp_ref)

    lbp = lbp_ref[...]
    ex = jnp.exp(lbp - jnp.max(lbp, axis=0, keepdims=True))
    sm = ex / jnp.sum(ex, axis=0, keepdims=True)
    lb = jnp.zeros((1, bw), F32)
    for j in range(1, layer + 1):
        lb = lb + sm[j:j + 1, :]

    hsl = [slice(h * HEAD_DIM, (h + 1) * HEAD_DIM) for h in range(HEADS)]
    row = lax.broadcasted_iota(jnp.int32, (c, 1), 0)
    lvl = lvl_ref[...]
    t1 = jnp.log(lb)
    t2_lb = jnp.log1p(-lb)

    def intra_chunk(ci):
        rows = slice(ci * c, (ci + 1) * c)
        bc_ref = b_ref.at[ci]
        gc_ref = gp_ref.at[ci]
        q = _silu(pa_ref[rows, 0:bw])
        z = pa_ref[rows, bw:2 * bw]
        v = pa_ref[rows, 2 * bw:3 * bw].astype(BF16)
        e = jnp.exp(-jnp.abs(z))
        one_e = 1.0 + e
        r = 1.0 / one_e
        log_sig = jnp.minimum(z, 0.0) - jnp.log(one_e)
        sig_neg = jnp.where(z >= 0.0, e * r, r)
        t2 = t2_lb + log_sig
        lf = jnp.maximum(t1, t2) + jnp.log(1.0 + jnp.exp(-jnp.abs(t1 - t2)))
        k = (1.0 - lb) * sig_neg

        lf2 = lf * LOG2E
        b = jnp.dot(tril_ref[...], lf2, precision=HIGHEST, preferred_element_type=F32)
        bc_ref[...] = b
        gc_ref[SUBLANES:SUBLANES + c, :] = lf2

        q16 = q.astype(BF16)
        k16 = k.astype(BF16)
        diag = lvl == 0
        acc = [jnp.where(diag, _dot_nt(q16[:, sl], k16[:, sl]), 0.0) for sl in hsl]

        for li, m in enumerate(_a_levels()):
            blocks = range(0, c, 2 * m)
            if m == 1:
                neg = jnp.where((row & 1) == 1, lf2, 0.0)
            elif m == 2:
                r4 = row & 3
                g_next = gc_ref[SUBLANES + 1:SUBLANES + 1 + c, :]
                g_prev = gc_ref[SUBLANES - 1:SUBLANES - 1 + c, :]
                neg = jnp.where(r4 == 0, g_next, jnp.where(r4 == 1, 0.0, jnp.where(r4 == 2, lf2, lf2 + g_prev)))
            elif m < SUBLANES:
                neg = jnp.concatenate(
                    [-jnp.abs(bc_ref[lo:lo + 2 * m, :] - bc_ref[lo + m - 1:lo + m, :]) for lo in blocks], axis=0)
            else:
                pieces = []
                for lo in blocks:
                    anchor = bc_ref[lo + m - 1:lo + m, :]
                    pieces += [anchor - bc_ref[lo:lo + m, :], bc_ref[lo + m:lo + 2 * m, :] - anchor]
                neg = jnp.concatenate(pieces, axis=0)
            if m < SUBLANES:
                x = jnp.where((row & m) != 0, q, k)
            else:
                x = jnp.concatenate(
                    [t for lo in blocks for t in (k[lo:lo + m, :], q[lo + m:lo + 2 * m, :])], axis=0)
            xt = (x * jnp.exp2(neg)).astype(BF16)
            sel = lvl == li + 1
            for h, sl in enumerate(hsl):
                acc[h] = jnp.where(sel, _dot_nt(xt[:, sl], xt[:, sl]), acc[h])

        b_end = bc_ref[c - 1:c, :]
        qd = (q * jnp.exp2(b)).astype(BF16)
        kd = (k * jnp.exp2(b_end - b)).astype(BF16)
        intra = [_dot(acc[h].astype(BF16), v[:, hsl[h]]) for h in range(HEADS)]
        upd = [_dot_tn(v[:, hsl[h]], kd[:, hsl[h]]) for h in range(HEADS)]
        return intra, upd, qd, jnp.exp2(b_end)

    parts = [intra_chunk(ci) for ci in range(A_TILE_CHUNKS)]
    nw = nw_ref[...]
    sts = [state_ref[h] for h in range(HEADS)]
    for ci, (intra, upd, qd, s_decay) in enumerate(parts):
        rows = slice(ci * c, (ci + 1) * c)
        inter = [_dot_nt(qd[:, hsl[h]], sts[h].astype(BF16)) for h in range(HEADS)]
        sts = [sts[h] * s_decay[:, hsl[h]] + upd[h] for h in range(HEADS)]
        for h, sl in enumerate(hsl):
            o = intra[h] + inter[h]
            y = o * lax.rsqrt(jnp.mean(o * o, axis=-1, keepdims=True) + EPS) * nw
            gate = pa_ref[rows, 3 * bw + h * HEAD_DIM:3 * bw + (h + 1) * HEAD_DIM]
            o_ref[rows, sl] = (y * _silu(gate)).astype(BF16)
    for h in range(HEADS):
        state_ref[h] = sts[h]


def _hgrn2(pa, lbp, nw, batch, seq, layer):
    c = A_CHUNK
    tc = A_TILE_CHUNKS
    nt = seq // (c * tc)
    tril = jnp.asarray(np.tril(np.ones((c, c), np.float32)))
    lvl = jnp.asarray(_a_level_ids())
    const2 = lambda b, i: (0, 0)
    return pl.pallas_call(
        functools.partial(_hgrn2_kernel, layer=layer),
        grid=(batch, nt),
        in_specs=[pl.BlockSpec((tc * c, PA_WIDTH), lambda b, i: (b * nt + i, 0)),
                  pl.BlockSpec(lbp.shape, const2),
                  pl.BlockSpec((1, HEAD_DIM), const2),
                  pl.BlockSpec((c, c), const2),
                  pl.BlockSpec((c, c), const2)],
        out_specs=pl.BlockSpec((tc * c, BRANCH_WIDTH), lambda b, i: (b * nt + i, 0)),
        out_shape=jax.ShapeDtypeStruct((batch * seq, BRANCH_WIDTH), BF16),
        scratch_shapes=[pltpu.VMEM((HEADS, HEAD_DIM, HEAD_DIM), F32),
                        pltpu.VMEM((tc, c, BRANCH_WIDTH), F32),
                        pltpu.VMEM((tc, c + 2 * SUBLANES, BRANCH_WIDTH), F32)],
        compiler_params=pltpu.CompilerParams(
            dimension_semantics=("arbitrary", "arbitrary"), vmem_limit_bytes=VMEM_LIMIT),
        name="hgrn2",
    )(pa, lbp, nw, tril, lvl)


def _unit_lower_inverses(mats, ti, si):
    eye = (ti == si).astype(F32)
    ts = [eye] * len(mats)
    s = 1
    while s < B_CHUNK:
        shift = int(math.log2(2 * s))
        off = ((ti >> shift) == (si >> shift)) & ((ti & s) != 0) & ((si & s) == 0)
        a_off = [jnp.where(off, a, 0.0) for a in mats]
        if s == 1:
            ts = [t - ao for t, ao in zip(ts, a_off)]
        else:
            t16 = [t.astype(BF16) for t in ts]
            inner = [_dot(ao.astype(BF16), t) for ao, t in zip(a_off, t16)]
            outer = [_dot(t, x.astype(BF16)) for t, x in zip(t16, inner)]
            ts = [t - x for t, x in zip(ts, outer)]
        s *= 2
    return ts


def _dn_prep_kernel(pb_ref, halo_ref, cw_ref, alog_ref, dtb_ref, tril_ref,
                    u_ref, w_ref, qd_ref, kd_ref, qk_ref, ge_ref, xp_ref, qkv_ref, *, tiles_per_seq):
    ts = B_TILE
    c = B_CHUNK
    pr = B_PAIR
    bw = BRANCH_WIDTH
    sl8 = SUBLANES

    keep = jnp.where(pl.program_id(0) % tiles_per_seq == 0, 0.0, 1.0)
    half = ts // 2
    for s in range(B_CONV_CH // LANES):
        lanes = slice(s * LANES, (s + 1) * LANES)
        xp_ref[s, 0:sl8, :] = halo_ref[:, lanes] * keep
        xp_ref[s, sl8:sl8 + ts, :] = pb_ref[:, lanes]
    for s in range(B_CONV_CH // LANES):
        lanes = slice(s * LANES, (s + 1) * LANES)
        cw = [cw_ref[j:j + 1, lanes] for j in range(B_CONV)]
        ld = {off: xp_ref[s, pl.ds(sl8 + off, half, stride=2), :] for off in range(1 - B_CONV, 2)}
        even = None
        odd = None
        for j in range(B_CONV):
            te = cw[j] * ld[j - (B_CONV - 1)]
            to = cw[j] * ld[j - (B_CONV - 1) + 1]
            even = te if even is None else even + te
            odd = to if odd is None else odd + to
        qkv_ref[s, pl.ds(0, half, stride=2), :] = _silu(even)
        qkv_ref[s, pl.ds(1, half, stride=2), :] = _silu(odd)

    small = pb_ref[:, B_CONV_CH + bw:B_CONV_CH + bw + SMALL_PAD]
    beta_all = _sigmoid(small)
    g_all = -jnp.exp(alog_ref[...]) * _softplus(small + dtb_ref[...])
    gcum = jnp.concatenate(
        [jnp.dot(tril_ref[...], g_all[p * pr:(p + 1) * pr, :], precision=HIGHEST, preferred_element_type=F32)
         for p in range(ts // pr)], axis=0)
    gcum_t = gcum.T
    for j in range(ts // c):
        ge_ref[0, j:j + 1, :] = jnp.exp(gcum[j * c + c - 1:j * c + c, :])

    ti = lax.broadcasted_iota(jnp.int32, (pr, pr), 0)
    si = lax.broadcasted_iota(jnp.int32, (pr, pr), 1)
    same = (ti >> int(math.log2(c))) == (si >> int(math.log2(c)))
    incl = same & (ti >= si)
    strict = same & (ti > si)
    first_chunk = lax.broadcasted_iota(jnp.int32, (pr, 1), 0) < c

    probs = [(h, p) for h in range(HEADS) for p in range(ts // pr)]
    k16s, kb16s, q16s, decays, rhs = [], [], [], [], []
    for h, p in probs:
        sl = slice(h * HEAD_DIM, (h + 1) * HEAD_DIM)
        rows = slice(p * pr, (p + 1) * pr)
        q = qkv_ref[h, rows, :]
        k = qkv_ref[HEADS + h, rows, :]
        v = qkv_ref[2 * HEADS + h, rows, :]
        q = q * lax.rsqrt(jnp.sum(q * q, axis=-1, keepdims=True) + EPS) * (HEAD_DIM ** -0.5)
        k = k * lax.rsqrt(jnp.sum(k * k, axis=-1, keepdims=True) + EPS)
        beta = beta_all[rows, h:h + 1]
        gc = gcum[rows, HEADS + h:HEADS + h + 1]
        gr = gcum_t[HEADS + h:HEADS + h + 1, rows]
        g_last = jnp.where(first_chunk, gc[c - 1:c, :], gc[pr - 1:pr, :])
        egc = jnp.exp(gc)
        kb = k * beta
        decays.append(jnp.where(incl, jnp.exp(jnp.minimum(gc - gr, 0.0)), 0.0))
        k16s.append(k.astype(BF16))
        kb16s.append(kb.astype(BF16))
        q16s.append(q.astype(BF16))
        rhs.append(jnp.concatenate([v * beta, kb * egc], axis=1).astype(BF16))
        qd_ref[rows, sl] = (q * egc).astype(BF16)
        kd_ref[rows, sl] = (k * jnp.exp(g_last - gc)).astype(BF16)

    kk = [_dot_nt(kb16, k16) for kb16, k16 in zip(kb16s, k16s)]
    qk = [_dot_nt(q16, k16) for q16, k16 in zip(q16s, k16s)]
    mats = [jnp.where(strict, x * d, 0.0) for x, d in zip(kk, decays)]
    tinv = _unit_lower_inverses(mats, ti, si)
    uw = [_dot(t.astype(BF16), r) for t, r in zip(tinv, rhs)]
    for i, (h, p) in enumerate(probs):
        sl = slice(h * HEAD_DIM, (h + 1) * HEAD_DIM)
        rows = slice(p * pr, (p + 1) * pr)
        u_ref[rows, sl] = uw[i][:, 0:HEAD_DIM]
        w_ref[rows, sl] = uw[i][:, HEAD_DIM:].astype(BF16)
        qk_ref[p, h] = (qk[i] * decays[i]).astype(BF16)


def _dn_scan_kernel(u_ref, w_ref, qd_ref, kd_ref, qk_ref, ge_ref, z_ref, nw_ref, o_ref, state_ref):
    c = B_CHUNK
    batch = u_ref.shape[0]

    @pl.when(pl.program_id(0) == 0)
    def _():
        state_ref[...] = jnp.zeros_like(state_ref)

    nw = nw_ref[...]
    chains = [(b, h) for b in range(batch) for h in range(HEADS)]
    hsl = lambda h: slice(h * HEAD_DIM, (h + 1) * HEAD_DIM)
    states = [state_ref[b * HEADS + h] for b, h in chains]
    for pi in range(B_SCAN_PAIRS):
        o_inter = [[] for _ in chains]
        v_new = [[] for _ in chains]
        for j in range(B_PAIR // c):
            rows = slice(pi * B_PAIR + j * c, pi * B_PAIR + (j + 1) * c)
            st16 = [st.astype(BF16) for st in states]
            lhs = [jnp.concatenate([w_ref[b, rows, hsl(h)], qd_ref[b, rows, hsl(h)]], axis=0)
                   for b, h in chains]
            prod = [_dot(x, s) for x, s in zip(lhs, st16)]
            vn16 = []
            for i, (b, h) in enumerate(chains):
                vn = (u_ref[b, rows, hsl(h)] - prod[i][0:c, :]).astype(BF16)
                vn16.append(vn)
                v_new[i].append(vn)
                o_inter[i].append(prod[i][c:, :])
            upd = [_dot_tn(kd_ref[b, rows, hsl(h)], vn) for (b, h), vn in zip(chains, vn16)]
            states = [st * ge_ref[b, pi, j:j + 1, HEADS + h:HEADS + h + 1] + x
                      for (b, h), st, x in zip(chains, states, upd)]
        intra = [_dot(qk_ref[b, pi, h], jnp.concatenate(v_new[i], axis=0)) for i, (b, h) in enumerate(chains)]
        prows = slice(pi * B_PAIR, (pi + 1) * B_PAIR)
        for i, (b, h) in enumerate(chains):
            o = jnp.concatenate(o_inter[i], axis=0) + intra[i]
            y = o * lax.rsqrt(jnp.mean(o * o, axis=-1, keepdims=True) + EPS) * nw
            o_ref[b, prows, hsl(h)] = (y * _silu(z_ref[b, prows, hsl(h)])).astype(BF16)
    for i, (b, h) in enumerate(chains):
        state_ref[b * HEADS + h] = states[i]


def _deltanet(pb, cw, alog, dtb, nw, batch, seq):
    c = B_CHUNK
    ts = B_TILE
    pr = B_PAIR
    bw = BRANCH_WIDTH
    t = batch * seq
    n_tiles = t // ts
    tril = jnp.asarray(np.kron(np.eye(pr // c, dtype=np.float32), np.tril(np.ones((c, c), np.float32))))
    const = lambda i: (0, 0)
    row = lambda i: (i, 0)
    halo_blocks = ts // SUBLANES
    u, w, qd, kd, qk, ge = pl.pallas_call(
        functools.partial(_dn_prep_kernel, tiles_per_seq=seq // ts),
        grid=(n_tiles,),
        in_specs=[pl.BlockSpec((ts, PB_WIDTH), row),
                  pl.BlockSpec((SUBLANES, B_CONV_CH), lambda i: (jnp.maximum(i * halo_blocks - 1, 0), 0)),
                  pl.BlockSpec(cw.shape, const),
                  pl.BlockSpec((1, SMALL_PAD), const),
                  pl.BlockSpec((1, SMALL_PAD), const),
                  pl.BlockSpec((pr, pr), const)],
        out_specs=[pl.BlockSpec((ts, bw), row),
                   pl.BlockSpec((ts, bw), row),
                   pl.BlockSpec((ts, bw), row),
                   pl.BlockSpec((ts, bw), row),
                   pl.BlockSpec((ts // pr, HEADS, pr, pr), lambda i: (i, 0, 0, 0)),
                   pl.BlockSpec((1, ts // c, SMALL_PAD), lambda i: (i, 0, 0))],
        out_shape=[jax.ShapeDtypeStruct((t, bw), F32),
                   jax.ShapeDtypeStruct((t, bw), BF16),
                   jax.ShapeDtypeStruct((t, bw), BF16),
                   jax.ShapeDtypeStruct((t, bw), BF16),
                   jax.ShapeDtypeStruct((t // pr, HEADS, pr, pr), BF16),
                   jax.ShapeDtypeStruct((n_tiles, ts // c, SMALL_PAD), F32)],
        scratch_shapes=[pltpu.VMEM((B_CONV_CH // LANES, ts + SUBLANES, LANES), F32),
                        pltpu.VMEM((B_CONV_CH // LANES, ts, LANES), F32)],
        compiler_params=pltpu.CompilerParams(
            dimension_semantics=("arbitrary",), vmem_limit_bytes=VMEM_LIMIT),
        name="dn_prep",
    )(pb, pb, cw, alog, dtb, tril)

    np_seq = seq // pr
    seq3 = lambda x: x.reshape(batch, seq, x.shape[-1])
    sp = B_SCAN_PAIRS
    blk3 = pl.BlockSpec((batch, sp * pr, bw), lambda i: (0, i, 0))
    y = pl.pallas_call(
        _dn_scan_kernel,
        grid=(np_seq // sp,),
        in_specs=[blk3, blk3, blk3, blk3,
                  pl.BlockSpec((batch, sp, HEADS, pr, pr), lambda i: (0, i, 0, 0, 0)),
                  pl.BlockSpec((batch, sp, pr // c, SMALL_PAD), lambda i: (0, i, 0, 0)),
                  pl.BlockSpec((batch, sp * pr, bw), lambda i: (0, i, B_CONV_CH // bw)),
                  pl.BlockSpec((1, HEAD_DIM), const)],
        out_specs=blk3,
        out_shape=jax.ShapeDtypeStruct((batch, seq, bw), BF16),
        scratch_shapes=[pltpu.VMEM((batch * HEADS, HEAD_DIM, HEAD_DIM), F32)],
        compiler_params=pltpu.CompilerParams(
            dimension_semantics=("arbitrary",), vmem_limit_bytes=VMEM_LIMIT),
        name="dn_scan",
    )(seq3(u), seq3(w), seq3(qd), seq3(kd),
      qk.reshape(batch, np_seq, HEADS, pr, pr),
      ge.reshape(batch, np_seq, pr // c, SMALL_PAD),
      seq3(pb), nw)
    return y.reshape(t, bw)


def _band_buckets():
    i = np.arange(C_BLOCK)[:, None]
    j = np.arange(2 * C_BLOCK)[None, :]
    dist = i + C_BLOCK - j
    max_exact = N_BUCKETS // 2
    d_f = np.maximum(dist, 1).astype(np.float32)
    large = max_exact + (np.log(d_f / np.float32(max_exact)) / np.float32(math.log(MAX_DISTANCE / max_exact))
                         * np.float32(N_BUCKETS - max_exact)).astype(np.int32)
    large = np.minimum(large, N_BUCKETS - 1)
    bucket = np.where(dist < max_exact, dist, large)
    return np.where((dist >= 0) & (dist < WINDOW), bucket, -1).astype(np.int32)


def _swa_kernel(rb_ref, sink_ref, cur_ref, prev_ref, bucket_ref, o_ref, bias_ref):
    blk = C_BLOCK
    bw = BRANCH_WIDTH
    n = pl.program_id(1)

    @pl.when((pl.program_id(0) == 0) & (n == 0))
    def _():
        bucket = bucket_ref[...]
        for h in range(C_Q_HEADS):
            acc = jnp.full((blk, 2 * blk), NEG_BIG, F32)
            for bk in range(N_BUCKETS):
                acc = jnp.where(bucket == bk, rb_ref[bk, h], acc)
            bias_ref[h] = acc

    kv = jnp.concatenate([prev_ref[...], cur_ref[:, bw:bw + 2 * C_KV_WIDTH]], axis=0)
    rows = kv.shape[0]
    lane = lax.broadcasted_iota(jnp.int32, (rows, C_KV_WIDTH), 1)

    def halves(x, j):
        own = jnp.where((lane >= j * C_HEAD_DIM) & (lane < (j + 1) * C_HEAD_DIM), x, 0.0)
        other = pltpu.roll(own, C_HEAD_DIM, 1)
        pair = (own, other) if j == 0 else (other, own)
        return [t.astype(BF16) for t in pair]

    kz = [halves(kv[:, 0:C_KV_WIDTH], j) for j in range(C_KV_HEADS)]
    vz = [halves(kv[:, C_KV_WIDTH:], j) for j in range(C_KV_HEADS)]
    scale = C_HEAD_DIM ** -0.5
    group = C_Q_HEADS // C_KV_HEADS
    key_ok = lax.broadcasted_iota(jnp.int32, (1, 2 * blk), 1) >= jnp.where(n > 0, 0, blk)

    chains = [(i, h) for i in range(C_TILE_BLOCKS) for h in range(C_Q_HEADS)]
    qrows = lambda i: slice(i * blk, (i + 1) * blk)
    krows = lambda i: slice(i * blk, (i + 2) * blk)
    q2 = [[(cur_ref[qrows(i), p * LANES:(p + 1) * LANES] * scale).astype(BF16)
           for p in range(C_Q_HEADS // 2)] for i in range(C_TILE_BLOCKS)]
    logits = [_dot_nt(q2[i][h // 2], kz[h // group][h % 2][krows(i), :]) + bias_ref[h] for i, h in chains]
    logits = [jnp.where(key_ok, lg, NEG_BIG) if i == 0 else lg for (i, h), lg in zip(chains, logits)]
    mx = [jnp.maximum(jnp.max(lg, axis=-1, keepdims=True), sink_ref[h]) for (i, h), lg in zip(chains, logits)]
    pr = [jnp.exp(lg - m) for lg, m in zip(logits, mx)]
    den = [jnp.sum(x, axis=-1, keepdims=True) + jnp.exp(sink_ref[h] - m) for (i, h), x, m in zip(chains, pr, mx)]
    pv = [_dot(x.astype(BF16), vz[h // group][h % 2][krows(i), :]) for (i, h), x in zip(chains, pr)]
    for i in range(C_TILE_BLOCKS):
        for p in range(C_Q_HEADS // 2):
            a, b = i * C_Q_HEADS + 2 * p, i * C_Q_HEADS + 2 * p + 1
            out = pv[a] / den[a] + pv[b] / den[b]
            gate = cur_ref[qrows(i), bw + 2 * C_KV_WIDTH + p * LANES:bw + 2 * C_KV_WIDTH + (p + 1) * LANES]
            o_ref[qrows(i), p * LANES:(p + 1) * LANES] = (out * _silu(gate)).astype(BF16)


def _swa(pc, rel_bias, sinks, batch, seq):
    blk = C_BLOCK
    nb = seq // blk
    bucket = jnp.asarray(_band_buckets())
    kv_col = BRANCH_WIDTH // (2 * C_KV_WIDTH)
    smem = pl.BlockSpec(memory_space=pltpu.SMEM)
    tb = C_TILE_BLOCKS
    nt = nb // tb
    return pl.pallas_call(
        _swa_kernel,
        grid=(batch, nt),
        in_specs=[smem, smem,
                  pl.BlockSpec((tb * blk, PC_WIDTH), lambda b, i: (b * nt + i, 0)),
                  pl.BlockSpec((blk, 2 * C_KV_WIDTH), lambda b, i: (b * nb + jnp.maximum(i * tb - 1, 0), kv_col)),
                  pl.BlockSpec((blk, 2 * blk), lambda b, i: (0, 0))],
        out_specs=pl.BlockSpec((tb * blk, BRANCH_WIDTH), lambda b, i: (b * nt + i, 0)),
        out_shape=jax.ShapeDtypeStruct((batch * seq, BRANCH_WIDTH), BF16),
        scratch_shapes=[pltpu.VMEM((C_Q_HEADS, blk, 2 * blk), F32)],
        compiler_params=pltpu.CompilerParams(
            dimension_semantics=("arbitrary", "arbitrary"), vmem_limit_bytes=VMEM_LIMIT),
        name="swa",
    )(rel_bias, sinks, pc, pc, bucket)


def _split_w_in(w):
    bw = BRANCH_WIDTH
    widths = (bw, bw, bw, bw, B_CONV_CH, bw, HEADS, HEADS, bw, C_KV_WIDTH, C_KV_WIDTH, bw, N_BRANCHES * D_MODEL)
    offs = np.concatenate([[0], np.cumsum(widths)])
    col = lambda i: w[:, offs[i]:offs[i + 1]]
    pad = jnp.zeros((D_MODEL, SMALL_PAD - 2 * HEADS), w.dtype)
    w_mix = jnp.concatenate([w[:, offs[0]:offs[8]], pad, w[:, offs[8]:offs[12]]], axis=1)
    return w_mix.astype(BF16), col(12).astype(BF16)


def _lane_pad(v, offset):
    return jnp.zeros((1, SMALL_PAD), F32).at[0, offset:offset + v.shape[0]].set(v.astype(F32))


def kernel(x, norm_w, w_in, conv_w, a_log, dt_bias, lb_param, norm_a, norm_b, sinks, rel_bias,
           w_branch, w_out, final_norm):
    batch, seq, _ = x.shape
    depth = w_in.shape[0]
    x2 = x.reshape(batch * seq, D_MODEL)
    fn = final_norm.reshape(1, D_MODEL)
    for l in range(depth):
        w_mix, wg = _split_w_in(w_in[l])
        nw = norm_w[l].reshape(1, D_MODEL)
        pa, pb, pc = _inproj(x2, nw, w_mix)
        ya = _hgrn2(pa, lb_param, norm_a[l].reshape(1, HEAD_DIM), batch, seq, l)
        yb = _deltanet(pb, conv_w[l], _lane_pad(a_log[l], HEADS), _lane_pad(dt_bias[l], HEADS),
                       norm_b[l].reshape(1, HEAD_DIM), batch, seq)
        yc = _swa(pc, rel_bias, sinks[l], batch, seq)
        x2 = _merge(x2, nw, wg, ya, yb, yc, w_branch[l].astype(BF16), w_out[l].astype(BF16), fn,
                    final=(l == depth - 1))
    return x2.reshape(batch, seq, D_MODEL)
```

```python
import functools
import math

import numpy as np
import jax
import jax.numpy as jnp
from jax import lax
from jax.experimental import pallas as pl
from jax.experimental.pallas import tpu as pltpu

F32 = jnp.float32
BF16 = jnp.bfloat16

D_MODEL = 1024
BRANCH_WIDTH = D_MODEL // 2
N_BRANCHES = 3
EPS = 1e-6
HEADS = 4
HEAD_DIM = BRANCH_WIDTH // HEADS
B_CONV = 4
B_CONV_CH = 3 * BRANCH_WIDTH
C_Q_HEADS = 8
C_KV_HEADS = 2
C_HEAD_DIM = BRANCH_WIDTH // C_Q_HEADS
C_KV_WIDTH = C_KV_HEADS * C_HEAD_DIM
WINDOW = 128
C_BLOCK = 128
C_TILE_BLOCKS = 4
N_BUCKETS = 32
MAX_DISTANCE = 128
LANES = 128
SUBLANES = 8
NEG_BIG = -1e30
LOG2E = math.log2(math.e)

A_CHUNK = 128
A_TILE_CHUNKS = 4
B_CHUNK = 64
B_PAIR = 2 * B_CHUNK
B_TILE = 512
B_SCAN_PAIRS = 2
PROJ_ROWS = 512
SMALL_PAD = LANES
VMEM_LIMIT = 52 * 1024 * 1024

PA_WIDTH = 4 * BRANCH_WIDTH
PB_WIDTH = B_CONV_CH + BRANCH_WIDTH + SMALL_PAD
PC_WIDTH = 2 * BRANCH_WIDTH + 2 * C_KV_WIDTH


def _dot(a, b):
    return jnp.dot(a, b, preferred_element_type=F32)


def _dot_nt(a, b):
    return lax.dot_general(a, b, (((1,), (1,)), ((), ())), preferred_element_type=F32)


def _dot_tn(a, b):
    return lax.dot_general(a, b, (((0,), (0,)), ((), ())), preferred_element_type=F32)


def _sigmoid(x):
    return 0.5 * jnp.tanh(0.5 * x) + 0.5


def _silu(x):
    h = 0.5 * x
    return h * jnp.tanh(h) + h


def _cumsum_rows(tril16, x):
    hi = x.astype(BF16)
    r1 = x - hi.astype(F32)
    mid = r1.astype(BF16)
    lo = (r1 - mid.astype(F32)).astype(BF16)
    return _dot(tril16, hi) + (_dot(tril16, mid) + _dot(tril16, lo))


def _softplus(x):
    return jnp.maximum(x, 0.0) + jnp.log(1.0 + jnp.exp(-jnp.abs(x)))


def _rms(x, w):
    return x * lax.rsqrt(jnp.mean(x * x, axis=-1, keepdims=True) + EPS) * w


def _resident(shape):
    return pl.BlockSpec(shape, lambda i: (0,) * len(shape), pipeline_mode=pl.Buffered(1))


def _layer_resident(shape, layer):
    return pl.BlockSpec((None,) + tuple(shape[1:]), lambda i: (layer,) + (0,) * (len(shape) - 1),
                        pipeline_mode=pl.Buffered(1))


def _inproj_kernel(x_ref, nw_ref, w_ref, pa_ref, pb_ref, pc_ref):
    h = _rms(x_ref[...], nw_ref[...]).astype(BF16)
    base = 0
    for o_ref in (pa_ref, pb_ref, pc_ref):
        n = o_ref.shape[1]
        for j in range(0, n, BRANCH_WIDTH):
            wd = min(BRANCH_WIDTH, n - j)
            o_ref[:, j:j + wd] = _dot(h, w_ref[:, base + j:base + j + wd])
        base += n


def _inproj(x2, nw, w, layer):
    t = x2.shape[0]
    tm = PROJ_ROWS
    return pl.pallas_call(
        _inproj_kernel,
        grid=(t // tm,),
        in_specs=[pl.BlockSpec((tm, D_MODEL), lambda i: (i, 0)),
                  _resident((1, D_MODEL)),
                  _layer_resident(w.shape, layer)],
        out_specs=[pl.BlockSpec((tm, PA_WIDTH), lambda i: (i, 0)),
                   pl.BlockSpec((tm, PB_WIDTH), lambda i: (i, 0)),
                   pl.BlockSpec((tm, PC_WIDTH), lambda i: (i, 0))],
        out_shape=[jax.ShapeDtypeStruct((t, PA_WIDTH), F32),
                   jax.ShapeDtypeStruct((t, PB_WIDTH), F32),
                   jax.ShapeDtypeStruct((t, PC_WIDTH), F32)],
        compiler_params=pltpu.CompilerParams(
            dimension_semantics=("arbitrary",), vmem_limit_bytes=VMEM_LIMIT),
        name="inproj",
    )(x2, nw, w)


def _merge_kernel(x_ref, nw_ref, wg_ref, ya_ref, yb_ref, yc_ref, wbr_ref, wo_ref, fn_ref,
                  o_ref, *, final):
    x = x_ref[...]
    h = _rms(x, nw_ref[...]).astype(BF16)
    merged = None
    for n, y_ref in enumerate((ya_ref, yb_ref, yc_ref)):
        gate = _sigmoid(_dot(h, wg_ref[:, n * D_MODEL:(n + 1) * D_MODEL]))
        term = gate * _dot(y_ref[...], wbr_ref[n])
        merged = term if merged is None else merged + term
    out = x + _dot(merged.astype(BF16), wo_ref[...])
    if final:
        out = _rms(out, fn_ref[...])
    o_ref[...] = out


def _merge(x2, nw, wg, ya, yb, yc, wbr, wo, fn, layer, final):
    t = x2.shape[0]
    tm = PROJ_ROWS
    row = lambda i: (i, 0)
    return pl.pallas_call(
        functools.partial(_merge_kernel, final=final),
        grid=(t // tm,),
        in_specs=[pl.BlockSpec((tm, D_MODEL), row),
                  _resident((1, D_MODEL)),
                  _layer_resident(wg.shape, layer),
                  pl.BlockSpec((tm, BRANCH_WIDTH), row),
                  pl.BlockSpec((tm, BRANCH_WIDTH), row),
                  pl.BlockSpec((tm, BRANCH_WIDTH), row),
                  _layer_resident(wbr.shape, layer),
                  _layer_resident(wo.shape, layer),
                  _resident((1, D_MODEL))],
        out_specs=pl.BlockSpec((tm, D_MODEL), row),
        out_shape=jax.ShapeDtypeStruct((t, D_MODEL), F32),
        compiler_params=pltpu.CompilerParams(
            dimension_semantics=("arbitrary",), vmem_limit_bytes=VMEM_LIMIT),
        name="merge",
    )(x2, nw, wg, ya, yb, yc, wbr, wo, fn)


def _a_levels():
    return [1 << i for i in range(int(math.log2(A_CHUNK)))]


def _a_level_ids():
    idx = np.arange(A_CHUNK)
    ids = np.where(idx[:, None] == idx[None, :], 0, -1).astype(np.int32)
    for i, m in enumerate(_a_levels()):
        blk = idx // (2 * m)
        upper = (idx & m) != 0
        ids[(blk[:, None] == blk[None, :]) & upper[:, None] & ~upper[None, :]] = i + 1
    return ids


def _hgrn2_kernel(pa_ref, lbp_ref, nw_ref, tril_ref, lvl_ref, o_ref,
                  state_ref, b_ref, gp_ref, *, layer):
    c = A_CHUNK
    bw = BRANCH_WIDTH

    @pl.when(pl.program_id(1) == 0)
    def _():
        state_ref[...] = jnp.zeros_like(state_ref)
        gp_ref[...] = jnp.zeros_like(gp_ref)

    lbp = lbp_ref[...]
    ex = jnp.exp(lbp - jnp.max(lbp, axis=0, keepdims=True))
    sm = ex / jnp.sum(ex, axis=0, keepdims=True)
    lb = jnp.zeros((1, bw), F32)
    for j in range(1, layer + 1):
        lb = lb + sm[j:j + 1, :]

    hsl = [slice(h * HEAD_DIM, (h + 1) * HEAD_DIM) for h in range(HEADS)]
    row = lax.broadcasted_iota(jnp.int32, (c, 1), 0)
    lvl = lvl_ref[...]
    t1 = jnp.log(lb)
    t2_lb = jnp.log1p(-lb)

    def intra_chunk(ci):
        rows = slice(ci * c, (ci + 1) * c)
        bc_ref = b_ref.at[ci]
        gc_ref = gp_ref.at[ci]
        q = _silu(pa_ref[rows, 0:bw])
        z = pa_ref[rows, bw:2 * bw]
        v = pa_ref[rows, 2 * bw:3 * bw].astype(BF16)
        log_sig = jnp.minimum(z, 0.0) - jnp.log(1.0 + jnp.exp(-jnp.abs(z)))
        sig_neg = 0.5 - 0.5 * jnp.tanh(0.5 * z)
        t2 = t2_lb + log_sig
        lf = jnp.maximum(t1, t2) + jnp.log(1.0 + jnp.exp(-jnp.abs(t1 - t2)))
        k = (1.0 - lb) * sig_neg

        lf2 = lf * LOG2E
        b = _cumsum_rows(tril_ref[...], lf2)
        bc_ref[...] = b
        gc_ref[SUBLANES:SUBLANES + c, :] = lf2

        q16 = q.astype(BF16)
        k16 = k.astype(BF16)
        diag = lvl == 0
        acc = [jnp.where(diag, _dot_nt(q16[:, sl], k16[:, sl]), 0.0) for sl in hsl]

        for li, m in enumerate(_a_levels()):
            blocks = range(0, c, 2 * m)
            if m == 1:
                neg = jnp.where((row & 1) == 1, lf2, 0.0)
            elif m == 2:
                r4 = row & 3
                g_next = gc_ref[SUBLANES + 1:SUBLANES + 1 + c, :]
                g_prev = gc_ref[SUBLANES - 1:SUBLANES - 1 + c, :]
                neg = jnp.where(r4 == 0, g_next, jnp.where(r4 == 1, 0.0, jnp.where(r4 == 2, lf2, lf2 + g_prev)))
            elif m < SUBLANES:
                neg = jnp.concatenate(
                    [-jnp.abs(bc_ref[lo:lo + 2 * m, :] - bc_ref[lo + m - 1:lo + m, :]) for lo in blocks], axis=0)
            else:
                pieces = []
                for lo in blocks:
                    anchor = bc_ref[lo + m - 1:lo + m, :]
                    pieces += [anchor - bc_ref[lo:lo + m, :], bc_ref[lo + m:lo + 2 * m, :] - anchor]
                neg = jnp.concatenate(pieces, axis=0)
            if m < SUBLANES:
                x = jnp.where((row & m) != 0, q, k)
            else:
                x = jnp.concatenate(
                    [t for lo in blocks for t in (k[lo:lo + m, :], q[lo + m:lo + 2 * m, :])], axis=0)
            xt = (x * jnp.exp2(neg)).astype(BF16)
            sel = lvl == li + 1
            for h, sl in enumerate(hsl):
                acc[h] = jnp.where(sel, _dot_nt(xt[:, sl], xt[:, sl]), acc[h])

        b_end = bc_ref[c - 1:c, :]
        qd = (q * jnp.exp2(b)).astype(BF16)
        kd = (k * jnp.exp2(b_end - b)).astype(BF16)
        intra = [_dot(acc[h].astype(BF16), v[:, hsl[h]]) for h in range(HEADS)]
        upd = [_dot_tn(v[:, hsl[h]], kd[:, hsl[h]]) for h in range(HEADS)]
        return intra, upd, qd, jnp.exp2(b_end)

    parts = [intra_chunk(ci) for ci in range(A_TILE_CHUNKS)]
    nw = nw_ref[...]
    sts = [state_ref[h] for h in range(HEADS)]
    for ci, (intra, upd, qd, s_decay) in enumerate(parts):
        rows = slice(ci * c, (ci + 1) * c)
        inter = [_dot_nt(qd[:, hsl[h]], sts[h].astype(BF16)) for h in range(HEADS)]
        sts = [sts[h] * s_decay[:, hsl[h]] + upd[h] for h in range(HEADS)]
        for h, sl in enumerate(hsl):
            o = intra[h] + inter[h]
            y = o * lax.rsqrt(jnp.mean(o * o, axis=-1, keepdims=True) + EPS) * nw
            gate = pa_ref[rows, 3 * bw + h * HEAD_DIM:3 * bw + (h + 1) * HEAD_DIM]
            o_ref[rows, sl] = (y * _silu(gate)).astype(BF16)
    for h in range(HEADS):
        state_ref[h] = sts[h]


def _hgrn2(pa, lbp, nw, batch, seq, layer):
    c = A_CHUNK
    tc = A_TILE_CHUNKS
    nt = seq // (c * tc)
    tril = jnp.asarray(np.tril(np.ones((c, c), np.float32)), dtype=BF16)
    lvl = jnp.asarray(_a_level_ids())
    const2 = lambda b, i: (0, 0)
    return pl.pallas_call(
        functools.partial(_hgrn2_kernel, layer=layer),
        grid=(batch, nt),
        in_specs=[pl.BlockSpec((tc * c, PA_WIDTH), lambda b, i: (b * nt + i, 0)),
                  pl.BlockSpec(lbp.shape, const2),
                  pl.BlockSpec((1, HEAD_DIM), const2),
                  pl.BlockSpec((c, c), const2),
                  pl.BlockSpec((c, c), const2)],
        out_specs=pl.BlockSpec((tc * c, BRANCH_WIDTH), lambda b, i: (b * nt + i, 0)),
        out_shape=jax.ShapeDtypeStruct((batch * seq, BRANCH_WIDTH), BF16),
        scratch_shapes=[pltpu.VMEM((HEADS, HEAD_DIM, HEAD_DIM), F32),
                        pltpu.VMEM((tc, c, BRANCH_WIDTH), F32),
                        pltpu.VMEM((tc, c + 2 * SUBLANES, BRANCH_WIDTH), F32)],
        compiler_params=pltpu.CompilerParams(
            dimension_semantics=("arbitrary", "arbitrary"), vmem_limit_bytes=VMEM_LIMIT),
        name="hgrn2",
    )(pa, lbp, nw, tril, lvl)


def _unit_lower_inverses(mats, ti, si):
    eye = (ti == si).astype(F32)
    ts = [eye] * len(mats)
    s = 1
    while s < B_CHUNK:
        shift = int(math.log2(2 * s))
        off = ((ti >> shift) == (si >> shift)) & ((ti & s) != 0) & ((si & s) == 0)
        a_off = [jnp.where(off, a, 0.0) for a in mats]
        if s == 1:
            ts = [t - ao for t, ao in zip(ts, a_off)]
        else:
            t16 = [t.astype(BF16) for t in ts]
            inner = [_dot(ao.astype(BF16), t) for ao, t in zip(a_off, t16)]
            outer = [_dot(t, x.astype(BF16)) for t, x in zip(t16, inner)]
            ts = [t - x for t, x in zip(ts, outer)]
        s *= 2
    return ts


def _dn_prep_kernel(pb_ref, halo_ref, cw_ref, alog_ref, dtb_ref, tril_ref,
                    u_ref, w_ref, qd_ref, kd_ref, qk_ref, ge_ref, xp_ref, qkv_ref, *, tiles_per_seq):
    ts = B_TILE
    c = B_CHUNK
    pr = B_PAIR
    bw = BRANCH_WIDTH
    sl8 = SUBLANES

    keep = jnp.where(pl.program_id(0) % tiles_per_seq == 0, 0.0, 1.0)
    half = ts // 2
    for s in range(B_CONV_CH // LANES):
        lanes = slice(s * LANES, (s + 1) * LANES)
        xp_ref[s, 0:sl8, :] = halo_ref[:, lanes] * keep
        xp_ref[s, sl8:sl8 + ts, :] = pb_ref[:, lanes]
    for s in range(B_CONV_CH // LANES):
        lanes = slice(s * LANES, (s + 1) * LANES)
        cw = [cw_ref[j:j + 1, lanes] for j in range(B_CONV)]
        ld = {off: xp_ref[s, pl.ds(sl8 + off, half, stride=2), :] for off in range(1 - B_CONV, 2)}
        even = None
        odd = None
        for j in range(B_CONV):
            te = cw[j] * ld[j - (B_CONV - 1)]
            to = cw[j] * ld[j - (B_CONV - 1) + 1]
            even = te if even is None else even + te
            odd = to if odd is None else odd + to
        qkv_ref[s, pl.ds(0, half, stride=2), :] = _silu(even)
        qkv_ref[s, pl.ds(1, half, stride=2), :] = _silu(odd)

    small = pb_ref[:, B_CONV_CH + bw:B_CONV_CH + bw + SMALL_PAD]
    beta_all = _sigmoid(small)
    g_all = -jnp.exp(alog_ref[...]) * _softplus(small + dtb_ref[...])
    gcum = jnp.concatenate(
        [_cumsum_rows(tril_ref[...], g_all[p * pr:(p + 1) * pr, :]) for p in range(ts // pr)], axis=0)
    gcum_t = gcum.T
    for j in range(ts // c):
        ge_ref[0, j:j + 1, :] = jnp.exp(gcum[j * c + c - 1:j * c + c, :])

    ti = lax.broadcasted_iota(jnp.int32, (pr, pr), 0)
    si = lax.broadcasted_iota(jnp.int32, (pr, pr), 1)
    same = (ti >> int(math.log2(c))) == (si >> int(math.log2(c)))
    incl = same & (ti >= si)
    strict = same & (ti > si)
    first_chunk = lax.broadcasted_iota(jnp.int32, (pr, 1), 0) < c

    probs = [(h, p) for h in range(HEADS) for p in range(ts // pr)]
    k16s, kb16s, q16s, decays, rhs = [], [], [], [], []
    for h, p in probs:
        sl = slice(h * HEAD_DIM, (h + 1) * HEAD_DIM)
        rows = slice(p * pr, (p + 1) * pr)
        q = qkv_ref[h, rows, :]
        k = qkv_ref[HEADS + h, rows, :]
        v = qkv_ref[2 * HEADS + h, rows, :]
        q = q * lax.rsqrt(jnp.sum(q * q, axis=-1, keepdims=True) + EPS) * (HEAD_DIM ** -0.5)
        k = k * lax.rsqrt(jnp.sum(k * k, axis=-1, keepdims=True) + EPS)
        beta = beta_all[rows, h:h + 1]
        gc = gcum[rows, HEADS + h:HEADS + h + 1]
        gr = gcum_t[HEADS + h:HEADS + h + 1, rows]
        g_last = jnp.where(first_chunk, gc[c - 1:c, :], gc[pr - 1:pr, :])
        egc = jnp.exp(gc)
        kb = k * beta
        decays.append(jnp.where(incl, jnp.exp(jnp.minimum(gc - gr, 0.0)), 0.0))
        k16s.append(k.astype(BF16))
        kb16s.append(kb.astype(BF16))
        q16s.append(q.astype(BF16))
        rhs.append(jnp.concatenate([v * beta, kb * egc], axis=1).astype(BF16))
        qd_ref[rows, sl] = (q * egc).astype(BF16)
        kd_ref[rows, sl] = (k * jnp.exp(g_last - gc)).astype(BF16)

    kk = [_dot_nt(kb16, k16) for kb16, k16 in zip(kb16s, k16s)]
    qk = [_dot_nt(q16, k16) for q16, k16 in zip(q16s, k16s)]
    mats = [jnp.where(strict, x * d, 0.0) for x, d in zip(kk, decays)]
    tinv = _unit_lower_inverses(mats, ti, si)
    uw = [_dot(t.astype(BF16), r) for t, r in zip(tinv, rhs)]
    for i, (h, p) in enumerate(probs):
        sl = slice(h * HEAD_DIM, (h + 1) * HEAD_DIM)
        rows = slice(p * pr, (p + 1) * pr)
        u_ref[rows, sl] = uw[i][:, 0:HEAD_DIM]
        w_ref[rows, sl] = uw[i][:, HEAD_DIM:].astype(BF16)
        qk_ref[p, h] = (qk[i] * decays[i]).astype(BF16)


def _dn_scan_kernel(u_ref, w_ref, qd_ref, kd_ref, qk_ref, ge_ref, z_ref, nw_ref, o_ref, state_ref):
    c = B_CHUNK
    batch = u_ref.shape[0]

    @pl.when(pl.program_id(0) == 0)
    def _():
        state_ref[...] = jnp.zeros_like(state_ref)

    nw = nw_ref[...]
    chains = [(b, h) for b in range(batch) for h in range(HEADS)]
    hsl = lambda h: slice(h * HEAD_DIM, (h + 1) * HEAD_DIM)
    states = [state_ref[b * HEADS + h] for b, h in chains]
    for pi in range(B_SCAN_PAIRS):
        o_inter = [[] for _ in chains]
        v_new = [[] for _ in chains]
        for j in range(B_PAIR // c):
            rows = slice(pi * B_PAIR + j * c, pi * B_PAIR + (j + 1) * c)
            st16 = [st.astype(BF16) for st in states]
            lhs = [jnp.concatenate([w_ref[b, rows, hsl(h)], qd_ref[b, rows, hsl(h)]], axis=0)
                   for b, h in chains]
            prod = [_dot(x, s) for x, s in zip(lhs, st16)]
            vn16 = []
            for i, (b, h) in enumerate(chains):
                vn = (u_ref[b, rows, hsl(h)] - prod[i][0:c, :]).astype(BF16)
                vn16.append(vn)
                v_new[i].append(vn)
                o_inter[i].append(prod[i][c:, :])
            upd = [_dot_tn(kd_ref[b, rows, hsl(h)], vn) for (b, h), vn in zip(chains, vn16)]
            states = [st * ge_ref[b, pi, j:j + 1, HEADS + h:HEADS + h + 1] + x
                      for (b, h), st, x in zip(chains, states, upd)]
        intra = [_dot(qk_ref[b, pi, h], jnp.concatenate(v_new[i], axis=0)) for i, (b, h) in enumerate(chains)]
        prows = slice(pi * B_PAIR, (pi + 1) * B_PAIR)
        for i, (b, h) in enumerate(chains):
            o = jnp.concatenate(o_inter[i], axis=0) + intra[i]
            y = o * lax.rsqrt(jnp.mean(o * o, axis=-1, keepdims=True) + EPS) * nw
            o_ref[b, prows, hsl(h)] = (y * _silu(z_ref[b, prows, hsl(h)])).astype(BF16)
    for i, (b, h) in enumerate(chains):
        state_ref[b * HEADS + h] = states[i]


def _deltanet(pb, cw, alog, dtb, nw, batch, seq):
    c = B_CHUNK
    ts = B_TILE
    pr = B_PAIR
    bw = BRANCH_WIDTH
    t = batch * seq
    n_tiles = t // ts
    tril = jnp.asarray(np.kron(np.eye(pr // c, dtype=np.float32), np.tril(np.ones((c, c), np.float32))),
                       dtype=BF16)
    const = lambda i: (0, 0)
    row = lambda i: (i, 0)
    halo_blocks = ts // SUBLANES
    u, w, qd, kd, qk, ge = pl.pallas_call(
        functools.partial(_dn_prep_kernel, tiles_per_seq=seq // ts),
        grid=(n_tiles,),
        in_specs=[pl.BlockSpec((ts, PB_WIDTH), row),
                  pl.BlockSpec((SUBLANES, B_CONV_CH), lambda i: (jnp.maximum(i * halo_blocks - 1, 0), 0)),
                  pl.BlockSpec(cw.shape, const),
                  pl.BlockSpec((1, SMALL_PAD), const),
                  pl.BlockSpec((1, SMALL_PAD), const),
                  pl.BlockSpec((pr, pr), const)],
        out_specs=[pl.BlockSpec((ts, bw), row),
                   pl.BlockSpec((ts, bw), row),
                   pl.BlockSpec((ts, bw), row),
                   pl.BlockSpec((ts, bw), row),
                   pl.BlockSpec((ts // pr, HEADS, pr, pr), lambda i: (i, 0, 0, 0)),
                   pl.BlockSpec((1, ts // c, SMALL_PAD), lambda i: (i, 0, 0))],
        out_shape=[jax.ShapeDtypeStruct((t, bw), F32),
                   jax.ShapeDtypeStruct((t, bw), BF16),
                   jax.ShapeDtypeStruct((t, bw), BF16),
                   jax.ShapeDtypeStruct((t, bw), BF16),
                   jax.ShapeDtypeStruct((t // pr, HEADS, pr, pr), BF16),
                   jax.ShapeDtypeStruct((n_tiles, ts // c, SMALL_PAD), F32)],
        scratch_shapes=[pltpu.VMEM((B_CONV_CH // LANES, ts + SUBLANES, LANES), F32),
                        pltpu.VMEM((B_CONV_CH // LANES, ts, LANES), F32)],
        compiler_params=pltpu.CompilerParams(
            dimension_semantics=("arbitrary",), vmem_limit_bytes=VMEM_LIMIT),
        name="dn_prep",
    )(pb, pb, cw, alog, dtb, tril)

    np_seq = seq // pr
    seq3 = lambda x: x.reshape(batch, seq, x.shape[-1])
    sp = B_SCAN_PAIRS
    blk3 = pl.BlockSpec((batch, sp * pr, bw), lambda i: (0, i, 0))
    y = pl.pallas_call(
        _dn_scan_kernel,
        grid=(np_seq // sp,),
        in_specs=[blk3, blk3, blk3, blk3,
                  pl.BlockSpec((batch, sp, HEADS, pr, pr), lambda i: (0, i, 0, 0, 0)),
                  pl.BlockSpec((batch, sp, pr // c, SMALL_PAD), lambda i: (0, i, 0, 0)),
                  pl.BlockSpec((batch, sp * pr, bw), lambda i: (0, i, B_CONV_CH // bw)),
                  pl.BlockSpec((1, HEAD_DIM), const)],
        out_specs=blk3,
        out_shape=jax.ShapeDtypeStruct((batch, seq, bw), BF16),
        scratch_shapes=[pltpu.VMEM((batch * HEADS, HEAD_DIM, HEAD_DIM), F32)],
        compiler_params=pltpu.CompilerParams(
            dimension_semantics=("arbitrary",), vmem_limit_bytes=VMEM_LIMIT),
        name="dn_scan",
    )(seq3(u), seq3(w), seq3(qd), seq3(kd),
      qk.reshape(batch, np_seq, HEADS, pr, pr),
      ge.reshape(batch, np_seq, pr // c, SMALL_PAD),
      seq3(pb), nw)
    return y.reshape(t, bw)


def _band_buckets():
    i = np.arange(C_BLOCK)[:, None]
    j = np.arange(2 * C_BLOCK)[None, :]
    dist = i + C_BLOCK - j
    max_exact = N_BUCKETS // 2
    d_f = np.maximum(dist, 1).astype(np.float32)
    large = max_exact + (np.log(d_f / np.float32(max_exact)) / np.float32(math.log(MAX_DISTANCE / max_exact))
                         * np.float32(N_BUCKETS - max_exact)).astype(np.int32)
    large = np.minimum(large, N_BUCKETS - 1)
    bucket = np.where(dist < max_exact, dist, large)
    return np.where((dist >= 0) & (dist < WINDOW), bucket, -1).astype(np.int32)


def _swa_kernel(rb_ref, sink_ref, cur_ref, prev_ref, bucket_ref, o_ref, bias_ref):
    blk = C_BLOCK
    bw = BRANCH_WIDTH
    n = pl.program_id(1)

    @pl.when((pl.program_id(0) == 0) & (n == 0))
    def _():
        bucket = bucket_ref[...]
        for h in range(C_Q_HEADS):
            acc = jnp.full((blk, 2 * blk), NEG_BIG, F32)
            for bk in range(N_BUCKETS):
                acc = jnp.where(bucket == bk, rb_ref[bk, h], acc)
            bias_ref[h] = acc

    kv = jnp.concatenate([prev_ref[...], cur_ref[:, bw:bw + 2 * C_KV_WIDTH]], axis=0)
    rows = kv.shape[0]
    lane = lax.broadcasted_iota(jnp.int32, (rows, C_KV_WIDTH), 1)

    def halves(x, j):
        own = jnp.where((lane >= j * C_HEAD_DIM) & (lane < (j + 1) * C_HEAD_DIM), x, 0.0)
        other = pltpu.roll(own, C_HEAD_DIM, 1)
        pair = (own, other) if j == 0 else (other, own)
        return [t.astype(BF16) for t in pair]

    kz = [halves(kv[:, 0:C_KV_WIDTH], j) for j in range(C_KV_HEADS)]
    vz = [halves(kv[:, C_KV_WIDTH:], j) for j in range(C_KV_HEADS)]
    scale = C_HEAD_DIM ** -0.5
    group = C_Q_HEADS // C_KV_HEADS
    key_ok = lax.broadcasted_iota(jnp.int32, (1, 2 * blk), 1) >= jnp.where(n > 0, 0, blk)

    chains = [(i, h) for i in range(C_TILE_BLOCKS) for h in range(C_Q_HEADS)]
    qrows = lambda i: slice(i * blk, (i + 1) * blk)
    krows = lambda i: slice(i * blk, (i + 2) * blk)
    q2 = [[(cur_ref[qrows(i), p * LANES:(p + 1) * LANES] * scale).astype(BF16)
           for p in range(C_Q_HEADS // 2)] for i in range(C_TILE_BLOCKS)]
    logits = [_dot_nt(q2[i][h // 2], kz[h // group][h % 2][krows(i), :]) + bias_ref[h] for i, h in chains]
    logits = [jnp.where(key_ok, lg, NEG_BIG) if i == 0 else lg for (i, h), lg in zip(chains, logits)]
    mx = [jnp.maximum(jnp.max(lg, axis=-1, keepdims=True), sink_ref[h]) for (i, h), lg in zip(chains, logits)]
    pr = [jnp.exp(lg - m) for lg, m in zip(logits, mx)]
    den = [jnp.sum(x, axis=-1, keepdims=True) + jnp.exp(sink_ref[h] - m) for (i, h), x, m in zip(chains, pr, mx)]
    pv = [_dot(x.astype(BF16), vz[h // group][h % 2][krows(i), :]) for (i, h), x in zip(chains, pr)]
    for i in range(C_TILE_BLOCKS):
        for p in range(C_Q_HEADS // 2):
            a, b = i * C_Q_HEADS + 2 * p, i * C_Q_HEADS + 2 * p + 1
            out = pv[a] / den[a] + pv[b] / den[b]
            gate = cur_ref[qrows(i), bw + 2 * C_KV_WIDTH + p * LANES:bw + 2 * C_KV_WIDTH + (p + 1) * LANES]
            o_ref[qrows(i), p * LANES:(p + 1) * LANES] = (out * _silu(gate)).astype(BF16)


def _swa(pc, rel_bias, sinks, batch, seq):
    blk = C_BLOCK
    nb = seq // blk
    bucket = jnp.asarray(_band_buckets())
    kv_col = BRANCH_WIDTH // (2 * C_KV_WIDTH)
    smem = pl.BlockSpec(memory_space=pltpu.SMEM)
    tb = C_TILE_BLOCKS
    nt = nb // tb
    return pl.pallas_call(
        _swa_kernel,
        grid=(batch, nt),
        in_specs=[smem, smem,
                  pl.BlockSpec((tb * blk, PC_WIDTH), lambda b, i: (b * nt + i, 0)),
                  pl.BlockSpec((blk, 2 * C_KV_WIDTH), lambda b, i: (b * nb + jnp.maximum(i * tb - 1, 0), kv_col)),
                  pl.BlockSpec((blk, 2 * blk), lambda b, i: (0, 0))],
        out_specs=pl.BlockSpec((tb * blk, BRANCH_WIDTH), lambda b, i: (b * nt + i, 0)),
        out_shape=jax.ShapeDtypeStruct((batch * seq, BRANCH_WIDTH), BF16),
        scratch_shapes=[pltpu.VMEM((C_Q_HEADS, blk, 2 * blk), F32)],
        compiler_params=pltpu.CompilerParams(
            dimension_semantics=("arbitrary", "arbitrary"), vmem_limit_bytes=VMEM_LIMIT),
        name="swa",
    )(rel_bias, sinks, pc, pc, bucket)


W_AB = PA_WIDTH + PB_WIDTH - (SMALL_PAD - 2 * HEADS)
W_ABC = W_AB + PC_WIDTH
W_ROWS = 256


def _wprep_kernel(w_ref, mix_ref, gate_ref):
    w = w_ref[0]
    mix_ref[0, :, 0:W_AB] = w[:, 0:W_AB].astype(BF16)
    mix_ref[0, :, W_AB:PA_WIDTH + PB_WIDTH] = jnp.zeros((w.shape[0], PA_WIDTH + PB_WIDTH - W_AB), BF16)
    mix_ref[0, :, PA_WIDTH + PB_WIDTH:] = w[:, W_AB:W_ABC].astype(BF16)
    gate_ref[0] = w[:, W_ABC:].astype(BF16)


def _wprep(w_in):
    depth, rows, n_in = w_in.shape
    n_mix = PA_WIDTH + PB_WIDTH + PC_WIDTH
    blk = lambda n: pl.BlockSpec((1, W_ROWS, n), lambda l, i: (l, i, 0))
    return pl.pallas_call(
        _wprep_kernel,
        grid=(depth, rows // W_ROWS),
        in_specs=[blk(n_in)],
        out_specs=[blk(n_mix), blk(n_in - W_ABC)],
        out_shape=[jax.ShapeDtypeStruct((depth, rows, n_mix), BF16),
                   jax.ShapeDtypeStruct((depth, rows, n_in - W_ABC), BF16)],
        compiler_params=pltpu.CompilerParams(
            dimension_semantics=("arbitrary", "arbitrary"), vmem_limit_bytes=VMEM_LIMIT),
        name="wprep",
    )(w_in)


def _lane_pad(v, offset):
    return jnp.zeros((1, SMALL_PAD), F32).at[0, offset:offset + v.shape[0]].set(v.astype(F32))


def kernel(x, norm_w, w_in, conv_w, a_log, dt_bias, lb_param, norm_a, norm_b, sinks, rel_bias,
           w_branch, w_out, final_norm):
    batch, seq, _ = x.shape
    depth = w_in.shape[0]
    x2 = x.reshape(batch * seq, D_MODEL)
    fn = final_norm.reshape(1, D_MODEL)
    w_mix, wg = _wprep(w_in)
    wbr = w_branch.astype(BF16)
    wo = w_out.astype(BF16)
    for l in range(depth):
        nw = norm_w[l].reshape(1, D_MODEL)
        pa, pb, pc = _inproj(x2, nw, w_mix, l)
        ya = _hgrn2(pa, lb_param, norm_a[l].reshape(1, HEAD_DIM), batch, seq, l)
        yb = _deltanet(pb, conv_w[l], _lane_pad(a_log[l], HEADS), _lane_pad(dt_bias[l], HEADS),
                       norm_b[l].reshape(1, HEAD_DIM), batch, seq)
        yc = _swa(pc, rel_bias, sinks[l], batch, seq)
        x2 = _merge(x2, nw, wg, ya, yb, yc, wbr, wo, fn, layer=l, final=(l == depth - 1))
    return x2.reshape(batch, seq, D_MODEL)
```

```python
import functools
import math

import numpy as np
import jax
import jax.numpy as jnp
from jax import lax
from jax.experimental import pallas as pl
from jax.experimental.pallas import tpu as pltpu

F32 = jnp.float32
BF16 = jnp.bfloat16

D_MODEL = 1024
BRANCH_WIDTH = D_MODEL // 2
N_BRANCHES = 3
EPS = 1e-6
HEADS = 4
HEAD_DIM = BRANCH_WIDTH // HEADS
B_CONV = 4
B_CONV_CH = 3 * BRANCH_WIDTH
C_Q_HEADS = 8
C_KV_HEADS = 2
C_HEAD_DIM = BRANCH_WIDTH // C_Q_HEADS
C_KV_WIDTH = C_KV_HEADS * C_HEAD_DIM
WINDOW = 128
C_BLOCK = 128
C_TILE_BLOCKS = 4
N_BUCKETS = 32
MAX_DISTANCE = 128
LANES = 128
SUBLANES = 8
NEG_BIG = -1e30
LOG2E = math.log2(math.e)

A_CHUNK = 128
A_TILE_CHUNKS = 4
B_CHUNK = 64
B_PAIR = 2 * B_CHUNK
B_TILE = 512
B_SCAN_PAIRS = 2
PROJ_ROWS = 512
SMALL_PAD = LANES
VMEM_LIMIT = 52 * 1024 * 1024

PA_WIDTH = 4 * BRANCH_WIDTH
PB_WIDTH = B_CONV_CH + BRANCH_WIDTH + SMALL_PAD
PC_WIDTH = 2 * BRANCH_WIDTH + 2 * C_KV_WIDTH


def _dot(a, b):
    return jnp.dot(a, b, preferred_element_type=F32)


def _dot_nt(a, b):
    return lax.dot_general(a, b, (((1,), (1,)), ((), ())), preferred_element_type=F32)


def _dot_tn(a, b):
    return lax.dot_general(a, b, (((0,), (0,)), ((), ())), preferred_element_type=F32)


def _sigmoid(x):
    return 0.5 * jnp.tanh(0.5 * x) + 0.5


def _silu(x):
    h = 0.5 * x
    return h * jnp.tanh(h) + h


def _cumsum_rows(tril16, x):
    hi = x.astype(BF16)
    r1 = x - hi.astype(F32)
    mid = r1.astype(BF16)
    lo = (r1 - mid.astype(F32)).astype(BF16)
    return _dot(tril16, hi) + (_dot(tril16, mid) + _dot(tril16, lo))


def _softplus(x):
    return jnp.maximum(x, 0.0) + jnp.log(1.0 + jnp.exp(-jnp.abs(x)))


def _rms(x, w):
    return x * lax.rsqrt(jnp.mean(x * x, axis=-1, keepdims=True) + EPS) * w


def _resident(shape):
    return pl.BlockSpec(shape, lambda i: (0,) * len(shape), pipeline_mode=pl.Buffered(1))


def _layer_resident(shape, layer):
    return pl.BlockSpec((None,) + tuple(shape[1:]), lambda i: (layer,) + (0,) * (len(shape) - 1),
                        pipeline_mode=pl.Buffered(1))


def _inproj_kernel(x_ref, nw_ref, w_ref, pa_ref, pb_ref, pc_ref):
    h = _rms(x_ref[...], nw_ref[...]).astype(BF16)
    base = 0
    for o_ref in (pa_ref, pb_ref, pc_ref):
        n = o_ref.shape[1]
        for j in range(0, n, BRANCH_WIDTH):
            wd = min(BRANCH_WIDTH, n - j)
            o_ref[:, j:j + wd] = _dot(h, w_ref[:, base + j:base + j + wd])
        base += n


def _inproj(x2, nw, w, layer):
    t = x2.shape[0]
    tm = PROJ_ROWS
    return pl.pallas_call(
        _inproj_kernel,
        grid=(t // tm,),
        in_specs=[pl.BlockSpec((tm, D_MODEL), lambda i: (i, 0)),
                  _resident((1, D_MODEL)),
                  _layer_resident(w.shape, layer)],
        out_specs=[pl.BlockSpec((tm, PA_WIDTH), lambda i: (i, 0)),
                   pl.BlockSpec((tm, PB_WIDTH), lambda i: (i, 0)),
                   pl.BlockSpec((tm, PC_WIDTH), lambda i: (i, 0))],
        out_shape=[jax.ShapeDtypeStruct((t, PA_WIDTH), F32),
                   jax.ShapeDtypeStruct((t, PB_WIDTH), F32),
                   jax.ShapeDtypeStruct((t, PC_WIDTH), F32)],
        compiler_params=pltpu.CompilerParams(
            dimension_semantics=("arbitrary",), vmem_limit_bytes=VMEM_LIMIT),
        name="inproj",
    )(x2, nw, w)


def _merge_kernel(x_ref, nw_ref, wg_ref, ya_ref, yb_ref, yc_ref, wbr_ref, wo_ref, fn_ref,
                  o_ref, *, final):
    x = x_ref[...]
    h = _rms(x, nw_ref[...]).astype(BF16)
    merged = None
    for n, y_ref in enumerate((ya_ref, yb_ref, yc_ref)):
        gate = _sigmoid(_dot(h, wg_ref[:, n * D_MODEL:(n + 1) * D_MODEL]))
        term = gate * _dot(y_ref[...], wbr_ref[n])
        merged = term if merged is None else merged + term
    out = x + _dot(merged.astype(BF16), wo_ref[...])
    if final:
        out = _rms(out, fn_ref[...])
    o_ref[...] = out


def _merge(x2, nw, wg, ya, yb, yc, wbr, wo, fn, layer, final):
    t = x2.shape[0]
    tm = PROJ_ROWS
    row = lambda i: (i, 0)
    return pl.pallas_call(
        functools.partial(_merge_kernel, final=final),
        grid=(t // tm,),
        in_specs=[pl.BlockSpec((tm, D_MODEL), row),
                  _resident((1, D_MODEL)),
                  _layer_resident(wg.shape, layer),
                  pl.BlockSpec((tm, BRANCH_WIDTH), row),
                  pl.BlockSpec((tm, BRANCH_WIDTH), row),
                  pl.BlockSpec((tm, BRANCH_WIDTH), row),
                  _layer_resident(wbr.shape, layer),
                  _layer_resident(wo.shape, layer),
                  _resident((1, D_MODEL))],
        out_specs=pl.BlockSpec((tm, D_MODEL), row),
        out_shape=jax.ShapeDtypeStruct((t, D_MODEL), F32),
        compiler_params=pltpu.CompilerParams(
            dimension_semantics=("arbitrary",), vmem_limit_bytes=VMEM_LIMIT),
        name="merge",
    )(x2, nw, wg, ya, yb, yc, wbr, wo, fn)


def _a_levels():
    return [1 << i for i in range(int(math.log2(A_CHUNK)))]


def _a_level_ids():
    idx = np.arange(A_CHUNK)
    ids = np.where(idx[:, None] == idx[None, :], 0, -1).astype(np.int32)
    for i, m in enumerate(_a_levels()):
        blk = idx // (2 * m)
        upper = (idx & m) != 0
        ids[(blk[:, None] == blk[None, :]) & upper[:, None] & ~upper[None, :]] = i + 1
    return ids


def _hgrn2_kernel(pa_ref, lbp_ref, nw_ref, tril_ref, lvl_ref, o_ref,
                  state_ref, b_ref, gp_ref, *, layer):
    c = A_CHUNK
    bw = BRANCH_WIDTH

    @pl.when(pl.program_id(1) == 0)
    def _():
        state_ref[...] = jnp.zeros_like(state_ref)
        gp_ref[...] = jnp.zeros_like(gp_ref)

    lbp = lbp_ref[...]
    ex = jnp.exp(lbp - jnp.max(lbp, axis=0, keepdims=True))
    sm = ex / jnp.sum(ex, axis=0, keepdims=True)
    lb = jnp.zeros((1, bw), F32)
    for j in range(1, layer + 1):
        lb = lb + sm[j:j + 1, :]

    hsl = [slice(h * HEAD_DIM, (h + 1) * HEAD_DIM) for h in range(HEADS)]
    row = lax.broadcasted_iota(jnp.int32, (c, 1), 0)
    lvl = lvl_ref[...]
    t1 = jnp.log(lb)
    t2_lb = jnp.log1p(-lb)

    def intra_chunk(ci):
        rows = slice(ci * c, (ci + 1) * c)
        bc_ref = b_ref.at[ci]
        gc_ref = gp_ref.at[ci]
        q = _silu(pa_ref[rows, 0:bw])
        z = pa_ref[rows, bw:2 * bw]
        v = pa_ref[rows, 2 * bw:3 * bw].astype(BF16)
        log_sig = jnp.minimum(z, 0.0) - jnp.log(1.0 + jnp.exp(-jnp.abs(z)))
        sig_neg = 0.5 - 0.5 * jnp.tanh(0.5 * z)
        t2 = t2_lb + log_sig
        lf = jnp.maximum(t1, t2) + jnp.log(1.0 + jnp.exp(-jnp.abs(t1 - t2)))
        k = (1.0 - lb) * sig_neg

        lf2 = lf * LOG2E
        b = _cumsum_rows(tril_ref[...], lf2)
        bc_ref[...] = b
        gc_ref[SUBLANES:SUBLANES + c, :] = lf2

        q16 = q.astype(BF16)
        k16 = k.astype(BF16)
        diag = lvl == 0
        acc = [jnp.where(diag, _dot_nt(q16[:, sl], k16[:, sl]), 0.0) for sl in hsl]

        for li, m in enumerate(_a_levels()):
            blocks = range(0, c, 2 * m)
            if m == 1:
                neg = jnp.where((row & 1) == 1, lf2, 0.0)
            elif m == 2:
                r4 = row & 3
                g_next = gc_ref[SUBLANES + 1:SUBLANES + 1 + c, :]
                g_prev = gc_ref[SUBLANES - 1:SUBLANES - 1 + c, :]
                neg = jnp.where(r4 == 0, g_next, jnp.where(r4 == 1, 0.0, jnp.where(r4 == 2, lf2, lf2 + g_prev)))
            elif m < SUBLANES:
                neg = jnp.concatenate(
                    [-jnp.abs(bc_ref[lo:lo + 2 * m, :] - bc_ref[lo + m - 1:lo + m, :]) for lo in blocks], axis=0)
            else:
                pieces = []
                for lo in blocks:
                    anchor = bc_ref[lo + m - 1:lo + m, :]
                    pieces += [anchor - bc_ref[lo:lo + m, :], bc_ref[lo + m:lo + 2 * m, :] - anchor]
                neg = jnp.concatenate(pieces, axis=0)
            if m < SUBLANES:
                x = jnp.where((row & m) != 0, q, k)
            else:
                x = jnp.concatenate(
                    [t for lo in blocks for t in (k[lo:lo + m, :], q[lo + m:lo + 2 * m, :])], axis=0)
            xt = (x * jnp.exp2(neg)).astype(BF16)
            sel = lvl == li + 1
            for h, sl in enumerate(hsl):
                acc[h] = jnp.where(sel, _dot_nt(xt[:, sl], xt[:, sl]), acc[h])

        b_end = bc_ref[c - 1:c, :]
        qd = (q * jnp.exp2(b)).astype(BF16)
        kd = (k * jnp.exp2(b_end - b)).astype(BF16)
        intra = [_dot(acc[h].astype(BF16), v[:, hsl[h]]) for h in range(HEADS)]
        upd = [_dot_tn(v[:, hsl[h]], kd[:, hsl[h]]) for h in range(HEADS)]
        return intra, upd, qd, jnp.exp2(b_end)

    parts = [intra_chunk(ci) for ci in range(A_TILE_CHUNKS)]
    nw = nw_ref[...]
    sts = [state_ref[h] for h in range(HEADS)]
    for ci, (intra, upd, qd, s_decay) in enumerate(parts):
        rows = slice(ci * c, (ci + 1) * c)
        inter = [_dot_nt(qd[:, hsl[h]], sts[h].astype(BF16)) for h in range(HEADS)]
        sts = [sts[h] * s_decay[:, hsl[h]] + upd[h] for h in range(HEADS)]
        for h, sl in enumerate(hsl):
            o = intra[h] + inter[h]
            y = o * lax.rsqrt(jnp.mean(o * o, axis=-1, keepdims=True) + EPS) * nw
            gate = pa_ref[rows, 3 * bw + h * HEAD_DIM:3 * bw + (h + 1) * HEAD_DIM]
            o_ref[rows, sl] = (y * _silu(gate)).astype(BF16)
    for h in range(HEADS):
        state_ref[h] = sts[h]


def _hgrn2(pa, lbp, nw, batch, seq, layer):
    c = A_CHUNK
    tc = A_TILE_CHUNKS
    nt = seq // (c * tc)
    tril = jnp.asarray(np.tril(np.ones((c, c), np.float32)), dtype=BF16)
    lvl = jnp.asarray(_a_level_ids())
    const2 = lambda b, i: (0, 0)
    return pl.pallas_call(
        functools.partial(_hgrn2_kernel, layer=layer),
        grid=(batch, nt),
        in_specs=[pl.BlockSpec((tc * c, PA_WIDTH), lambda b, i: (b * nt + i, 0)),
                  pl.BlockSpec(lbp.shape, const2),
                  pl.BlockSpec((1, HEAD_DIM), const2),
                  pl.BlockSpec((c, c), const2),
                  pl.BlockSpec((c, c), const2)],
        out_specs=pl.BlockSpec((tc * c, BRANCH_WIDTH), lambda b, i: (b * nt + i, 0)),
        out_shape=jax.ShapeDtypeStruct((batch * seq, BRANCH_WIDTH), BF16),
        scratch_shapes=[pltpu.VMEM((HEADS, HEAD_DIM, HEAD_DIM), F32),
                        pltpu.VMEM((tc, c, BRANCH_WIDTH), F32),
                        pltpu.VMEM((tc, c + 2 * SUBLANES, BRANCH_WIDTH), F32)],
        compiler_params=pltpu.CompilerParams(
            dimension_semantics=("arbitrary", "arbitrary"), vmem_limit_bytes=VMEM_LIMIT),
        name="hgrn2",
    )(pa, lbp, nw, tril, lvl)


def _b_levels():
    return [1 << i for i in range(int(math.log2(B_CHUNK)))]


def _unit_lower_inverses(mats, ti, si):
    eye = (ti == si).astype(F32)
    ts = [eye] * len(mats)
    for s in _b_levels():
        shift = int(math.log2(2 * s))
        off = ((ti >> shift) == (si >> shift)) & ((ti & s) != 0) & ((si & s) == 0)
        a_off = [jnp.where(off, a, 0.0) for a in mats]
        if s == 1:
            ts = [t - ao for t, ao in zip(ts, a_off)]
        else:
            t16 = [t.astype(BF16) for t in ts]
            inner = [_dot(ao.astype(BF16), t) for ao, t in zip(a_off, t16)]
            outer = [_dot(t, x.astype(BF16)) for t, x in zip(t16, inner)]
            ts = [t - x for t, x in zip(ts, outer)]
    return ts


def _dn_prep_kernel(pb_ref, halo_ref, cw_ref, alog_ref, dtb_ref, tril_ref,
                    u_ref, w_ref, qd_ref, kd_ref, qk_ref, ge_ref, xp_ref, qkv_ref, *, tiles_per_seq):
    ts = B_TILE
    c = B_CHUNK
    pr = B_PAIR
    bw = BRANCH_WIDTH
    sl8 = SUBLANES

    keep = jnp.where(pl.program_id(0) % tiles_per_seq == 0, 0.0, 1.0)
    half = ts // 2
    for s in range(B_CONV_CH // LANES):
        lanes = slice(s * LANES, (s + 1) * LANES)
        xp_ref[s, 0:sl8, :] = halo_ref[:, lanes] * keep
        xp_ref[s, sl8:sl8 + ts, :] = pb_ref[:, lanes]
    for s in range(B_CONV_CH // LANES):
        lanes = slice(s * LANES, (s + 1) * LANES)
        cw = [cw_ref[j:j + 1, lanes] for j in range(B_CONV)]
        ld = {off: xp_ref[s, pl.ds(sl8 + off, half, stride=2), :] for off in range(1 - B_CONV, 2)}
        even = None
        odd = None
        for j in range(B_CONV):
            te = cw[j] * ld[j - (B_CONV - 1)]
            to = cw[j] * ld[j - (B_CONV - 1) + 1]
            even = te if even is None else even + te
            odd = to if odd is None else odd + to
        qkv_ref[s, pl.ds(0, half, stride=2), :] = _silu(even)
        qkv_ref[s, pl.ds(1, half, stride=2), :] = _silu(odd)

    small = pb_ref[:, B_CONV_CH + bw:B_CONV_CH + bw + SMALL_PAD]
    beta_all = _sigmoid(small)
    g_all = -jnp.exp(alog_ref[...]) * _softplus(small + dtb_ref[...])
    gcum = jnp.concatenate(
        [_cumsum_rows(tril_ref[...], g_all[p * pr:(p + 1) * pr, :]) for p in range(ts // pr)], axis=0)
    gcum_t = gcum.T
    for j in range(ts // c):
        ge_ref[0, j:j + 1, :] = jnp.exp(gcum[j * c + c - 1:j * c + c, :])

    ti = lax.broadcasted_iota(jnp.int32, (pr, pr), 0)
    si = lax.broadcasted_iota(jnp.int32, (pr, pr), 1)
    same = (ti >> int(math.log2(c))) == (si >> int(math.log2(c)))
    incl = same & (ti >= si)
    first_chunk = lax.broadcasted_iota(jnp.int32, (pr, 1), 0) < c

    probs = [(h, p) for h in range(HEADS) for p in range(ts // pr)]
    k16s, kb16s, q16s, decays, rhs = [], [], [], [], []
    for h, p in probs:
        sl = slice(h * HEAD_DIM, (h + 1) * HEAD_DIM)
        rows = slice(p * pr, (p + 1) * pr)
        q = qkv_ref[h, rows, :]
        k = qkv_ref[HEADS + h, rows, :]
        v = qkv_ref[2 * HEADS + h, rows, :]
        q = q * lax.rsqrt(jnp.sum(q * q, axis=-1, keepdims=True) + EPS) * (HEAD_DIM ** -0.5)
        k = k * lax.rsqrt(jnp.sum(k * k, axis=-1, keepdims=True) + EPS)
        beta = beta_all[rows, h:h + 1]
        gc = gcum[rows, HEADS + h:HEADS + h + 1]
        gr = gcum_t[HEADS + h:HEADS + h + 1, rows]
        g_last = jnp.where(first_chunk, gc[c - 1:c, :], gc[pr - 1:pr, :])
        egc = jnp.exp(gc)
        kb = k * beta
        decays.append(jnp.where(incl, jnp.exp(jnp.minimum(gc - gr, 0.0)), 0.0))
        k16s.append(k.astype(BF16))
        kb16s.append(kb.astype(BF16))
        q16s.append(q.astype(BF16))
        rhs.append(jnp.concatenate([v * beta, kb * egc], axis=1).astype(BF16))
        qd_ref[rows, sl] = (q * egc).astype(BF16)
        kd_ref[rows, sl] = (k * jnp.exp(g_last - gc)).astype(BF16)

    kk = [_dot_nt(kb16, k16) for kb16, k16 in zip(kb16s, k16s)]
    qk = [_dot_nt(q16, k16) for q16, k16 in zip(q16s, k16s)]
    mats = [x * d for x, d in zip(kk, decays)]
    tinv = _unit_lower_inverses(mats, ti, si)
    uw = [_dot(t.astype(BF16), r) for t, r in zip(tinv, rhs)]
    for i, (h, p) in enumerate(probs):
        sl = slice(h * HEAD_DIM, (h + 1) * HEAD_DIM)
        rows = slice(p * pr, (p + 1) * pr)
        u_ref[rows, sl] = uw[i][:, 0:HEAD_DIM]
        w_ref[rows, sl] = uw[i][:, HEAD_DIM:].astype(BF16)
        qk_ref[p, h] = (qk[i] * decays[i]).astype(BF16)


def _dn_scan_kernel(u_ref, w_ref, qd_ref, kd_ref, qk_ref, ge_ref, z_ref, nw_ref, o_ref, state_ref):
    c = B_CHUNK
    batch = u_ref.shape[0]

    @pl.when(pl.program_id(0) == 0)
    def _():
        state_ref[...] = jnp.zeros_like(state_ref)

    nw = nw_ref[...]
    chains = [(b, h) for b in range(batch) for h in range(HEADS)]
    hsl = lambda h: slice(h * HEAD_DIM, (h + 1) * HEAD_DIM)
    states = [state_ref[b * HEADS + h] for b, h in chains]
    for pi in range(B_SCAN_PAIRS):
        o_inter = [[] for _ in chains]
        v_new = [[] for _ in chains]
        for j in range(B_PAIR // c):
            rows = slice(pi * B_PAIR + j * c, pi * B_PAIR + (j + 1) * c)
            st16 = [st.astype(BF16) for st in states]
            lhs = [jnp.concatenate([w_ref[b, rows, hsl(h)], qd_ref[b, rows, hsl(h)]], axis=0)
                   for b, h in chains]
            prod = [_dot(x, s) for x, s in zip(lhs, st16)]
            vn16 = []
            for i, (b, h) in enumerate(chains):
                vn = (u_ref[b, rows, hsl(h)] - prod[i][0:c, :]).astype(BF16)
                vn16.append(vn)
                v_new[i].append(vn)
                o_inter[i].append(prod[i][c:, :])
            upd = [_dot_tn(kd_ref[b, rows, hsl(h)], vn) for (b, h), vn in zip(chains, vn16)]
            states = [st * ge_ref[b, pi, j:j + 1, HEADS + h:HEADS + h + 1] + x
                      for (b, h), st, x in zip(chains, states, upd)]
        intra = [_dot(qk_ref[b, pi, h], jnp.concatenate(v_new[i], axis=0)) for i, (b, h) in enumerate(chains)]
        prows = slice(pi * B_PAIR, (pi + 1) * B_PAIR)
        for i, (b, h) in enumerate(chains):
            o = jnp.concatenate(o_inter[i], axis=0) + intra[i]
            y = o * lax.rsqrt(jnp.mean(o * o, axis=-1, keepdims=True) + EPS) * nw
            o_ref[b, prows, hsl(h)] = (y * _silu(z_ref[b, prows, hsl(h)])).astype(BF16)
    for i, (b, h) in enumerate(chains):
        state_ref[b * HEADS + h] = states[i]


def _deltanet(pb, cw, alog, dtb, nw, batch, seq):
    c = B_CHUNK
    ts = B_TILE
    pr = B_PAIR
    bw = BRANCH_WIDTH
    t = batch * seq
    n_tiles = t // ts
    tril = jnp.asarray(np.kron(np.eye(pr // c, dtype=np.float32), np.tril(np.ones((c, c), np.float32))),
                       dtype=BF16)
    const = lambda i: (0, 0)
    row = lambda i: (i, 0)
    halo_blocks = ts // SUBLANES
    u, w, qd, kd, qk, ge = pl.pallas_call(
        functools.partial(_dn_prep_kernel, tiles_per_seq=seq // ts),
        grid=(n_tiles,),
        in_specs=[pl.BlockSpec((ts, PB_WIDTH), row),
                  pl.BlockSpec((SUBLANES, B_CONV_CH), lambda i: (jnp.maximum(i * halo_blocks - 1, 0), 0)),
                  pl.BlockSpec(cw.shape, const),
                  pl.BlockSpec((1, SMALL_PAD), const),
                  pl.BlockSpec((1, SMALL_PAD), const),
                  pl.BlockSpec((pr, pr), const)],
        out_specs=[pl.BlockSpec((ts, bw), row),
                   pl.BlockSpec((ts, bw), row),
                   pl.BlockSpec((ts, bw), row),
                   pl.BlockSpec((ts, bw), row),
                   pl.BlockSpec((ts // pr, HEADS, pr, pr), lambda i: (i, 0, 0, 0)),
                   pl.BlockSpec((1, ts // c, SMALL_PAD), lambda i: (i, 0, 0))],
        out_shape=[jax.ShapeDtypeStruct((t, bw), F32),
                   jax.ShapeDtypeStruct((t, bw), BF16),
                   jax.ShapeDtypeStruct((t, bw), BF16),
                   jax.ShapeDtypeStruct((t, bw), BF16),
                   jax.ShapeDtypeStruct((t // pr, HEADS, pr, pr), BF16),
                   jax.ShapeDtypeStruct((n_tiles, ts // c, SMALL_PAD), F32)],
        scratch_shapes=[pltpu.VMEM((B_CONV_CH // LANES, ts + SUBLANES, LANES), F32),
                        pltpu.VMEM((B_CONV_CH // LANES, ts, LANES), F32)],
        compiler_params=pltpu.CompilerParams(
            dimension_semantics=("arbitrary",), vmem_limit_bytes=VMEM_LIMIT),
        name="dn_prep",
    )(pb, pb, cw, alog, dtb, tril)

    np_seq = seq // pr
    seq3 = lambda x: x.reshape(batch, seq, x.shape[-1])
    sp = B_SCAN_PAIRS
    blk3 = pl.BlockSpec((batch, sp * pr, bw), lambda i: (0, i, 0))
    y = pl.pallas_call(
        _dn_scan_kernel,
        grid=(np_seq // sp,),
        in_specs=[blk3, blk3, blk3, blk3,
                  pl.BlockSpec((batch, sp, HEADS, pr, pr), lambda i: (0, i, 0, 0, 0)),
                  pl.BlockSpec((batch, sp, pr // c, SMALL_PAD), lambda i: (0, i, 0, 0)),
                  pl.BlockSpec((batch, sp * pr, bw), lambda i: (0, i, B_CONV_CH // bw)),
                  pl.BlockSpec((1, HEAD_DIM), const)],
        out_specs=blk3,
        out_shape=jax.ShapeDtypeStruct((batch, seq, bw), BF16),
        scratch_shapes=[pltpu.VMEM((batch * HEADS, HEAD_DIM, HEAD_DIM), F32)],
        compiler_params=pltpu.CompilerParams(
            dimension_semantics=("arbitrary",), vmem_limit_bytes=VMEM_LIMIT),
        name="dn_scan",
    )(seq3(u), seq3(w), seq3(qd), seq3(kd),
      qk.reshape(batch, np_seq, HEADS, pr, pr),
      ge.reshape(batch, np_seq, pr // c, SMALL_PAD),
      seq3(pb), nw)
    return y.reshape(t, bw)


def _band_buckets():
    i = np.arange(C_BLOCK)[:, None]
    j = np.arange(2 * C_BLOCK)[None, :]
    dist = i + C_BLOCK - j
    max_exact = N_BUCKETS // 2
    d_f = np.maximum(dist, 1).astype(np.float32)
    large = max_exact + (np.log(d_f / np.float32(max_exact)) / np.float32(math.log(MAX_DISTANCE / max_exact))
                         * np.float32(N_BUCKETS - max_exact)).astype(np.int32)
    large = np.minimum(large, N_BUCKETS - 1)
    bucket = np.where(dist < max_exact, dist, large)
    return np.where((dist >= 0) & (dist < WINDOW), bucket, -1).astype(np.int32)


def _swa_kernel(rb_ref, sink_ref, cur_ref, prev_ref, bucket_ref, o_ref, bias_ref):
    blk = C_BLOCK
    bw = BRANCH_WIDTH
    n = pl.program_id(1)

    @pl.when((pl.program_id(0) == 0) & (n == 0))
    def _():
        bucket = bucket_ref[...]
        for h in range(C_Q_HEADS):
            acc = jnp.full((blk, 2 * blk), NEG_BIG, F32)
            for bk in range(N_BUCKETS):
                acc = jnp.where(bucket == bk, rb_ref[bk, h], acc)
            bias_ref[h] = acc

    kv = jnp.concatenate([prev_ref[...], cur_ref[:, bw:bw + 2 * C_KV_WIDTH]], axis=0)
    rows = kv.shape[0]
    lane = lax.broadcasted_iota(jnp.int32, (rows, C_KV_WIDTH), 1)

    def halves(x, j):
        own = jnp.where((lane >= j * C_HEAD_DIM) & (lane < (j + 1) * C_HEAD_DIM), x, 0.0)
        other = pltpu.roll(own, C_HEAD_DIM, 1)
        pair = (own, other) if j == 0 else (other, own)
        return [t.astype(BF16) for t in pair]

    kz = [halves(kv[:, 0:C_KV_WIDTH], j) for j in range(C_KV_HEADS)]
    vz = [halves(kv[:, C_KV_WIDTH:], j) for j in range(C_KV_HEADS)]
    scale = C_HEAD_DIM ** -0.5
    group = C_Q_HEADS // C_KV_HEADS
    key_ok = lax.broadcasted_iota(jnp.int32, (1, 2 * blk), 1) >= jnp.where(n > 0, 0, blk)

    chains = [(i, h) for i in range(C_TILE_BLOCKS) for h in range(C_Q_HEADS)]
    qrows = lambda i: slice(i * blk, (i + 1) * blk)
    krows = lambda i: slice(i * blk, (i + 2) * blk)
    q2 = [[(cur_ref[qrows(i), p * LANES:(p + 1) * LANES] * scale).astype(BF16)
           for p in range(C_Q_HEADS // 2)] for i in range(C_TILE_BLOCKS)]
    logits = [_dot_nt(q2[i][h // 2], kz[h // group][h % 2][krows(i), :]) + bias_ref[h] for i, h in chains]
    logits = [jnp.where(key_ok, lg, NEG_BIG) if i == 0 else lg for (i, h), lg in zip(chains, logits)]
    mx = [jnp.maximum(jnp.max(lg, axis=-1, keepdims=True), sink_ref[h]) for (i, h), lg in zip(chains, logits)]
    pr = [jnp.exp(lg - m) for lg, m in zip(logits, mx)]
    den = [jnp.sum(x, axis=-1, keepdims=True) + jnp.exp(sink_ref[h] - m) for (i, h), x, m in zip(chains, pr, mx)]
    pv = [_dot(x.astype(BF16), vz[h // group][h % 2][krows(i), :]) for (i, h), x in zip(chains, pr)]
    for i in range(C_TILE_BLOCKS):
        for p in range(C_Q_HEADS // 2):
            a, b = i * C_Q_HEADS + 2 * p, i * C_Q_HEADS + 2 * p + 1
            out = pv[a] / den[a] + pv[b] / den[b]
            gate = cur_ref[qrows(i), bw + 2 * C_KV_WIDTH + p * LANES:bw + 2 * C_KV_WIDTH + (p + 1) * LANES]
            o_ref[qrows(i), p * LANES:(p + 1) * LANES] = (out * _silu(gate)).astype(BF16)


def _swa(pc, rel_bias, sinks, batch, seq):
    blk = C_BLOCK
    nb = seq // blk
    bucket = jnp.asarray(_band_buckets())
    kv_col = BRANCH_WIDTH // (2 * C_KV_WIDTH)
    smem = pl.BlockSpec(memory_space=pltpu.SMEM)
    tb = C_TILE_BLOCKS
    nt = nb // tb
    return pl.pallas_call(
        _swa_kernel,
        grid=(batch, nt),
        in_specs=[smem, smem,
                  pl.BlockSpec((tb * blk, PC_WIDTH), lambda b, i: (b * nt + i, 0)),
                  pl.BlockSpec((blk, 2 * C_KV_WIDTH), lambda b, i: (b * nb + jnp.maximum(i * tb - 1, 0), kv_col)),
                  pl.BlockSpec((blk, 2 * blk), lambda b, i: (0, 0))],
        out_specs=pl.BlockSpec((tb * blk, BRANCH_WIDTH), lambda b, i: (b * nt + i, 0)),
        out_shape=jax.ShapeDtypeStruct((batch * seq, BRANCH_WIDTH), BF16),
        scratch_shapes=[pltpu.VMEM((C_Q_HEADS, blk, 2 * blk), F32)],
        compiler_params=pltpu.CompilerParams(
            dimension_semantics=("arbitrary", "arbitrary"), vmem_limit_bytes=VMEM_LIMIT),
        name="swa",
    )(rel_bias, sinks, pc, pc, bucket)


W_AB = PA_WIDTH + PB_WIDTH - (SMALL_PAD - 2 * HEADS)
W_ABC = W_AB + PC_WIDTH


def _regroup_w_in(w_in):
    w16 = w_in.astype(BF16)
    pad = jnp.zeros(w_in.shape[:2] + (PA_WIDTH + PB_WIDTH - W_AB,), BF16)
    w_mix = jnp.concatenate([w16[:, :, 0:W_AB], pad, w16[:, :, W_AB:W_ABC]], axis=2)
    return w_mix, w16[:, :, W_ABC:]


def _lane_pad(v, offset):
    return jnp.zeros((1, SMALL_PAD), F32).at[0, offset:offset + v.shape[0]].set(v.astype(F32))


def kernel(x, norm_w, w_in, conv_w, a_log, dt_bias, lb_param, norm_a, norm_b, sinks, rel_bias,
           w_branch, w_out, final_norm):
    batch, seq, _ = x.shape
    depth = w_in.shape[0]
    x2 = x.reshape(batch * seq, D_MODEL)
    fn = final_norm.reshape(1, D_MODEL)
    w_mix, wg = _regroup_w_in(w_in)
    wbr = w_branch.astype(BF16)
    wo = w_out.astype(BF16)
    for l in range(depth):
        nw = norm_w[l].reshape(1, D_MODEL)
        pa, pb, pc = _inproj(x2, nw, w_mix, l)
        ya = _hgrn2(pa, lb_param, norm_a[l].reshape(1, HEAD_DIM), batch, seq, l)
        yb = _deltanet(pb, conv_w[l], _lane_pad(a_log[l], HEADS), _lane_pad(dt_bias[l], HEADS),
                       norm_b[l].reshape(1, HEAD_DIM), batch, seq)
        yc = _swa(pc, rel_bias, sinks[l], batch, seq)
        x2 = _merge(x2, nw, wg, ya, yb, yc, wbr, wo, fn, layer=l, final=(l == depth - 1))
    return x2.reshape(batch, seq, D_MODEL)
```

```python
import functools
import math

import numpy as np
import jax
import jax.numpy as jnp
from jax import lax
from jax.experimental import pallas as pl
from jax.experimental.pallas import tpu as pltpu

F32 = jnp.float32
BF16 = jnp.bfloat16

D_MODEL = 1024
BRANCH_WIDTH = D_MODEL // 2
N_BRANCHES = 3
EPS = 1e-6
HEADS = 4
HEAD_DIM = BRANCH_WIDTH // HEADS
B_CONV = 4
B_CONV_CH = 3 * BRANCH_WIDTH
C_Q_HEADS = 8
C_KV_HEADS = 2
C_HEAD_DIM = BRANCH_WIDTH // C_Q_HEADS
C_KV_WIDTH = C_KV_HEADS * C_HEAD_DIM
WINDOW = 128
C_BLOCK = 128
C_TILE_BLOCKS = 4
N_BUCKETS = 32
MAX_DISTANCE = 128
LANES = 128
SUBLANES = 8
NEG_BIG = -1e30
LOG2E = math.log2(math.e)

A_CHUNK = 128
A_TILE_CHUNKS = 4
B_CHUNK = 64
B_PAIR = 2 * B_CHUNK
B_TILE = 512
B_SCAN_PAIRS = 2
PROJ_ROWS = 512
SMALL_PAD = LANES
VMEM_LIMIT = 52 * 1024 * 1024

PA_WIDTH = 4 * BRANCH_WIDTH
PB_WIDTH = B_CONV_CH + BRANCH_WIDTH + SMALL_PAD
PC_WIDTH = 2 * BRANCH_WIDTH + 2 * C_KV_WIDTH
W_AB = PA_WIDTH + PB_WIDTH - (SMALL_PAD - 2 * HEADS)
W_ABC = W_AB + PC_WIDTH


def _dot(a, b):
    return jnp.dot(a, b, preferred_element_type=F32)


def _dot_nt(a, b):
    return lax.dot_general(a, b, (((1,), (1,)), ((), ())), preferred_element_type=F32)


def _dot_tn(a, b):
    return lax.dot_general(a, b, (((0,), (0,)), ((), ())), preferred_element_type=F32)


def _sigmoid(x):
    return 0.5 * jnp.tanh(0.5 * x) + 0.5


def _silu(x):
    h = 0.5 * x
    return h * jnp.tanh(h) + h


def _cumsum_rows(tril16, x):
    hi = x.astype(BF16)
    r1 = x - hi.astype(F32)
    mid = r1.astype(BF16)
    lo = (r1 - mid.astype(F32)).astype(BF16)
    return _dot(tril16, hi) + (_dot(tril16, mid) + _dot(tril16, lo))


def _softplus(x):
    return jnp.maximum(x, 0.0) + jnp.log(1.0 + jnp.exp(-jnp.abs(x)))


def _rms(x, w):
    return x * lax.rsqrt(jnp.mean(x * x, axis=-1, keepdims=True) + EPS) * w


def _resident(shape):
    return pl.BlockSpec(shape, lambda i: (0,) * len(shape), pipeline_mode=pl.Buffered(1))


def _layer_resident(shape, layer):
    return pl.BlockSpec((None,) + tuple(shape[1:]), lambda i: (layer,) + (0,) * (len(shape) - 1),
                        pipeline_mode=pl.Buffered(1))


def _inproj_kernel(x_ref, nw_ref, w_ref, pa_ref, pb_ref, pc_ref, wc_ref):
    @pl.when(pl.program_id(0) == 0)
    def _():
        wc_ref[...] = w_ref[:, W_AB:W_ABC]

    h = _rms(x_ref[...], nw_ref[...]).astype(BF16)
    for o_ref, src_ref, base in ((pa_ref, w_ref, 0), (pb_ref, w_ref, PA_WIDTH), (pc_ref, wc_ref, 0)):
        n = o_ref.shape[1]
        for j in range(0, n, BRANCH_WIDTH):
            wd = min(BRANCH_WIDTH, n - j)
            o_ref[:, j:j + wd] = _dot(h, src_ref[:, base + j:base + j + wd])


def _inproj(x2, nw, w, layer):
    t = x2.shape[0]
    tm = PROJ_ROWS
    return pl.pallas_call(
        _inproj_kernel,
        grid=(t // tm,),
        in_specs=[pl.BlockSpec((tm, D_MODEL), lambda i: (i, 0)),
                  _resident((1, D_MODEL)),
                  _layer_resident(w.shape, layer)],
        out_specs=[pl.BlockSpec((tm, PA_WIDTH), lambda i: (i, 0)),
                   pl.BlockSpec((tm, PB_WIDTH), lambda i: (i, 0)),
                   pl.BlockSpec((tm, PC_WIDTH), lambda i: (i, 0))],
        out_shape=[jax.ShapeDtypeStruct((t, PA_WIDTH), F32),
                   jax.ShapeDtypeStruct((t, PB_WIDTH), F32),
                   jax.ShapeDtypeStruct((t, PC_WIDTH), F32)],
        scratch_shapes=[pltpu.VMEM((D_MODEL, PC_WIDTH), BF16)],
        compiler_params=pltpu.CompilerParams(
            dimension_semantics=("arbitrary",), vmem_limit_bytes=VMEM_LIMIT),
        name="inproj",
    )(x2, nw, w)


def _merge_kernel(x_ref, nw_ref, w_ref, ya_ref, yb_ref, yc_ref, wbr_ref, wo_ref, fn_ref,
                  o_ref, wg_ref, *, final):
    @pl.when(pl.program_id(0) == 0)
    def _():
        wg_ref[...] = w_ref[:, W_ABC:]

    x = x_ref[...]
    h = _rms(x, nw_ref[...]).astype(BF16)
    merged = None
    for n, y_ref in enumerate((ya_ref, yb_ref, yc_ref)):
        gate = _sigmoid(_dot(h, wg_ref[:, n * D_MODEL:(n + 1) * D_MODEL]))
        term = gate * _dot(y_ref[...], wbr_ref[n])
        merged = term if merged is None else merged + term
    out = x + _dot(merged.astype(BF16), wo_ref[...])
    if final:
        out = _rms(out, fn_ref[...])
    o_ref[...] = out


def _merge(x2, nw, w, ya, yb, yc, wbr, wo, fn, layer, final):
    t = x2.shape[0]
    tm = PROJ_ROWS
    row = lambda i: (i, 0)
    return pl.pallas_call(
        functools.partial(_merge_kernel, final=final),
        grid=(t // tm,),
        in_specs=[pl.BlockSpec((tm, D_MODEL), row),
                  _resident((1, D_MODEL)),
                  _layer_resident(w.shape, layer),
                  pl.BlockSpec((tm, BRANCH_WIDTH), row),
                  pl.BlockSpec((tm, BRANCH_WIDTH), row),
                  pl.BlockSpec((tm, BRANCH_WIDTH), row),
                  _layer_resident(wbr.shape, layer),
                  _layer_resident(wo.shape, layer),
                  _resident((1, D_MODEL))],
        out_specs=pl.BlockSpec((tm, D_MODEL), row),
        out_shape=jax.ShapeDtypeStruct((t, D_MODEL), F32),
        scratch_shapes=[pltpu.VMEM((D_MODEL, N_BRANCHES * D_MODEL), BF16)],
        compiler_params=pltpu.CompilerParams(
            dimension_semantics=("arbitrary",), vmem_limit_bytes=VMEM_LIMIT),
        name="merge",
    )(x2, nw, w, ya, yb, yc, wbr, wo, fn)


def _a_levels():
    return [1 << i for i in range(int(math.log2(A_CHUNK)))]


def _a_level_ids():
    idx = np.arange(A_CHUNK)
    ids = np.where(idx[:, None] == idx[None, :], 0, -1).astype(np.int32)
    for i, m in enumerate(_a_levels()):
        blk = idx // (2 * m)
        upper = (idx & m) != 0
        ids[(blk[:, None] == blk[None, :]) & upper[:, None] & ~upper[None, :]] = i + 1
    return ids


def _hgrn2_kernel(pa_ref, lbp_ref, nw_ref, tril_ref, lvl_ref, o_ref,
                  state_ref, b_ref, gp_ref, *, layer):
    c = A_CHUNK
    bw = BRANCH_WIDTH

    @pl.when(pl.program_id(1) == 0)
    def _():
        state_ref[...] = jnp.zeros_like(state_ref)
        gp_ref[...] = jnp.zeros_like(gp_ref)

    lbp = lbp_ref[...]
    ex = jnp.exp(lbp - jnp.max(lbp, axis=0, keepdims=True))
    sm = ex / jnp.sum(ex, axis=0, keepdims=True)
    lb = jnp.zeros((1, bw), F32)
    for j in range(1, layer + 1):
        lb = lb + sm[j:j + 1, :]

    hsl = [slice(h * HEAD_DIM, (h + 1) * HEAD_DIM) for h in range(HEADS)]
    row = lax.broadcasted_iota(jnp.int32, (c, 1), 0)
    lvl = lvl_ref[...]
    t1 = jnp.log(lb)
    t2_lb = jnp.log1p(-lb)

    def intra_chunk(ci):
        rows = slice(ci * c, (ci + 1) * c)
        bc_ref = b_ref.at[ci]
        gc_ref = gp_ref.at[ci]
        q = _silu(pa_ref[rows, 0:bw])
        z = pa_ref[rows, bw:2 * bw]
        v = pa_ref[rows, 2 * bw:3 * bw].astype(BF16)
        log_sig = jnp.minimum(z, 0.0) - jnp.log(1.0 + jnp.exp(-jnp.abs(z)))
        sig_neg = 0.5 - 0.5 * jnp.tanh(0.5 * z)
        t2 = t2_lb + log_sig
        lf = jnp.maximum(t1, t2) + jnp.log(1.0 + jnp.exp(-jnp.abs(t1 - t2)))
        k = (1.0 - lb) * sig_neg

        lf2 = lf * LOG2E
        b = _cumsum_rows(tril_ref[...], lf2)
        bc_ref[...] = b
        gc_ref[SUBLANES:SUBLANES + c, :] = lf2

        q16 = q.astype(BF16)
        k16 = k.astype(BF16)
        diag = lvl == 0
        acc = [jnp.where(diag, _dot_nt(q16[:, sl], k16[:, sl]), 0.0) for sl in hsl]

        for li, m in enumerate(_a_levels()):
            blocks = range(0, c, 2 * m)
            if m == 1:
                neg = jnp.where((row & 1) == 1, lf2, 0.0)
            elif m == 2:
                r4 = row & 3
                g_next = gc_ref[SUBLANES + 1:SUBLANES + 1 + c, :]
                g_prev = gc_ref[SUBLANES - 1:SUBLANES - 1 + c, :]
                neg = jnp.where(r4 == 0, g_next, jnp.where(r4 == 1, 0.0, jnp.where(r4 == 2, lf2, lf2 + g_prev)))
            elif m < SUBLANES:
                neg = jnp.concatenate(
                    [-jnp.abs(bc_ref[lo:lo + 2 * m, :] - bc_ref[lo + m - 1:lo + m, :]) for lo in blocks], axis=0)
            else:
                pieces = []
                for lo in blocks:
                    anchor = bc_ref[lo + m - 1:lo + m, :]
                    pieces += [anchor - bc_ref[lo:lo + m, :], bc_ref[lo + m:lo + 2 * m, :] - anchor]
                neg = jnp.concatenate(pieces, axis=0)
            if m < SUBLANES:
                x = jnp.where((row & m) != 0, q, k)
            else:
                x = jnp.concatenate(
                    [t for lo in blocks for t in (k[lo:lo + m, :], q[lo + m:lo + 2 * m, :])], axis=0)
            xt = (x * jnp.exp2(neg)).astype(BF16)
            sel = lvl == li + 1
            for h, sl in enumerate(hsl):
                acc[h] = jnp.where(sel, _dot_nt(xt[:, sl], xt[:, sl]), acc[h])

        b_end = bc_ref[c - 1:c, :]
        qd = (q * jnp.exp2(b)).astype(BF16)
        kd = (k * jnp.exp2(b_end - b)).astype(BF16)
        intra = [_dot(acc[h].astype(BF16), v[:, hsl[h]]) for h in range(HEADS)]
        upd = [_dot_tn(v[:, hsl[h]], kd[:, hsl[h]]) for h in range(HEADS)]
        return intra, upd, qd, jnp.exp2(b_end)

    parts = [intra_chunk(ci) for ci in range(A_TILE_CHUNKS)]
    nw = nw_ref[...]
    sts = [state_ref[h] for h in range(HEADS)]
    for ci, (intra, upd, qd, s_decay) in enumerate(parts):
        rows = slice(ci * c, (ci + 1) * c)
        inter = [_dot_nt(qd[:, hsl[h]], sts[h].astype(BF16)) for h in range(HEADS)]
        sts = [sts[h] * s_decay[:, hsl[h]] + upd[h] for h in range(HEADS)]
        for h, sl in enumerate(hsl):
            o = intra[h] + inter[h]
            y = o * lax.rsqrt(jnp.mean(o * o, axis=-1, keepdims=True) + EPS) * nw
            gate = pa_ref[rows, 3 * bw + h * HEAD_DIM:3 * bw + (h + 1) * HEAD_DIM]
            o_ref[rows, sl] = (y * _silu(gate)).astype(BF16)
    for h in range(HEADS):
        state_ref[h] = sts[h]


def _hgrn2(pa, lbp, nw, batch, seq, layer):
    c = A_CHUNK
    tc = A_TILE_CHUNKS
    nt = seq // (c * tc)
    tril = jnp.asarray(np.tril(np.ones((c, c), np.float32)), dtype=BF16)
    lvl = jnp.asarray(_a_level_ids())
    const2 = lambda b, i: (0, 0)
    return pl.pallas_call(
        functools.partial(_hgrn2_kernel, layer=layer),
        grid=(batch, nt),
        in_specs=[pl.BlockSpec((tc * c, PA_WIDTH), lambda b, i: (b * nt + i, 0)),
                  pl.BlockSpec(lbp.shape, const2),
                  pl.BlockSpec((1, HEAD_DIM), const2),
                  pl.BlockSpec((c, c), const2),
                  pl.BlockSpec((c, c), const2)],
        out_specs=pl.BlockSpec((tc * c, BRANCH_WIDTH), lambda b, i: (b * nt + i, 0)),
        out_shape=jax.ShapeDtypeStruct((batch * seq, BRANCH_WIDTH), BF16),
        scratch_shapes=[pltpu.VMEM((HEADS, HEAD_DIM, HEAD_DIM), F32),
                        pltpu.VMEM((tc, c, BRANCH_WIDTH), F32),
                        pltpu.VMEM((tc, c + 2 * SUBLANES, BRANCH_WIDTH), F32)],
        compiler_params=pltpu.CompilerParams(
            dimension_semantics=("arbitrary", "arbitrary"), vmem_limit_bytes=VMEM_LIMIT),
        name="hgrn2",
    )(pa, lbp, nw, tril, lvl)


def _b_levels():
    return [1 << i for i in range(int(math.log2(B_CHUNK)))]


def _unit_lower_inverses(mats, ti, si):
    eye = (ti == si).astype(F32)
    ts = [eye] * len(mats)
    for s in _b_levels():
        shift = int(math.log2(2 * s))
        off = ((ti >> shift) == (si >> shift)) & ((ti & s) != 0) & ((si & s) == 0)
        a_off = [jnp.where(off, a, 0.0) for a in mats]
        if s == 1:
            ts = [t - ao for t, ao in zip(ts, a_off)]
        else:
            t16 = [t.astype(BF16) for t in ts]
            inner = [_dot(ao.astype(BF16), t) for ao, t in zip(a_off, t16)]
            outer = [_dot(t, x.astype(BF16)) for t, x in zip(t16, inner)]
            ts = [t - x for t, x in zip(ts, outer)]
    return ts


def _dn_prep_kernel(pb_ref, halo_ref, cw_ref, alog_ref, dtb_ref, tril_ref,
                    u_ref, w_ref, qd_ref, kd_ref, qk_ref, ge_ref, xp_ref, qkv_ref, *, tiles_per_seq):
    ts = B_TILE
    c = B_CHUNK
    pr = B_PAIR
    bw = BRANCH_WIDTH
    sl8 = SUBLANES

    keep = jnp.where(pl.program_id(0) % tiles_per_seq == 0, 0.0, 1.0)
    half = ts // 2
    for s in range(B_CONV_CH // LANES):
        lanes = slice(s * LANES, (s + 1) * LANES)
        xp_ref[s, 0:sl8, :] = halo_ref[:, lanes] * keep
        xp_ref[s, sl8:sl8 + ts, :] = pb_ref[:, lanes]
    for s in range(B_CONV_CH // LANES):
        lanes = slice(s * LANES, (s + 1) * LANES)
        cw = [cw_ref[j:j + 1, lanes] for j in range(B_CONV)]
        ld = {off: xp_ref[s, pl.ds(sl8 + off, half, stride=2), :] for off in range(1 - B_CONV, 2)}
        even = None
        odd = None
        for j in range(B_CONV):
            te = cw[j] * ld[j - (B_CONV - 1)]
            to = cw[j] * ld[j - (B_CONV - 1) + 1]
            even = te if even is None else even + te
            odd = to if odd is None else odd + to
        qkv_ref[s, pl.ds(0, half, stride=2), :] = _silu(even)
        qkv_ref[s, pl.ds(1, half, stride=2), :] = _silu(odd)

    small = pb_ref[:, B_CONV_CH + bw:B_CONV_CH + bw + SMALL_PAD]
    beta_all = _sigmoid(small)
    g_all = -jnp.exp(alog_ref[...]) * _softplus(small + dtb_ref[...])
    gcum = jnp.concatenate(
        [_cumsum_rows(tril_ref[...], g_all[p * pr:(p + 1) * pr, :]) for p in range(ts // pr)], axis=0)
    gcum_t = gcum.T
    for j in range(ts // c):
        ge_ref[0, j:j + 1, :] = jnp.exp(gcum[j * c + c - 1:j * c + c, :])

    ti = lax.broadcasted_iota(jnp.int32, (pr, pr), 0)
    si = lax.broadcasted_iota(jnp.int32, (pr, pr), 1)
    same = (ti >> int(math.log2(c))) == (si >> int(math.log2(c)))
    incl = same & (ti >= si)
    first_chunk = lax.broadcasted_iota(jnp.int32, (pr, 1), 0) < c

    probs = [(h, p) for h in range(HEADS) for p in range(ts // pr)]
    k16s, kb16s, q16s, decays, rhs = [], [], [], [], []
    for h, p in probs:
        sl = slice(h * HEAD_DIM, (h + 1) * HEAD_DIM)
        rows = slice(p * pr, (p + 1) * pr)
        q = qkv_ref[h, rows, :]
        k = qkv_ref[HEADS + h, rows, :]
        v = qkv_ref[2 * HEADS + h, rows, :]
        q = q * lax.rsqrt(jnp.sum(q * q, axis=-1, keepdims=True) + EPS) * (HEAD_DIM ** -0.5)
        k = k * lax.rsqrt(jnp.sum(k * k, axis=-1, keepdims=True) + EPS)
        beta = beta_all[rows, h:h + 1]
        gc = gcum[rows, HEADS + h:HEADS + h + 1]
        gr = gcum_t[HEADS + h:HEADS + h + 1, rows]
        g_last = jnp.where(first_chunk, gc[c - 1:c, :], gc[pr - 1:pr, :])
        egc = jnp.exp(gc)
        kb = k * beta
        decays.append(jnp.where(incl, jnp.exp(jnp.minimum(gc - gr, 0.0)), 0.0))
        k16s.append(k.astype(BF16))
        kb16s.append(kb.astype(BF16))
        q16s.append(q.astype(BF16))
        rhs.append(jnp.concatenate([v * beta, kb * egc], axis=1).astype(BF16))
        qd_ref[rows, sl] = (q * egc).astype(BF16)
        kd_ref[rows, sl] = (k * jnp.exp(g_last - gc)).astype(BF16)

    kk = [_dot_nt(kb16, k16) for kb16, k16 in zip(kb16s, k16s)]
    qk = [_dot_nt(q16, k16) for q16, k16 in zip(q16s, k16s)]
    mats = [x * d for x, d in zip(kk, decays)]
    tinv = _unit_lower_inverses(mats, ti, si)
    uw = [_dot(t.astype(BF16), r) for t, r in zip(tinv, rhs)]
    for i, (h, p) in enumerate(probs):
        sl = slice(h * HEAD_DIM, (h + 1) * HEAD_DIM)
        rows = slice(p * pr, (p + 1) * pr)
        u_ref[rows, sl] = uw[i][:, 0:HEAD_DIM]
        w_ref[rows, sl] = uw[i][:, HEAD_DIM:].astype(BF16)
        qk_ref[p, h] = (qk[i] * decays[i]).astype(BF16)


def _dn_scan_kernel(u_ref, w_ref, qd_ref, kd_ref, qk_ref, ge_ref, z_ref, nw_ref, o_ref, state_ref):
    c = B_CHUNK
    batch = u_ref.shape[0]

    @pl.when(pl.program_id(0) == 0)
    def _():
        state_ref[...] = jnp.zeros_like(state_ref)

    nw = nw_ref[...]
    chains = [(b, h) for b in range(batch) for h in range(HEADS)]
    hsl = lambda h: slice(h * HEAD_DIM, (h + 1) * HEAD_DIM)
    states = [state_ref[b * HEADS + h] for b, h in chains]
    for pi in range(B_SCAN_PAIRS):
        o_inter = [[] for _ in chains]
        v_new = [[] for _ in chains]
        for j in range(B_PAIR // c):
            rows = slice(pi * B_PAIR + j * c, pi * B_PAIR + (j + 1) * c)
            st16 = [st.astype(BF16) for st in states]
            lhs = [jnp.concatenate([w_ref[b, rows, hsl(h)], qd_ref[b, rows, hsl(h)]], axis=0)
                   for b, h in chains]
            prod = [_dot(x, s) for x, s in zip(lhs, st16)]
            vn16 = []
            for i, (b, h) in enumerate(chains):
                vn = (u_ref[b, rows, hsl(h)] - prod[i][0:c, :]).astype(BF16)
                vn16.append(vn)
                v_new[i].append(vn)
                o_inter[i].append(prod[i][c:, :])
            upd = [_dot_tn(kd_ref[b, rows, hsl(h)], vn) for (b, h), vn in zip(chains, vn16)]
            states = [st * ge_ref[b, pi, j:j + 1, HEADS + h:HEADS + h + 1] + x
                      for (b, h), st, x in zip(chains, states, upd)]
        intra = [_dot(qk_ref[b, pi, h], jnp.concatenate(v_new[i], axis=0)) for i, (b, h) in enumerate(chains)]
        prows = slice(pi * B_PAIR, (pi + 1) * B_PAIR)
        for i, (b, h) in enumerate(chains):
            o = jnp.concatenate(o_inter[i], axis=0) + intra[i]
            y = o * lax.rsqrt(jnp.mean(o * o, axis=-1, keepdims=True) + EPS) * nw
            o_ref[b, prows, hsl(h)] = (y * _silu(z_ref[b, prows, hsl(h)])).astype(BF16)
    for i, (b, h) in enumerate(chains):
        state_ref[b * HEADS + h] = states[i]


def _deltanet(pb, cw, alog, dtb, nw, batch, seq):
    c = B_CHUNK
    ts = B_TILE
    pr = B_PAIR
    bw = BRANCH_WIDTH
    t = batch * seq
    n_tiles = t // ts
    tril = jnp.asarray(np.kron(np.eye(pr // c, dtype=np.float32), np.tril(np.ones((c, c), np.float32))),
                       dtype=BF16)
    const = lambda i: (0, 0)
    row = lambda i: (i, 0)
    halo_blocks = ts // SUBLANES
    u, w, qd, kd, qk, ge = pl.pallas_call(
        functools.partial(_dn_prep_kernel, tiles_per_seq=seq // ts),
        grid=(n_tiles,),
        in_specs=[pl.BlockSpec((ts, PB_WIDTH), row),
                  pl.BlockSpec((SUBLANES, B_CONV_CH), lambda i: (jnp.maximum(i * halo_blocks - 1, 0), 0)),
                  pl.BlockSpec(cw.shape, const),
                  pl.BlockSpec((1, SMALL_PAD), const),
                  pl.BlockSpec((1, SMALL_PAD), const),
                  pl.BlockSpec((pr, pr), const)],
        out_specs=[pl.BlockSpec((ts, bw), row),
                   pl.BlockSpec((ts, bw), row),
                   pl.BlockSpec((ts, bw), row),
                   pl.BlockSpec((ts, bw), row),
                   pl.BlockSpec((ts // pr, HEADS, pr, pr), lambda i: (i, 0, 0, 0)),
                   pl.BlockSpec((1, ts // c, SMALL_PAD), lambda i: (i, 0, 0))],
        out_shape=[jax.ShapeDtypeStruct((t, bw), F32),
                   jax.ShapeDtypeStruct((t, bw), BF16),
                   jax.ShapeDtypeStruct((t, bw), BF16),
                   jax.ShapeDtypeStruct((t, bw), BF16),
                   jax.ShapeDtypeStruct((t // pr, HEADS, pr, pr), BF16),
                   jax.ShapeDtypeStruct((n_tiles, ts // c, SMALL_PAD), F32)],
        scratch_shapes=[pltpu.VMEM((B_CONV_CH // LANES, ts + SUBLANES, LANES), F32),
                        pltpu.VMEM((B_CONV_CH // LANES, ts, LANES), F32)],
        compiler_params=pltpu.CompilerParams(
            dimension_semantics=("arbitrary",), vmem_limit_bytes=VMEM_LIMIT),
        name="dn_prep",
    )(pb, pb, cw, alog, dtb, tril)

    np_seq = seq // pr
    seq3 = lambda x: x.reshape(batch, seq, x.shape[-1])
    sp = B_SCAN_PAIRS
    blk3 = pl.BlockSpec((batch, sp * pr, bw), lambda i: (0, i, 0))
    y = pl.pallas_call(
        _dn_scan_kernel,
        grid=(np_seq // sp,),
        in_specs=[blk3, blk3, blk3, blk3,
                  pl.BlockSpec((batch, sp, HEADS, pr, pr), lambda i: (0, i, 0, 0, 0)),
                  pl.BlockSpec((batch, sp, pr // c, SMALL_PAD), lambda i: (0, i, 0, 0)),
                  pl.BlockSpec((batch, sp * pr, bw), lambda i: (0, i, B_CONV_CH // bw)),
                  pl.BlockSpec((1, HEAD_DIM), const)],
        out_specs=blk3,
        out_shape=jax.ShapeDtypeStruct((batch, seq, bw), BF16),
        scratch_shapes=[pltpu.VMEM((batch * HEADS, HEAD_DIM, HEAD_DIM), F32)],
        compiler_params=pltpu.CompilerParams(
            dimension_semantics=("arbitrary",), vmem_limit_bytes=VMEM_LIMIT),
        name="dn_scan",
    )(seq3(u), seq3(w), seq3(qd), seq3(kd),
      qk.reshape(batch, np_seq, HEADS, pr, pr),
      ge.reshape(batch, np_seq, pr // c, SMALL_PAD),
      seq3(pb), nw)
    return y.reshape(t, bw)


def _band_buckets():
    i = np.arange(C_BLOCK)[:, None]
    j = np.arange(2 * C_BLOCK)[None, :]
    dist = i + C_BLOCK - j
    max_exact = N_BUCKETS // 2
    d_f = np.maximum(dist, 1).astype(np.float32)
    large = max_exact + (np.log(d_f / np.float32(max_exact)) / np.float32(math.log(MAX_DISTANCE / max_exact))
                         * np.float32(N_BUCKETS - max_exact)).astype(np.int32)
    large = np.minimum(large, N_BUCKETS - 1)
    bucket = np.where(dist < max_exact, dist, large)
    return np.where((dist >= 0) & (dist < WINDOW), bucket, -1).astype(np.int32)


def _swa_kernel(rb_ref, sink_ref, cur_ref, prev_ref, bucket_ref, o_ref, bias_ref):
    blk = C_BLOCK
    bw = BRANCH_WIDTH
    n = pl.program_id(1)

    @pl.when((pl.program_id(0) == 0) & (n == 0))
    def _():
        bucket = bucket_ref[...]
        for h in range(C_Q_HEADS):
            acc = jnp.full((blk, 2 * blk), NEG_BIG, F32)
            for bk in range(N_BUCKETS):
                acc = jnp.where(bucket == bk, rb_ref[bk, h], acc)
            bias_ref[h] = acc

    kv = jnp.concatenate([prev_ref[...], cur_ref[:, bw:bw + 2 * C_KV_WIDTH]], axis=0)
    rows = kv.shape[0]
    lane = lax.broadcasted_iota(jnp.int32, (rows, C_KV_WIDTH), 1)

    def halves(x, j):
        own = jnp.where((lane >= j * C_HEAD_DIM) & (lane < (j + 1) * C_HEAD_DIM), x, 0.0)
        other = pltpu.roll(own, C_HEAD_DIM, 1)
        pair = (own, other) if j == 0 else (other, own)
        return [t.astype(BF16) for t in pair]

    kz = [halves(kv[:, 0:C_KV_WIDTH], j) for j in range(C_KV_HEADS)]
    vz = [halves(kv[:, C_KV_WIDTH:], j) for j in range(C_KV_HEADS)]
    scale = C_HEAD_DIM ** -0.5
    group = C_Q_HEADS // C_KV_HEADS
    key_ok = lax.broadcasted_iota(jnp.int32, (1, 2 * blk), 1) >= jnp.where(n > 0, 0, blk)

    chains = [(i, h) for i in range(C_TILE_BLOCKS) for h in range(C_Q_HEADS)]
    qrows = lambda i: slice(i * blk, (i + 1) * blk)
    krows = lambda i: slice(i * blk, (i + 2) * blk)
    q2 = [[(cur_ref[qrows(i), p * LANES:(p + 1) * LANES] * scale).astype(BF16)
           for p in range(C_Q_HEADS // 2)] for i in range(C_TILE_BLOCKS)]
    logits = [_dot_nt(q2[i][h // 2], kz[h // group][h % 2][krows(i), :]) + bias_ref[h] for i, h in chains]
    logits = [jnp.where(key_ok, lg, NEG_BIG) if i == 0 else lg for (i, h), lg in zip(chains, logits)]
    mx = [jnp.maximum(jnp.max(lg, axis=-1, keepdims=True), sink_ref[h]) for (i, h), lg in zip(chains, logits)]
    pr = [jnp.exp(lg - m) for lg, m in zip(logits, mx)]
    den = [jnp.sum(x, axis=-1, keepdims=True) + jnp.exp(sink_ref[h] - m) for (i, h), x, m in zip(chains, pr, mx)]
    pv = [_dot(x.astype(BF16), vz[h // group][h % 2][krows(i), :]) for (i, h), x in zip(chains, pr)]
    for i in range(C_TILE_BLOCKS):
        for p in range(C_Q_HEADS // 2):
            a, b = i * C_Q_HEADS + 2 * p, i * C_Q_HEADS + 2 * p + 1
            out = pv[a] / den[a] + pv[b] / den[b]
            gate = cur_ref[qrows(i), bw + 2 * C_KV_WIDTH + p * LANES:bw + 2 * C_KV_WIDTH + (p + 1) * LANES]
            o_ref[qrows(i), p * LANES:(p + 1) * LANES] = (out * _silu(gate)).astype(BF16)


def _swa(pc, rel_bias, sinks, batch, seq):
    blk = C_BLOCK
    nb = seq // blk
    bucket = jnp.asarray(_band_buckets())
    kv_col = BRANCH_WIDTH // (2 * C_KV_WIDTH)
    smem = pl.BlockSpec(memory_space=pltpu.SMEM)
    tb = C_TILE_BLOCKS
    nt = nb // tb
    return pl.pallas_call(
        _swa_kernel,
        grid=(batch, nt),
        in_specs=[smem, smem,
                  pl.BlockSpec((tb * blk, PC_WIDTH), lambda b, i: (b * nt + i, 0)),
                  pl.BlockSpec((blk, 2 * C_KV_WIDTH), lambda b, i: (b * nb + jnp.maximum(i * tb - 1, 0), kv_col)),
                  pl.BlockSpec((blk, 2 * blk), lambda b, i: (0, 0))],
        out_specs=pl.BlockSpec((tb * blk, BRANCH_WIDTH), lambda b, i: (b * nt + i, 0)),
        out_shape=jax.ShapeDtypeStruct((batch * seq, BRANCH_WIDTH), BF16),
        scratch_shapes=[pltpu.VMEM((C_Q_HEADS, blk, 2 * blk), F32)],
        compiler_params=pltpu.CompilerParams(
            dimension_semantics=("arbitrary", "arbitrary"), vmem_limit_bytes=VMEM_LIMIT),
        name="swa",
    )(rel_bias, sinks, pc, pc, bucket)


def _lane_pad(v, offset):
    return jnp.zeros((1, SMALL_PAD), F32).at[0, offset:offset + v.shape[0]].set(v.astype(F32))


def kernel(x, norm_w, w_in, conv_w, a_log, dt_bias, lb_param, norm_a, norm_b, sinks, rel_bias,
           w_branch, w_out, final_norm):
    batch, seq, _ = x.shape
    depth = w_in.shape[0]
    x2 = x.reshape(batch * seq, D_MODEL)
    fn = final_norm.reshape(1, D_MODEL)
    w16 = w_in.astype(BF16)
    wbr = w_branch.astype(BF16)
    wo = w_out.astype(BF16)
    for l in range(depth):
        nw = norm_w[l].reshape(1, D_MODEL)
        pa, pb, pc = _inproj(x2, nw, w16, l)
        ya = _hgrn2(pa, lb_param, norm_a[l].reshape(1, HEAD_DIM), batch, seq, l)
        yb = _deltanet(pb, conv_w[l], _lane_pad(a_log[l], HEADS), _lane_pad(dt_bias[l], HEADS),
                       norm_b[l].reshape(1, HEAD_DIM), batch, seq)
        yc = _swa(pc, rel_bias, sinks[l], batch, seq)
        x2 = _merge(x2, nw, w16, ya, yb, yc, wbr, wo, fn, layer=l, final=(l == depth - 1))
    return x2.reshape(batch, seq, D_MODEL)
```

```python
import functools
import math

import numpy as np
import jax
import jax.numpy as jnp
from jax import lax
from jax.experimental import pallas as pl
from jax.experimental.pallas import tpu as pltpu

F32 = jnp.float32
BF16 = jnp.bfloat16

D_MODEL = 1024
BRANCH_WIDTH = D_MODEL // 2
N_BRANCHES = 3
EPS = 1e-6
HEADS = 4
HEAD_DIM = BRANCH_WIDTH // HEADS
B_CONV = 4
B_CONV_CH = 3 * BRANCH_WIDTH
C_Q_HEADS = 8
C_KV_HEADS = 2
C_HEAD_DIM = BRANCH_WIDTH // C_Q_HEADS
C_KV_WIDTH = C_KV_HEADS * C_HEAD_DIM
WINDOW = 128
C_BLOCK = 128
C_TILE_BLOCKS = 8
C_LOCKSTEP_BLOCKS = 2
N_BUCKETS = 32
MAX_DISTANCE = 128
LANES = 128
SUBLANES = 8
NEG_BIG = -1e30
LOG2E = math.log2(math.e)

A_CHUNK = 128
A_TILE_CHUNKS = 8
B_CHUNK = 64
B_PAIR = 2 * B_CHUNK
B_TILE = 512
B_SCAN_PAIRS = 4
PROJ_ROWS = 512
SMALL_PAD = LANES
VMEM_LIMIT = 52 * 1024 * 1024

PA_WIDTH = 4 * BRANCH_WIDTH
PB_WIDTH = B_CONV_CH + BRANCH_WIDTH + SMALL_PAD
PC_WIDTH = 2 * BRANCH_WIDTH + 2 * C_KV_WIDTH
W_AB = PA_WIDTH + PB_WIDTH - (SMALL_PAD - 2 * HEADS)
W_ABC = W_AB + PC_WIDTH


def _dot(a, b):
    return jnp.dot(a, b, preferred_element_type=F32)


def _dot_nt(a, b):
    return lax.dot_general(a, b, (((1,), (1,)), ((), ())), preferred_element_type=F32)


def _dot_tn(a, b):
    return lax.dot_general(a, b, (((0,), (0,)), ((), ())), preferred_element_type=F32)


def _sigmoid(x):
    return 0.5 * jnp.tanh(0.5 * x) + 0.5


def _silu(x):
    h = 0.5 * x
    return h * jnp.tanh(h) + h


def _cumsum_rows(tril16, x):
    hi = x.astype(BF16)
    r1 = x - hi.astype(F32)
    mid = r1.astype(BF16)
    lo = (r1 - mid.astype(F32)).astype(BF16)
    return _dot(tril16, hi) + (_dot(tril16, mid) + _dot(tril16, lo))


def _softplus(x):
    return jnp.maximum(x, 0.0) + jnp.log(1.0 + jnp.exp(-jnp.abs(x)))


def _rms(x, w):
    return x * lax.rsqrt(jnp.mean(x * x, axis=-1, keepdims=True) + EPS) * w


def _resident(shape):
    return pl.BlockSpec(shape, lambda i: (0,) * len(shape), pipeline_mode=pl.Buffered(1))


def _layer_resident(shape, layer):
    return pl.BlockSpec((None,) + tuple(shape[1:]), lambda i: (layer,) + (0,) * (len(shape) - 1),
                        pipeline_mode=pl.Buffered(1))


def _inproj_kernel(x_ref, nw_ref, w_ref, pa_ref, pb_ref, pc_ref, wc_ref):
    @pl.when(pl.program_id(0) == 0)
    def _():
        wc_ref[...] = w_ref[:, W_AB:W_ABC]

    h = _rms(x_ref[...], nw_ref[...]).astype(BF16)
    for o_ref, src_ref, base in ((pa_ref, w_ref, 0), (pb_ref, w_ref, PA_WIDTH), (pc_ref, wc_ref, 0)):
        n = o_ref.shape[1]
        for j in range(0, n, BRANCH_WIDTH):
            wd = min(BRANCH_WIDTH, n - j)
            o_ref[:, j:j + wd] = _dot(h, src_ref[:, base + j:base + j + wd])


def _inproj(x2, nw, w, layer):
    t = x2.shape[0]
    tm = PROJ_ROWS
    return pl.pallas_call(
        _inproj_kernel,
        grid=(t // tm,),
        in_specs=[pl.BlockSpec((tm, D_MODEL), lambda i: (i, 0)),
                  _resident((1, D_MODEL)),
                  _layer_resident(w.shape, layer)],
        out_specs=[pl.BlockSpec((tm, PA_WIDTH), lambda i: (i, 0)),
                   pl.BlockSpec((tm, PB_WIDTH), lambda i: (i, 0)),
                   pl.BlockSpec((tm, PC_WIDTH), lambda i: (i, 0))],
        out_shape=[jax.ShapeDtypeStruct((t, PA_WIDTH), F32),
                   jax.ShapeDtypeStruct((t, PB_WIDTH), F32),
                   jax.ShapeDtypeStruct((t, PC_WIDTH), F32)],
        scratch_shapes=[pltpu.VMEM((D_MODEL, PC_WIDTH), BF16)],
        compiler_params=pltpu.CompilerParams(
            dimension_semantics=("arbitrary",), vmem_limit_bytes=VMEM_LIMIT),
        name="inproj",
    )(x2, nw, w)


def _merge_kernel(x_ref, nw_ref, w_ref, ya_ref, yb_ref, yc_ref, wbr_ref, wo_ref, fn_ref,
                  o_ref, wg_ref, *, final):
    @pl.when(pl.program_id(0) == 0)
    def _():
        wg_ref[...] = w_ref[:, W_ABC:]

    x = x_ref[...]
    h = _rms(x, nw_ref[...]).astype(BF16)
    merged = None
    for n, y_ref in enumerate((ya_ref, yb_ref, yc_ref)):
        gate = _sigmoid(_dot(h, wg_ref[:, n * D_MODEL:(n + 1) * D_MODEL]))
        term = gate * _dot(y_ref[...], wbr_ref[n])
        merged = term if merged is None else merged + term
    out = x + _dot(merged.astype(BF16), wo_ref[...])
    if final:
        out = _rms(out, fn_ref[...])
    o_ref[...] = out


def _merge(x2, nw, w, ya, yb, yc, wbr, wo, fn, layer, final):
    t = x2.shape[0]
    tm = PROJ_ROWS
    row = lambda i: (i, 0)
    return pl.pallas_call(
        functools.partial(_merge_kernel, final=final),
        grid=(t // tm,),
        in_specs=[pl.BlockSpec((tm, D_MODEL), row),
                  _resident((1, D_MODEL)),
                  _layer_resident(w.shape, layer),
                  pl.BlockSpec((tm, BRANCH_WIDTH), row),
                  pl.BlockSpec((tm, BRANCH_WIDTH), row),
                  pl.BlockSpec((tm, BRANCH_WIDTH), row),
                  _layer_resident(wbr.shape, layer),
                  _layer_resident(wo.shape, layer),
                  _resident((1, D_MODEL))],
        out_specs=pl.BlockSpec((tm, D_MODEL), row),
        out_shape=jax.ShapeDtypeStruct((t, D_MODEL), F32),
        scratch_shapes=[pltpu.VMEM((D_MODEL, N_BRANCHES * D_MODEL), BF16)],
        compiler_params=pltpu.CompilerParams(
            dimension_semantics=("arbitrary",), vmem_limit_bytes=VMEM_LIMIT),
        name="merge",
    )(x2, nw, w, ya, yb, yc, wbr, wo, fn)


def _a_levels():
    return [1 << i for i in range(int(math.log2(A_CHUNK)))]


def _a_level_ids():
    idx = np.arange(A_CHUNK)
    ids = np.where(idx[:, None] == idx[None, :], 0, -1).astype(np.int32)
    for i, m in enumerate(_a_levels()):
        blk = idx // (2 * m)
        upper = (idx & m) != 0
        ids[(blk[:, None] == blk[None, :]) & upper[:, None] & ~upper[None, :]] = i + 1
    return ids


def _hgrn2_kernel(pa_ref, lbp_ref, nw_ref, tril_ref, lvl_ref, o_ref,
                  state_ref, b_ref, gp_ref, *, layer):
    c = A_CHUNK
    bw = BRANCH_WIDTH

    @pl.when(pl.program_id(1) == 0)
    def _():
        state_ref[...] = jnp.zeros_like(state_ref)
        gp_ref[...] = jnp.zeros_like(gp_ref)

    lbp = lbp_ref[...]
    ex = jnp.exp(lbp - jnp.max(lbp, axis=0, keepdims=True))
    sm = ex / jnp.sum(ex, axis=0, keepdims=True)
    lb = jnp.zeros((1, bw), F32)
    for j in range(1, layer + 1):
        lb = lb + sm[j:j + 1, :]

    hsl = [slice(h * HEAD_DIM, (h + 1) * HEAD_DIM) for h in range(HEADS)]
    row = lax.broadcasted_iota(jnp.int32, (c, 1), 0)
    lvl = lvl_ref[...]
    t1 = jnp.log(lb)
    t2_lb = jnp.log1p(-lb)

    def intra_chunk(ci):
        rows = slice(ci * c, (ci + 1) * c)
        bc_ref = b_ref.at[ci]
        gc_ref = gp_ref.at[ci]
        q = _silu(pa_ref[rows, 0:bw])
        z = pa_ref[rows, bw:2 * bw]
        v = pa_ref[rows, 2 * bw:3 * bw].astype(BF16)
        log_sig = jnp.minimum(z, 0.0) - jnp.log(1.0 + jnp.exp(-jnp.abs(z)))
        sig_neg = 0.5 - 0.5 * jnp.tanh(0.5 * z)
        t2 = t2_lb + log_sig
        lf = jnp.maximum(t1, t2) + jnp.log(1.0 + jnp.exp(-jnp.abs(t1 - t2)))
        k = (1.0 - lb) * sig_neg

        lf2 = lf * LOG2E
        b = _cumsum_rows(tril_ref[...], lf2)
        bc_ref[...] = b
        gc_ref[SUBLANES:SUBLANES + c, :] = lf2

        q16 = q.astype(BF16)
        k16 = k.astype(BF16)
        diag = lvl == 0
        acc = [jnp.where(diag, _dot_nt(q16[:, sl], k16[:, sl]), 0.0) for sl in hsl]

        for li, m in enumerate(_a_levels()):
            blocks = range(0, c, 2 * m)
            if m == 1:
                neg = jnp.where((row & 1) == 1, lf2, 0.0)
            elif m == 2:
                r4 = row & 3
                g_next = gc_ref[SUBLANES + 1:SUBLANES + 1 + c, :]
                g_prev = gc_ref[SUBLANES - 1:SUBLANES - 1 + c, :]
                neg = jnp.where(r4 == 0, g_next, jnp.where(r4 == 1, 0.0, jnp.where(r4 == 2, lf2, lf2 + g_prev)))
            elif m < SUBLANES:
                neg = jnp.concatenate(
                    [-jnp.abs(bc_ref[lo:lo + 2 * m, :] - bc_ref[lo + m - 1:lo + m, :]) for lo in blocks], axis=0)
            else:
                pieces = []
                for lo in blocks:
                    anchor = bc_ref[lo + m - 1:lo + m, :]
                    pieces += [anchor - bc_ref[lo:lo + m, :], bc_ref[lo + m:lo + 2 * m, :] - anchor]
                neg = jnp.concatenate(pieces, axis=0)
            if m < SUBLANES:
                x = jnp.where((row & m) != 0, q, k)
            else:
                x = jnp.concatenate(
                    [t for lo in blocks for t in (k[lo:lo + m, :], q[lo + m:lo + 2 * m, :])], axis=0)
            xt = (x * jnp.exp2(neg)).astype(BF16)
            sel = lvl == li + 1
            for h, sl in enumerate(hsl):
                acc[h] = jnp.where(sel, _dot_nt(xt[:, sl], xt[:, sl]), acc[h])

        b_end = bc_ref[c - 1:c, :]
        qd = (q * jnp.exp2(b)).astype(BF16)
        kd = (k * jnp.exp2(b_end - b)).astype(BF16)
        intra = [_dot(acc[h].astype(BF16), v[:, hsl[h]]) for h in range(HEADS)]
        upd = [_dot_tn(v[:, hsl[h]], kd[:, hsl[h]]) for h in range(HEADS)]
        return intra, upd, qd, jnp.exp2(b_end)

    parts = [intra_chunk(ci) for ci in range(A_TILE_CHUNKS)]
    nw = nw_ref[...]
    sts = [state_ref[h] for h in range(HEADS)]
    for ci, (intra, upd, qd, s_decay) in enumerate(parts):
        rows = slice(ci * c, (ci + 1) * c)
        inter = [_dot_nt(qd[:, hsl[h]], sts[h].astype(BF16)) for h in range(HEADS)]
        sts = [sts[h] * s_decay[:, hsl[h]] + upd[h] for h in range(HEADS)]
        for h, sl in enumerate(hsl):
            o = intra[h] + inter[h]
            y = o * lax.rsqrt(jnp.mean(o * o, axis=-1, keepdims=True) + EPS) * nw
            gate = pa_ref[rows, 3 * bw + h * HEAD_DIM:3 * bw + (h + 1) * HEAD_DIM]
            o_ref[rows, sl] = (y * _silu(gate)).astype(BF16)
    for h in range(HEADS):
        state_ref[h] = sts[h]


def _hgrn2(pa, lbp, nw, batch, seq, layer):
    c = A_CHUNK
    tc = A_TILE_CHUNKS
    nt = seq // (c * tc)
    tril = jnp.asarray(np.tril(np.ones((c, c), np.float32)), dtype=BF16)
    lvl = jnp.asarray(_a_level_ids())
    const2 = lambda b, i: (0, 0)
    return pl.pallas_call(
        functools.partial(_hgrn2_kernel, layer=layer),
        grid=(batch, nt),
        in_specs=[pl.BlockSpec((tc * c, PA_WIDTH), lambda b, i: (b * nt + i, 0)),
                  pl.BlockSpec(lbp.shape, const2),
                  pl.BlockSpec((1, HEAD_DIM), const2),
                  pl.BlockSpec((c, c), const2),
                  pl.BlockSpec((c, c), const2)],
        out_specs=pl.BlockSpec((tc * c, BRANCH_WIDTH), lambda b, i: (b * nt + i, 0)),
        out_shape=jax.ShapeDtypeStruct((batch * seq, BRANCH_WIDTH), BF16),
        scratch_shapes=[pltpu.VMEM((HEADS, HEAD_DIM, HEAD_DIM), F32),
                        pltpu.VMEM((tc, c, BRANCH_WIDTH), F32),
                        pltpu.VMEM((tc, c + 2 * SUBLANES, BRANCH_WIDTH), F32)],
        compiler_params=pltpu.CompilerParams(
            dimension_semantics=("arbitrary", "arbitrary"), vmem_limit_bytes=VMEM_LIMIT),
        name="hgrn2",
    )(pa, lbp, nw, tril, lvl)


def _b_levels():
    return [1 << i for i in range(int(math.log2(B_CHUNK)))]


def _unit_lower_inverses(mats, ti, si):
    eye = (ti == si).astype(F32)
    ts = [eye] * len(mats)
    for s in _b_levels():
        shift = int(math.log2(2 * s))
        off = ((ti >> shift) == (si >> shift)) & ((ti & s) != 0) & ((si & s) == 0)
        a_off = [jnp.where(off, a, 0.0) for a in mats]
        if s == 1:
            ts = [t - ao for t, ao in zip(ts, a_off)]
        else:
            t16 = [t.astype(BF16) for t in ts]
            inner = [_dot(ao.astype(BF16), t) for ao, t in zip(a_off, t16)]
            outer = [_dot(t, x.astype(BF16)) for t, x in zip(t16, inner)]
            ts = [t - x for t, x in zip(ts, outer)]
    return ts


def _dn_prep_kernel(pb_ref, halo_ref, cw_ref, alog_ref, dtb_ref, tril_ref,
                    u_ref, w_ref, qd_ref, kd_ref, qk_ref, ge_ref, xp_ref, qkv_ref, *, tiles_per_seq):
    ts = B_TILE
    c = B_CHUNK
    pr = B_PAIR
    bw = BRANCH_WIDTH
    sl8 = SUBLANES

    keep = jnp.where(pl.program_id(0) % tiles_per_seq == 0, 0.0, 1.0)
    half = ts // 2
    for s in range(B_CONV_CH // LANES):
        lanes = slice(s * LANES, (s + 1) * LANES)
        xp_ref[s, 0:sl8, :] = halo_ref[:, lanes] * keep
        xp_ref[s, sl8:sl8 + ts, :] = pb_ref[:, lanes]
    for s in range(B_CONV_CH // LANES):
        lanes = slice(s * LANES, (s + 1) * LANES)
        cw = [cw_ref[j:j + 1, lanes] for j in range(B_CONV)]
        ld = {off: xp_ref[s, pl.ds(sl8 + off, half, stride=2), :] for off in range(1 - B_CONV, 2)}
        even = None
        odd = None
        for j in range(B_CONV):
            te = cw[j] * ld[j - (B_CONV - 1)]
            to = cw[j] * ld[j - (B_CONV - 1) + 1]
            even = te if even is None else even + te
            odd = to if odd is None else odd + to
        qkv_ref[s, pl.ds(0, half, stride=2), :] = _silu(even)
        qkv_ref[s, pl.ds(1, half, stride=2), :] = _silu(odd)

    small = pb_ref[:, B_CONV_CH + bw:B_CONV_CH + bw + SMALL_PAD]
    beta_all = _sigmoid(small)
    g_all = -jnp.exp(alog_ref[...]) * _softplus(small + dtb_ref[...])
    gcum = jnp.concatenate(
        [_cumsum_rows(tril_ref[...], g_all[p * pr:(p + 1) * pr, :]) for p in range(ts // pr)], axis=0)
    gcum_t = gcum.T
    for j in range(ts // c):
        ge_ref[0, j:j + 1, :] = jnp.exp(gcum[j * c + c - 1:j * c + c, :])

    ti = lax.broadcasted_iota(jnp.int32, (pr, pr), 0)
    si = lax.broadcasted_iota(jnp.int32, (pr, pr), 1)
    same = (ti >> int(math.log2(c))) == (si >> int(math.log2(c)))
    incl = same & (ti >= si)
    first_chunk = lax.broadcasted_iota(jnp.int32, (pr, 1), 0) < c

    probs = [(h, p) for h in range(HEADS) for p in range(ts // pr)]
    k16s, kb16s, q16s, decays, rhs = [], [], [], [], []
    for h, p in probs:
        sl = slice(h * HEAD_DIM, (h + 1) * HEAD_DIM)
        rows = slice(p * pr, (p + 1) * pr)
        q = qkv_ref[h, rows, :]
        k = qkv_ref[HEADS + h, rows, :]
        v = qkv_ref[2 * HEADS + h, rows, :]
        q = q * lax.rsqrt(jnp.sum(q * q, axis=-1, keepdims=True) + EPS) * (HEAD_DIM ** -0.5)
        k = k * lax.rsqrt(jnp.sum(k * k, axis=-1, keepdims=True) + EPS)
        beta = beta_all[rows, h:h + 1]
        gc = gcum[rows, HEADS + h:HEADS + h + 1]
        gr = gcum_t[HEADS + h:HEADS + h + 1, rows]
        g_last = jnp.where(first_chunk, gc[c - 1:c, :], gc[pr - 1:pr, :])
        egc = jnp.exp(gc)
        kb = k * beta
        decays.append(jnp.where(incl, jnp.exp(jnp.minimum(gc - gr, 0.0)), 0.0))
        k16s.append(k.astype(BF16))
        kb16s.append(kb.astype(BF16))
        q16s.append(q.astype(BF16))
        rhs.append(jnp.concatenate([v * beta, kb * egc], axis=1).astype(BF16))
        qd_ref[rows, sl] = (q * egc).astype(BF16)
        kd_ref[rows, sl] = (k * jnp.exp(g_last - gc)).astype(BF16)

    kk = [_dot_nt(kb16, k16) for kb16, k16 in zip(kb16s, k16s)]
    qk = [_dot_nt(q16, k16) for q16, k16 in zip(q16s, k16s)]
    mats = [x * d for x, d in zip(kk, decays)]
    tinv = _unit_lower_inverses(mats, ti, si)
    uw = [_dot(t.astype(BF16), r) for t, r in zip(tinv, rhs)]
    for i, (h, p) in enumerate(probs):
        sl = slice(h * HEAD_DIM, (h + 1) * HEAD_DIM)
        rows = slice(p * pr, (p + 1) * pr)
        u_ref[rows, sl] = uw[i][:, 0:HEAD_DIM]
        w_ref[rows, sl] = uw[i][:, HEAD_DIM:].astype(BF16)
        qk_ref[p, h] = (qk[i] * decays[i]).astype(BF16)


def _dn_scan_kernel(u_ref, w_ref, qd_ref, kd_ref, qk_ref, ge_ref, z_ref, nw_ref, o_ref, state_ref):
    c = B_CHUNK
    batch = u_ref.shape[0]

    @pl.when(pl.program_id(0) == 0)
    def _():
        state_ref[...] = jnp.zeros_like(state_ref)

    nw = nw_ref[...]
    chains = [(b, h) for b in range(batch) for h in range(HEADS)]
    hsl = lambda h: slice(h * HEAD_DIM, (h + 1) * HEAD_DIM)
    states = [state_ref[b * HEADS + h] for b, h in chains]
    for pi in range(B_SCAN_PAIRS):
        o_inter = [[] for _ in chains]
        v_new = [[] for _ in chains]
        for j in range(B_PAIR // c):
            rows = slice(pi * B_PAIR + j * c, pi * B_PAIR + (j + 1) * c)
            st16 = [st.astype(BF16) for st in states]
            lhs = [jnp.concatenate([w_ref[b, rows, hsl(h)], qd_ref[b, rows, hsl(h)]], axis=0)
                   for b, h in chains]
            prod = [_dot(x, s) for x, s in zip(lhs, st16)]
            vn16 = []
            for i, (b, h) in enumerate(chains):
                vn = (u_ref[b, rows, hsl(h)] - prod[i][0:c, :]).astype(BF16)
                vn16.append(vn)
                v_new[i].append(vn)
                o_inter[i].append(prod[i][c:, :])
            upd = [_dot_tn(kd_ref[b, rows, hsl(h)], vn) for (b, h), vn in zip(chains, vn16)]
            states = [st * ge_ref[b, pi, j:j + 1, HEADS + h:HEADS + h + 1] + x
                      for (b, h), st, x in zip(chains, states, upd)]
        intra = [_dot(qk_ref[b, pi, h], jnp.concatenate(v_new[i], axis=0)) for i, (b, h) in enumerate(chains)]
        prows = slice(pi * B_PAIR, (pi + 1) * B_PAIR)
        for i, (b, h) in enumerate(chains):
            o = jnp.concatenate(o_inter[i], axis=0) + intra[i]
            y = o * lax.rsqrt(jnp.mean(o * o, axis=-1, keepdims=True) + EPS) * nw
            o_ref[b, prows, hsl(h)] = (y * _silu(z_ref[b, prows, hsl(h)])).astype(BF16)
    for i, (b, h) in enumerate(chains):
        state_ref[b * HEADS + h] = states[i]


def _deltanet(pb, cw, alog, dtb, nw, batch, seq):
    c = B_CHUNK
    ts = B_TILE
    pr = B_PAIR
    bw = BRANCH_WIDTH
    t = batch * seq
    n_tiles = t // ts
    tril = jnp.asarray(np.kron(np.eye(pr // c, dtype=np.float32), np.tril(np.ones((c, c), np.float32))),
                       dtype=BF16)
    const = lambda i: (0, 0)
    row = lambda i: (i, 0)
    halo_blocks = ts // SUBLANES
    u, w, qd, kd, qk, ge = pl.pallas_call(
        functools.partial(_dn_prep_kernel, tiles_per_seq=seq // ts),
        grid=(n_tiles,),
        in_specs=[pl.BlockSpec((ts, PB_WIDTH), row),
                  pl.BlockSpec((SUBLANES, B_CONV_CH), lambda i: (jnp.maximum(i * halo_blocks - 1, 0), 0)),
                  pl.BlockSpec(cw.shape, const),
                  pl.BlockSpec((1, SMALL_PAD), const),
                  pl.BlockSpec((1, SMALL_PAD), const),
                  pl.BlockSpec((pr, pr), const)],
        out_specs=[pl.BlockSpec((ts, bw), row),
                   pl.BlockSpec((ts, bw), row),
                   pl.BlockSpec((ts, bw), row),
                   pl.BlockSpec((ts, bw), row),
                   pl.BlockSpec((ts // pr, HEADS, pr, pr), lambda i: (i, 0, 0, 0)),
                   pl.BlockSpec((1, ts // c, SMALL_PAD), lambda i: (i, 0, 0))],
        out_shape=[jax.ShapeDtypeStruct((t, bw), F32),
                   jax.ShapeDtypeStruct((t, bw), BF16),
                   jax.ShapeDtypeStruct((t, bw), BF16),
                   jax.ShapeDtypeStruct((t, bw), BF16),
                   jax.ShapeDtypeStruct((t // pr, HEADS, pr, pr), BF16),
                   jax.ShapeDtypeStruct((n_tiles, ts // c, SMALL_PAD), F32)],
        scratch_shapes=[pltpu.VMEM((B_CONV_CH // LANES, ts + SUBLANES, LANES), F32),
                        pltpu.VMEM((B_CONV_CH // LANES, ts, LANES), F32)],
        compiler_params=pltpu.CompilerParams(
            dimension_semantics=("arbitrary",), vmem_limit_bytes=VMEM_LIMIT),
        name="dn_prep",
    )(pb, pb, cw, alog, dtb, tril)

    np_seq = seq // pr
    seq3 = lambda x: x.reshape(batch, seq, x.shape[-1])
    sp = B_SCAN_PAIRS
    blk3 = pl.BlockSpec((batch, sp * pr, bw), lambda i: (0, i, 0))
    y = pl.pallas_call(
        _dn_scan_kernel,
        grid=(np_seq // sp,),
        in_specs=[blk3, blk3, blk3, blk3,
                  pl.BlockSpec((batch, sp, HEADS, pr, pr), lambda i: (0, i, 0, 0, 0)),
                  pl.BlockSpec((batch, sp, pr // c, SMALL_PAD), lambda i: (0, i, 0, 0)),
                  pl.BlockSpec((batch, sp * pr, bw), lambda i: (0, i, B_CONV_CH // bw)),
                  pl.BlockSpec((1, HEAD_DIM), const)],
        out_specs=blk3,
        out_shape=jax.ShapeDtypeStruct((batch, seq, bw), BF16),
        scratch_shapes=[pltpu.VMEM((batch * HEADS, HEAD_DIM, HEAD_DIM), F32)],
        compiler_params=pltpu.CompilerParams(
            dimension_semantics=("arbitrary",), vmem_limit_bytes=VMEM_LIMIT),
        name="dn_scan",
    )(seq3(u), seq3(w), seq3(qd), seq3(kd),
      qk.reshape(batch, np_seq, HEADS, pr, pr),
      ge.reshape(batch, np_seq, pr // c, SMALL_PAD),
      seq3(pb), nw)
    return y.reshape(t, bw)


def _band_buckets():
    assert WINDOW == C_BLOCK
    r = np.arange(C_BLOCK)[:, None]
    c = np.arange(C_BLOCK)[None, :]
    dist = np.where(c > r, r + C_BLOCK - c, r - c)
    max_exact = N_BUCKETS // 2
    d_f = np.maximum(dist, 1).astype(np.float32)
    large = max_exact + (np.log(d_f / np.float32(max_exact)) / np.float32(math.log(MAX_DISTANCE / max_exact))
                         * np.float32(N_BUCKETS - max_exact)).astype(np.int32)
    large = np.minimum(large, N_BUCKETS - 1)
    return np.where(dist < max_exact, dist, large).astype(np.int32)


def _band_halves():
    r = np.arange(C_BLOCK)[:, None]
    c = np.arange(C_BLOCK)[None, :]
    return np.stack([c > r, c <= r]).astype(np.float32)


def _swa_kernel(rb_ref, sink_ref, cur_ref, prev_ref, bucket_ref, half_ref, o_ref, bias_ref):
    blk = C_BLOCK
    bw = BRANCH_WIDTH
    n = pl.program_id(1)

    @pl.when((pl.program_id(0) == 0) & (n == 0))
    def _():
        bucket = bucket_ref[...]
        for h in range(C_Q_HEADS):
            acc = jnp.zeros((blk, blk), F32)
            for bk in range(N_BUCKETS):
                acc = jnp.where(bucket == bk, rb_ref[bk, h], acc)
            bias_ref[h] = acc

    kv = jnp.concatenate([prev_ref[...], cur_ref[:, bw:bw + 2 * C_KV_WIDTH]], axis=0)
    rows = kv.shape[0]
    lane = lax.broadcasted_iota(jnp.int32, (rows, C_KV_WIDTH), 1)

    def halves(x, j):
        own = jnp.where((lane >= j * C_HEAD_DIM) & (lane < (j + 1) * C_HEAD_DIM), x, 0.0)
        other = pltpu.roll(own, C_HEAD_DIM, 1)
        pair = (own, other) if j == 0 else (other, own)
        return [t.astype(BF16) for t in pair]

    kz = [halves(kv[:, 0:C_KV_WIDTH], j) for j in range(C_KV_HEADS)]
    vz = [halves(kv[:, C_KV_WIDTH:], j) for j in range(C_KV_HEADS)]
    scale = C_HEAD_DIM ** -0.5
    group = C_Q_HEADS // C_KV_HEADS
    from_prev = (lax.broadcasted_iota(jnp.int32, (blk, blk), 1) > lax.broadcasted_iota(jnp.int32, (blk, blk), 0))

    qrows = lambda i: slice(i * blk, (i + 1) * blk)
    krows = lambda i: slice(i * blk, (i + 2) * blk)
    for first in range(0, C_TILE_BLOCKS, C_LOCKSTEP_BLOCKS):
        blocks = range(first, first + C_LOCKSTEP_BLOCKS)
        chains = [(i, h) for i in blocks for h in range(C_Q_HEADS)]
        q2 = {i: [(cur_ref[qrows(i), p * LANES:(p + 1) * LANES] * scale).astype(BF16)
                  for p in range(C_Q_HEADS // 2)] for i in blocks}
        both = [_dot_nt(q2[i][h // 2], kz[h // group][h % 2][krows(i), :]) for i, h in chains]
        logits = []
        for (i, h), lg in zip(chains, both):
            prev_part = lg[:, 0:blk]
            if i == 0:
                prev_part = prev_part + jnp.where(n > 0, 0.0, NEG_BIG)
            logits.append(jnp.where(from_prev, prev_part, lg[:, blk:]) + bias_ref[h])
        mx = [jnp.maximum(jnp.max(lg, axis=-1, keepdims=True), sink_ref[h])
              for (i, h), lg in zip(chains, logits)]
        pr = [jnp.exp(lg - m) for lg, m in zip(logits, mx)]
        den = [jnp.sum(x, axis=-1, keepdims=True) + jnp.exp(sink_ref[h] - m)
               for (i, h), x, m in zip(chains, pr, mx)]
        pr16 = [x.astype(BF16) for x in pr]
        pcat = [jnp.concatenate([x * half_ref[0], x * half_ref[1]], axis=1) for x in pr16]
        pv = [_dot(x, vz[h // group][h % 2][krows(i), :]) for (i, h), x in zip(chains, pcat)]
        for bi, i in enumerate(blocks):
            for p in range(C_Q_HEADS // 2):
                a, b = bi * C_Q_HEADS + 2 * p, bi * C_Q_HEADS + 2 * p + 1
                out = pv[a] / den[a] + pv[b] / den[b]
                gate = cur_ref[qrows(i), bw + 2 * C_KV_WIDTH + p * LANES:bw + 2 * C_KV_WIDTH + (p + 1) * LANES]
                o_ref[qrows(i), p * LANES:(p + 1) * LANES] = (out * _silu(gate)).astype(BF16)


def _swa(pc, rel_bias, sinks, batch, seq):
    blk = C_BLOCK
    nb = seq // blk
    bucket = jnp.asarray(_band_buckets())
    halves = jnp.asarray(_band_halves(), dtype=BF16)
    kv_col = BRANCH_WIDTH // (2 * C_KV_WIDTH)
    smem = pl.BlockSpec(memory_space=pltpu.SMEM)
    tb = C_TILE_BLOCKS
    nt = nb // tb
    return pl.pallas_call(
        _swa_kernel,
        grid=(batch, nt),
        in_specs=[smem, smem,
                  pl.BlockSpec((tb * blk, PC_WIDTH), lambda b, i: (b * nt + i, 0)),
                  pl.BlockSpec((blk, 2 * C_KV_WIDTH), lambda b, i: (b * nb + jnp.maximum(i * tb - 1, 0), kv_col)),
                  pl.BlockSpec((blk, blk), lambda b, i: (0, 0)),
                  pl.BlockSpec((2, blk, blk), lambda b, i: (0, 0, 0))],
        out_specs=pl.BlockSpec((tb * blk, BRANCH_WIDTH), lambda b, i: (b * nt + i, 0)),
        out_shape=jax.ShapeDtypeStruct((batch * seq, BRANCH_WIDTH), BF16),
        scratch_shapes=[pltpu.VMEM((C_Q_HEADS, blk, blk), F32)],
        compiler_params=pltpu.CompilerParams(
            dimension_semantics=("arbitrary", "arbitrary"), vmem_limit_bytes=VMEM_LIMIT),
        name="swa",
    )(rel_bias, sinks, pc, pc, bucket, halves)


def _lane_pad(v, offset):
    return jnp.zeros((1, SMALL_PAD), F32).at[0, offset:offset + v.shape[0]].set(v.astype(F32))


def kernel(x, norm_w, w_in, conv_w, a_log, dt_bias, lb_param, norm_a, norm_b, sinks, rel_bias,
           w_branch, w_out, final_norm):
    batch, seq, _ = x.shape
    depth = w_in.shape[0]
    x2 = x.reshape(batch * seq, D_MODEL)
    fn = final_norm.reshape(1, D_MODEL)
    w16 = w_in.astype(BF16)
    wbr = w_branch.astype(BF16)
    wo = w_out.astype(BF16)
    for l in range(depth):
        nw = norm_w[l].reshape(1, D_MODEL)
        pa, pb, pc = _inproj(x2, nw, w16, l)
        ya = _hgrn2(pa, lb_param, norm_a[l].reshape(1, HEAD_DIM), batch, seq, l)
        yb = _deltanet(pb, conv_w[l], _lane_pad(a_log[l], HEADS), _lane_pad(dt_bias[l], HEADS),
                       norm_b[l].reshape(1, HEAD_DIM), batch, seq)
        yc = _swa(pc, rel_bias, sinks[l], batch, seq)
        x2 = _merge(x2, nw, w16, ya, yb, yc, wbr, wo, fn, layer=l, final=(l == depth - 1))
    return x2.reshape(batch, seq, D_MODEL)
```

```python
import functools
import math

import numpy as np
import jax
import jax.numpy as jnp
from jax import lax
from jax.experimental import pallas as pl
from jax.experimental.pallas import tpu as pltpu

F32 = jnp.float32
BF16 = jnp.bfloat16

D_MODEL = 1024
BRANCH_WIDTH = D_MODEL // 2
N_BRANCHES = 3
EPS = 1e-6
HEADS = 4
HEAD_DIM = BRANCH_WIDTH // HEADS
B_CONV = 4
B_CONV_CH = 3 * BRANCH_WIDTH
C_Q_HEADS = 8
C_KV_HEADS = 2
C_HEAD_DIM = BRANCH_WIDTH // C_Q_HEADS
C_KV_WIDTH = C_KV_HEADS * C_HEAD_DIM
WINDOW = 128
C_BLOCK = 128
C_TILE_BLOCKS = 8
C_LOCKSTEP_BLOCKS = 2
N_BUCKETS = 32
MAX_DISTANCE = 128
LANES = 128
SUBLANES = 8
NEG_BIG = -1e30
LOG2E = math.log2(math.e)

A_CHUNK = 128
A_TILE_CHUNKS = 8
B_CHUNK = 64
B_PAIR = 2 * B_CHUNK
B_TILE = 512
B_SCAN_PAIRS = 4
PROJ_ROWS = 512
SMALL_PAD = LANES
VMEM_LIMIT = 52 * 1024 * 1024

PA_WIDTH = 4 * BRANCH_WIDTH
PB_WIDTH = B_CONV_CH + BRANCH_WIDTH + SMALL_PAD
PC_WIDTH = 2 * BRANCH_WIDTH + 2 * C_KV_WIDTH
W_AB = PA_WIDTH + PB_WIDTH - (SMALL_PAD - 2 * HEADS)
W_ABC = W_AB + PC_WIDTH


def _dot(a, b):
    return jnp.dot(a, b, preferred_element_type=F32)


def _dot_nt(a, b):
    return lax.dot_general(a, b, (((1,), (1,)), ((), ())), preferred_element_type=F32)


def _dot_tn(a, b):
    return lax.dot_general(a, b, (((0,), (0,)), ((), ())), preferred_element_type=F32)


def _sigmoid(x):
    return 0.5 * jnp.tanh(0.5 * x) + 0.5


def _silu(x):
    h = 0.5 * x
    return h * jnp.tanh(h) + h


def _cumsum_rows(tril16, x):
    hi = x.astype(BF16)
    r1 = x - hi.astype(F32)
    mid = r1.astype(BF16)
    lo = (r1 - mid.astype(F32)).astype(BF16)
    return _dot(tril16, hi) + (_dot(tril16, mid) + _dot(tril16, lo))


def _softplus(x):
    return jnp.maximum(x, 0.0) + jnp.log(1.0 + jnp.exp(-jnp.abs(x)))


def _rms(x, w):
    return x * lax.rsqrt(jnp.mean(x * x, axis=-1, keepdims=True) + EPS) * w


def _resident(shape):
    return pl.BlockSpec(shape, lambda i: (0,) * len(shape), pipeline_mode=pl.Buffered(1))


def _layer_resident(shape, layer):
    return pl.BlockSpec((None,) + tuple(shape[1:]), lambda i: (layer,) + (0,) * (len(shape) - 1),
                        pipeline_mode=pl.Buffered(1))


def _inproj_kernel(x_ref, nw_ref, w_ref, pa_ref, pb_ref, pc_ref, wc_ref):
    @pl.when(pl.program_id(0) == 0)
    def _():
        wc_ref[...] = w_ref[:, W_AB:W_ABC]

    h = _rms(x_ref[...], nw_ref[...]).astype(BF16)
    for o_ref, src_ref, base in ((pa_ref, w_ref, 0), (pb_ref, w_ref, PA_WIDTH), (pc_ref, wc_ref, 0)):
        n = o_ref.shape[1]
        for j in range(0, n, BRANCH_WIDTH):
            wd = min(BRANCH_WIDTH, n - j)
            o_ref[:, j:j + wd] = _dot(h, src_ref[:, base + j:base + j + wd])


def _inproj(x2, nw, w, layer):
    t = x2.shape[0]
    tm = PROJ_ROWS
    return pl.pallas_call(
        _inproj_kernel,
        grid=(t // tm,),
        in_specs=[pl.BlockSpec((tm, D_MODEL), lambda i: (i, 0)),
                  _resident((1, D_MODEL)),
                  _layer_resident(w.shape, layer)],
        out_specs=[pl.BlockSpec((tm, PA_WIDTH), lambda i: (i, 0)),
                   pl.BlockSpec((tm, PB_WIDTH), lambda i: (i, 0)),
                   pl.BlockSpec((tm, PC_WIDTH), lambda i: (i, 0))],
        out_shape=[jax.ShapeDtypeStruct((t, PA_WIDTH), F32),
                   jax.ShapeDtypeStruct((t, PB_WIDTH), F32),
                   jax.ShapeDtypeStruct((t, PC_WIDTH), F32)],
        scratch_shapes=[pltpu.VMEM((D_MODEL, PC_WIDTH), BF16)],
        compiler_params=pltpu.CompilerParams(
            dimension_semantics=("arbitrary",), vmem_limit_bytes=VMEM_LIMIT),
        name="inproj",
    )(x2, nw, w)


def _merge_kernel(x_ref, nw_ref, w_ref, ya_ref, yb_ref, yc_ref, wbr_ref, wo_ref, fn_ref,
                  o_ref, wg_ref, *, final):
    @pl.when(pl.program_id(0) == 0)
    def _():
        wg_ref[...] = w_ref[:, W_ABC:]

    x = x_ref[...]
    h = _rms(x, nw_ref[...]).astype(BF16)
    merged = None
    for n, y_ref in enumerate((ya_ref, yb_ref, yc_ref)):
        gate = _sigmoid(_dot(h, wg_ref[:, n * D_MODEL:(n + 1) * D_MODEL]))
        term = gate * _dot(y_ref[...], wbr_ref[n])
        merged = term if merged is None else merged + term
    out = x + _dot(merged.astype(BF16), wo_ref[...])
    if final:
        out = _rms(out, fn_ref[...])
    o_ref[...] = out


def _merge(x2, nw, w, ya, yb, yc, wbr, wo, fn, layer, final):
    t = x2.shape[0]
    tm = PROJ_ROWS
    row = lambda i: (i, 0)
    return pl.pallas_call(
        functools.partial(_merge_kernel, final=final),
        grid=(t // tm,),
        in_specs=[pl.BlockSpec((tm, D_MODEL), row),
                  _resident((1, D_MODEL)),
                  _layer_resident(w.shape, layer),
                  pl.BlockSpec((tm, BRANCH_WIDTH), row),
                  pl.BlockSpec((tm, BRANCH_WIDTH), row),
                  pl.BlockSpec((tm, BRANCH_WIDTH), row),
                  _layer_resident(wbr.shape, layer),
                  _layer_resident(wo.shape, layer),
                  _resident((1, D_MODEL))],
        out_specs=pl.BlockSpec((tm, D_MODEL), row),
        out_shape=jax.ShapeDtypeStruct((t, D_MODEL), F32),
        scratch_shapes=[pltpu.VMEM((D_MODEL, N_BRANCHES * D_MODEL), BF16)],
        compiler_params=pltpu.CompilerParams(
            dimension_semantics=("arbitrary",), vmem_limit_bytes=VMEM_LIMIT),
        name="merge",
    )(x2, nw, w, ya, yb, yc, wbr, wo, fn)


def _a_levels():
    return [1 << i for i in range(int(math.log2(A_CHUNK)))]


def _a_level_ids():
    idx = np.arange(A_CHUNK)
    ids = np.where(idx[:, None] == idx[None, :], 0, -1).astype(np.int32)
    for i, m in enumerate(_a_levels()):
        blk = idx // (2 * m)
        upper = (idx & m) != 0
        ids[(blk[:, None] == blk[None, :]) & upper[:, None] & ~upper[None, :]] = i + 1
    return ids


def _hgrn2_kernel(pa_ref, lbp_ref, nw_ref, tril_ref, lvl_ref, o_ref,
                  state_ref, b_ref, gp_ref, *, layer):
    c = A_CHUNK
    bw = BRANCH_WIDTH

    @pl.when(pl.program_id(1) == 0)
    def _():
        state_ref[...] = jnp.zeros_like(state_ref)
        gp_ref[...] = jnp.zeros_like(gp_ref)

    lbp = lbp_ref[...]
    ex = jnp.exp(lbp - jnp.max(lbp, axis=0, keepdims=True))
    sm = ex / jnp.sum(ex, axis=0, keepdims=True)
    lb = jnp.zeros((1, bw), F32)
    for j in range(1, layer + 1):
        lb = lb + sm[j:j + 1, :]

    hsl = [slice(h * HEAD_DIM, (h + 1) * HEAD_DIM) for h in range(HEADS)]
    row = lax.broadcasted_iota(jnp.int32, (c, 1), 0)
    lvl = lvl_ref[...]
    t1 = jnp.log(lb)
    t2_lb = jnp.log1p(-lb)

    def intra_chunk(ci):
        rows = slice(ci * c, (ci + 1) * c)
        bc_ref = b_ref.at[ci]
        gc_ref = gp_ref.at[ci]
        q = _silu(pa_ref[rows, 0:bw])
        z = pa_ref[rows, bw:2 * bw]
        v = pa_ref[rows, 2 * bw:3 * bw].astype(BF16)
        log_sig = jnp.minimum(z, 0.0) - jnp.log(1.0 + jnp.exp(-jnp.abs(z)))
        sig_neg = 0.5 - 0.5 * jnp.tanh(0.5 * z)
        t2 = t2_lb + log_sig
        lf = jnp.maximum(t1, t2) + jnp.log(1.0 + jnp.exp(-jnp.abs(t1 - t2)))
        k = (1.0 - lb) * sig_neg

        lf2 = lf * LOG2E
        b = _cumsum_rows(tril_ref[...], lf2)
        bc_ref[...] = b
        gc_ref[SUBLANES:SUBLANES + c, :] = lf2

        intra = []
        for sl in hsl:
            qh = q[:, sl]
            kh = k[:, sl]
            lfh = lf2[:, sl]
            acc = jnp.where(lvl == 0, _dot_nt(qh.astype(BF16), kh.astype(BF16)), 0.0)
            for li, m in enumerate(_a_levels()):
                blocks = range(0, c, 2 * m)
                if m == 1:
                    neg = jnp.where((row & 1) == 1, lfh, 0.0)
                elif m == 2:
                    r4 = row & 3
                    g_next = gc_ref[SUBLANES + 1:SUBLANES + 1 + c, sl]
                    g_prev = gc_ref[SUBLANES - 1:SUBLANES - 1 + c, sl]
                    neg = jnp.where(r4 == 0, g_next,
                                    jnp.where(r4 == 1, 0.0, jnp.where(r4 == 2, lfh, lfh + g_prev)))
                elif m < SUBLANES:
                    neg = jnp.concatenate(
                        [-jnp.abs(bc_ref[lo:lo + 2 * m, sl] - bc_ref[lo + m - 1:lo + m, sl]) for lo in blocks],
                        axis=0)
                else:
                    pieces = []
                    for lo in blocks:
                        anchor = bc_ref[lo + m - 1:lo + m, sl]
                        pieces += [anchor - bc_ref[lo:lo + m, sl], bc_ref[lo + m:lo + 2 * m, sl] - anchor]
                    neg = jnp.concatenate(pieces, axis=0)
                if m < SUBLANES:
                    x = jnp.where((row & m) != 0, qh, kh)
                else:
                    x = jnp.concatenate(
                        [t for lo in blocks for t in (kh[lo:lo + m, :], qh[lo + m:lo + 2 * m, :])], axis=0)
                xt = (x * jnp.exp2(neg)).astype(BF16)
                acc = jnp.where(lvl == li + 1, _dot_nt(xt, xt), acc)
            intra.append(_dot(acc.astype(BF16), v[:, sl]))

        b_end = bc_ref[c - 1:c, :]
        qd = (q * jnp.exp2(b)).astype(BF16)
        kd = (k * jnp.exp2(b_end - b)).astype(BF16)
        upd = [_dot_tn(v[:, hsl[h]], kd[:, hsl[h]]) for h in range(HEADS)]
        return intra, upd, qd, jnp.exp2(b_end)

    parts = [intra_chunk(ci) for ci in range(A_TILE_CHUNKS)]
    nw = nw_ref[...]
    sts = [state_ref[h] for h in range(HEADS)]
    for ci, (intra, upd, qd, s_decay) in enumerate(parts):
        rows = slice(ci * c, (ci + 1) * c)
        inter = [_dot_nt(qd[:, hsl[h]], sts[h].astype(BF16)) for h in range(HEADS)]
        sts = [sts[h] * s_decay[:, hsl[h]] + upd[h] for h in range(HEADS)]
        for h, sl in enumerate(hsl):
            o = intra[h] + inter[h]
            y = o * lax.rsqrt(jnp.mean(o * o, axis=-1, keepdims=True) + EPS) * nw
            gate = pa_ref[rows, 3 * bw + h * HEAD_DIM:3 * bw + (h + 1) * HEAD_DIM]
            o_ref[rows, sl] = (y * _silu(gate)).astype(BF16)
    for h in range(HEADS):
        state_ref[h] = sts[h]


def _hgrn2(pa, lbp, nw, batch, seq, layer):
    c = A_CHUNK
    tc = A_TILE_CHUNKS
    nt = seq // (c * tc)
    tril = jnp.asarray(np.tril(np.ones((c, c), np.float32)), dtype=BF16)
    lvl = jnp.asarray(_a_level_ids())
    const2 = lambda b, i: (0, 0)
    return pl.pallas_call(
        functools.partial(_hgrn2_kernel, layer=layer),
        grid=(batch, nt),
        in_specs=[pl.BlockSpec((tc * c, PA_WIDTH), lambda b, i: (b * nt + i, 0)),
                  pl.BlockSpec(lbp.shape, const2),
                  pl.BlockSpec((1, HEAD_DIM), const2),
                  pl.BlockSpec((c, c), const2),
                  pl.BlockSpec((c, c), const2)],
        out_specs=pl.BlockSpec((tc * c, BRANCH_WIDTH), lambda b, i: (b * nt + i, 0)),
        out_shape=jax.ShapeDtypeStruct((batch * seq, BRANCH_WIDTH), BF16),
        scratch_shapes=[pltpu.VMEM((HEADS, HEAD_DIM, HEAD_DIM), F32),
                        pltpu.VMEM((tc, c, BRANCH_WIDTH), F32),
                        pltpu.VMEM((tc, c + 2 * SUBLANES, BRANCH_WIDTH), F32)],
        compiler_params=pltpu.CompilerParams(
            dimension_semantics=("arbitrary", "arbitrary"), vmem_limit_bytes=VMEM_LIMIT),
        name="hgrn2",
    )(pa, lbp, nw, tril, lvl)


def _b_levels():
    return [1 << i for i in range(int(math.log2(B_CHUNK)))]


def _unit_lower_inverses(mats, ti, si):
    eye = (ti == si).astype(F32)
    ts = [eye] * len(mats)
    for s in _b_levels():
        shift = int(math.log2(2 * s))
        off = ((ti >> shift) == (si >> shift)) & ((ti & s) != 0) & ((si & s) == 0)
        a_off = [jnp.where(off, a, 0.0) for a in mats]
        if s == 1:
            ts = [t - ao for t, ao in zip(ts, a_off)]
        else:
            t16 = [t.astype(BF16) for t in ts]
            inner = [_dot(ao.astype(BF16), t) for ao, t in zip(a_off, t16)]
            outer = [_dot(t, x.astype(BF16)) for t, x in zip(t16, inner)]
            ts = [t - x for t, x in zip(ts, outer)]
    return ts


def _dn_prep_kernel(pb_ref, halo_ref, cw_ref, alog_ref, dtb_ref, tril_ref,
                    u_ref, w_ref, qd_ref, kd_ref, qk_ref, ge_ref, xp_ref, qkv_ref, *, tiles_per_seq):
    ts = B_TILE
    c = B_CHUNK
    pr = B_PAIR
    bw = BRANCH_WIDTH
    sl8 = SUBLANES

    keep = jnp.where(pl.program_id(0) % tiles_per_seq == 0, 0.0, 1.0)
    half = ts // 2
    for s in range(B_CONV_CH // LANES):
        lanes = slice(s * LANES, (s + 1) * LANES)
        xp_ref[s, 0:sl8, :] = halo_ref[:, lanes] * keep
        xp_ref[s, sl8:sl8 + ts, :] = pb_ref[:, lanes]
    for s in range(B_CONV_CH // LANES):
        lanes = slice(s * LANES, (s + 1) * LANES)
        cw = [0.5 * cw_ref[j:j + 1, lanes] for j in range(B_CONV)]
        ld = {off: xp_ref[s, pl.ds(sl8 + off, half, stride=2), :] for off in range(1 - B_CONV, 2)}
        even = None
        odd = None
        for j in range(B_CONV):
            te = cw[j] * ld[j - (B_CONV - 1)]
            to = cw[j] * ld[j - (B_CONV - 1) + 1]
            even = te if even is None else even + te
            odd = to if odd is None else odd + to
        qkv_ref[s, pl.ds(0, half, stride=2), :] = even * jnp.tanh(even) + even
        qkv_ref[s, pl.ds(1, half, stride=2), :] = odd * jnp.tanh(odd) + odd

    small = pb_ref[:, B_CONV_CH + bw:B_CONV_CH + bw + SMALL_PAD]
    beta_all = _sigmoid(small)
    g_all = -jnp.exp(alog_ref[...]) * _softplus(small + dtb_ref[...])
    gcum = jnp.concatenate(
        [_cumsum_rows(tril_ref[...], g_all[p * pr:(p + 1) * pr, :]) for p in range(ts // pr)], axis=0)
    gcum_t = gcum.T
    for j in range(ts // c):
        ge_ref[0, j:j + 1, :] = jnp.exp(gcum[j * c + c - 1:j * c + c, :])

    ti = lax.broadcasted_iota(jnp.int32, (pr, pr), 0)
    si = lax.broadcasted_iota(jnp.int32, (pr, pr), 1)
    same = (ti >> int(math.log2(c))) == (si >> int(math.log2(c)))
    incl = same & (ti >= si)
    first_chunk = lax.broadcasted_iota(jnp.int32, (pr, 1), 0) < c

    probs = [(h, p) for h in range(HEADS) for p in range(ts // pr)]
    k16s, kb16s, q16s, decays, rhs = [], [], [], [], []
    for h, p in probs:
        sl = slice(h * HEAD_DIM, (h + 1) * HEAD_DIM)
        rows = slice(p * pr, (p + 1) * pr)
        q = qkv_ref[h, rows, :]
        k = qkv_ref[HEADS + h, rows, :]
        v = qkv_ref[2 * HEADS + h, rows, :]
        q = q * (lax.rsqrt(jnp.sum(q * q, axis=-1, keepdims=True) + EPS) * (HEAD_DIM ** -0.5))
        k = k * lax.rsqrt(jnp.sum(k * k, axis=-1, keepdims=True) + EPS)
        beta = beta_all[rows, h:h + 1]
        gc = gcum[rows, HEADS + h:HEADS + h + 1]
        gr = gcum_t[HEADS + h:HEADS + h + 1, rows]
        g_last = jnp.where(first_chunk, gc[c - 1:c, :], gc[pr - 1:pr, :])
        egc = jnp.exp(gc)
        kb = k * beta
        decays.append(jnp.where(incl, jnp.exp(jnp.minimum(gc - gr, 0.0)), 0.0))
        k16s.append(k.astype(BF16))
        kb16s.append(kb.astype(BF16))
        q16s.append(q.astype(BF16))
        rhs.append(jnp.concatenate([v * beta, kb * egc], axis=1).astype(BF16))
        qd_ref[rows, sl] = (q * egc).astype(BF16)
        kd_ref[rows, sl] = (k * jnp.exp(g_last - gc)).astype(BF16)

    kk = [_dot_nt(kb16, k16) for kb16, k16 in zip(kb16s, k16s)]
    qk = [_dot_nt(q16, k16) for q16, k16 in zip(q16s, k16s)]
    mats = [x * d for x, d in zip(kk, decays)]
    tinv = _unit_lower_inverses(mats, ti, si)
    uw = [_dot(t.astype(BF16), r) for t, r in zip(tinv, rhs)]
    for i, (h, p) in enumerate(probs):
        sl = slice(h * HEAD_DIM, (h + 1) * HEAD_DIM)
        rows = slice(p * pr, (p + 1) * pr)
        u_ref[rows, sl] = uw[i][:, 0:HEAD_DIM]
        w_ref[rows, sl] = uw[i][:, HEAD_DIM:].astype(BF16)
        qk_ref[p, h] = (qk[i] * decays[i]).astype(BF16)


def _dn_scan_kernel(u_ref, w_ref, qd_ref, kd_ref, qk_ref, ge_ref, z_ref, nw_ref, o_ref, state_ref):
    c = B_CHUNK
    batch = u_ref.shape[0]

    @pl.when(pl.program_id(0) == 0)
    def _():
        state_ref[...] = jnp.zeros_like(state_ref)

    nw = nw_ref[...]
    chains = [(b, h) for b in range(batch) for h in range(HEADS)]
    hsl = lambda h: slice(h * HEAD_DIM, (h + 1) * HEAD_DIM)
    states = [state_ref[b * HEADS + h] for b, h in chains]
    for pi in range(B_SCAN_PAIRS):
        o_inter = [[] for _ in chains]
        v_new = [[] for _ in chains]
        for j in range(B_PAIR // c):
            rows = slice(pi * B_PAIR + j * c, pi * B_PAIR + (j + 1) * c)
            st16 = [st.astype(BF16) for st in states]
            lhs = [jnp.concatenate([w_ref[b, rows, hsl(h)], qd_ref[b, rows, hsl(h)]], axis=0)
                   for b, h in chains]
            prod = [_dot(x, s) for x, s in zip(lhs, st16)]
            vn16 = []
            for i, (b, h) in enumerate(chains):
                vn = (u_ref[b, rows, hsl(h)] - prod[i][0:c, :]).astype(BF16)
                vn16.append(vn)
                v_new[i].append(vn)
                o_inter[i].append(prod[i][c:, :])
            upd = [_dot_tn(kd_ref[b, rows, hsl(h)], vn) for (b, h), vn in zip(chains, vn16)]
            states = [st * ge_ref[b, pi, j:j + 1, HEADS + h:HEADS + h + 1] + x
                      for (b, h), st, x in zip(chains, states, upd)]
        intra = [_dot(qk_ref[b, pi, h], jnp.concatenate(v_new[i], axis=0)) for i, (b, h) in enumerate(chains)]
        prows = slice(pi * B_PAIR, (pi + 1) * B_PAIR)
        for i, (b, h) in enumerate(chains):
            o = jnp.concatenate(o_inter[i], axis=0) + intra[i]
            y = o * lax.rsqrt(jnp.mean(o * o, axis=-1, keepdims=True) + EPS) * nw
            o_ref[b, prows, hsl(h)] = (y * _silu(z_ref[b, prows, hsl(h)])).astype(BF16)
    for i, (b, h) in enumerate(chains):
        state_ref[b * HEADS + h] = states[i]


def _deltanet(pb, cw, alog, dtb, nw, batch, seq):
    c = B_CHUNK
    ts = B_TILE
    pr = B_PAIR
    bw = BRANCH_WIDTH
    t = batch * seq
    n_tiles = t // ts
    tril = jnp.asarray(np.kron(np.eye(pr // c, dtype=np.float32), np.tril(np.ones((c, c), np.float32))),
                       dtype=BF16)
    const = lambda i: (0, 0)
    row = lambda i: (i, 0)
    halo_blocks = ts // SUBLANES
    u, w, qd, kd, qk, ge = pl.pallas_call(
        functools.partial(_dn_prep_kernel, tiles_per_seq=seq // ts),
        grid=(n_tiles,),
        in_specs=[pl.BlockSpec((ts, PB_WIDTH), row),
                  pl.BlockSpec((SUBLANES, B_CONV_CH), lambda i: (jnp.maximum(i * halo_blocks - 1, 0), 0)),
                  pl.BlockSpec(cw.shape, const),
                  pl.BlockSpec((1, SMALL_PAD), const),
                  pl.BlockSpec((1, SMALL_PAD), const),
                  pl.BlockSpec((pr, pr), const)],
        out_specs=[pl.BlockSpec((ts, bw), row),
                   pl.BlockSpec((ts, bw), row),
                   pl.BlockSpec((ts, bw), row),
                   pl.BlockSpec((ts, bw), row),
                   pl.BlockSpec((ts // pr, HEADS, pr, pr), lambda i: (i, 0, 0, 0)),
                   pl.BlockSpec((1, ts // c, SMALL_PAD), lambda i: (i, 0, 0))],
        out_shape=[jax.ShapeDtypeStruct((t, bw), F32),
                   jax.ShapeDtypeStruct((t, bw), BF16),
                   jax.ShapeDtypeStruct((t, bw), BF16),
                   jax.ShapeDtypeStruct((t, bw), BF16),
                   jax.ShapeDtypeStruct((t // pr, HEADS, pr, pr), BF16),
                   jax.ShapeDtypeStruct((n_tiles, ts // c, SMALL_PAD), F32)],
        scratch_shapes=[pltpu.VMEM((B_CONV_CH // LANES, ts + SUBLANES, LANES), F32),
                        pltpu.VMEM((B_CONV_CH // LANES, ts, LANES), F32)],
        compiler_params=pltpu.CompilerParams(
            dimension_semantics=("arbitrary",), vmem_limit_bytes=VMEM_LIMIT),
        name="dn_prep",
    )(pb, pb, cw, alog, dtb, tril)

    np_seq = seq // pr
    seq3 = lambda x: x.reshape(batch, seq, x.shape[-1])
    sp = B_SCAN_PAIRS
    blk3 = pl.BlockSpec((batch, sp * pr, bw), lambda i: (0, i, 0))
    y = pl.pallas_call(
        _dn_scan_kernel,
        grid=(np_seq // sp,),
        in_specs=[blk3, blk3, blk3, blk3,
                  pl.BlockSpec((batch, sp, HEADS, pr, pr), lambda i: (0, i, 0, 0, 0)),
                  pl.BlockSpec((batch, sp, pr // c, SMALL_PAD), lambda i: (0, i, 0, 0)),
                  pl.BlockSpec((batch, sp * pr, bw), lambda i: (0, i, B_CONV_CH // bw)),
                  pl.BlockSpec((1, HEAD_DIM), const)],
        out_specs=blk3,
        out_shape=jax.ShapeDtypeStruct((batch, seq, bw), BF16),
        scratch_shapes=[pltpu.VMEM((batch * HEADS, HEAD_DIM, HEAD_DIM), F32)],
        compiler_params=pltpu.CompilerParams(
            dimension_semantics=("arbitrary",), vmem_limit_bytes=VMEM_LIMIT),
        name="dn_scan",
    )(seq3(u), seq3(w), seq3(qd), seq3(kd),
      qk.reshape(batch, np_seq, HEADS, pr, pr),
      ge.reshape(batch, np_seq, pr // c, SMALL_PAD),
      seq3(pb), nw)
    return y.reshape(t, bw)


def _band_buckets():
    assert WINDOW == C_BLOCK
    r = np.arange(C_BLOCK)[:, None]
    c = np.arange(C_BLOCK)[None, :]
    dist = np.where(c > r, r + C_BLOCK - c, r - c)
    max_exact = N_BUCKETS // 2
    d_f = np.maximum(dist, 1).astype(np.float32)
    large = max_exact + (np.log(d_f / np.float32(max_exact)) / np.float32(math.log(MAX_DISTANCE / max_exact))
                         * np.float32(N_BUCKETS - max_exact)).astype(np.int32)
    large = np.minimum(large, N_BUCKETS - 1)
    return np.where(dist < max_exact, dist, large).astype(np.int32)


def _band_halves():
    r = np.arange(C_BLOCK)[:, None]
    c = np.arange(C_BLOCK)[None, :]
    return np.stack([c > r, c <= r]).astype(np.float32)


def _swa_kernel(rb_ref, sink_ref, cur_ref, prev_ref, bucket_ref, half_ref, o_ref, bias_ref):
    blk = C_BLOCK
    bw = BRANCH_WIDTH
    n = pl.program_id(1)

    @pl.when((pl.program_id(0) == 0) & (n == 0))
    def _():
        bucket = bucket_ref[...]
        for h in range(C_Q_HEADS):
            acc = jnp.zeros((blk, blk), F32)
            for bk in range(N_BUCKETS):
                acc = jnp.where(bucket == bk, rb_ref[bk, h], acc)
            bias_ref[h] = acc

    kv = jnp.concatenate([prev_ref[...], cur_ref[:, bw:bw + 2 * C_KV_WIDTH]], axis=0)
    rows = kv.shape[0]
    lane = lax.broadcasted_iota(jnp.int32, (rows, C_KV_WIDTH), 1)

    def halves(x, j):
        own = jnp.where((lane >= j * C_HEAD_DIM) & (lane < (j + 1) * C_HEAD_DIM), x, 0.0)
        other = pltpu.roll(own, C_HEAD_DIM, 1)
        pair = (own, other) if j == 0 else (other, own)
        return [t.astype(BF16) for t in pair]

    kz = [halves(kv[:, 0:C_KV_WIDTH], j) for j in range(C_KV_HEADS)]
    vz = [halves(kv[:, C_KV_WIDTH:], j) for j in range(C_KV_HEADS)]
    scale = C_HEAD_DIM ** -0.5
    group = C_Q_HEADS // C_KV_HEADS
    from_prev = (lax.broadcasted_iota(jnp.int32, (blk, blk), 1) > lax.broadcasted_iota(jnp.int32, (blk, blk), 0))

    qrows = lambda i: slice(i * blk, (i + 1) * blk)
    krows = lambda i: slice(i * blk, (i + 2) * blk)
    for first in range(0, C_TILE_BLOCKS, C_LOCKSTEP_BLOCKS):
        blocks = range(first, first + C_LOCKSTEP_BLOCKS)
        chains = [(i, h) for i in blocks for h in range(C_Q_HEADS)]
        q2 = {i: [(cur_ref[qrows(i), p * LANES:(p + 1) * LANES] * scale).astype(BF16)
                  for p in range(C_Q_HEADS // 2)] for i in blocks}
        both = [_dot_nt(q2[i][h // 2], kz[h // group][h % 2][krows(i), :]) for i, h in chains]
        logits = []
        for (i, h), lg in zip(chains, both):
            prev_part = lg[:, 0:blk]
            if i == 0:
                prev_part = prev_part + jnp.where(n > 0, 0.0, NEG_BIG)
            logits.append(jnp.where(from_prev, prev_part, lg[:, blk:]) + bias_ref[h])
        mx = [jnp.maximum(jnp.max(lg, axis=-1, keepdims=True), sink_ref[h])
              for (i, h), lg in zip(chains, logits)]
        pr = [jnp.exp(lg - m) for lg, m in zip(logits, mx)]
        den = [jnp.sum(x, axis=-1, keepdims=True) + jnp.exp(sink_ref[h] - m)
               for (i, h), x, m in zip(chains, pr, mx)]
        pr16 = [x.astype(BF16) for x in pr]
        pcat = [jnp.concatenate([x * half_ref[0], x * half_ref[1]], axis=1) for x in pr16]
        pv = [_dot(x, vz[h // group][h % 2][krows(i), :]) for (i, h), x in zip(chains, pcat)]
        for bi, i in enumerate(blocks):
            for p in range(C_Q_HEADS // 2):
                a, b = bi * C_Q_HEADS + 2 * p, bi * C_Q_HEADS + 2 * p + 1
                out = pv[a] / den[a] + pv[b] / den[b]
                gate = cur_ref[qrows(i), bw + 2 * C_KV_WIDTH + p * LANES:bw + 2 * C_KV_WIDTH + (p + 1) * LANES]
                o_ref[qrows(i), p * LANES:(p + 1) * LANES] = (out * _silu(gate)).astype(BF16)


def _swa(pc, rel_bias, sinks, batch, seq):
    blk = C_BLOCK
    nb = seq // blk
    bucket = jnp.asarray(_band_buckets())
    halves = jnp.asarray(_band_halves(), dtype=BF16)
    kv_col = BRANCH_WIDTH // (2 * C_KV_WIDTH)
    smem = pl.BlockSpec(memory_space=pltpu.SMEM)
    tb = C_TILE_BLOCKS
    nt = nb // tb
    return pl.pallas_call(
        _swa_kernel,
        grid=(batch, nt),
        in_specs=[smem, smem,
                  pl.BlockSpec((tb * blk, PC_WIDTH), lambda b, i: (b * nt + i, 0)),
                  pl.BlockSpec((blk, 2 * C_KV_WIDTH), lambda b, i: (b * nb + jnp.maximum(i * tb - 1, 0), kv_col)),
                  pl.BlockSpec((blk, blk), lambda b, i: (0, 0)),
                  pl.BlockSpec((2, blk, blk), lambda b, i: (0, 0, 0))],
        out_specs=pl.BlockSpec((tb * blk, BRANCH_WIDTH), lambda b, i: (b * nt + i, 0)),
        out_shape=jax.ShapeDtypeStruct((batch * seq, BRANCH_WIDTH), BF16),
        scratch_shapes=[pltpu.VMEM((C_Q_HEADS, blk, blk), F32)],
        compiler_params=pltpu.CompilerParams(
            dimension_semantics=("arbitrary", "arbitrary"), vmem_limit_bytes=VMEM_LIMIT),
        name="swa",
    )(rel_bias, sinks, pc, pc, bucket, halves)


def _lane_pad(v, offset):
    return jnp.zeros((1, SMALL_PAD), F32).at[0, offset:offset + v.shape[0]].set(v.astype(F32))


def kernel(x, norm_w, w_in, conv_w, a_log, dt_bias, lb_param, norm_a, norm_b, sinks, rel_bias,
           w_branch, w_out, final_norm):
    batch, seq, _ = x.shape
    depth = w_in.shape[0]
    x2 = x.reshape(batch * seq, D_MODEL)
    fn = final_norm.reshape(1, D_MODEL)
    w16 = w_in.astype(BF16)
    wbr = w_branch.astype(BF16)
    wo = w_out.astype(BF16)
    for l in range(depth):
        nw = norm_w[l].reshape(1, D_MODEL)
        pa, pb, pc = _inproj(x2, nw, w16, l)
        ya = _hgrn2(pa, lb_param, norm_a[l].reshape(1, HEAD_DIM), batch, seq, l)
        yb = _deltanet(pb, conv_w[l], _lane_pad(a_log[l], HEADS), _lane_pad(dt_bias[l], HEADS),
                       norm_b[l].reshape(1, HEAD_DIM), batch, seq)
        yc = _swa(pc, rel_bias, sinks[l], batch, seq)
        x2 = _merge(x2, nw, w16, ya, yb, yc, wbr, wo, fn, layer=l, final=(l == depth - 1))
    return x2.reshape(batch, seq, D_MODEL)
```

```python
import functools
import math

import numpy as np
import jax
import jax.numpy as jnp
from jax import lax
from jax.experimental import pallas as pl
from jax.experimental.pallas import tpu as pltpu

F32 = jnp.float32
BF16 = jnp.bfloat16

D_MODEL = 1024
BRANCH_WIDTH = D_MODEL // 2
N_BRANCHES = 3
EPS = 1e-6
HEADS = 4
HEAD_DIM = BRANCH_WIDTH // HEADS
B_CONV = 4
B_CONV_CH = 3 * BRANCH_WIDTH
C_Q_HEADS = 8
C_KV_HEADS = 2
C_HEAD_DIM = BRANCH_WIDTH // C_Q_HEADS
C_KV_WIDTH = C_KV_HEADS * C_HEAD_DIM
WINDOW = 128
C_BLOCK = 128
C_LOCKSTEP_BLOCKS = 2
N_BUCKETS = 32
MAX_DISTANCE = 128
LANES = 128
SUBLANES = 8
NEG_BIG = -1e30
LOG2E = math.log2(math.e)

A_CHUNK = 128
A_TILE_CHUNKS = 8
B_CHUNK = 64
B_PAIR = 2 * B_CHUNK
B_TILE = 512
C_TILE_BLOCKS = B_TILE // C_BLOCK
B_LOCKSTEP = 16
B_SCAN_PAIRS = 4
PROJ_ROWS = 512
SMALL_PAD = LANES
VMEM_LIMIT = 52 * 1024 * 1024

PA_WIDTH = 4 * BRANCH_WIDTH
PB_WIDTH = B_CONV_CH + BRANCH_WIDTH + SMALL_PAD
PC_WIDTH = 2 * BRANCH_WIDTH + 2 * C_KV_WIDTH
W_AB = PA_WIDTH + PB_WIDTH - (SMALL_PAD - 2 * HEADS)
W_ABC = W_AB + PC_WIDTH


def _dot(a, b):
    return jnp.dot(a, b, preferred_element_type=F32)


def _dot_nt(a, b):
    return lax.dot_general(a, b, (((1,), (1,)), ((), ())), preferred_element_type=F32)


def _dot_tn(a, b):
    return lax.dot_general(a, b, (((0,), (0,)), ((), ())), preferred_element_type=F32)


def _sigmoid(x):
    return 0.5 * jnp.tanh(0.5 * x) + 0.5


def _silu(x):
    h = 0.5 * x
    return h * jnp.tanh(h) + h


def _cumsum_rows(tril16, x):
    hi = x.astype(BF16)
    r1 = x - hi.astype(F32)
    mid = r1.astype(BF16)
    lo = (r1 - mid.astype(F32)).astype(BF16)
    return _dot(tril16, hi) + (_dot(tril16, mid) + _dot(tril16, lo))


def _softplus(x):
    return jnp.maximum(x, 0.0) + jnp.log(1.0 + jnp.exp(-jnp.abs(x)))


def _rms(x, w):
    return x * lax.rsqrt(jnp.mean(x * x, axis=-1, keepdims=True) + EPS) * w


def _resident(shape):
    return pl.BlockSpec(shape, lambda i: (0,) * len(shape), pipeline_mode=pl.Buffered(1))


def _layer_resident(shape, layer):
    return pl.BlockSpec((None,) + tuple(shape[1:]), lambda i: (layer,) + (0,) * (len(shape) - 1),
                        pipeline_mode=pl.Buffered(1))


def _inproj_kernel(x_ref, nw_ref, w_ref, pa_ref, pb_ref, pc_ref, wc_ref):
    @pl.when(pl.program_id(0) == 0)
    def _():
        wc_ref[...] = w_ref[:, W_AB:W_ABC]

    h = _rms(x_ref[...], nw_ref[...]).astype(BF16)
    for o_ref, src_ref, base in ((pa_ref, w_ref, 0), (pb_ref, w_ref, PA_WIDTH), (pc_ref, wc_ref, 0)):
        n = o_ref.shape[1]
        for j in range(0, n, BRANCH_WIDTH):
            wd = min(BRANCH_WIDTH, n - j)
            o_ref[:, j:j + wd] = _dot(h, src_ref[:, base + j:base + j + wd])


def _inproj(x2, nw, w, layer):
    t = x2.shape[0]
    tm = PROJ_ROWS
    return pl.pallas_call(
        _inproj_kernel,
        grid=(t // tm,),
        in_specs=[pl.BlockSpec((tm, D_MODEL), lambda i: (i, 0)),
                  _resident((1, D_MODEL)),
                  _layer_resident(w.shape, layer)],
        out_specs=[pl.BlockSpec((tm, PA_WIDTH), lambda i: (i, 0)),
                   pl.BlockSpec((tm, PB_WIDTH), lambda i: (i, 0)),
                   pl.BlockSpec((tm, PC_WIDTH), lambda i: (i, 0))],
        out_shape=[jax.ShapeDtypeStruct((t, PA_WIDTH), F32),
                   jax.ShapeDtypeStruct((t, PB_WIDTH), F32),
                   jax.ShapeDtypeStruct((t, PC_WIDTH), F32)],
        scratch_shapes=[pltpu.VMEM((D_MODEL, PC_WIDTH), BF16)],
        compiler_params=pltpu.CompilerParams(
            dimension_semantics=("arbitrary",), vmem_limit_bytes=VMEM_LIMIT),
        name="inproj",
    )(x2, nw, w)


def _merge_kernel(x_ref, nw_ref, w_ref, ya_ref, yb_ref, yc_ref, wbr_ref, wo_ref, fn_ref,
                  o_ref, wg_ref, *, final):
    @pl.when(pl.program_id(0) == 0)
    def _():
        wg_ref[...] = w_ref[:, W_ABC:]

    x = x_ref[...]
    h = _rms(x, nw_ref[...]).astype(BF16)
    merged = None
    for n, y_ref in enumerate((ya_ref, yb_ref, yc_ref)):
        gate = _sigmoid(_dot(h, wg_ref[:, n * D_MODEL:(n + 1) * D_MODEL]))
        term = gate * _dot(y_ref[...], wbr_ref[n])
        merged = term if merged is None else merged + term
    out = x + _dot(merged.astype(BF16), wo_ref[...])
    if final:
        out = _rms(out, fn_ref[...])
    o_ref[...] = out


def _merge(x2, nw, w, ya, yb, yc, wbr, wo, fn, layer, final):
    t = x2.shape[0]
    tm = PROJ_ROWS
    row = lambda i: (i, 0)
    return pl.pallas_call(
        functools.partial(_merge_kernel, final=final),
        grid=(t // tm,),
        in_specs=[pl.BlockSpec((tm, D_MODEL), row),
                  _resident((1, D_MODEL)),
                  _layer_resident(w.shape, layer),
                  pl.BlockSpec((tm, BRANCH_WIDTH), row),
                  pl.BlockSpec((tm, BRANCH_WIDTH), row),
                  pl.BlockSpec((tm, BRANCH_WIDTH), row),
                  _layer_resident(wbr.shape, layer),
                  _layer_resident(wo.shape, layer),
                  _resident((1, D_MODEL))],
        out_specs=pl.BlockSpec((tm, D_MODEL), row),
        out_shape=jax.ShapeDtypeStruct((t, D_MODEL), F32),
        scratch_shapes=[pltpu.VMEM((D_MODEL, N_BRANCHES * D_MODEL), BF16)],
        compiler_params=pltpu.CompilerParams(
            dimension_semantics=("arbitrary",), vmem_limit_bytes=VMEM_LIMIT),
        name="merge",
    )(x2, nw, w, ya, yb, yc, wbr, wo, fn)


def _a_levels():
    return [1 << i for i in range(int(math.log2(A_CHUNK)))]


def _a_level_ids():
    idx = np.arange(A_CHUNK)
    ids = np.where(idx[:, None] == idx[None, :], 0, -1).astype(np.int32)
    for i, m in enumerate(_a_levels()):
        blk = idx // (2 * m)
        upper = (idx & m) != 0
        ids[(blk[:, None] == blk[None, :]) & upper[:, None] & ~upper[None, :]] = i + 1
    return ids


def _hgrn2_kernel(pa_ref, lbp_ref, nw_ref, tril_ref, lvl_ref, o_ref,
                  state_ref, b_ref, gp_ref, *, layer):
    c = A_CHUNK
    bw = BRANCH_WIDTH

    @pl.when(pl.program_id(1) == 0)
    def _():
        state_ref[...] = jnp.zeros_like(state_ref)
        gp_ref[...] = jnp.zeros_like(gp_ref)

    lbp = lbp_ref[...]
    ex = jnp.exp(lbp - jnp.max(lbp, axis=0, keepdims=True))
    sm = ex / jnp.sum(ex, axis=0, keepdims=True)
    lb = jnp.zeros((1, bw), F32)
    for j in range(1, layer + 1):
        lb = lb + sm[j:j + 1, :]

    hsl = [slice(h * HEAD_DIM, (h + 1) * HEAD_DIM) for h in range(HEADS)]
    row = lax.broadcasted_iota(jnp.int32, (c, 1), 0)
    lvl = lvl_ref[...]
    t1 = jnp.log(lb)
    t2_lb = jnp.log1p(-lb)

    def intra_chunk(ci):
        rows = slice(ci * c, (ci + 1) * c)
        bc_ref = b_ref.at[ci]
        gc_ref = gp_ref.at[ci]
        q = _silu(pa_ref[rows, 0:bw])
        z = pa_ref[rows, bw:2 * bw]
        v = pa_ref[rows, 2 * bw:3 * bw].astype(BF16)
        log_sig = jnp.minimum(z, 0.0) - jnp.log(1.0 + jnp.exp(-jnp.abs(z)))
        sig_neg = 0.5 - 0.5 * jnp.tanh(0.5 * z)
        t2 = t2_lb + log_sig
        lf = jnp.maximum(t1, t2) + jnp.log(1.0 + jnp.exp(-jnp.abs(t1 - t2)))
        k = (1.0 - lb) * sig_neg

        lf2 = lf * LOG2E
        b = _cumsum_rows(tril_ref[...], lf2)
        bc_ref[...] = b
        gc_ref[SUBLANES:SUBLANES + c, :] = lf2

        intra = []
        for sl in hsl:
            qh = q[:, sl]
            kh = k[:, sl]
            lfh = lf2[:, sl]
            acc = jnp.where(lvl == 0, _dot_nt(qh.astype(BF16), kh.astype(BF16)), 0.0)
            for li, m in enumerate(_a_levels()):
                blocks = range(0, c, 2 * m)
                if m == 1:
                    neg = jnp.where((row & 1) == 1, lfh, 0.0)
                elif m == 2:
                    r4 = row & 3
                    g_next = gc_ref[SUBLANES + 1:SUBLANES + 1 + c, sl]
                    g_prev = gc_ref[SUBLANES - 1:SUBLANES - 1 + c, sl]
                    neg = jnp.where(r4 == 0, g_next,
                                    jnp.where(r4 == 1, 0.0, jnp.where(r4 == 2, lfh, lfh + g_prev)))
                elif m < SUBLANES:
                    neg = jnp.concatenate(
                        [-jnp.abs(bc_ref[lo:lo + 2 * m, sl] - bc_ref[lo + m - 1:lo + m, sl]) for lo in blocks],
                        axis=0)
                else:
                    pieces = []
                    for lo in blocks:
                        anchor = bc_ref[lo + m - 1:lo + m, sl]
                        pieces += [anchor - bc_ref[lo:lo + m, sl], bc_ref[lo + m:lo + 2 * m, sl] - anchor]
                    neg = jnp.concatenate(pieces, axis=0)
                if m < SUBLANES:
                    x = jnp.where((row & m) != 0, qh, kh)
                else:
                    x = jnp.concatenate(
                        [t for lo in blocks for t in (kh[lo:lo + m, :], qh[lo + m:lo + 2 * m, :])], axis=0)
                xt = (x * jnp.exp2(neg)).astype(BF16)
                acc = jnp.where(lvl == li + 1, _dot_nt(xt, xt), acc)
            intra.append(_dot(acc.astype(BF16), v[:, sl]))

        b_end = bc_ref[c - 1:c, :]
        qd = (q * jnp.exp2(b)).astype(BF16)
        kd = (k * jnp.exp2(b_end - b)).astype(BF16)
        upd = [_dot_tn(v[:, hsl[h]], kd[:, hsl[h]]) for h in range(HEADS)]
        return intra, upd, qd, jnp.exp2(b_end)

    parts = [intra_chunk(ci) for ci in range(A_TILE_CHUNKS)]
    nw = nw_ref[...]
    sts = [state_ref[h] for h in range(HEADS)]
    for ci, (intra, upd, qd, s_decay) in enumerate(parts):
        rows = slice(ci * c, (ci + 1) * c)
        inter = [_dot_nt(qd[:, hsl[h]], sts[h].astype(BF16)) for h in range(HEADS)]
        sts = [sts[h] * s_decay[:, hsl[h]] + upd[h] for h in range(HEADS)]
        for h, sl in enumerate(hsl):
            o = intra[h] + inter[h]
            y = o * lax.rsqrt(jnp.mean(o * o, axis=-1, keepdims=True) + EPS) * nw
            gate = pa_ref[rows, 3 * bw + h * HEAD_DIM:3 * bw + (h + 1) * HEAD_DIM]
            o_ref[rows, sl] = (y * _silu(gate)).astype(BF16)
    for h in range(HEADS):
        state_ref[h] = sts[h]


def _hgrn2(pa, lbp, nw, batch, seq, layer):
    c = A_CHUNK
    tc = A_TILE_CHUNKS
    nt = seq // (c * tc)
    tril = jnp.asarray(np.tril(np.ones((c, c), np.float32)), dtype=BF16)
    lvl = jnp.asarray(_a_level_ids())
    const2 = lambda b, i: (0, 0)
    return pl.pallas_call(
        functools.partial(_hgrn2_kernel, layer=layer),
        grid=(batch, nt),
        in_specs=[pl.BlockSpec((tc * c, PA_WIDTH), lambda b, i: (b * nt + i, 0)),
                  pl.BlockSpec(lbp.shape, const2),
                  pl.BlockSpec((1, HEAD_DIM), const2),
                  pl.BlockSpec((c, c), const2),
                  pl.BlockSpec((c, c), const2)],
        out_specs=pl.BlockSpec((tc * c, BRANCH_WIDTH), lambda b, i: (b * nt + i, 0)),
        out_shape=jax.ShapeDtypeStruct((batch * seq, BRANCH_WIDTH), BF16),
        scratch_shapes=[pltpu.VMEM((HEADS, HEAD_DIM, HEAD_DIM), F32),
                        pltpu.VMEM((tc, c, BRANCH_WIDTH), F32),
                        pltpu.VMEM((tc, c + 2 * SUBLANES, BRANCH_WIDTH), F32)],
        compiler_params=pltpu.CompilerParams(
            dimension_semantics=("arbitrary", "arbitrary"), vmem_limit_bytes=VMEM_LIMIT),
        name="hgrn2",
    )(pa, lbp, nw, tril, lvl)


def _b_levels():
    return [1 << i for i in range(int(math.log2(B_CHUNK)))]


def _unit_lower_inverses(mats, ti, si):
    eye = (ti == si).astype(F32)
    ts = [eye] * len(mats)
    for s in _b_levels():
        shift = int(math.log2(2 * s))
        off = ((ti >> shift) == (si >> shift)) & ((ti & s) != 0) & ((si & s) == 0)
        a_off = [jnp.where(off, a, 0.0) for a in mats]
        if s == 1:
            ts = [t - ao for t, ao in zip(ts, a_off)]
        else:
            t16 = [t.astype(BF16) for t in ts]
            inner = [_dot(ao.astype(BF16), t) for ao, t in zip(a_off, t16)]
            yield
            outer = [_dot(t, x.astype(BF16)) for t, x in zip(t16, inner)]
            yield
            ts = [t - x for t, x in zip(ts, outer)]
    return ts


def _dn_prep_stages(seq_tile, pb_ref, halo_ref, cw_ref, alog_ref, dtb_ref, tril_ref,
                    u_ref, w_ref, qd_ref, kd_ref, qk_ref, ge_ref, xp_ref, qkv_ref):
    ts = B_TILE
    c = B_CHUNK
    pr = B_PAIR
    bw = BRANCH_WIDTH
    sl8 = SUBLANES

    keep = jnp.where(seq_tile == 0, 0.0, 1.0)
    half = ts // 2
    for s in range(B_CONV_CH // LANES):
        lanes = slice(s * LANES, (s + 1) * LANES)
        xp_ref[s, 0:sl8, :] = halo_ref[:, lanes] * keep
        xp_ref[s, sl8:sl8 + ts, :] = pb_ref[:, lanes]
    yield
    for s in range(B_CONV_CH // LANES):
        lanes = slice(s * LANES, (s + 1) * LANES)
        cw = [0.5 * cw_ref[j:j + 1, lanes] for j in range(B_CONV)]
        ld = {off: xp_ref[s, pl.ds(sl8 + off, half, stride=2), :] for off in range(1 - B_CONV, 2)}
        even = None
        odd = None
        for j in range(B_CONV):
            te = cw[j] * ld[j - (B_CONV - 1)]
            to = cw[j] * ld[j - (B_CONV - 1) + 1]
            even = te if even is None else even + te
            odd = to if odd is None else odd + to
        qkv_ref[s, pl.ds(0, half, stride=2), :] = even * jnp.tanh(even) + even
        qkv_ref[s, pl.ds(1, half, stride=2), :] = odd * jnp.tanh(odd) + odd
        yield

    small = pb_ref[:, B_CONV_CH + bw:B_CONV_CH + bw + SMALL_PAD]
    beta_all = _sigmoid(small)
    g_all = -jnp.exp(alog_ref[...]) * _softplus(small + dtb_ref[...])
    gcum = jnp.concatenate(
        [_cumsum_rows(tril_ref[...], g_all[p * pr:(p + 1) * pr, :]) for p in range(ts // pr)], axis=0)
    gcum_t = gcum.T
    for j in range(ts // c):
        ge_ref[0, j:j + 1, :] = jnp.exp(gcum[j * c + c - 1:j * c + c, :])

    ti = lax.broadcasted_iota(jnp.int32, (pr, pr), 0)
    si = lax.broadcasted_iota(jnp.int32, (pr, pr), 1)
    same = (ti >> int(math.log2(c))) == (si >> int(math.log2(c)))
    incl = same & (ti >= si)
    first_chunk = lax.broadcasted_iota(jnp.int32, (pr, 1), 0) < c
    yield

    all_probs = [(h, p) for h in range(HEADS) for p in range(ts // pr)]
    for first in range(0, len(all_probs), B_LOCKSTEP):
        probs = all_probs[first:first + B_LOCKSTEP]
        k16s, kb16s, q16s, decays, rhs = [], [], [], [], []
        for h, p in probs:
            sl = slice(h * HEAD_DIM, (h + 1) * HEAD_DIM)
            rows = slice(p * pr, (p + 1) * pr)
            q = qkv_ref[h, rows, :]
            k = qkv_ref[HEADS + h, rows, :]
            v = qkv_ref[2 * HEADS + h, rows, :]
            q = q * (lax.rsqrt(jnp.sum(q * q, axis=-1, keepdims=True) + EPS) * (HEAD_DIM ** -0.5))
            k = k * lax.rsqrt(jnp.sum(k * k, axis=-1, keepdims=True) + EPS)
            beta = beta_all[rows, h:h + 1]
            gc = gcum[rows, HEADS + h:HEADS + h + 1]
            gr = gcum_t[HEADS + h:HEADS + h + 1, rows]
            g_last = jnp.where(first_chunk, gc[c - 1:c, :], gc[pr - 1:pr, :])
            egc = jnp.exp(gc)
            kb = k * beta
            decays.append(jnp.where(incl, jnp.exp(jnp.minimum(gc - gr, 0.0)), 0.0))
            k16s.append(k.astype(BF16))
            kb16s.append(kb.astype(BF16))
            q16s.append(q.astype(BF16))
            rhs.append(jnp.concatenate([v * beta, kb * egc], axis=1).astype(BF16))
            qd_ref[rows, sl] = (q * egc).astype(BF16)
            kd_ref[rows, sl] = (k * jnp.exp(g_last - gc)).astype(BF16)
            yield

        kk = [_dot_nt(kb16, k16) for kb16, k16 in zip(kb16s, k16s)]
        yield
        qk = [_dot_nt(q16, k16) for q16, k16 in zip(q16s, k16s)]
        yield
        mats = [x * d for x, d in zip(kk, decays)]
        tinv = yield from _unit_lower_inverses(mats, ti, si)
        uw = [_dot(t.astype(BF16), r) for t, r in zip(tinv, rhs)]
        yield
        for i, (h, p) in enumerate(probs):
            sl = slice(h * HEAD_DIM, (h + 1) * HEAD_DIM)
            rows = slice(p * pr, (p + 1) * pr)
            u_ref[rows, sl] = uw[i][:, 0:HEAD_DIM]
            w_ref[rows, sl] = uw[i][:, HEAD_DIM:].astype(BF16)
            qk_ref[p, h] = (qk[i] * decays[i]).astype(BF16)


def _dn_scan_kernel(u_ref, w_ref, qd_ref, kd_ref, qk_ref, ge_ref, z_ref, nw_ref, o_ref, state_ref):
    c = B_CHUNK
    batch = u_ref.shape[0]

    @pl.when(pl.program_id(0) == 0)
    def _():
        state_ref[...] = jnp.zeros_like(state_ref)

    nw = nw_ref[...]
    chains = [(b, h) for b in range(batch) for h in range(HEADS)]
    hsl = lambda h: slice(h * HEAD_DIM, (h + 1) * HEAD_DIM)
    states = [state_ref[b * HEADS + h] for b, h in chains]
    for pi in range(B_SCAN_PAIRS):
        o_inter = [[] for _ in chains]
        v_new = [[] for _ in chains]
        for j in range(B_PAIR // c):
            rows = slice(pi * B_PAIR + j * c, pi * B_PAIR + (j + 1) * c)
            st16 = [st.astype(BF16) for st in states]
            lhs = [jnp.concatenate([w_ref[b, rows, hsl(h)], qd_ref[b, rows, hsl(h)]], axis=0)
                   for b, h in chains]
            prod = [_dot(x, s) for x, s in zip(lhs, st16)]
            vn16 = []
            for i, (b, h) in enumerate(chains):
                vn = (u_ref[b, rows, hsl(h)] - prod[i][0:c, :]).astype(BF16)
                vn16.append(vn)
                v_new[i].append(vn)
                o_inter[i].append(prod[i][c:, :])
            upd = [_dot_tn(kd_ref[b, rows, hsl(h)], vn) for (b, h), vn in zip(chains, vn16)]
            states = [st * ge_ref[b, pi, j:j + 1, HEADS + h:HEADS + h + 1] + x
                      for (b, h), st, x in zip(chains, states, upd)]
        intra = [_dot(qk_ref[b, pi, h], jnp.concatenate(v_new[i], axis=0)) for i, (b, h) in enumerate(chains)]
        prows = slice(pi * B_PAIR, (pi + 1) * B_PAIR)
        for i, (b, h) in enumerate(chains):
            o = jnp.concatenate(o_inter[i], axis=0) + intra[i]
            y = o * lax.rsqrt(jnp.mean(o * o, axis=-1, keepdims=True) + EPS) * nw
            o_ref[b, prows, hsl(h)] = (y * _silu(z_ref[b, prows, hsl(h)])).astype(BF16)
    for i, (b, h) in enumerate(chains):
        state_ref[b * HEADS + h] = states[i]


def _dn_scan(pb, u, w, qd, kd, qk, ge, nw, batch, seq):
    c = B_CHUNK
    pr = B_PAIR
    bw = BRANCH_WIDTH
    t = batch * seq
    const = lambda i: (0, 0)
    np_seq = seq // pr
    seq3 = lambda x: x.reshape(batch, seq, x.shape[-1])
    sp = B_SCAN_PAIRS
    blk3 = pl.BlockSpec((batch, sp * pr, bw), lambda i: (0, i, 0))
    y = pl.pallas_call(
        _dn_scan_kernel,
        grid=(np_seq // sp,),
        in_specs=[blk3, blk3, blk3, blk3,
                  pl.BlockSpec((batch, sp, HEADS, pr, pr), lambda i: (0, i, 0, 0, 0)),
                  pl.BlockSpec((batch, sp, pr // c, SMALL_PAD), lambda i: (0, i, 0, 0)),
                  pl.BlockSpec((batch, sp * pr, bw), lambda i: (0, i, B_CONV_CH // bw)),
                  pl.BlockSpec((1, HEAD_DIM), const)],
        out_specs=blk3,
        out_shape=jax.ShapeDtypeStruct((batch, seq, bw), BF16),
        scratch_shapes=[pltpu.VMEM((batch * HEADS, HEAD_DIM, HEAD_DIM), F32)],
        compiler_params=pltpu.CompilerParams(
            dimension_semantics=("arbitrary",), vmem_limit_bytes=VMEM_LIMIT),
        name="dn_scan",
    )(seq3(u), seq3(w), seq3(qd), seq3(kd),
      qk.reshape(batch, np_seq, HEADS, pr, pr),
      ge.reshape(batch, np_seq, pr // c, SMALL_PAD),
      seq3(pb), nw)
    return y.reshape(t, bw)


def _band_buckets():
    assert WINDOW == C_BLOCK
    r = np.arange(C_BLOCK)[:, None]
    c = np.arange(C_BLOCK)[None, :]
    dist = np.where(c > r, r + C_BLOCK - c, r - c)
    max_exact = N_BUCKETS // 2
    d_f = np.maximum(dist, 1).astype(np.float32)
    large = max_exact + (np.log(d_f / np.float32(max_exact)) / np.float32(math.log(MAX_DISTANCE / max_exact))
                         * np.float32(N_BUCKETS - max_exact)).astype(np.int32)
    large = np.minimum(large, N_BUCKETS - 1)
    return np.where(dist < max_exact, dist, large).astype(np.int32)


def _band_halves():
    r = np.arange(C_BLOCK)[:, None]
    c = np.arange(C_BLOCK)[None, :]
    return np.stack([c > r, c <= r]).astype(np.float32)


def _swa_stages(n, first_step, rb_ref, sink_ref, cur_ref, prev_ref, bucket_ref, half_ref, o_ref, bias_ref):
    blk = C_BLOCK
    bw = BRANCH_WIDTH

    @pl.when(first_step)
    def _():
        bucket = bucket_ref[...]
        for h in range(C_Q_HEADS):
            acc = jnp.zeros((blk, blk), F32)
            for bk in range(N_BUCKETS):
                acc = jnp.where(bucket == bk, rb_ref[bk, h], acc)
            bias_ref[h] = acc

    kv = jnp.concatenate([prev_ref[...], cur_ref[:, bw:bw + 2 * C_KV_WIDTH]], axis=0)
    rows = kv.shape[0]
    lane = lax.broadcasted_iota(jnp.int32, (rows, C_KV_WIDTH), 1)

    def halves(x, j):
        own = jnp.where((lane >= j * C_HEAD_DIM) & (lane < (j + 1) * C_HEAD_DIM), x, 0.0)
        other = pltpu.roll(own, C_HEAD_DIM, 1)
        pair = (own, other) if j == 0 else (other, own)
        return [t.astype(BF16) for t in pair]

    kz = [halves(kv[:, 0:C_KV_WIDTH], j) for j in range(C_KV_HEADS)]
    vz = [halves(kv[:, C_KV_WIDTH:], j) for j in range(C_KV_HEADS)]
    scale = C_HEAD_DIM ** -0.5
    group = C_Q_HEADS // C_KV_HEADS
    from_prev = (lax.broadcasted_iota(jnp.int32, (blk, blk), 1) > lax.broadcasted_iota(jnp.int32, (blk, blk), 0))
    yield

    qrows = lambda i: slice(i * blk, (i + 1) * blk)
    krows = lambda i: slice(i * blk, (i + 2) * blk)
    for first in range(0, C_TILE_BLOCKS, C_LOCKSTEP_BLOCKS):
        blocks = range(first, first + C_LOCKSTEP_BLOCKS)
        chains = [(i, h) for i in blocks for h in range(C_Q_HEADS)]
        q2 = {i: [(cur_ref[qrows(i), p * LANES:(p + 1) * LANES] * scale).astype(BF16)
                  for p in range(C_Q_HEADS // 2)] for i in blocks}
        both = [_dot_nt(q2[i][h // 2], kz[h // group][h % 2][krows(i), :]) for i, h in chains]
        yield
        logits = []
        for (i, h), lg in zip(chains, both):
            prev_part = lg[:, 0:blk]
            if i == 0:
                prev_part = prev_part + jnp.where(n > 0, 0.0, NEG_BIG)
            logits.append(jnp.where(from_prev, prev_part, lg[:, blk:]) + bias_ref[h])
        yield
        mx = [jnp.maximum(jnp.max(lg, axis=-1, keepdims=True), sink_ref[h])
              for (i, h), lg in zip(chains, logits)]
        yield
        pr = [jnp.exp(lg - m) for lg, m in zip(logits, mx)]
        den = [jnp.sum(x, axis=-1, keepdims=True) + jnp.exp(sink_ref[h] - m)
               for (i, h), x, m in zip(chains, pr, mx)]
        yield
        pr16 = [x.astype(BF16) for x in pr]
        pcat = [jnp.concatenate([x * half_ref[0], x * half_ref[1]], axis=1) for x in pr16]
        pv = [_dot(x, vz[h // group][h % 2][krows(i), :]) for (i, h), x in zip(chains, pcat)]
        yield
        for bi, i in enumerate(blocks):
            for p in range(C_Q_HEADS // 2):
                a, b = bi * C_Q_HEADS + 2 * p, bi * C_Q_HEADS + 2 * p + 1
                out = pv[a] / den[a] + pv[b] / den[b]
                gate = cur_ref[qrows(i), bw + 2 * C_KV_WIDTH + p * LANES:bw + 2 * C_KV_WIDTH + (p + 1) * LANES]
                o_ref[qrows(i), p * LANES:(p + 1) * LANES] = (out * _silu(gate)).astype(BF16)
        yield


N_DN_IN = 6
N_DN_OUT = 6
N_SWA_IN = 6


def _mix_bc_kernel(*refs, tiles_per_seq):
    dn_in = refs[0:N_DN_IN]
    swa_in = refs[N_DN_IN:N_DN_IN + N_SWA_IN]
    outs = refs[N_DN_IN + N_SWA_IN:N_DN_IN + N_SWA_IN + N_DN_OUT + 1]
    xp_ref, qkv_ref, bias_ref = refs[N_DN_IN + N_SWA_IN + N_DN_OUT + 1:]
    step = pl.program_id(0)
    seq_tile = step % tiles_per_seq
    jobs = [_dn_prep_stages(seq_tile, *dn_in, *outs[0:N_DN_OUT], xp_ref, qkv_ref),
            _swa_stages(seq_tile, step == 0, *swa_in, outs[N_DN_OUT], bias_ref)]
    while jobs:
        for job in list(jobs):
            try:
                next(job)
            except StopIteration:
                jobs.remove(job)


def _mix_bc(pb, pc, cw, alog, dtb, rel_bias, sinks, batch, seq):
    c = B_CHUNK
    ts = B_TILE
    pr = B_PAIR
    bw = BRANCH_WIDTH
    blk = C_BLOCK
    t = batch * seq
    n_tiles = t // ts
    tril = jnp.asarray(np.kron(np.eye(pr // c, dtype=np.float32), np.tril(np.ones((c, c), np.float32))),
                       dtype=BF16)
    bucket = jnp.asarray(_band_buckets())
    halves = jnp.asarray(_band_halves(), dtype=BF16)
    const = lambda i: (0, 0)
    row = lambda i: (i, 0)
    halo_blocks = ts // SUBLANES
    kv_col = BRANCH_WIDTH // (2 * C_KV_WIDTH)
    smem = pl.BlockSpec(memory_space=pltpu.SMEM)
    return pl.pallas_call(
        functools.partial(_mix_bc_kernel, tiles_per_seq=seq // ts),
        grid=(n_tiles,),
        in_specs=[pl.BlockSpec((ts, PB_WIDTH), row),
                  pl.BlockSpec((SUBLANES, B_CONV_CH), lambda i: (jnp.maximum(i * halo_blocks - 1, 0), 0)),
                  pl.BlockSpec(cw.shape, const),
                  pl.BlockSpec((1, SMALL_PAD), const),
                  pl.BlockSpec((1, SMALL_PAD), const),
                  pl.BlockSpec((pr, pr), const),
                  smem, smem,
                  pl.BlockSpec((ts, PC_WIDTH), row),
                  pl.BlockSpec((blk, 2 * C_KV_WIDTH), lambda i: (jnp.maximum(i * C_TILE_BLOCKS - 1, 0), kv_col)),
                  pl.BlockSpec((blk, blk), const),
                  pl.BlockSpec((2, blk, blk), lambda i: (0, 0, 0))],
        out_specs=[pl.BlockSpec((ts, bw), row),
                   pl.BlockSpec((ts, bw), row),
                   pl.BlockSpec((ts, bw), row),
                   pl.BlockSpec((ts, bw), row),
                   pl.BlockSpec((ts // pr, HEADS, pr, pr), lambda i: (i, 0, 0, 0)),
                   pl.BlockSpec((1, ts // c, SMALL_PAD), lambda i: (i, 0, 0)),
                   pl.BlockSpec((ts, bw), row)],
        out_shape=[jax.ShapeDtypeStruct((t, bw), F32),
                   jax.ShapeDtypeStruct((t, bw), BF16),
                   jax.ShapeDtypeStruct((t, bw), BF16),
                   jax.ShapeDtypeStruct((t, bw), BF16),
                   jax.ShapeDtypeStruct((t // pr, HEADS, pr, pr), BF16),
                   jax.ShapeDtypeStruct((n_tiles, ts // c, SMALL_PAD), F32),
                   jax.ShapeDtypeStruct((t, bw), BF16)],
        scratch_shapes=[pltpu.VMEM((B_CONV_CH // LANES, ts + SUBLANES, LANES), F32),
                        pltpu.VMEM((B_CONV_CH // LANES, ts, LANES), F32),
                        pltpu.VMEM((C_Q_HEADS, blk, blk), F32)],
        compiler_params=pltpu.CompilerParams(
            dimension_semantics=("arbitrary",), vmem_limit_bytes=VMEM_LIMIT),
        name="mix_bc",
    )(pb, pb, cw, alog, dtb, tril, rel_bias, sinks, pc, pc, bucket, halves)


def _lane_pad(v, offset):
    return jnp.zeros((1, SMALL_PAD), F32).at[0, offset:offset + v.shape[0]].set(v.astype(F32))


def kernel(x, norm_w, w_in, conv_w, a_log, dt_bias, lb_param, norm_a, norm_b, sinks, rel_bias,
           w_branch, w_out, final_norm):
    batch, seq, _ = x.shape
    depth = w_in.shape[0]
    x2 = x.reshape(batch * seq, D_MODEL)
    fn = final_norm.reshape(1, D_MODEL)
    w16 = w_in.astype(BF16)
    wbr = w_branch.astype(BF16)
    wo = w_out.astype(BF16)
    for l in range(depth):
        nw = norm_w[l].reshape(1, D_MODEL)
        pa, pb, pc = _inproj(x2, nw, w16, l)
        ya = _hgrn2(pa, lb_param, norm_a[l].reshape(1, HEAD_DIM), batch, seq, l)
        *dn, yc = _mix_bc(pb, pc, conv_w[l], _lane_pad(a_log[l], HEADS), _lane_pad(dt_bias[l], HEADS),
                          rel_bias, sinks[l], batch, seq)
        yb = _dn_scan(pb, *dn, norm_b[l].reshape(1, HEAD_DIM), batch, seq)
        x2 = _merge(x2, nw, w16, ya, yb, yc, wbr, wo, fn, layer=l, final=(l == depth - 1))
    return x2.reshape(batch, seq, D_MODEL)
```

```python
import functools
import math

import numpy as np
import jax
import jax.numpy as jnp
from jax import lax
from jax.experimental import pallas as pl
from jax.experimental.pallas import tpu as pltpu

F32 = jnp.float32
BF16 = jnp.bfloat16

D_MODEL = 1024
BRANCH_WIDTH = D_MODEL // 2
N_BRANCHES = 3
EPS = 1e-6
HEADS = 4
HEAD_DIM = BRANCH_WIDTH // HEADS
B_CONV = 4
B_CONV_CH = 3 * BRANCH_WIDTH
C_Q_HEADS = 8
C_KV_HEADS = 2
C_HEAD_DIM = BRANCH_WIDTH // C_Q_HEADS
C_KV_WIDTH = C_KV_HEADS * C_HEAD_DIM
WINDOW = 128
C_BLOCK = 128
C_LOCKSTEP_BLOCKS = 2
N_BUCKETS = 32
MAX_DISTANCE = 128
LANES = 128
SUBLANES = 8
NEG_BIG = -1e30
LOG2E = math.log2(math.e)

A_CHUNK = 128
A_TILE_CHUNKS = 8
B_CHUNK = 64
B_PAIR = 2 * B_CHUNK
B_TILE = 512
C_TILE_BLOCKS = B_TILE // C_BLOCK
B_LOCKSTEP = 16
PROJ_ROWS = 512
SMALL_PAD = LANES
VMEM_LIMIT = 52 * 1024 * 1024

PA_WIDTH = 4 * BRANCH_WIDTH
PB_WIDTH = B_CONV_CH + BRANCH_WIDTH + SMALL_PAD
PC_WIDTH = 2 * BRANCH_WIDTH + 2 * C_KV_WIDTH
W_AB = PA_WIDTH + PB_WIDTH - (SMALL_PAD - 2 * HEADS)
W_ABC = W_AB + PC_WIDTH


def _dot(a, b):
    return jnp.dot(a, b, preferred_element_type=F32)


def _dot_nt(a, b):
    return lax.dot_general(a, b, (((1,), (1,)), ((), ())), preferred_element_type=F32)


def _dot_tn(a, b):
    return lax.dot_general(a, b, (((0,), (0,)), ((), ())), preferred_element_type=F32)


def _sigmoid(x):
    return 0.5 * jnp.tanh(0.5 * x) + 0.5


def _silu(x):
    h = 0.5 * x
    return h * jnp.tanh(h) + h


def _cumsum_rows(tril16, x):
    hi = x.astype(BF16)
    r1 = x - hi.astype(F32)
    mid = r1.astype(BF16)
    lo = (r1 - mid.astype(F32)).astype(BF16)
    return _dot(tril16, hi) + (_dot(tril16, mid) + _dot(tril16, lo))


def _softplus(x):
    return jnp.maximum(x, 0.0) + jnp.log(1.0 + jnp.exp(-jnp.abs(x)))


def _rms(x, w):
    return x * lax.rsqrt(jnp.mean(x * x, axis=-1, keepdims=True) + EPS) * w


def _resident(shape):
    return pl.BlockSpec(shape, lambda i: (0,) * len(shape), pipeline_mode=pl.Buffered(1))


def _layer_resident(shape, layer):
    return pl.BlockSpec((None,) + tuple(shape[1:]), lambda i: (layer,) + (0,) * (len(shape) - 1),
                        pipeline_mode=pl.Buffered(1))


def _inproj_kernel(x_ref, nw_ref, w_ref, pa_ref, pb_ref, pc_ref, wc_ref):
    @pl.when(pl.program_id(0) == 0)
    def _():
        wc_ref[...] = w_ref[:, W_AB:W_ABC]

    h = _rms(x_ref[...], nw_ref[...]).astype(BF16)
    for o_ref, src_ref, base in ((pa_ref, w_ref, 0), (pb_ref, w_ref, PA_WIDTH), (pc_ref, wc_ref, 0)):
        n = o_ref.shape[1]
        for j in range(0, n, BRANCH_WIDTH):
            wd = min(BRANCH_WIDTH, n - j)
            o_ref[:, j:j + wd] = _dot(h, src_ref[:, base + j:base + j + wd])


def _inproj(x2, nw, w, layer):
    t = x2.shape[0]
    tm = PROJ_ROWS
    return pl.pallas_call(
        _inproj_kernel,
        grid=(t // tm,),
        in_specs=[pl.BlockSpec((tm, D_MODEL), lambda i: (i, 0)),
                  _resident((1, D_MODEL)),
                  _layer_resident(w.shape, layer)],
        out_specs=[pl.BlockSpec((tm, PA_WIDTH), lambda i: (i, 0)),
                   pl.BlockSpec((tm, PB_WIDTH), lambda i: (i, 0)),
                   pl.BlockSpec((tm, PC_WIDTH), lambda i: (i, 0))],
        out_shape=[jax.ShapeDtypeStruct((t, PA_WIDTH), F32),
                   jax.ShapeDtypeStruct((t, PB_WIDTH), F32),
                   jax.ShapeDtypeStruct((t, PC_WIDTH), F32)],
        scratch_shapes=[pltpu.VMEM((D_MODEL, PC_WIDTH), BF16)],
        compiler_params=pltpu.CompilerParams(
            dimension_semantics=("arbitrary",), vmem_limit_bytes=VMEM_LIMIT),
        name="inproj",
    )(x2, nw, w)


def _merge_kernel(x_ref, nw_ref, w_ref, ya_ref, yb_ref, yc_ref, wbr_ref, wo_ref, fn_ref,
                  o_ref, wg_ref, *, final):
    @pl.when(pl.program_id(0) == 0)
    def _():
        wg_ref[...] = w_ref[:, W_ABC:]

    x = x_ref[...]
    h = _rms(x, nw_ref[...]).astype(BF16)
    merged = None
    for n, y_ref in enumerate((ya_ref, yb_ref, yc_ref)):
        gate = _sigmoid(_dot(h, wg_ref[:, n * D_MODEL:(n + 1) * D_MODEL]))
        term = gate * _dot(y_ref[...], wbr_ref[n])
        merged = term if merged is None else merged + term
    out = x + _dot(merged.astype(BF16), wo_ref[...])
    if final:
        out = _rms(out, fn_ref[...])
    o_ref[...] = out


def _merge(x2, nw, w, ya, yb, yc, wbr, wo, fn, layer, final):
    t = x2.shape[0]
    tm = PROJ_ROWS
    row = lambda i: (i, 0)
    return pl.pallas_call(
        functools.partial(_merge_kernel, final=final),
        grid=(t // tm,),
        in_specs=[pl.BlockSpec((tm, D_MODEL), row),
                  _resident((1, D_MODEL)),
                  _layer_resident(w.shape, layer),
                  pl.BlockSpec((tm, BRANCH_WIDTH), row),
                  pl.BlockSpec((tm, BRANCH_WIDTH), row),
                  pl.BlockSpec((tm, BRANCH_WIDTH), row),
                  _layer_resident(wbr.shape, layer),
                  _layer_resident(wo.shape, layer),
                  _resident((1, D_MODEL))],
        out_specs=pl.BlockSpec((tm, D_MODEL), row),
        out_shape=jax.ShapeDtypeStruct((t, D_MODEL), F32),
        scratch_shapes=[pltpu.VMEM((D_MODEL, N_BRANCHES * D_MODEL), BF16)],
        compiler_params=pltpu.CompilerParams(
            dimension_semantics=("arbitrary",), vmem_limit_bytes=VMEM_LIMIT),
        name="merge",
    )(x2, nw, w, ya, yb, yc, wbr, wo, fn)


def _a_levels():
    return [1 << i for i in range(int(math.log2(A_CHUNK)))]


def _a_level_ids():
    idx = np.arange(A_CHUNK)
    ids = np.where(idx[:, None] == idx[None, :], 0, -1).astype(np.int32)
    for i, m in enumerate(_a_levels()):
        blk = idx // (2 * m)
        upper = (idx & m) != 0
        ids[(blk[:, None] == blk[None, :]) & upper[:, None] & ~upper[None, :]] = i + 1
    return ids


def _hgrn2_kernel(pa_ref, lbp_ref, nw_ref, tril_ref, lvl_ref, o_ref,
                  state_ref, b_ref, gp_ref, *, layer):
    c = A_CHUNK
    bw = BRANCH_WIDTH

    @pl.when(pl.program_id(1) == 0)
    def _():
        state_ref[...] = jnp.zeros_like(state_ref)
        gp_ref[...] = jnp.zeros_like(gp_ref)

    lbp = lbp_ref[...]
    ex = jnp.exp(lbp - jnp.max(lbp, axis=0, keepdims=True))
    sm = ex / jnp.sum(ex, axis=0, keepdims=True)
    lb = jnp.zeros((1, bw), F32)
    for j in range(1, layer + 1):
        lb = lb + sm[j:j + 1, :]

    hsl = [slice(h * HEAD_DIM, (h + 1) * HEAD_DIM) for h in range(HEADS)]
    row = lax.broadcasted_iota(jnp.int32, (c, 1), 0)
    lvl = lvl_ref[...]
    t1 = jnp.log(lb)
    t2_lb = jnp.log1p(-lb)

    def intra_chunk(ci):
        rows = slice(ci * c, (ci + 1) * c)
        bc_ref = b_ref.at[ci]
        gc_ref = gp_ref.at[ci]
        q = _silu(pa_ref[rows, 0:bw])
        z = pa_ref[rows, bw:2 * bw]
        v = pa_ref[rows, 2 * bw:3 * bw].astype(BF16)
        log_sig = jnp.minimum(z, 0.0) - jnp.log(1.0 + jnp.exp(-jnp.abs(z)))
        sig_neg = 0.5 - 0.5 * jnp.tanh(0.5 * z)
        t2 = t2_lb + log_sig
        lf = jnp.maximum(t1, t2) + jnp.log(1.0 + jnp.exp(-jnp.abs(t1 - t2)))
        k = (1.0 - lb) * sig_neg

        lf2 = lf * LOG2E
        b = _cumsum_rows(tril_ref[...], lf2)
        bc_ref[...] = b
        gc_ref[SUBLANES:SUBLANES + c, :] = lf2

        intra = []
        for sl in hsl:
            qh = q[:, sl]
            kh = k[:, sl]
            lfh = lf2[:, sl]
            acc = jnp.where(lvl == 0, _dot_nt(qh.astype(BF16), kh.astype(BF16)), 0.0)
            for li, m in enumerate(_a_levels()):
                blocks = range(0, c, 2 * m)
                if m == 1:
                    neg = jnp.where((row & 1) == 1, lfh, 0.0)
                elif m == 2:
                    r4 = row & 3
                    g_next = gc_ref[SUBLANES + 1:SUBLANES + 1 + c, sl]
                    g_prev = gc_ref[SUBLANES - 1:SUBLANES - 1 + c, sl]
                    neg = jnp.where(r4 == 0, g_next,
                                    jnp.where(r4 == 1, 0.0, jnp.where(r4 == 2, lfh, lfh + g_prev)))
                elif m < SUBLANES:
                    neg = jnp.concatenate(
                        [-jnp.abs(bc_ref[lo:lo + 2 * m, sl] - bc_ref[lo + m - 1:lo + m, sl]) for lo in blocks],
                        axis=0)
                else:
                    pieces = []
                    for lo in blocks:
                        anchor = bc_ref[lo + m - 1:lo + m, sl]
                        pieces += [anchor - bc_ref[lo:lo + m, sl], bc_ref[lo + m:lo + 2 * m, sl] - anchor]
                    neg = jnp.concatenate(pieces, axis=0)
                if m < SUBLANES:
                    x = jnp.where((row & m) != 0, qh, kh)
                else:
                    x = jnp.concatenate(
                        [t for lo in blocks for t in (kh[lo:lo + m, :], qh[lo + m:lo + 2 * m, :])], axis=0)
                xt = (x * jnp.exp2(neg)).astype(BF16)
                acc = jnp.where(lvl == li + 1, _dot_nt(xt, xt), acc)
            intra.append(_dot(acc.astype(BF16), v[:, sl]))

        b_end = bc_ref[c - 1:c, :]
        qd = (q * jnp.exp2(b)).astype(BF16)
        kd = (k * jnp.exp2(b_end - b)).astype(BF16)
        upd = [_dot_tn(v[:, hsl[h]], kd[:, hsl[h]]) for h in range(HEADS)]
        return intra, upd, qd, jnp.exp2(b_end)

    parts = [intra_chunk(ci) for ci in range(A_TILE_CHUNKS)]
    nw = nw_ref[...]
    sts = [state_ref[h] for h in range(HEADS)]
    for ci, (intra, upd, qd, s_decay) in enumerate(parts):
        rows = slice(ci * c, (ci + 1) * c)
        inter = [_dot_nt(qd[:, hsl[h]], sts[h].astype(BF16)) for h in range(HEADS)]
        sts = [sts[h] * s_decay[:, hsl[h]] + upd[h] for h in range(HEADS)]
        for h, sl in enumerate(hsl):
            o = intra[h] + inter[h]
            y = o * lax.rsqrt(jnp.mean(o * o, axis=-1, keepdims=True) + EPS) * nw
            gate = pa_ref[rows, 3 * bw + h * HEAD_DIM:3 * bw + (h + 1) * HEAD_DIM]
            o_ref[rows, sl] = (y * _silu(gate)).astype(BF16)
    for h in range(HEADS):
        state_ref[h] = sts[h]


def _hgrn2(pa, lbp, nw, batch, seq, layer):
    c = A_CHUNK
    tc = A_TILE_CHUNKS
    nt = seq // (c * tc)
    tril = jnp.asarray(np.tril(np.ones((c, c), np.float32)), dtype=BF16)
    lvl = jnp.asarray(_a_level_ids())
    const2 = lambda b, i: (0, 0)
    return pl.pallas_call(
        functools.partial(_hgrn2_kernel, layer=layer),
        grid=(batch, nt),
        in_specs=[pl.BlockSpec((tc * c, PA_WIDTH), lambda b, i: (b * nt + i, 0)),
                  pl.BlockSpec(lbp.shape, const2),
                  pl.BlockSpec((1, HEAD_DIM), const2),
                  pl.BlockSpec((c, c), const2),
                  pl.BlockSpec((c, c), const2)],
        out_specs=pl.BlockSpec((tc * c, BRANCH_WIDTH), lambda b, i: (b * nt + i, 0)),
        out_shape=jax.ShapeDtypeStruct((batch * seq, BRANCH_WIDTH), BF16),
        scratch_shapes=[pltpu.VMEM((HEADS, HEAD_DIM, HEAD_DIM), F32),
                        pltpu.VMEM((tc, c, BRANCH_WIDTH), F32),
                        pltpu.VMEM((tc, c + 2 * SUBLANES, BRANCH_WIDTH), F32)],
        compiler_params=pltpu.CompilerParams(
            dimension_semantics=("arbitrary", "arbitrary"), vmem_limit_bytes=VMEM_LIMIT),
        name="hgrn2",
    )(pa, lbp, nw, tril, lvl)


def _b_levels():
    return [1 << i for i in range(int(math.log2(B_CHUNK)))]


def _unit_lower_inverses(mats, ti, si):
    eye = (ti == si).astype(F32)
    ts = [eye] * len(mats)
    for s in _b_levels():
        shift = int(math.log2(2 * s))
        off = ((ti >> shift) == (si >> shift)) & ((ti & s) != 0) & ((si & s) == 0)
        a_off = [jnp.where(off, a, 0.0) for a in mats]
        if s == 1:
            ts = [t - ao for t, ao in zip(ts, a_off)]
        else:
            t16 = [t.astype(BF16) for t in ts]
            inner = [_dot(ao.astype(BF16), t) for ao, t in zip(a_off, t16)]
            yield
            outer = [_dot(t, x.astype(BF16)) for t, x in zip(t16, inner)]
            yield
            ts = [t - x for t, x in zip(ts, outer)]
    return ts


def _dn_prep_stages(seq_tile, pb_ref, halo_ref, cw_ref, alog_ref, dtb_ref, tril_ref,
                    u_ref, w_ref, qd_ref, kd_ref, qk_ref, ge_ref, xp_ref, qkv_ref):
    ts = B_TILE
    c = B_CHUNK
    pr = B_PAIR
    bw = BRANCH_WIDTH
    sl8 = SUBLANES

    keep = jnp.where(seq_tile == 0, 0.0, 1.0)
    half = ts // 2
    for s in range(B_CONV_CH // LANES):
        lanes = slice(s * LANES, (s + 1) * LANES)
        xp_ref[s, 0:sl8, :] = halo_ref[:, lanes] * keep
        xp_ref[s, sl8:sl8 + ts, :] = pb_ref[:, lanes]
    yield
    for s in range(B_CONV_CH // LANES):
        lanes = slice(s * LANES, (s + 1) * LANES)
        cw = [0.5 * cw_ref[j:j + 1, lanes] for j in range(B_CONV)]
        ld = {off: xp_ref[s, pl.ds(sl8 + off, half, stride=2), :] for off in range(1 - B_CONV, 2)}
        even = None
        odd = None
        for j in range(B_CONV):
            te = cw[j] * ld[j - (B_CONV - 1)]
            to = cw[j] * ld[j - (B_CONV - 1) + 1]
            even = te if even is None else even + te
            odd = to if odd is None else odd + to
        qkv_ref[s, pl.ds(0, half, stride=2), :] = even * jnp.tanh(even) + even
        qkv_ref[s, pl.ds(1, half, stride=2), :] = odd * jnp.tanh(odd) + odd
        yield

    small = pb_ref[:, B_CONV_CH + bw:B_CONV_CH + bw + SMALL_PAD]
    beta_all = _sigmoid(small)
    g_all = -jnp.exp(alog_ref[...]) * _softplus(small + dtb_ref[...])
    gcum = jnp.concatenate(
        [_cumsum_rows(tril_ref[...], g_all[p * pr:(p + 1) * pr, :]) for p in range(ts // pr)], axis=0)
    gcum_t = gcum.T
    for j in range(ts // c):
        ge_ref[j:j + 1, :] = jnp.exp(gcum[j * c + c - 1:j * c + c, :])

    ti = lax.broadcasted_iota(jnp.int32, (pr, pr), 0)
    si = lax.broadcasted_iota(jnp.int32, (pr, pr), 1)
    same = (ti >> int(math.log2(c))) == (si >> int(math.log2(c)))
    incl = same & (ti >= si)
    first_chunk = lax.broadcasted_iota(jnp.int32, (pr, 1), 0) < c
    yield

    all_probs = [(h, p) for h in range(HEADS) for p in range(ts // pr)]
    for first in range(0, len(all_probs), B_LOCKSTEP):
        probs = all_probs[first:first + B_LOCKSTEP]
        k16s, kb16s, q16s, decays, rhs = [], [], [], [], []
        for h, p in probs:
            sl = slice(h * HEAD_DIM, (h + 1) * HEAD_DIM)
            rows = slice(p * pr, (p + 1) * pr)
            q = qkv_ref[h, rows, :]
            k = qkv_ref[HEADS + h, rows, :]
            v = qkv_ref[2 * HEADS + h, rows, :]
            q = q * (lax.rsqrt(jnp.sum(q * q, axis=-1, keepdims=True) + EPS) * (HEAD_DIM ** -0.5))
            k = k * lax.rsqrt(jnp.sum(k * k, axis=-1, keepdims=True) + EPS)
            beta = beta_all[rows, h:h + 1]
            gc = gcum[rows, HEADS + h:HEADS + h + 1]
            gr = gcum_t[HEADS + h:HEADS + h + 1, rows]
            g_last = jnp.where(first_chunk, gc[c - 1:c, :], gc[pr - 1:pr, :])
            egc = jnp.exp(gc)
            kb = k * beta
            decays.append(jnp.where(incl, jnp.exp(jnp.minimum(gc - gr, 0.0)), 0.0))
            k16s.append(k.astype(BF16))
            kb16s.append(kb.astype(BF16))
            q16s.append(q.astype(BF16))
            rhs.append(jnp.concatenate([v * beta, kb * egc], axis=1).astype(BF16))
            qd_ref[rows, sl] = (q * egc).astype(BF16)
            kd_ref[rows, sl] = (k * jnp.exp(g_last - gc)).astype(BF16)
            yield

        kk = [_dot_nt(kb16, k16) for kb16, k16 in zip(kb16s, k16s)]
        yield
        qk = [_dot_nt(q16, k16) for q16, k16 in zip(q16s, k16s)]
        yield
        mats = [x * d for x, d in zip(kk, decays)]
        tinv = yield from _unit_lower_inverses(mats, ti, si)
        uw = [_dot(t.astype(BF16), r) for t, r in zip(tinv, rhs)]
        yield
        for i, (h, p) in enumerate(probs):
            sl = slice(h * HEAD_DIM, (h + 1) * HEAD_DIM)
            rows = slice(p * pr, (p + 1) * pr)
            u_ref[rows, sl] = uw[i][:, 0:HEAD_DIM]
            w_ref[rows, sl] = uw[i][:, HEAD_DIM:].astype(BF16)
            qk_ref[p, h] = (qk[i] * decays[i]).astype(BF16)


def _dn_scan_stages(reset, u_ref, w_ref, qd_ref, kd_ref, qk_ref, ge_ref, z_ref, nw_ref, o_ref, state_ref):
    c = B_CHUNK

    @pl.when(reset)
    def _():
        state_ref[...] = jnp.zeros_like(state_ref)

    nw = nw_ref[...]
    heads = range(HEADS)
    hsl = lambda h: slice(h * HEAD_DIM, (h + 1) * HEAD_DIM)
    states = [state_ref[h] for h in heads]
    for pi in range(B_TILE // B_PAIR):
        o_inter = [[] for _ in heads]
        v_new = [[] for _ in heads]
        for j in range(B_PAIR // c):
            chunk = pi * (B_PAIR // c) + j
            rows = slice(chunk * c, (chunk + 1) * c)
            st16 = [st.astype(BF16) for st in states]
            lhs = [jnp.concatenate([w_ref[rows, hsl(h)], qd_ref[rows, hsl(h)]], axis=0) for h in heads]
            prod = [_dot(x, s) for x, s in zip(lhs, st16)]
            yield
            vn16 = []
            for h in heads:
                vn = (u_ref[rows, hsl(h)] - prod[h][0:c, :]).astype(BF16)
                vn16.append(vn)
                v_new[h].append(vn)
                o_inter[h].append(prod[h][c:, :])
            upd = [_dot_tn(kd_ref[rows, hsl(h)], vn16[h]) for h in heads]
            yield
            states = [states[h] * ge_ref[chunk:chunk + 1, HEADS + h:HEADS + h + 1] + upd[h] for h in heads]
        intra = [_dot(qk_ref[pi, h], jnp.concatenate(v_new[h], axis=0)) for h in heads]
        yield
        prows = slice(pi * B_PAIR, (pi + 1) * B_PAIR)
        for h in heads:
            o = jnp.concatenate(o_inter[h], axis=0) + intra[h]
            y = o * lax.rsqrt(jnp.mean(o * o, axis=-1, keepdims=True) + EPS) * nw
            o_ref[prows, hsl(h)] = (y * _silu(z_ref[prows, hsl(h)])).astype(BF16)
        yield
    for h in heads:
        state_ref[h] = states[h]


def _band_buckets():
    assert WINDOW == C_BLOCK
    r = np.arange(C_BLOCK)[:, None]
    c = np.arange(C_BLOCK)[None, :]
    dist = np.where(c > r, r + C_BLOCK - c, r - c)
    max_exact = N_BUCKETS // 2
    d_f = np.maximum(dist, 1).astype(np.float32)
    large = max_exact + (np.log(d_f / np.float32(max_exact)) / np.float32(math.log(MAX_DISTANCE / max_exact))
                         * np.float32(N_BUCKETS - max_exact)).astype(np.int32)
    large = np.minimum(large, N_BUCKETS - 1)
    return np.where(dist < max_exact, dist, large).astype(np.int32)


def _band_halves():
    r = np.arange(C_BLOCK)[:, None]
    c = np.arange(C_BLOCK)[None, :]
    return np.stack([c > r, c <= r]).astype(np.float32)


def _swa_stages(n, first_step, rb_ref, sink_ref, cur_ref, prev_ref, bucket_ref, half_ref, o_ref, bias_ref):
    blk = C_BLOCK
    bw = BRANCH_WIDTH

    @pl.when(first_step)
    def _():
        bucket = bucket_ref[...]
        for h in range(C_Q_HEADS):
            acc = jnp.zeros((blk, blk), F32)
            for bk in range(N_BUCKETS):
                acc = jnp.where(bucket == bk, rb_ref[bk, h], acc)
            bias_ref[h] = acc

    kv = jnp.concatenate([prev_ref[...], cur_ref[:, bw:bw + 2 * C_KV_WIDTH]], axis=0)
    rows = kv.shape[0]
    lane = lax.broadcasted_iota(jnp.int32, (rows, C_KV_WIDTH), 1)

    def halves(x, j):
        own = jnp.where((lane >= j * C_HEAD_DIM) & (lane < (j + 1) * C_HEAD_DIM), x, 0.0)
        other = pltpu.roll(own, C_HEAD_DIM, 1)
        pair = (own, other) if j == 0 else (other, own)
        return [t.astype(BF16) for t in pair]

    kz = [halves(kv[:, 0:C_KV_WIDTH], j) for j in range(C_KV_HEADS)]
    vz = [halves(kv[:, C_KV_WIDTH:], j) for j in range(C_KV_HEADS)]
    scale = C_HEAD_DIM ** -0.5
    group = C_Q_HEADS // C_KV_HEADS
    from_prev = (lax.broadcasted_iota(jnp.int32, (blk, blk), 1) > lax.broadcasted_iota(jnp.int32, (blk, blk), 0))
    yield

    qrows = lambda i: slice(i * blk, (i + 1) * blk)
    krows = lambda i: slice(i * blk, (i + 2) * blk)
    for first in range(0, C_TILE_BLOCKS, C_LOCKSTEP_BLOCKS):
        blocks = range(first, first + C_LOCKSTEP_BLOCKS)
        chains = [(i, h) for i in blocks for h in range(C_Q_HEADS)]
        q2 = {i: [(cur_ref[qrows(i), p * LANES:(p + 1) * LANES] * scale).astype(BF16)
                  for p in range(C_Q_HEADS // 2)] for i in blocks}
        both = [_dot_nt(q2[i][h // 2], kz[h // group][h % 2][krows(i), :]) for i, h in chains]
        yield
        logits = []
        for (i, h), lg in zip(chains, both):
            prev_part = lg[:, 0:blk]
            if i == 0:
                prev_part = prev_part + jnp.where(n > 0, 0.0, NEG_BIG)
            logits.append(jnp.where(from_prev, prev_part, lg[:, blk:]) + bias_ref[h])
        yield
        mx = [jnp.maximum(jnp.max(lg, axis=-1, keepdims=True), sink_ref[h])
              for (i, h), lg in zip(chains, logits)]
        yield
        pr = [jnp.exp(lg - m) for lg, m in zip(logits, mx)]
        den = [jnp.sum(x, axis=-1, keepdims=True) + jnp.exp(sink_ref[h] - m)
               for (i, h), x, m in zip(chains, pr, mx)]
        yield
        pr16 = [x.astype(BF16) for x in pr]
        pcat = [jnp.concatenate([x * half_ref[0], x * half_ref[1]], axis=1) for x in pr16]
        pv = [_dot(x, vz[h // group][h % 2][krows(i), :]) for (i, h), x in zip(chains, pcat)]
        yield
        for bi, i in enumerate(blocks):
            for p in range(C_Q_HEADS // 2):
                a, b = bi * C_Q_HEADS + 2 * p, bi * C_Q_HEADS + 2 * p + 1
                out = pv[a] / den[a] + pv[b] / den[b]
                gate = cur_ref[qrows(i), bw + 2 * C_KV_WIDTH + p * LANES:bw + 2 * C_KV_WIDTH + (p + 1) * LANES]
                o_ref[qrows(i), p * LANES:(p + 1) * LANES] = (out * _silu(gate)).astype(BF16)
        yield


N_DN_IN = 6
N_SWA_IN = 6
N_PREP = 6


def _mix_bc_kernel(*refs, tiles_per_seq, n_tiles):
    dn_in = refs[0:N_DN_IN]
    swa_in = refs[N_DN_IN:N_DN_IN + N_SWA_IN]
    z_ref, nwb_ref, yb_ref, yc_ref = refs[N_DN_IN + N_SWA_IN:N_DN_IN + N_SWA_IN + 4]
    xp_ref, qkv_ref, bias_ref = refs[N_DN_IN + N_SWA_IN + 4:N_DN_IN + N_SWA_IN + 7]
    prep = refs[N_DN_IN + N_SWA_IN + 7:N_DN_IN + N_SWA_IN + 7 + N_PREP]
    state_ref = refs[N_DN_IN + N_SWA_IN + 7 + N_PREP]
    step = pl.program_id(0)
    seq_tile = jnp.minimum(step, n_tiles - 1) % tiles_per_seq
    prev_seq_tile = (step + tiles_per_seq - 1) % tiles_per_seq
    slot = step % 2

    @pl.when(step == 0)
    def _():
        for r in prep:
            r[1] = jnp.zeros(r.shape[1:], r.dtype)

    jobs = [_dn_prep_stages(seq_tile, *dn_in, *[r.at[slot] for r in prep], xp_ref, qkv_ref),
            _swa_stages(seq_tile, step == 0, *swa_in, yc_ref, bias_ref),
            _dn_scan_stages((prev_seq_tile == 0) | (step == 0), *[r.at[1 - slot] for r in prep],
                            z_ref, nwb_ref, yb_ref, state_ref)]
    while jobs:
        for job in list(jobs):
            try:
                next(job)
            except StopIteration:
                jobs.remove(job)


def _mix_bc(pb, pc, cw, alog, dtb, nwb, rel_bias, sinks, batch, seq):
    c = B_CHUNK
    ts = B_TILE
    pr = B_PAIR
    bw = BRANCH_WIDTH
    blk = C_BLOCK
    t = batch * seq
    n_tiles = t // ts
    tril = jnp.asarray(np.kron(np.eye(pr // c, dtype=np.float32), np.tril(np.ones((c, c), np.float32))),
                       dtype=BF16)
    bucket = jnp.asarray(_band_buckets())
    halves = jnp.asarray(_band_halves(), dtype=BF16)
    const = lambda i: (0, 0)
    cur = lambda i: (jnp.minimum(i, n_tiles - 1), 0)
    prev = lambda i: (jnp.maximum(i - 1, 0), 0)
    halo_blocks = ts // SUBLANES
    kv_col = BRANCH_WIDTH // (2 * C_KV_WIDTH)
    smem = pl.BlockSpec(memory_space=pltpu.SMEM)
    return pl.pallas_call(
        functools.partial(_mix_bc_kernel, tiles_per_seq=seq // ts, n_tiles=n_tiles),
        grid=(n_tiles + 1,),
        in_specs=[pl.BlockSpec((ts, PB_WIDTH), cur),
                  pl.BlockSpec((SUBLANES, B_CONV_CH),
                               lambda i: (jnp.maximum(jnp.minimum(i, n_tiles - 1) * halo_blocks - 1, 0), 0)),
                  pl.BlockSpec(cw.shape, const),
                  pl.BlockSpec((1, SMALL_PAD), const),
                  pl.BlockSpec((1, SMALL_PAD), const),
                  pl.BlockSpec((pr, pr), const),
                  smem, smem,
                  pl.BlockSpec((ts, PC_WIDTH), cur),
                  pl.BlockSpec((blk, 2 * C_KV_WIDTH),
                               lambda i: (jnp.maximum(jnp.minimum(i, n_tiles - 1) * C_TILE_BLOCKS - 1, 0), kv_col)),
                  pl.BlockSpec((blk, blk), const),
                  pl.BlockSpec((2, blk, blk), lambda i: (0, 0, 0)),
                  pl.BlockSpec((ts, bw), lambda i: (jnp.maximum(i - 1, 0), B_CONV_CH // bw)),
                  pl.BlockSpec((1, HEAD_DIM), const)],
        out_specs=[pl.BlockSpec((ts, bw), prev),
                   pl.BlockSpec((ts, bw), cur)],
        out_shape=[jax.ShapeDtypeStruct((t, bw), BF16),
                   jax.ShapeDtypeStruct((t, bw), BF16)],
        scratch_shapes=[pltpu.VMEM((B_CONV_CH // LANES, ts + SUBLANES, LANES), F32),
                        pltpu.VMEM((B_CONV_CH // LANES, ts, LANES), F32),
                        pltpu.VMEM((C_Q_HEADS, blk, blk), F32),
                        pltpu.VMEM((2, ts, bw), F32),
                        pltpu.VMEM((2, ts, bw), BF16),
                        pltpu.VMEM((2, ts, bw), BF16),
                        pltpu.VMEM((2, ts, bw), BF16),
                        pltpu.VMEM((2, ts // pr, HEADS, pr, pr), BF16),
                        pltpu.VMEM((2, ts // c, SMALL_PAD), F32),
                        pltpu.VMEM((HEADS, HEAD_DIM, HEAD_DIM), F32)],
        compiler_params=pltpu.CompilerParams(
            dimension_semantics=("arbitrary",), vmem_limit_bytes=VMEM_LIMIT),
        name="mix_bc",
    )(pb, pb, cw, alog, dtb, tril, rel_bias, sinks, pc, pc, bucket, halves, pb, nwb)


def _lane_pad(v, offset):
    return jnp.zeros((1, SMALL_PAD), F32).at[0, offset:offset + v.shape[0]].set(v.astype(F32))


def kernel(x, norm_w, w_in, conv_w, a_log, dt_bias, lb_param, norm_a, norm_b, sinks, rel_bias,
           w_branch, w_out, final_norm):
    batch, seq, _ = x.shape
    depth = w_in.shape[0]
    x2 = x.reshape(batch * seq, D_MODEL)
    fn = final_norm.reshape(1, D_MODEL)
    w16 = w_in.astype(BF16)
    wbr = w_branch.astype(BF16)
    wo = w_out.astype(BF16)
    for l in range(depth):
        nw = norm_w[l].reshape(1, D_MODEL)
        pa, pb, pc = _inproj(x2, nw, w16, l)
        ya = _hgrn2(pa, lb_param, norm_a[l].reshape(1, HEAD_DIM), batch, seq, l)
        yb, yc = _mix_bc(pb, pc, conv_w[l], _lane_pad(a_log[l], HEADS), _lane_pad(dt_bias[l], HEADS),
                         norm_b[l].reshape(1, HEAD_DIM), rel_bias, sinks[l], batch, seq)
        x2 = _merge(x2, nw, w16, ya, yb, yc, wbr, wo, fn, layer=l, final=(l == depth - 1))
    return x2.reshape(batch, seq, D_MODEL)
```

```python
import functools
import math

import numpy as np
import jax
import jax.numpy as jnp
from jax import lax
from jax.experimental import pallas as pl
from jax.experimental.pallas import tpu as pltpu

F32 = jnp.float32
BF16 = jnp.bfloat16

D_MODEL = 1024
BRANCH_WIDTH = D_MODEL // 2
N_BRANCHES = 3
EPS = 1e-6
HEADS = 4
HEAD_DIM = BRANCH_WIDTH // HEADS
B_CONV = 4
B_CONV_CH = 3 * BRANCH_WIDTH
C_Q_HEADS = 8
C_KV_HEADS = 2
C_HEAD_DIM = BRANCH_WIDTH // C_Q_HEADS
C_KV_WIDTH = C_KV_HEADS * C_HEAD_DIM
WINDOW = 128
C_BLOCK = 128
C_LOCKSTEP_BLOCKS = 2
N_BUCKETS = 32
MAX_DISTANCE = 128
LANES = 128
SUBLANES = 8
NEG_BIG = -1e30
LOG2E = math.log2(math.e)

A_CHUNK = 128
A_TILE_CHUNKS = 8
B_CHUNK = 64
B_PAIR = 2 * B_CHUNK
B_TILE = 512
C_TILE_BLOCKS = B_TILE // C_BLOCK
B_LOCKSTEP = 16
B_SCAN_PAIRS = 4
PROJ_ROWS = 512
SMALL_PAD = LANES
VMEM_LIMIT = 52 * 1024 * 1024

PA_WIDTH = 4 * BRANCH_WIDTH
PB_WIDTH = B_CONV_CH + BRANCH_WIDTH + SMALL_PAD
PC_WIDTH = 2 * BRANCH_WIDTH + 2 * C_KV_WIDTH
W_AB = PA_WIDTH + PB_WIDTH - (SMALL_PAD - 2 * HEADS)
W_ABC = W_AB + PC_WIDTH


def _dot(a, b):
    return jnp.dot(a, b, preferred_element_type=F32)


def _dot_nt(a, b):
    return lax.dot_general(a, b, (((1,), (1,)), ((), ())), preferred_element_type=F32)


def _dot_tn(a, b):
    return lax.dot_general(a, b, (((0,), (0,)), ((), ())), preferred_element_type=F32)


def _sigmoid(x):
    return 0.5 * jnp.tanh(0.5 * x) + 0.5


def _silu(x):
    h = 0.5 * x
    return h * jnp.tanh(h) + h


def _cumsum_rows(tril16, x):
    hi = x.astype(BF16)
    r1 = x - hi.astype(F32)
    mid = r1.astype(BF16)
    lo = (r1 - mid.astype(F32)).astype(BF16)
    return _dot(tril16, hi) + (_dot(tril16, mid) + _dot(tril16, lo))


def _softplus(x):
    return jnp.maximum(x, 0.0) + jnp.log(1.0 + jnp.exp(-jnp.abs(x)))


def _rms(x, w):
    return x * lax.rsqrt(jnp.mean(x * x, axis=-1, keepdims=True) + EPS) * w


def _resident(shape):
    return pl.BlockSpec(shape, lambda i: (0,) * len(shape), pipeline_mode=pl.Buffered(1))


def _layer_resident(shape, layer):
    return pl.BlockSpec((None,) + tuple(shape[1:]), lambda i: (layer,) + (0,) * (len(shape) - 1),
                        pipeline_mode=pl.Buffered(1))


def _inproj_kernel(x_ref, nw_ref, w_ref, pa_ref, pb_ref, pc_ref, wc_ref):
    @pl.when(pl.program_id(0) == 0)
    def _():
        wc_ref[...] = w_ref[:, W_AB:W_ABC]

    h = _rms(x_ref[...], nw_ref[...]).astype(BF16)
    for o_ref, src_ref, base in ((pa_ref, w_ref, 0), (pb_ref, w_ref, PA_WIDTH), (pc_ref, wc_ref, 0)):
        n = o_ref.shape[1]
        for j in range(0, n, BRANCH_WIDTH):
            wd = min(BRANCH_WIDTH, n - j)
            o_ref[:, j:j + wd] = _dot(h, src_ref[:, base + j:base + j + wd])


def _inproj(x2, nw, w, layer):
    t = x2.shape[0]
    tm = PROJ_ROWS
    return pl.pallas_call(
        _inproj_kernel,
        grid=(t // tm,),
        in_specs=[pl.BlockSpec((tm, D_MODEL), lambda i: (i, 0)),
                  _resident((1, D_MODEL)),
                  _layer_resident(w.shape, layer)],
        out_specs=[pl.BlockSpec((tm, PA_WIDTH), lambda i: (i, 0)),
                   pl.BlockSpec((tm, PB_WIDTH), lambda i: (i, 0)),
                   pl.BlockSpec((tm, PC_WIDTH), lambda i: (i, 0))],
        out_shape=[jax.ShapeDtypeStruct((t, PA_WIDTH), F32),
                   jax.ShapeDtypeStruct((t, PB_WIDTH), F32),
                   jax.ShapeDtypeStruct((t, PC_WIDTH), F32)],
        scratch_shapes=[pltpu.VMEM((D_MODEL, PC_WIDTH), BF16)],
        compiler_params=pltpu.CompilerParams(
            dimension_semantics=("arbitrary",), vmem_limit_bytes=VMEM_LIMIT),
        name="inproj",
    )(x2, nw, w)


def _merge_kernel(x_ref, nw_ref, w_ref, ya_ref, yb_ref, yc_ref, wbr_ref, wo_ref, fn_ref,
                  o_ref, wg_ref, *, final):
    @pl.when(pl.program_id(0) == 0)
    def _():
        wg_ref[...] = w_ref[:, W_ABC:]

    x = x_ref[...]
    h = _rms(x, nw_ref[...]).astype(BF16)
    merged = None
    for n, y_ref in enumerate((ya_ref, yb_ref, yc_ref)):
        gate = _sigmoid(_dot(h, wg_ref[:, n * D_MODEL:(n + 1) * D_MODEL]))
        term = gate * _dot(y_ref[...], wbr_ref[n])
        merged = term if merged is None else merged + term
    out = x + _dot(merged.astype(BF16), wo_ref[...])
    if final:
        out = _rms(out, fn_ref[...])
    o_ref[...] = out


def _merge(x2, nw, w, ya, yb, yc, wbr, wo, fn, layer, final):
    t = x2.shape[0]
    tm = PROJ_ROWS
    row = lambda i: (i, 0)
    return pl.pallas_call(
        functools.partial(_merge_kernel, final=final),
        grid=(t // tm,),
        in_specs=[pl.BlockSpec((tm, D_MODEL), row),
                  _resident((1, D_MODEL)),
                  _layer_resident(w.shape, layer),
                  pl.BlockSpec((tm, BRANCH_WIDTH), row),
                  pl.BlockSpec((tm, BRANCH_WIDTH), row),
                  pl.BlockSpec((tm, BRANCH_WIDTH), row),
                  _layer_resident(wbr.shape, layer),
                  _layer_resident(wo.shape, layer),
                  _resident((1, D_MODEL))],
        out_specs=pl.BlockSpec((tm, D_MODEL), row),
        out_shape=jax.ShapeDtypeStruct((t, D_MODEL), F32),
        scratch_shapes=[pltpu.VMEM((D_MODEL, N_BRANCHES * D_MODEL), BF16)],
        compiler_params=pltpu.CompilerParams(
            dimension_semantics=("arbitrary",), vmem_limit_bytes=VMEM_LIMIT),
        name="merge",
    )(x2, nw, w, ya, yb, yc, wbr, wo, fn)


def _a_levels():
    return [1 << i for i in range(int(math.log2(A_CHUNK)))]


def _a_level_ids():
    idx = np.arange(A_CHUNK)
    ids = np.where(idx[:, None] == idx[None, :], 0, -1).astype(np.int32)
    for i, m in enumerate(_a_levels()):
        blk = idx // (2 * m)
        upper = (idx & m) != 0
        ids[(blk[:, None] == blk[None, :]) & upper[:, None] & ~upper[None, :]] = i + 1
    return ids


def _hgrn2_kernel(pa_ref, lbp_ref, nw_ref, tril_ref, lvl_ref, o_ref,
                  state_ref, b_ref, gp_ref, coarse_ref, *, layer):
    c = A_CHUNK
    bw = BRANCH_WIDTH

    @pl.when(pl.program_id(1) == 0)
    def _():
        state_ref[...] = jnp.zeros_like(state_ref)
        gp_ref[...] = jnp.zeros_like(gp_ref)
        coarse_ref[...] = jnp.zeros_like(coarse_ref)

    lbp = lbp_ref[...]
    ex = jnp.exp(lbp - jnp.max(lbp, axis=0, keepdims=True))
    sm = ex / jnp.sum(ex, axis=0, keepdims=True)
    lb = jnp.zeros((1, bw), F32)
    for j in range(1, layer + 1):
        lb = lb + sm[j:j + 1, :]

    hsl = [slice(h * HEAD_DIM, (h + 1) * HEAD_DIM) for h in range(HEADS)]
    row = lax.broadcasted_iota(jnp.int32, (c, 1), 0)
    lvl = lvl_ref[...]
    t1 = jnp.log(lb)
    t2_lb = jnp.log1p(-lb)

    def intra_chunk(ci):
        rows = slice(ci * c, (ci + 1) * c)
        bc_ref = b_ref.at[ci]
        gc_ref = gp_ref.at[ci]
        q = _silu(pa_ref[rows, 0:bw])
        z = pa_ref[rows, bw:2 * bw]
        v = pa_ref[rows, 2 * bw:3 * bw].astype(BF16)
        log_sig = jnp.minimum(z, 0.0) - jnp.log(1.0 + jnp.exp(-jnp.abs(z)))
        sig_neg = 0.5 - 0.5 * jnp.tanh(0.5 * z)
        t2 = t2_lb + log_sig
        lf = jnp.maximum(t1, t2) + jnp.log(1.0 + jnp.exp(-jnp.abs(t1 - t2)))
        k = (1.0 - lb) * sig_neg

        lf2 = lf * LOG2E
        b = _cumsum_rows(tril_ref[...], lf2)
        bc_ref[...] = b
        gc_ref[SUBLANES:SUBLANES + c, :] = lf2

        intra = []
        for h, sl in enumerate(hsl):
            qh = q[:, sl]
            kh = k[:, sl]
            lfh = lf2[:, sl]
            acc = jnp.where(lvl == 0, _dot_nt(qh.astype(BF16), kh.astype(BF16)), 0.0)
            for li, m in enumerate(_a_levels()):
                blocks = range(0, c, 2 * m)
                if m == 1:
                    neg = jnp.where((row & 1) == 1, lfh, 0.0)
                elif m == 2:
                    r4 = row & 3
                    g_next = gc_ref[SUBLANES + 1:SUBLANES + 1 + c, sl]
                    g_prev = gc_ref[SUBLANES - 1:SUBLANES - 1 + c, sl]
                    neg = jnp.where(r4 == 0, g_next,
                                    jnp.where(r4 == 1, 0.0, jnp.where(r4 == 2, lfh, lfh + g_prev)))
                elif m < SUBLANES:
                    neg = jnp.concatenate(
                        [-jnp.abs(bc_ref[lo:lo + 2 * m, sl] - bc_ref[lo + m - 1:lo + m, sl]) for lo in blocks],
                        axis=0)
                else:
                    pieces = []
                    for lo in blocks:
                        anchor = bc_ref[lo + m - 1:lo + m, sl]
                        pieces += [anchor - bc_ref[lo:lo + m, sl], bc_ref[lo + m:lo + 2 * m, sl] - anchor]
                    neg = jnp.concatenate(pieces, axis=0)
                if m < SUBLANES:
                    x = jnp.where((row & m) != 0, qh, kh)
                else:
                    x = jnp.concatenate(
                        [t for lo in blocks for t in (kh[lo:lo + m, :], qh[lo + m:lo + 2 * m, :])], axis=0)
                xt = (x * jnp.exp2(neg)).astype(BF16)
                scores = _dot_nt(xt, xt)
                if m < SUBLANES:
                    acc = jnp.where(lvl == li + 1, scores, acc)
                else:
                    for lo in blocks:
                        coarse_ref[ci, h, lo + m:lo + 2 * m, lo:lo + m] = scores[lo + m:lo + 2 * m, lo:lo + m]
            intra.append(_dot((acc + coarse_ref[ci, h]).astype(BF16), v[:, sl]))

        b_end = bc_ref[c - 1:c, :]
        qd = (q * jnp.exp2(b)).astype(BF16)
        kd = (k * jnp.exp2(b_end - b)).astype(BF16)
        upd = [_dot_tn(v[:, hsl[h]], kd[:, hsl[h]]) for h in range(HEADS)]
        return intra, upd, qd, jnp.exp2(b_end)

    parts = [intra_chunk(ci) for ci in range(A_TILE_CHUNKS)]
    nw = nw_ref[...]
    sts = [state_ref[h] for h in range(HEADS)]
    for ci, (intra, upd, qd, s_decay) in enumerate(parts):
        rows = slice(ci * c, (ci + 1) * c)
        inter = [_dot_nt(qd[:, hsl[h]], sts[h].astype(BF16)) for h in range(HEADS)]
        sts = [sts[h] * s_decay[:, hsl[h]] + upd[h] for h in range(HEADS)]
        for h, sl in enumerate(hsl):
            o = intra[h] + inter[h]
            y = o * lax.rsqrt(jnp.mean(o * o, axis=-1, keepdims=True) + EPS) * nw
            gate = pa_ref[rows, 3 * bw + h * HEAD_DIM:3 * bw + (h + 1) * HEAD_DIM]
            o_ref[rows, sl] = (y * _silu(gate)).astype(BF16)
    for h in range(HEADS):
        state_ref[h] = sts[h]


def _hgrn2(pa, lbp, nw, batch, seq, layer):
    c = A_CHUNK
    tc = A_TILE_CHUNKS
    nt = seq // (c * tc)
    tril = jnp.asarray(np.tril(np.ones((c, c), np.float32)), dtype=BF16)
    lvl = jnp.asarray(_a_level_ids())
    const2 = lambda b, i: (0, 0)
    return pl.pallas_call(
        functools.partial(_hgrn2_kernel, layer=layer),
        grid=(batch, nt),
        in_specs=[pl.BlockSpec((tc * c, PA_WIDTH), lambda b, i: (b * nt + i, 0)),
                  pl.BlockSpec(lbp.shape, const2),
                  pl.BlockSpec((1, HEAD_DIM), const2),
                  pl.BlockSpec((c, c), const2),
                  pl.BlockSpec((c, c), const2)],
        out_specs=pl.BlockSpec((tc * c, BRANCH_WIDTH), lambda b, i: (b * nt + i, 0)),
        out_shape=jax.ShapeDtypeStruct((batch * seq, BRANCH_WIDTH), BF16),
        scratch_shapes=[pltpu.VMEM((HEADS, HEAD_DIM, HEAD_DIM), F32),
                        pltpu.VMEM((tc, c, BRANCH_WIDTH), F32),
                        pltpu.VMEM((tc, c + 2 * SUBLANES, BRANCH_WIDTH), F32),
                        pltpu.VMEM((tc, HEADS, c, c), F32)],
        compiler_params=pltpu.CompilerParams(
            dimension_semantics=("arbitrary", "arbitrary"), vmem_limit_bytes=VMEM_LIMIT),
        name="hgrn2",
    )(pa, lbp, nw, tril, lvl)


def _b_levels():
    return [1 << i for i in range(int(math.log2(B_CHUNK)))]


def _unit_lower_inverses(mats, ti, si):
    eye = (ti == si).astype(F32)
    ts = [eye] * len(mats)
    for s in _b_levels():
        shift = int(math.log2(2 * s))
        off = ((ti >> shift) == (si >> shift)) & ((ti & s) != 0) & ((si & s) == 0)
        a_off = [jnp.where(off, a, 0.0) for a in mats]
        if s == 1:
            ts = [t - ao for t, ao in zip(ts, a_off)]
        else:
            t16 = [t.astype(BF16) for t in ts]
            inner = [_dot(ao.astype(BF16), t) for ao, t in zip(a_off, t16)]
            yield
            outer = [_dot(t, x.astype(BF16)) for t, x in zip(t16, inner)]
            yield
            ts = [t - x for t, x in zip(ts, outer)]
    return ts


def _dn_prep_stages(seq_tile, pb_ref, halo_ref, cw_ref, alog_ref, dtb_ref, tril_ref,
                    u_ref, w_ref, qd_ref, kd_ref, qk_ref, ge_ref, xp_ref, qkv_ref):
    ts = B_TILE
    c = B_CHUNK
    pr = B_PAIR
    bw = BRANCH_WIDTH
    sl8 = SUBLANES

    keep = jnp.where(seq_tile == 0, 0.0, 1.0)
    half = ts // 2
    for s in range(B_CONV_CH // LANES):
        lanes = slice(s * LANES, (s + 1) * LANES)
        xp_ref[s, 0:sl8, :] = halo_ref[:, lanes] * keep
        xp_ref[s, sl8:sl8 + ts, :] = pb_ref[:, lanes]
    yield
    for s in range(B_CONV_CH // LANES):
        lanes = slice(s * LANES, (s + 1) * LANES)
        cw = [0.5 * cw_ref[j:j + 1, lanes] for j in range(B_CONV)]
        ld = {off: xp_ref[s, pl.ds(sl8 + off, half, stride=2), :] for off in range(1 - B_CONV, 2)}
        even = None
        odd = None
        for j in range(B_CONV):
            te = cw[j] * ld[j - (B_CONV - 1)]
            to = cw[j] * ld[j - (B_CONV - 1) + 1]
            even = te if even is None else even + te
            odd = to if odd is None else odd + to
        qkv_ref[s, pl.ds(0, half, stride=2), :] = even * jnp.tanh(even) + even
        qkv_ref[s, pl.ds(1, half, stride=2), :] = odd * jnp.tanh(odd) + odd
        yield

    small = pb_ref[:, B_CONV_CH + bw:B_CONV_CH + bw + SMALL_PAD]
    beta_all = _sigmoid(small)
    g_all = -jnp.exp(alog_ref[...]) * _softplus(small + dtb_ref[...])
    gcum = jnp.concatenate(
        [_cumsum_rows(tril_ref[...], g_all[p * pr:(p + 1) * pr, :]) for p in range(ts // pr)], axis=0)
    gcum_t = gcum.T
    for j in range(ts // c):
        ge_ref[0, j:j + 1, :] = jnp.exp(gcum[j * c + c - 1:j * c + c, :])

    ti = lax.broadcasted_iota(jnp.int32, (pr, pr), 0)
    si = lax.broadcasted_iota(jnp.int32, (pr, pr), 1)
    same = (ti >> int(math.log2(c))) == (si >> int(math.log2(c)))
    incl = same & (ti >= si)
    first_chunk = lax.broadcasted_iota(jnp.int32, (pr, 1), 0) < c
    yield

    all_probs = [(h, p) for h in range(HEADS) for p in range(ts // pr)]
    for first in range(0, len(all_probs), B_LOCKSTEP):
        probs = all_probs[first:first + B_LOCKSTEP]
        k16s, kb16s, q16s, decays, rhs = [], [], [], [], []
        for h, p in probs:
            sl = slice(h * HEAD_DIM, (h + 1) * HEAD_DIM)
            rows = slice(p * pr, (p + 1) * pr)
            q = qkv_ref[h, rows, :]
            k = qkv_ref[HEADS + h, rows, :]
            v = qkv_ref[2 * HEADS + h, rows, :]
            q = q * (lax.rsqrt(jnp.sum(q * q, axis=-1, keepdims=True) + EPS) * (HEAD_DIM ** -0.5))
            k = k * lax.rsqrt(jnp.sum(k * k, axis=-1, keepdims=True) + EPS)
            beta = beta_all[rows, h:h + 1]
            gc = gcum[rows, HEADS + h:HEADS + h + 1]
            gr = gcum_t[HEADS + h:HEADS + h + 1, rows]
            g_last = jnp.where(first_chunk, gc[c - 1:c, :], gc[pr - 1:pr, :])
            egc = jnp.exp(gc)
            kb = k * beta
            decays.append(jnp.where(incl, jnp.exp(jnp.minimum(gc - gr, 0.0)), 0.0))
            k16s.append(k.astype(BF16))
            kb16s.append(kb.astype(BF16))
            q16s.append(q.astype(BF16))
            rhs.append(jnp.concatenate([v * beta, kb * egc], axis=1).astype(BF16))
            qd_ref[rows, sl] = (q * egc).astype(BF16)
            kd_ref[rows, sl] = (k * jnp.exp(g_last - gc)).astype(BF16)
            yield

        kk = [_dot_nt(kb16, k16) for kb16, k16 in zip(kb16s, k16s)]
        yield
        qk = [_dot_nt(q16, k16) for q16, k16 in zip(q16s, k16s)]
        yield
        mats = [x * d for x, d in zip(kk, decays)]
        tinv = yield from _unit_lower_inverses(mats, ti, si)
        uw = [_dot(t.astype(BF16), r) for t, r in zip(tinv, rhs)]
        yield
        for i, (h, p) in enumerate(probs):
            sl = slice(h * HEAD_DIM, (h + 1) * HEAD_DIM)
            rows = slice(p * pr, (p + 1) * pr)
            u_ref[rows, sl] = uw[i][:, 0:HEAD_DIM]
            w_ref[rows, sl] = uw[i][:, HEAD_DIM:].astype(BF16)
            qk_ref[p, h] = (qk[i] * decays[i]).astype(BF16)


def _dn_scan_kernel(u_ref, w_ref, qd_ref, kd_ref, qk_ref, ge_ref, z_ref, nw_ref, o_ref, state_ref):
    c = B_CHUNK
    batch = u_ref.shape[0]

    @pl.when(pl.program_id(0) == 0)
    def _():
        state_ref[...] = jnp.zeros_like(state_ref)

    nw = nw_ref[...]
    chains = [(b, h) for b in range(batch) for h in range(HEADS)]
    hsl = lambda h: slice(h * HEAD_DIM, (h + 1) * HEAD_DIM)
    states = [state_ref[b * HEADS + h] for b, h in chains]
    for pi in range(B_SCAN_PAIRS):
        o_inter = [[] for _ in chains]
        v_new = [[] for _ in chains]
        for j in range(B_PAIR // c):
            rows = slice(pi * B_PAIR + j * c, pi * B_PAIR + (j + 1) * c)
            st16 = [st.astype(BF16) for st in states]
            lhs = [jnp.concatenate([w_ref[b, rows, hsl(h)], qd_ref[b, rows, hsl(h)]], axis=0)
                   for b, h in chains]
            prod = [_dot(x, s) for x, s in zip(lhs, st16)]
            vn16 = []
            for i, (b, h) in enumerate(chains):
                vn = (u_ref[b, rows, hsl(h)] - prod[i][0:c, :]).astype(BF16)
                vn16.append(vn)
                v_new[i].append(vn)
                o_inter[i].append(prod[i][c:, :])
            upd = [_dot_tn(kd_ref[b, rows, hsl(h)], vn) for (b, h), vn in zip(chains, vn16)]
            states = [st * ge_ref[b, pi, j:j + 1, HEADS + h:HEADS + h + 1] + x
                      for (b, h), st, x in zip(chains, states, upd)]
        intra = [_dot(qk_ref[b, pi, h], jnp.concatenate(v_new[i], axis=0)) for i, (b, h) in enumerate(chains)]
        prows = slice(pi * B_PAIR, (pi + 1) * B_PAIR)
        for i, (b, h) in enumerate(chains):
            o = jnp.concatenate(o_inter[i], axis=0) + intra[i]
            y = o * lax.rsqrt(jnp.mean(o * o, axis=-1, keepdims=True) + EPS) * nw
            o_ref[b, prows, hsl(h)] = (y * _silu(z_ref[b, prows, hsl(h)])).astype(BF16)
    for i, (b, h) in enumerate(chains):
        state_ref[b * HEADS + h] = states[i]


def _dn_scan(pb, u, w, qd, kd, qk, ge, nw, batch, seq):
    c = B_CHUNK
    pr = B_PAIR
    bw = BRANCH_WIDTH
    t = batch * seq
    const = lambda i: (0, 0)
    np_seq = seq // pr
    seq3 = lambda x: x.reshape(batch, seq, x.shape[-1])
    sp = B_SCAN_PAIRS
    blk3 = pl.BlockSpec((batch, sp * pr, bw), lambda i: (0, i, 0))
    y = pl.pallas_call(
        _dn_scan_kernel,
        grid=(np_seq // sp,),
        in_specs=[blk3, blk3, blk3, blk3,
                  pl.BlockSpec((batch, sp, HEADS, pr, pr), lambda i: (0, i, 0, 0, 0)),
                  pl.BlockSpec((batch, sp, pr // c, SMALL_PAD), lambda i: (0, i, 0, 0)),
                  pl.BlockSpec((batch, sp * pr, bw), lambda i: (0, i, B_CONV_CH // bw)),
                  pl.BlockSpec((1, HEAD_DIM), const)],
        out_specs=blk3,
        out_shape=jax.ShapeDtypeStruct((batch, seq, bw), BF16),
        scratch_shapes=[pltpu.VMEM((batch * HEADS, HEAD_DIM, HEAD_DIM), F32)],
        compiler_params=pltpu.CompilerParams(
            dimension_semantics=("arbitrary",), vmem_limit_bytes=VMEM_LIMIT),
        name="dn_scan",
    )(seq3(u), seq3(w), seq3(qd), seq3(kd),
      qk.reshape(batch, np_seq, HEADS, pr, pr),
      ge.reshape(batch, np_seq, pr // c, SMALL_PAD),
      seq3(pb), nw)
    return y.reshape(t, bw)


def _band_buckets():
    assert WINDOW == C_BLOCK
    r = np.arange(C_BLOCK)[:, None]
    c = np.arange(C_BLOCK)[None, :]
    dist = np.where(c > r, r + C_BLOCK - c, r - c)
    max_exact = N_BUCKETS // 2
    d_f = np.maximum(dist, 1).astype(np.float32)
    large = max_exact + (np.log(d_f / np.float32(max_exact)) / np.float32(math.log(MAX_DISTANCE / max_exact))
                         * np.float32(N_BUCKETS - max_exact)).astype(np.int32)
    large = np.minimum(large, N_BUCKETS - 1)
    return np.where(dist < max_exact, dist, large).astype(np.int32)


def _band_halves():
    r = np.arange(C_BLOCK)[:, None]
    c = np.arange(C_BLOCK)[None, :]
    return np.stack([c > r, c <= r]).astype(np.float32)


def _swa_stages(n, first_step, rb_ref, sink_ref, cur_ref, prev_ref, bucket_ref, half_ref, o_ref, bias_ref):
    blk = C_BLOCK
    bw = BRANCH_WIDTH

    @pl.when(first_step)
    def _():
        bucket = bucket_ref[...]
        for h in range(C_Q_HEADS):
            acc = jnp.zeros((blk, blk), F32)
            for bk in range(N_BUCKETS):
                acc = jnp.where(bucket == bk, rb_ref[bk, h], acc)
            bias_ref[h] = acc

    kv = jnp.concatenate([prev_ref[...], cur_ref[:, bw:bw + 2 * C_KV_WIDTH]], axis=0)
    rows = kv.shape[0]
    lane = lax.broadcasted_iota(jnp.int32, (rows, C_KV_WIDTH), 1)

    def halves(x, j):
        own = jnp.where((lane >= j * C_HEAD_DIM) & (lane < (j + 1) * C_HEAD_DIM), x, 0.0)
        other = pltpu.roll(own, C_HEAD_DIM, 1)
        pair = (own, other) if j == 0 else (other, own)
        return [t.astype(BF16) for t in pair]

    kz = [halves(kv[:, 0:C_KV_WIDTH], j) for j in range(C_KV_HEADS)]
    vz = [halves(kv[:, C_KV_WIDTH:], j) for j in range(C_KV_HEADS)]
    scale = C_HEAD_DIM ** -0.5
    group = C_Q_HEADS // C_KV_HEADS
    from_prev = (lax.broadcasted_iota(jnp.int32, (blk, blk), 1) > lax.broadcasted_iota(jnp.int32, (blk, blk), 0))
    yield

    qrows = lambda i: slice(i * blk, (i + 1) * blk)
    krows = lambda i: slice(i * blk, (i + 2) * blk)
    for first in range(0, C_TILE_BLOCKS, C_LOCKSTEP_BLOCKS):
        blocks = range(first, first + C_LOCKSTEP_BLOCKS)
        chains = [(i, h) for i in blocks for h in range(C_Q_HEADS)]
        q2 = {i: [(cur_ref[qrows(i), p * LANES:(p + 1) * LANES] * scale).astype(BF16)
                  for p in range(C_Q_HEADS // 2)] for i in blocks}
        both = [_dot_nt(q2[i][h // 2], kz[h // group][h % 2][krows(i), :]) for i, h in chains]
        yield
        logits = []
        for (i, h), lg in zip(chains, both):
            prev_part = lg[:, 0:blk]
            if i == 0:
                prev_part = prev_part + jnp.where(n > 0, 0.0, NEG_BIG)
            logits.append(jnp.where(from_prev, prev_part, lg[:, blk:]) + bias_ref[h])
        yield
        mx = [jnp.maximum(jnp.max(lg, axis=-1, keepdims=True), sink_ref[h])
              for (i, h), lg in zip(chains, logits)]
        yield
        pr = [jnp.exp(lg - m) for lg, m in zip(logits, mx)]
        den = [jnp.sum(x, axis=-1, keepdims=True) + jnp.exp(sink_ref[h] - m)
               for (i, h), x, m in zip(chains, pr, mx)]
        yield
        pr16 = [x.astype(BF16) for x in pr]
        pcat = [jnp.concatenate([x * half_ref[0], x * half_ref[1]], axis=1) for x in pr16]
        pv = [_dot(x, vz[h // group][h % 2][krows(i), :]) for (i, h), x in zip(chains, pcat)]
        yield
        for bi, i in enumerate(blocks):
            for p in range(C_Q_HEADS // 2):
                a, b = bi * C_Q_HEADS + 2 * p, bi * C_Q_HEADS + 2 * p + 1
                out = pv[a] / den[a] + pv[b] / den[b]
                gate = cur_ref[qrows(i), bw + 2 * C_KV_WIDTH + p * LANES:bw + 2 * C_KV_WIDTH + (p + 1) * LANES]
                o_ref[qrows(i), p * LANES:(p + 1) * LANES] = (out * _silu(gate)).astype(BF16)
        yield


N_DN_IN = 6
N_DN_OUT = 6
N_SWA_IN = 6


def _mix_bc_kernel(*refs, tiles_per_seq):
    dn_in = refs[0:N_DN_IN]
    swa_in = refs[N_DN_IN:N_DN_IN + N_SWA_IN]
    outs = refs[N_DN_IN + N_SWA_IN:N_DN_IN + N_SWA_IN + N_DN_OUT + 1]
    xp_ref, qkv_ref, bias_ref = refs[N_DN_IN + N_SWA_IN + N_DN_OUT + 1:]
    step = pl.program_id(0)
    seq_tile = step % tiles_per_seq
    jobs = [_dn_prep_stages(seq_tile, *dn_in, *outs[0:N_DN_OUT], xp_ref, qkv_ref),
            _swa_stages(seq_tile, step == 0, *swa_in, outs[N_DN_OUT], bias_ref)]
    while jobs:
        for job in list(jobs):
            try:
                next(job)
            except StopIteration:
                jobs.remove(job)


def _mix_bc(pb, pc, cw, alog, dtb, rel_bias, sinks, batch, seq):
    c = B_CHUNK
    ts = B_TILE
    pr = B_PAIR
    bw = BRANCH_WIDTH
    blk = C_BLOCK
    t = batch * seq
    n_tiles = t // ts
    tril = jnp.asarray(np.kron(np.eye(pr // c, dtype=np.float32), np.tril(np.ones((c, c), np.float32))),
                       dtype=BF16)
    bucket = jnp.asarray(_band_buckets())
    halves = jnp.asarray(_band_halves(), dtype=BF16)
    const = lambda i: (0, 0)
    row = lambda i: (i, 0)
    halo_blocks = ts // SUBLANES
    kv_col = BRANCH_WIDTH // (2 * C_KV_WIDTH)
    smem = pl.BlockSpec(memory_space=pltpu.SMEM)
    return pl.pallas_call(
        functools.partial(_mix_bc_kernel, tiles_per_seq=seq // ts),
        grid=(n_tiles,),
        in_specs=[pl.BlockSpec((ts, PB_WIDTH), row),
                  pl.BlockSpec((SUBLANES, B_CONV_CH), lambda i: (jnp.maximum(i * halo_blocks - 1, 0), 0)),
                  pl.BlockSpec(cw.shape, const),
                  pl.BlockSpec((1, SMALL_PAD), const),
                  pl.BlockSpec((1, SMALL_PAD), const),
                  pl.BlockSpec((pr, pr), const),
                  smem, smem,
                  pl.BlockSpec((ts, PC_WIDTH), row),
                  pl.BlockSpec((blk, 2 * C_KV_WIDTH), lambda i: (jnp.maximum(i * C_TILE_BLOCKS - 1, 0), kv_col)),
                  pl.BlockSpec((blk, blk), const),
                  pl.BlockSpec((2, blk, blk), lambda i: (0, 0, 0))],
        out_specs=[pl.BlockSpec((ts, bw), row),
                   pl.BlockSpec((ts, bw), row),
                   pl.BlockSpec((ts, bw), row),
                   pl.BlockSpec((ts, bw), row),
                   pl.BlockSpec((ts // pr, HEADS, pr, pr), lambda i: (i, 0, 0, 0)),
                   pl.BlockSpec((1, ts // c, SMALL_PAD), lambda i: (i, 0, 0)),
                   pl.BlockSpec((ts, bw), row)],
        out_shape=[jax.ShapeDtypeStruct((t, bw), F32),
                   jax.ShapeDtypeStruct((t, bw), BF16),
                   jax.ShapeDtypeStruct((t, bw), BF16),
                   jax.ShapeDtypeStruct((t, bw), BF16),
                   jax.ShapeDtypeStruct((t // pr, HEADS, pr, pr), BF16),
                   jax.ShapeDtypeStruct((n_tiles, ts // c, SMALL_PAD), F32),
                   jax.ShapeDtypeStruct((t, bw), BF16)],
        scratch_shapes=[pltpu.VMEM((B_CONV_CH // LANES, ts + SUBLANES, LANES), F32),
                        pltpu.VMEM((B_CONV_CH // LANES, ts, LANES), F32),
                        pltpu.VMEM((C_Q_HEADS, blk, blk), F32)],
        compiler_params=pltpu.CompilerParams(
            dimension_semantics=("arbitrary",), vmem_limit_bytes=VMEM_LIMIT),
        name="mix_bc",
    )(pb, pb, cw, alog, dtb, tril, rel_bias, sinks, pc, pc, bucket, halves)


def _lane_pad(v, offset):
    return jnp.zeros((1, SMALL_PAD), F32).at[0, offset:offset + v.shape[0]].set(v.astype(F32))


def kernel(x, norm_w, w_in, conv_w, a_log, dt_bias, lb_param, norm_a, norm_b, sinks, rel_bias,
           w_branch, w_out, final_norm):
    batch, seq, _ = x.shape
    depth = w_in.shape[0]
    x2 = x.reshape(batch * seq, D_MODEL)
    fn = final_norm.reshape(1, D_MODEL)
    w16 = w_in.astype(BF16)
    wbr = w_branch.astype(BF16)
    wo = w_out.astype(BF16)
    for l in range(depth):
        nw = norm_w[l].reshape(1, D_MODEL)
        pa, pb, pc = _inproj(x2, nw, w16, l)
        ya = _hgrn2(pa, lb_param, norm_a[l].reshape(1, HEAD_DIM), batch, seq, l)
        *dn, yc = _mix_bc(pb, pc, conv_w[l], _lane_pad(a_log[l], HEADS), _lane_pad(dt_bias[l], HEADS),
                          rel_bias, sinks[l], batch, seq)
        yb = _dn_scan(pb, *dn, norm_b[l].reshape(1, HEAD_DIM), batch, seq)
        x2 = _merge(x2, nw, w16, ya, yb, yc, wbr, wo, fn, layer=l, final=(l == depth - 1))
    return x2.reshape(batch, seq, D_MODEL)
```

```python
import functools
import math

import numpy as np
import jax
import jax.numpy as jnp
from jax import lax
from jax.experimental import pallas as pl
from jax.experimental.pallas import tpu as pltpu

F32 = jnp.float32
BF16 = jnp.bfloat16

D_MODEL = 1024
BRANCH_WIDTH = D_MODEL // 2
N_BRANCHES = 3
EPS = 1e-6
HEADS = 4
HEAD_DIM = BRANCH_WIDTH // HEADS
B_CONV = 4
B_CONV_CH = 3 * BRANCH_WIDTH
C_Q_HEADS = 8
C_KV_HEADS = 2
C_HEAD_DIM = BRANCH_WIDTH // C_Q_HEADS
C_KV_WIDTH = C_KV_HEADS * C_HEAD_DIM
WINDOW = 128
C_BLOCK = 128
C_LOCKSTEP_BLOCKS = 2
N_BUCKETS = 32
MAX_DISTANCE = 128
LANES = 128
SUBLANES = 8
NEG_BIG = -1e30
LOG2E = math.log2(math.e)

A_CHUNK = 128
A_TILE_CHUNKS = 8
B_CHUNK = 64
B_PAIR = 2 * B_CHUNK
B_TILE = 512
C_TILE_BLOCKS = B_TILE // C_BLOCK
B_LOCKSTEP = 16
B_SCAN_PAIRS = 4
PROJ_ROWS = 512
SMALL_PAD = LANES
VMEM_LIMIT = 52 * 1024 * 1024

PA_WIDTH = 4 * BRANCH_WIDTH
PB_WIDTH = B_CONV_CH + BRANCH_WIDTH + SMALL_PAD
PC_WIDTH = 2 * BRANCH_WIDTH + 2 * C_KV_WIDTH
W_AB = PA_WIDTH + PB_WIDTH - (SMALL_PAD - 2 * HEADS)
W_ABC = W_AB + PC_WIDTH


def _dot(a, b):
    return jnp.dot(a, b, preferred_element_type=F32)


def _dot_nt(a, b):
    return lax.dot_general(a, b, (((1,), (1,)), ((), ())), preferred_element_type=F32)


def _dot_tn(a, b):
    return lax.dot_general(a, b, (((0,), (0,)), ((), ())), preferred_element_type=F32)


def _sigmoid(x):
    return 0.5 * jnp.tanh(0.5 * x) + 0.5


def _silu(x):
    h = 0.5 * x
    return h * jnp.tanh(h) + h


def _cumsum_rows(tril16, x):
    hi = x.astype(BF16)
    r1 = x - hi.astype(F32)
    mid = r1.astype(BF16)
    lo = (r1 - mid.astype(F32)).astype(BF16)
    return _dot(tril16, hi) + (_dot(tril16, mid) + _dot(tril16, lo))


def _softplus(x):
    return jnp.maximum(x, 0.0) + jnp.log(1.0 + jnp.exp(-jnp.abs(x)))


def _rms(x, w):
    return x * lax.rsqrt(jnp.mean(x * x, axis=-1, keepdims=True) + EPS) * w


def _resident(shape):
    return pl.BlockSpec(shape, lambda i: (0,) * len(shape), pipeline_mode=pl.Buffered(1))


def _layer_resident(shape, layer):
    return pl.BlockSpec((None,) + tuple(shape[1:]), lambda i: (layer,) + (0,) * (len(shape) - 1),
                        pipeline_mode=pl.Buffered(1))


def _inproj_kernel(x_ref, nw_ref, w_ref, pa_ref, pb_ref, pc_ref, wc_ref):
    @pl.when(pl.program_id(0) == 0)
    def _():
        wc_ref[...] = w_ref[:, W_AB:W_ABC]

    h = _rms(x_ref[...], nw_ref[...]).astype(BF16)
    for o_ref, src_ref, base in ((pa_ref, w_ref, 0), (pb_ref, w_ref, PA_WIDTH), (pc_ref, wc_ref, 0)):
        n = o_ref.shape[1]
        for j in range(0, n, BRANCH_WIDTH):
            wd = min(BRANCH_WIDTH, n - j)
            o_ref[:, j:j + wd] = _dot(h, src_ref[:, base + j:base + j + wd])


def _inproj(x2, nw, w, layer):
    t = x2.shape[0]
    tm = PROJ_ROWS
    return pl.pallas_call(
        _inproj_kernel,
        grid=(t // tm,),
        in_specs=[pl.BlockSpec((tm, D_MODEL), lambda i: (i, 0)),
                  _resident((1, D_MODEL)),
                  _layer_resident(w.shape, layer)],
        out_specs=[pl.BlockSpec((tm, PA_WIDTH), lambda i: (i, 0)),
                   pl.BlockSpec((tm, PB_WIDTH), lambda i: (i, 0)),
                   pl.BlockSpec((tm, PC_WIDTH), lambda i: (i, 0))],
        out_shape=[jax.ShapeDtypeStruct((t, PA_WIDTH), F32),
                   jax.ShapeDtypeStruct((t, PB_WIDTH), F32),
                   jax.ShapeDtypeStruct((t, PC_WIDTH), F32)],
        scratch_shapes=[pltpu.VMEM((D_MODEL, PC_WIDTH), BF16)],
        compiler_params=pltpu.CompilerParams(
            dimension_semantics=("arbitrary",), vmem_limit_bytes=VMEM_LIMIT),
        name="inproj",
    )(x2, nw, w)


def _merge_kernel(x_ref, nw_ref, w_ref, ya_ref, yb_ref, yc_ref, wbr_ref, wo_ref, fn_ref,
                  o_ref, wg_ref, *, final):
    @pl.when(pl.program_id(0) == 0)
    def _():
        wg_ref[...] = w_ref[:, W_ABC:]

    x = x_ref[...]
    h = _rms(x, nw_ref[...]).astype(BF16)
    merged = None
    for n, y_ref in enumerate((ya_ref, yb_ref, yc_ref)):
        gate = _sigmoid(_dot(h, wg_ref[:, n * D_MODEL:(n + 1) * D_MODEL]))
        term = gate * _dot(y_ref[...], wbr_ref[n])
        merged = term if merged is None else merged + term
    out = x + _dot(merged.astype(BF16), wo_ref[...])
    if final:
        out = _rms(out, fn_ref[...])
    o_ref[...] = out


def _merge(x2, nw, w, ya, yb, yc, wbr, wo, fn, layer, final):
    t = x2.shape[0]
    tm = PROJ_ROWS
    row = lambda i: (i, 0)
    return pl.pallas_call(
        functools.partial(_merge_kernel, final=final),
        grid=(t // tm,),
        in_specs=[pl.BlockSpec((tm, D_MODEL), row),
                  _resident((1, D_MODEL)),
                  _layer_resident(w.shape, layer),
                  pl.BlockSpec((tm, BRANCH_WIDTH), row),
                  pl.BlockSpec((tm, BRANCH_WIDTH), row),
                  pl.BlockSpec((tm, BRANCH_WIDTH), row),
                  _layer_resident(wbr.shape, layer),
                  _layer_resident(wo.shape, layer),
                  _resident((1, D_MODEL))],
        out_specs=pl.BlockSpec((tm, D_MODEL), row),
        out_shape=jax.ShapeDtypeStruct((t, D_MODEL), F32),
        scratch_shapes=[pltpu.VMEM((D_MODEL, N_BRANCHES * D_MODEL), BF16)],
        compiler_params=pltpu.CompilerParams(
            dimension_semantics=("arbitrary",), vmem_limit_bytes=VMEM_LIMIT),
        name="merge",
    )(x2, nw, w, ya, yb, yc, wbr, wo, fn)


def _a_levels():
    return [1 << i for i in range(int(math.log2(A_CHUNK)))]


def _a_level_ids():
    idx = np.arange(A_CHUNK)
    ids = np.where(idx[:, None] == idx[None, :], 0, -1).astype(np.int32)
    for i, m in enumerate(_a_levels()):
        blk = idx // (2 * m)
        upper = (idx & m) != 0
        ids[(blk[:, None] == blk[None, :]) & upper[:, None] & ~upper[None, :]] = i + 1
    return ids


def _hgrn2_kernel(pa_ref, lbp_ref, nw_ref, tril_ref, lvl_ref, o_ref,
                  state_ref, b_ref, gp_ref, *, layer):
    c = A_CHUNK
    bw = BRANCH_WIDTH

    @pl.when(pl.program_id(1) == 0)
    def _():
        state_ref[...] = jnp.zeros_like(state_ref)
        gp_ref[...] = jnp.zeros_like(gp_ref)

    lbp = lbp_ref[...]
    ex = jnp.exp(lbp - jnp.max(lbp, axis=0, keepdims=True))
    sm = ex / jnp.sum(ex, axis=0, keepdims=True)
    lb = jnp.zeros((1, bw), F32)
    for j in range(1, layer + 1):
        lb = lb + sm[j:j + 1, :]

    hsl = [slice(h * HEAD_DIM, (h + 1) * HEAD_DIM) for h in range(HEADS)]
    row = lax.broadcasted_iota(jnp.int32, (c, 1), 0)
    lvl = lvl_ref[...]
    t1 = jnp.log(lb)
    t2_lb = jnp.log1p(-lb)

    def intra_chunk(ci):
        rows = slice(ci * c, (ci + 1) * c)
        bc_ref = b_ref.at[ci]
        gc_ref = gp_ref.at[ci]
        q = _silu(pa_ref[rows, 0:bw])
        z = pa_ref[rows, bw:2 * bw]
        v = pa_ref[rows, 2 * bw:3 * bw].astype(BF16)
        log_sig = jnp.minimum(z, 0.0) - jnp.log(1.0 + jnp.exp(-jnp.abs(z)))
        sig_neg = 0.5 - 0.5 * jnp.tanh(0.5 * z)
        t2 = t2_lb + log_sig
        lf = jnp.maximum(t1, t2) + jnp.log(1.0 + jnp.exp(-jnp.abs(t1 - t2)))
        k = (1.0 - lb) * sig_neg

        lf2 = lf * LOG2E
        b = _cumsum_rows(tril_ref[...], lf2)
        bc_ref[...] = b
        gc_ref[SUBLANES:SUBLANES + c, :] = lf2

        intra = []
        for sl in hsl:
            qh = q[:, sl]
            kh = k[:, sl]
            lfh = lf2[:, sl]
            acc = jnp.where(lvl == 0, _dot_nt(qh.astype(BF16), kh.astype(BF16)), 0.0)
            for li, m in enumerate(_a_levels()):
                blocks = range(0, c, 2 * m)
                if m == 1:
                    neg = jnp.where((row & 1) == 1, lfh, 0.0)
                elif m == 2:
                    r4 = row & 3
                    g_next = gc_ref[SUBLANES + 1:SUBLANES + 1 + c, sl]
                    g_prev = gc_ref[SUBLANES - 1:SUBLANES - 1 + c, sl]
                    neg = jnp.where(r4 == 0, g_next,
                                    jnp.where(r4 == 1, 0.0, jnp.where(r4 == 2, lfh, lfh + g_prev)))
                elif m < SUBLANES:
                    neg = jnp.concatenate(
                        [-jnp.abs(bc_ref[lo:lo + 2 * m, sl] - bc_ref[lo + m - 1:lo + m, sl]) for lo in blocks],
                        axis=0)
                else:
                    pieces = []
                    for lo in blocks:
                        anchor = bc_ref[lo + m - 1:lo + m, sl]
                        pieces += [anchor - bc_ref[lo:lo + m, sl], bc_ref[lo + m:lo + 2 * m, sl] - anchor]
                    neg = jnp.concatenate(pieces, axis=0)
                if m < SUBLANES:
                    x = jnp.where((row & m) != 0, qh, kh)
                else:
                    x = jnp.concatenate(
                        [t for lo in blocks for t in (kh[lo:lo + m, :], qh[lo + m:lo + 2 * m, :])], axis=0)
                xt = (x * jnp.exp2(neg)).astype(BF16)
                acc = jnp.where(lvl == li + 1, _dot_nt(xt, xt), acc)
            intra.append(_dot(acc.astype(BF16), v[:, sl]))

        b_end = bc_ref[c - 1:c, :]
        qd = (q * jnp.exp2(b)).astype(BF16)
        kd = (k * jnp.exp2(b_end - b)).astype(BF16)
        upd = [_dot_tn(v[:, hsl[h]], kd[:, hsl[h]]) for h in range(HEADS)]
        return intra, upd, qd, jnp.exp2(b_end)

    parts = [intra_chunk(ci) for ci in range(A_TILE_CHUNKS)]
    nw = nw_ref[...]
    sts = [state_ref[h] for h in range(HEADS)]
    for ci, (intra, upd, qd, s_decay) in enumerate(parts):
        rows = slice(ci * c, (ci + 1) * c)
        inter = [_dot_nt(qd[:, hsl[h]], sts[h].astype(BF16)) for h in range(HEADS)]
        sts = [sts[h] * s_decay[:, hsl[h]] + upd[h] for h in range(HEADS)]
        for h, sl in enumerate(hsl):
            o = intra[h] + inter[h]
            y = o * lax.rsqrt(jnp.mean(o * o, axis=-1, keepdims=True) + EPS) * nw
            gate = pa_ref[rows, 3 * bw + h * HEAD_DIM:3 * bw + (h + 1) * HEAD_DIM]
            o_ref[rows, sl] = (y * _silu(gate)).astype(BF16)
    for h in range(HEADS):
        state_ref[h] = sts[h]


def _hgrn2(pa, lbp, nw, batch, seq, layer):
    c = A_CHUNK
    tc = A_TILE_CHUNKS
    nt = seq // (c * tc)
    tril = jnp.asarray(np.tril(np.ones((c, c), np.float32)), dtype=BF16)
    lvl = jnp.asarray(_a_level_ids())
    const2 = lambda b, i: (0, 0)
    return pl.pallas_call(
        functools.partial(_hgrn2_kernel, layer=layer),
        grid=(batch, nt),
        in_specs=[pl.BlockSpec((tc * c, PA_WIDTH), lambda b, i: (b * nt + i, 0)),
                  pl.BlockSpec(lbp.shape, const2),
                  pl.BlockSpec((1, HEAD_DIM), const2),
                  pl.BlockSpec((c, c), const2),
                  pl.BlockSpec((c, c), const2)],
        out_specs=pl.BlockSpec((tc * c, BRANCH_WIDTH), lambda b, i: (b * nt + i, 0)),
        out_shape=jax.ShapeDtypeStruct((batch * seq, BRANCH_WIDTH), BF16),
        scratch_shapes=[pltpu.VMEM((HEADS, HEAD_DIM, HEAD_DIM), F32),
                        pltpu.VMEM((tc, c, BRANCH_WIDTH), F32),
                        pltpu.VMEM((tc, c + 2 * SUBLANES, BRANCH_WIDTH), F32)],
        compiler_params=pltpu.CompilerParams(
            dimension_semantics=("arbitrary", "arbitrary"), vmem_limit_bytes=VMEM_LIMIT),
        name="hgrn2",
    )(pa, lbp, nw, tril, lvl)


def _b_levels():
    return [1 << i for i in range(int(math.log2(B_CHUNK)))]


def _unit_lower_inverses(mats, ti, si):
    eye = (ti == si).astype(F32)
    ts = [eye] * len(mats)
    for s in _b_levels():
        shift = int(math.log2(2 * s))
        off = ((ti >> shift) == (si >> shift)) & ((ti & s) != 0) & ((si & s) == 0)
        a_off = [jnp.where(off, a, 0.0) for a in mats]
        if s == 1:
            ts = [t - ao for t, ao in zip(ts, a_off)]
        else:
            t16 = [t.astype(BF16) for t in ts]
            inner = [_dot(ao.astype(BF16), t) for ao, t in zip(a_off, t16)]
            yield
            outer = [_dot(t, x.astype(BF16)) for t, x in zip(t16, inner)]
            yield
            ts = [t - x for t, x in zip(ts, outer)]
    return ts


def _dn_prep_stages(seq_tile, pb_ref, halo_ref, cw_ref, alog_ref, dtb_ref, tril_ref,
                    u_ref, w_ref, qd_ref, kd_ref, qk_ref, ge_ref, xp_ref, qkv_ref):
    ts = B_TILE
    c = B_CHUNK
    pr = B_PAIR
    bw = BRANCH_WIDTH
    sl8 = SUBLANES

    keep = jnp.where(seq_tile == 0, 0.0, 1.0)
    half = ts // 2
    for s in range(B_CONV_CH // LANES):
        lanes = slice(s * LANES, (s + 1) * LANES)
        xp_ref[s, 0:sl8, :] = halo_ref[:, lanes] * keep
        xp_ref[s, sl8:sl8 + ts, :] = pb_ref[:, lanes]
    yield
    for s in range(B_CONV_CH // LANES):
        lanes = slice(s * LANES, (s + 1) * LANES)
        cw = [0.5 * cw_ref[j:j + 1, lanes] for j in range(B_CONV)]
        ld = {off: xp_ref[s, pl.ds(sl8 + off, half, stride=2), :] for off in range(1 - B_CONV, 2)}
        even = None
        odd = None
        for j in range(B_CONV):
            te = cw[j] * ld[j - (B_CONV - 1)]
            to = cw[j] * ld[j - (B_CONV - 1) + 1]
            even = te if even is None else even + te
            odd = to if odd is None else odd + to
        qkv_ref[s, pl.ds(0, half, stride=2), :] = even * jnp.tanh(even) + even
        qkv_ref[s, pl.ds(1, half, stride=2), :] = odd * jnp.tanh(odd) + odd
        yield

    small = pb_ref[:, B_CONV_CH + bw:B_CONV_CH + bw + SMALL_PAD]
    beta_all = _sigmoid(small)
    g_all = -jnp.exp(alog_ref[...]) * _softplus(small + dtb_ref[...])
    gcum = jnp.concatenate(
        [_cumsum_rows(tril_ref[...], g_all[p * pr:(p + 1) * pr, :]) for p in range(ts // pr)], axis=0)
    gcum_t = gcum.T
    for j in range(ts // c):
        ge_ref[0, j:j + 1, :] = jnp.exp(gcum[j * c + c - 1:j * c + c, :])

    ti = lax.broadcasted_iota(jnp.int32, (pr, pr), 0)
    si = lax.broadcasted_iota(jnp.int32, (pr, pr), 1)
    same = (ti >> int(math.log2(c))) == (si >> int(math.log2(c)))
    incl = same & (ti >= si)
    first_chunk = lax.broadcasted_iota(jnp.int32, (pr, 1), 0) < c
    yield

    all_probs = [(h, p) for h in range(HEADS) for p in range(ts // pr)]
    for first in range(0, len(all_probs), B_LOCKSTEP):
        probs = all_probs[first:first + B_LOCKSTEP]
        k16s, kb16s, q16s, decays, rhs = [], [], [], [], []
        for h, p in probs:
            sl = slice(h * HEAD_DIM, (h + 1) * HEAD_DIM)
            rows = slice(p * pr, (p + 1) * pr)
            q = qkv_ref[h, rows, :]
            k = qkv_ref[HEADS + h, rows, :]
            v = qkv_ref[2 * HEADS + h, rows, :]
            q = q * (lax.rsqrt(jnp.sum(q * q, axis=-1, keepdims=True) + EPS) * (HEAD_DIM ** -0.5))
            k = k * lax.rsqrt(jnp.sum(k * k, axis=-1, keepdims=True) + EPS)
            beta = beta_all[rows, h:h + 1]
            gc = gcum[rows, HEADS + h:HEADS + h + 1]
            gr = gcum_t[HEADS + h:HEADS + h + 1, rows]
            g_last = jnp.where(first_chunk, gc[c - 1:c, :], gc[pr - 1:pr, :])
            egc = jnp.exp(gc)
            kb = k * beta
            decays.append(jnp.where(incl, jnp.exp(jnp.minimum(gc - gr, 0.0)), 0.0))
            k16s.append(k.astype(BF16))
            kb16s.append(kb.astype(BF16))
            q16s.append(q.astype(BF16))
            rhs.append(jnp.concatenate([v * beta, kb * egc], axis=1).astype(BF16))
            qd_ref[rows, sl] = (q * egc).astype(BF16)
            kd_ref[rows, sl] = (k * jnp.exp(g_last - gc)).astype(BF16)
            yield

        kk = [_dot_nt(kb16, k16) for kb16, k16 in zip(kb16s, k16s)]
        yield
        qk = [_dot_nt(q16, k16) for q16, k16 in zip(q16s, k16s)]
        yield
        mats = [x * d for x, d in zip(kk, decays)]
        tinv = yield from _unit_lower_inverses(mats, ti, si)
        uw = [_dot(t.astype(BF16), r) for t, r in zip(tinv, rhs)]
        yield
        for i, (h, p) in enumerate(probs):
            sl = slice(h * HEAD_DIM, (h + 1) * HEAD_DIM)
            rows = slice(p * pr, (p + 1) * pr)
            u_ref[rows, sl] = uw[i][:, 0:HEAD_DIM]
            w_ref[rows, sl] = uw[i][:, HEAD_DIM:].astype(BF16)
            qk_ref[p, h] = (qk[i] * decays[i]).astype(BF16)


def _dn_scan_kernel(u_ref, w_ref, qd_ref, kd_ref, qk_ref, ge_ref, z_ref, nw_ref, o_ref, state_ref):
    c = B_CHUNK
    batch = u_ref.shape[0]

    @pl.when(pl.program_id(0) == 0)
    def _():
        state_ref[...] = jnp.zeros_like(state_ref)

    nw = nw_ref[...]
    chains = [(b, h) for b in range(batch) for h in range(HEADS)]
    hsl = lambda h: slice(h * HEAD_DIM, (h + 1) * HEAD_DIM)
    states = [state_ref[b * HEADS + h] for b, h in chains]
    for pi in range(B_SCAN_PAIRS):
        o_inter = [[] for _ in chains]
        v_new = [[] for _ in chains]
        for j in range(B_PAIR // c):
            rows = slice(pi * B_PAIR + j * c, pi * B_PAIR + (j + 1) * c)
            st16 = [st.astype(BF16) for st in states]
            lhs = [jnp.concatenate([w_ref[b, rows, hsl(h)], qd_ref[b, rows, hsl(h)]], axis=0)
                   for b, h in chains]
            prod = [_dot(x, s) for x, s in zip(lhs, st16)]
            vn16 = []
            for i, (b, h) in enumerate(chains):
                vn = (u_ref[b, rows, hsl(h)] - prod[i][0:c, :]).astype(BF16)
                vn16.append(vn)
                v_new[i].append(vn)
                o_inter[i].append(prod[i][c:, :])
            upd = [_dot_tn(kd_ref[b, rows, hsl(h)], vn) for (b, h), vn in zip(chains, vn16)]
            states = [st * ge_ref[b, pi, j:j + 1, HEADS + h:HEADS + h + 1] + x
                      for (b, h), st, x in zip(chains, states, upd)]
        intra = [_dot(qk_ref[b, pi, h], jnp.concatenate(v_new[i], axis=0)) for i, (b, h) in enumerate(chains)]
        prows = slice(pi * B_PAIR, (pi + 1) * B_PAIR)
        for i, (b, h) in enumerate(chains):
            o = jnp.concatenate(o_inter[i], axis=0) + intra[i]
            y = o * lax.rsqrt(jnp.mean(o * o, axis=-1, keepdims=True) + EPS) * nw
            o_ref[b, prows, hsl(h)] = (y * _silu(z_ref[b, prows, hsl(h)])).astype(BF16)
    for i, (b, h) in enumerate(chains):
        state_ref[b * HEADS + h] = states[i]


def _dn_scan(pb, u, w, qd, kd, qk, ge, nw, batch, seq):
    c = B_CHUNK
    pr = B_PAIR
    bw = BRANCH_WIDTH
    t = batch * seq
    const = lambda i: (0, 0)
    np_seq = seq // pr
    seq3 = lambda x: x.reshape(batch, seq, x.shape[-1])
    sp = B_SCAN_PAIRS
    blk3 = pl.BlockSpec((batch, sp * pr, bw), lambda i: (0, i, 0))
    y = pl.pallas_call(
        _dn_scan_kernel,
        grid=(np_seq // sp,),
        in_specs=[blk3, blk3, blk3, blk3,
                  pl.BlockSpec((batch, sp, HEADS, pr, pr), lambda i: (0, i, 0, 0, 0)),
                  pl.BlockSpec((batch, sp, pr // c, SMALL_PAD), lambda i: (0, i, 0, 0)),
                  pl.BlockSpec((batch, sp * pr, bw), lambda i: (0, i, B_CONV_CH // bw)),
                  pl.BlockSpec((1, HEAD_DIM), const)],
        out_specs=blk3,
        out_shape=jax.ShapeDtypeStruct((batch, seq, bw), BF16),
        scratch_shapes=[pltpu.VMEM((batch * HEADS, HEAD_DIM, HEAD_DIM), F32)],
        compiler_params=pltpu.CompilerParams(
            dimension_semantics=("arbitrary",), vmem_limit_bytes=VMEM_LIMIT),
        name="dn_scan",
    )(seq3(u), seq3(w), seq3(qd), seq3(kd),
      qk.reshape(batch, np_seq, HEADS, pr, pr),
      ge.reshape(batch, np_seq, pr // c, SMALL_PAD),
      seq3(pb), nw)
    return y.reshape(t, bw)


def _band_buckets():
    assert WINDOW == C_BLOCK
    r = np.arange(C_BLOCK)[:, None]
    c = np.arange(C_BLOCK)[None, :]
    dist = np.where(c > r, r + C_BLOCK - c, r - c)
    max_exact = N_BUCKETS // 2
    d_f = np.maximum(dist, 1).astype(np.float32)
    large = max_exact + (np.log(d_f / np.float32(max_exact)) / np.float32(math.log(MAX_DISTANCE / max_exact))
                         * np.float32(N_BUCKETS - max_exact)).astype(np.int32)
    large = np.minimum(large, N_BUCKETS - 1)
    return np.where(dist < max_exact, dist, large).astype(np.int32)


def _band_halves():
    r = np.arange(C_BLOCK)[:, None]
    c = np.arange(C_BLOCK)[None, :]
    return np.stack([c > r, c <= r]).astype(np.float32)


def _swa_stages(n, first_step, rb_ref, sink_ref, cur_ref, prev_ref, bucket_ref, half_ref, o_ref, bias_ref):
    blk = C_BLOCK
    bw = BRANCH_WIDTH

    @pl.when(first_step)
    def _():
        bucket = bucket_ref[...]
        for h in range(C_Q_HEADS):
            acc = jnp.zeros((blk, blk), F32)
            for bk in range(N_BUCKETS):
                acc = jnp.where(bucket == bk, rb_ref[bk, h], acc)
            bias_ref[h] = acc

    kv = jnp.concatenate([prev_ref[...], cur_ref[:, bw:bw + 2 * C_KV_WIDTH]], axis=0)
    rows = kv.shape[0]
    lane = lax.broadcasted_iota(jnp.int32, (rows, C_KV_WIDTH), 1)

    def halves(x, j):
        own = jnp.where((lane >= j * C_HEAD_DIM) & (lane < (j + 1) * C_HEAD_DIM), x, 0.0)
        other = pltpu.roll(own, C_HEAD_DIM, 1)
        pair = (own, other) if j == 0 else (other, own)
        return [t.astype(BF16) for t in pair]

    kz = [halves(kv[:, 0:C_KV_WIDTH], j) for j in range(C_KV_HEADS)]
    vz = [halves(kv[:, C_KV_WIDTH:], j) for j in range(C_KV_HEADS)]
    scale = C_HEAD_DIM ** -0.5
    group = C_Q_HEADS // C_KV_HEADS
    from_prev = (lax.broadcasted_iota(jnp.int32, (blk, blk), 1) > lax.broadcasted_iota(jnp.int32, (blk, blk), 0))
    yield

    qrows = lambda i: slice(i * blk, (i + 1) * blk)
    krows = lambda i: slice(i * blk, (i + 2) * blk)
    for first in range(0, C_TILE_BLOCKS, C_LOCKSTEP_BLOCKS):
        blocks = range(first, first + C_LOCKSTEP_BLOCKS)
        chains = [(i, h) for i in blocks for h in range(C_Q_HEADS)]
        q2 = {i: [(cur_ref[qrows(i), p * LANES:(p + 1) * LANES] * scale).astype(BF16)
                  for p in range(C_Q_HEADS // 2)] for i in blocks}
        both = [_dot_nt(q2[i][h // 2], kz[h // group][h % 2][krows(i), :]) for i, h in chains]
        yield
        logits = []
        for (i, h), lg in zip(chains, both):
            prev_part = lg[:, 0:blk]
            if i == 0:
                prev_part = prev_part + jnp.where(n > 0, 0.0, NEG_BIG)
            logits.append(jnp.where(from_prev, prev_part, lg[:, blk:]) + bias_ref[h])
        yield
        mx = [jnp.maximum(jnp.max(lg, axis=-1, keepdims=True), sink_ref[h])
              for (i, h), lg in zip(chains, logits)]
        yield
        pr = [jnp.exp(lg - m) for lg, m in zip(logits, mx)]
        den = [jnp.sum(x, axis=-1, keepdims=True) + jnp.exp(sink_ref[h] - m)
               for (i, h), x, m in zip(chains, pr, mx)]
        yield
        pr16 = [x.astype(BF16) for x in pr]
        pcat = [jnp.concatenate([x * half_ref[0], x * half_ref[1]], axis=1) for x in pr16]
        pv = [_dot(x, vz[h // group][h % 2][krows(i), :]) for (i, h), x in zip(chains, pcat)]
        yield
        for bi, i in enumerate(blocks):
            for p in range(C_Q_HEADS // 2):
                a, b = bi * C_Q_HEADS + 2 * p, bi * C_Q_HEADS + 2 * p + 1
                out = pv[a] / den[a] + pv[b] / den[b]
                gate = cur_ref[qrows(i), bw + 2 * C_KV_WIDTH + p * LANES:bw + 2 * C_KV_WIDTH + (p + 1) * LANES]
                o_ref[qrows(i), p * LANES:(p + 1) * LANES] = (out * _silu(gate)).astype(BF16)
        yield


N_DN_IN = 6
N_DN_OUT = 6
N_SWA_IN = 6


def _mix_bc_kernel(*refs, tiles_per_seq):
    dn_in = refs[0:N_DN_IN]
    swa_in = refs[N_DN_IN:N_DN_IN + N_SWA_IN]
    outs = refs[N_DN_IN + N_SWA_IN:N_DN_IN + N_SWA_IN + N_DN_OUT + 1]
    xp_ref, qkv_ref, bias_ref = refs[N_DN_IN + N_SWA_IN + N_DN_OUT + 1:]
    step = pl.program_id(0)
    seq_tile = step % tiles_per_seq
    jobs = [_dn_prep_stages(seq_tile, *dn_in, *outs[0:N_DN_OUT], xp_ref, qkv_ref),
            _swa_stages(seq_tile, step == 0, *swa_in, outs[N_DN_OUT], bias_ref)]
    for job in jobs:
        for _ in job:
            pass


def _mix_bc(pb, pc, cw, alog, dtb, rel_bias, sinks, batch, seq):
    c = B_CHUNK
    ts = B_TILE
    pr = B_PAIR
    bw = BRANCH_WIDTH
    blk = C_BLOCK
    t = batch * seq
    n_tiles = t // ts
    tril = jnp.asarray(np.kron(np.eye(pr // c, dtype=np.float32), np.tril(np.ones((c, c), np.float32))),
                       dtype=BF16)
    bucket = jnp.asarray(_band_buckets())
    halves = jnp.asarray(_band_halves(), dtype=BF16)
    const = lambda i: (0, 0)
    row = lambda i: (i, 0)
    halo_blocks = ts // SUBLANES
    kv_col = BRANCH_WIDTH // (2 * C_KV_WIDTH)
    smem = pl.BlockSpec(memory_space=pltpu.SMEM)
    return pl.pallas_call(
        functools.partial(_mix_bc_kernel, tiles_per_seq=seq // ts),
        grid=(n_tiles,),
        in_specs=[pl.BlockSpec((ts, PB_WIDTH), row),
                  pl.BlockSpec((SUBLANES, B_CONV_CH), lambda i: (jnp.maximum(i * halo_blocks - 1, 0), 0)),
                  pl.BlockSpec(cw.shape, const),
                  pl.BlockSpec((1, SMALL_PAD), const),
                  pl.BlockSpec((1, SMALL_PAD), const),
                  pl.BlockSpec((pr, pr), const),
                  smem, smem,
                  pl.BlockSpec((ts, PC_WIDTH), row),
                  pl.BlockSpec((blk, 2 * C_KV_WIDTH), lambda i: (jnp.maximum(i * C_TILE_BLOCKS - 1, 0), kv_col)),
                  pl.BlockSpec((blk, blk), const),
                  pl.BlockSpec((2, blk, blk), lambda i: (0, 0, 0))],
        out_specs=[pl.BlockSpec((ts, bw), row),
                   pl.BlockSpec((ts, bw), row),
                   pl.BlockSpec((ts, bw), row),
                   pl.BlockSpec((ts, bw), row),
                   pl.BlockSpec((ts // pr, HEADS, pr, pr), lambda i: (i, 0, 0, 0)),
                   pl.BlockSpec((1, ts // c, SMALL_PAD), lambda i: (i, 0, 0)),
                   pl.BlockSpec((ts, bw), row)],
        out_shape=[jax.ShapeDtypeStruct((t, bw), F32),
                   jax.ShapeDtypeStruct((t, bw), BF16),
                   jax.ShapeDtypeStruct((t, bw), BF16),
                   jax.ShapeDtypeStruct((t, bw), BF16),
                   jax.ShapeDtypeStruct((t // pr, HEADS, pr, pr), BF16),
                   jax.ShapeDtypeStruct((n_tiles, ts // c, SMALL_PAD), F32),
                   jax.ShapeDtypeStruct((t, bw), BF16)],
        scratch_shapes=[pltpu.VMEM((B_CONV_CH // LANES, ts + SUBLANES, LANES), F32),
                        pltpu.VMEM((B_CONV_CH // LANES, ts, LANES), F32),
                        pltpu.VMEM((C_Q_HEADS, blk, blk), F32)],
        compiler_params=pltpu.CompilerParams(
            dimension_semantics=("arbitrary",), vmem_limit_bytes=VMEM_LIMIT),
        name="mix_bc",
    )(pb, pb, cw, alog, dtb, tril, rel_bias, sinks, pc, pc, bucket, halves)


def _lane_pad(v, offset):
    return jnp.zeros((1, SMALL_PAD), F32).at[0, offset:offset + v.shape[0]].set(v.astype(F32))


def kernel(x, norm_w, w_in, conv_w, a_log, dt_bias, lb_param, norm_a, norm_b, sinks, rel_bias,
           w_branch, w_out, final_norm):
    batch, seq, _ = x.shape
    depth = w_in.shape[0]
    x2 = x.reshape(batch * seq, D_MODEL)
    fn = final_norm.reshape(1, D_MODEL)
    w16 = w_in.astype(BF16)
    wbr = w_branch.astype(BF16)
    wo = w_out.astype(BF16)
    for l in range(depth):
        nw = norm_w[l].reshape(1, D_MODEL)
        pa, pb, pc = _inproj(x2, nw, w16, l)
        ya = _hgrn2(pa, lb_param, norm_a[l].reshape(1, HEAD_DIM), batch, seq, l)
        *dn, yc = _mix_bc(pb, pc, conv_w[l], _lane_pad(a_log[l], HEADS), _lane_pad(dt_bias[l], HEADS),
                          rel_bias, sinks[l], batch, seq)
        yb = _dn_scan(pb, *dn, norm_b[l].reshape(1, HEAD_DIM), batch, seq)
        x2 = _merge(x2, nw, w16, ya, yb, yc, wbr, wo, fn, layer=l, final=(l == depth - 1))
    return x2.reshape(batch, seq, D_MODEL)
```

```python
import functools
import math

import numpy as np
import jax
import jax.numpy as jnp
from jax import lax
from jax.experimental import pallas as pl
from jax.experimental.pallas import tpu as pltpu

F32 = jnp.float32
BF16 = jnp.bfloat16

D_MODEL = 1024
BRANCH_WIDTH = D_MODEL // 2
N_BRANCHES = 3
EPS = 1e-6
HEADS = 4
HEAD_DIM = BRANCH_WIDTH // HEADS
B_CONV = 4
B_CONV_CH = 3 * BRANCH_WIDTH
C_Q_HEADS = 8
C_KV_HEADS = 2
C_HEAD_DIM = BRANCH_WIDTH // C_Q_HEADS
C_KV_WIDTH = C_KV_HEADS * C_HEAD_DIM
WINDOW = 128
C_BLOCK = 128
C_LOCKSTEP_BLOCKS = 2
N_BUCKETS = 32
MAX_DISTANCE = 128
LANES = 128
SUBLANES = 8
NEG_BIG = -1e30
LOG2E = math.log2(math.e)

A_CHUNK = 128
A_TILE_CHUNKS = 8
B_CHUNK = 64
B_PAIR = 2 * B_CHUNK
B_TILE = 512
C_TILE_BLOCKS = B_TILE // C_BLOCK
B_LOCKSTEP = 16
B_SCAN_PAIRS = 4
PROJ_ROWS = 512
MERGE_ROWS = 1024
SMALL_PAD = LANES
VMEM_LIMIT = 52 * 1024 * 1024

PA_WIDTH = 4 * BRANCH_WIDTH
PB_WIDTH = B_CONV_CH + BRANCH_WIDTH + SMALL_PAD
PC_WIDTH = 2 * BRANCH_WIDTH + 2 * C_KV_WIDTH
W_AB = PA_WIDTH + PB_WIDTH - (SMALL_PAD - 2 * HEADS)
W_ABC = W_AB + PC_WIDTH


def _dot(a, b):
    return jnp.dot(a, b, preferred_element_type=F32)


def _dot_nt(a, b):
    return lax.dot_general(a, b, (((1,), (1,)), ((), ())), preferred_element_type=F32)


def _dot_tn(a, b):
    return lax.dot_general(a, b, (((0,), (0,)), ((), ())), preferred_element_type=F32)


def _sigmoid(x):
    return 0.5 * jnp.tanh(0.5 * x) + 0.5


def _silu(x):
    h = 0.5 * x
    return h * jnp.tanh(h) + h


def _cumsum_rows(tril16, x):
    hi = x.astype(BF16)
    r1 = x - hi.astype(F32)
    mid = r1.astype(BF16)
    lo = (r1 - mid.astype(F32)).astype(BF16)
    return _dot(tril16, hi) + (_dot(tril16, mid) + _dot(tril16, lo))


def _softplus(x):
    return jnp.maximum(x, 0.0) + jnp.log(1.0 + jnp.exp(-jnp.abs(x)))


def _rms(x, w):
    return x * lax.rsqrt(jnp.mean(x * x, axis=-1, keepdims=True) + EPS) * w


def _resident(shape):
    return pl.BlockSpec(shape, lambda i: (0,) * len(shape), pipeline_mode=pl.Buffered(1))


def _layer_resident(shape, layer):
    return pl.BlockSpec((None,) + tuple(shape[1:]), lambda i: (layer,) + (0,) * (len(shape) - 1),
                        pipeline_mode=pl.Buffered(1))


def _inproj_kernel(x_ref, nw_ref, w_ref, pa_ref, pb_ref, pc_ref, wc_ref):
    @pl.when(pl.program_id(0) == 0)
    def _():
        wc_ref[...] = w_ref[:, W_AB:W_ABC]

    h = _rms(x_ref[...], nw_ref[...]).astype(BF16)
    for o_ref, src_ref, base in ((pa_ref, w_ref, 0), (pb_ref, w_ref, PA_WIDTH), (pc_ref, wc_ref, 0)):
        n = o_ref.shape[1]
        for j in range(0, n, BRANCH_WIDTH):
            wd = min(BRANCH_WIDTH, n - j)
            o_ref[:, j:j + wd] = _dot(h, src_ref[:, base + j:base + j + wd])


def _inproj(x2, nw, w, layer):
    t = x2.shape[0]
    tm = PROJ_ROWS
    return pl.pallas_call(
        _inproj_kernel,
        grid=(t // tm,),
        in_specs=[pl.BlockSpec((tm, D_MODEL), lambda i: (i, 0)),
                  _resident((1, D_MODEL)),
                  _layer_resident(w.shape, layer)],
        out_specs=[pl.BlockSpec((tm, PA_WIDTH), lambda i: (i, 0)),
                   pl.BlockSpec((tm, PB_WIDTH), lambda i: (i, 0)),
                   pl.BlockSpec((tm, PC_WIDTH), lambda i: (i, 0))],
        out_shape=[jax.ShapeDtypeStruct((t, PA_WIDTH), F32),
                   jax.ShapeDtypeStruct((t, PB_WIDTH), F32),
                   jax.ShapeDtypeStruct((t, PC_WIDTH), F32)],
        scratch_shapes=[pltpu.VMEM((D_MODEL, PC_WIDTH), BF16)],
        compiler_params=pltpu.CompilerParams(
            dimension_semantics=("arbitrary",), vmem_limit_bytes=VMEM_LIMIT),
        name="inproj",
    )(x2, nw, w)


def _merge_kernel(x_ref, nw_ref, wg_ref, ya_ref, yb_ref, yc_ref, wbr_ref, wo_ref, fn_ref, o_ref, *, final):
    x = x_ref[...]
    h = _rms(x, nw_ref[...]).astype(BF16)
    merged = None
    for n, y_ref in enumerate((ya_ref, yb_ref, yc_ref)):
        gate = _sigmoid(_dot(h, wg_ref[:, n * D_MODEL:(n + 1) * D_MODEL]))
        term = gate * _dot(y_ref[...], wbr_ref[n])
        merged = term if merged is None else merged + term
    out = x + _dot(merged.astype(BF16), wo_ref[...])
    if final:
        out = _rms(out, fn_ref[...])
    o_ref[...] = out


def _merge(x2, nw, wg, ya, yb, yc, wbr, wo, fn, layer, final):
    t = x2.shape[0]
    tm = MERGE_ROWS
    row = lambda i: (i, 0)
    return pl.pallas_call(
        functools.partial(_merge_kernel, final=final),
        grid=(t // tm,),
        in_specs=[pl.BlockSpec((tm, D_MODEL), row),
                  _resident((1, D_MODEL)),
                  _layer_resident(wg.shape, layer),
                  pl.BlockSpec((tm, BRANCH_WIDTH), row),
                  pl.BlockSpec((tm, BRANCH_WIDTH), row),
                  pl.BlockSpec((tm, BRANCH_WIDTH), row),
                  _layer_resident(wbr.shape, layer),
                  _layer_resident(wo.shape, layer),
                  _resident((1, D_MODEL))],
        out_specs=pl.BlockSpec((tm, D_MODEL), row),
        out_shape=jax.ShapeDtypeStruct((t, D_MODEL), F32),
        compiler_params=pltpu.CompilerParams(
            dimension_semantics=("arbitrary",), vmem_limit_bytes=VMEM_LIMIT),
        name="merge",
    )(x2, nw, wg, ya, yb, yc, wbr, wo, fn)


def _a_levels():
    return [1 << i for i in range(int(math.log2(A_CHUNK)))]


def _a_level_ids():
    idx = np.arange(A_CHUNK)
    ids = np.where(idx[:, None] == idx[None, :], 0, -1).astype(np.int32)
    for i, m in enumerate(_a_levels()):
        blk = idx // (2 * m)
        upper = (idx & m) != 0
        ids[(blk[:, None] == blk[None, :]) & upper[:, None] & ~upper[None, :]] = i + 1
    return ids


def _hgrn2_kernel(pa_ref, lbp_ref, nw_ref, tril_ref, lvl_ref, o_ref,
                  state_ref, b_ref, gp_ref, *, layer):
    c = A_CHUNK
    bw = BRANCH_WIDTH

    @pl.when(pl.program_id(1) == 0)
    def _():
        state_ref[...] = jnp.zeros_like(state_ref)
        gp_ref[...] = jnp.zeros_like(gp_ref)

    lbp = lbp_ref[...]
    ex = jnp.exp(lbp - jnp.max(lbp, axis=0, keepdims=True))
    sm = ex / jnp.sum(ex, axis=0, keepdims=True)
    lb = jnp.zeros((1, bw), F32)
    for j in range(1, layer + 1):
        lb = lb + sm[j:j + 1, :]

    hsl = [slice(h * HEAD_DIM, (h + 1) * HEAD_DIM) for h in range(HEADS)]
    row = lax.broadcasted_iota(jnp.int32, (c, 1), 0)
    lvl = lvl_ref[...]
    t1 = jnp.log(lb)
    t2_lb = jnp.log1p(-lb)

    def intra_chunk(ci):
        rows = slice(ci * c, (ci + 1) * c)
        bc_ref = b_ref.at[ci]
        gc_ref = gp_ref.at[ci]
        q = _silu(pa_ref[rows, 0:bw])
        z = pa_ref[rows, bw:2 * bw]
        v = pa_ref[rows, 2 * bw:3 * bw].astype(BF16)
        log_sig = jnp.minimum(z, 0.0) - jnp.log(1.0 + jnp.exp(-jnp.abs(z)))
        sig_neg = 0.5 - 0.5 * jnp.tanh(0.5 * z)
        t2 = t2_lb + log_sig
        lf = jnp.maximum(t1, t2) + jnp.log(1.0 + jnp.exp(-jnp.abs(t1 - t2)))
        k = (1.0 - lb) * sig_neg

        lf2 = lf * LOG2E
        b = _cumsum_rows(tril_ref[...], lf2)
        bc_ref[...] = b
        gc_ref[SUBLANES:SUBLANES + c, :] = lf2

        intra = []
        for sl in hsl:
            qh = q[:, sl]
            kh = k[:, sl]
            lfh = lf2[:, sl]
            acc = jnp.where(lvl == 0, _dot_nt(qh.astype(BF16), kh.astype(BF16)), 0.0)
            for li, m in enumerate(_a_levels()):
                blocks = range(0, c, 2 * m)
                if m == 1:
                    neg = jnp.where((row & 1) == 1, lfh, 0.0)
                elif m == 2:
                    r4 = row & 3
                    g_next = gc_ref[SUBLANES + 1:SUBLANES + 1 + c, sl]
                    g_prev = gc_ref[SUBLANES - 1:SUBLANES - 1 + c, sl]
                    neg = jnp.where(r4 == 0, g_next,
                                    jnp.where(r4 == 1, 0.0, jnp.where(r4 == 2, lfh, lfh + g_prev)))
                elif m < SUBLANES:
                    neg = jnp.concatenate(
                        [-jnp.abs(bc_ref[lo:lo + 2 * m, sl] - bc_ref[lo + m - 1:lo + m, sl]) for lo in blocks],
                        axis=0)
                else:
                    pieces = []
                    for lo in blocks:
                        anchor = bc_ref[lo + m - 1:lo + m, sl]
                        pieces += [anchor - bc_ref[lo:lo + m, sl], bc_ref[lo + m:lo + 2 * m, sl] - anchor]
                    neg = jnp.concatenate(pieces, axis=0)
                if m < SUBLANES:
                    x = jnp.where((row & m) != 0, qh, kh)
                else:
                    x = jnp.concatenate(
                        [t for lo in blocks for t in (kh[lo:lo + m, :], qh[lo + m:lo + 2 * m, :])], axis=0)
                xt = (x * jnp.exp2(neg)).astype(BF16)
                acc = jnp.where(lvl == li + 1, _dot_nt(xt, xt), acc)
            intra.append(_dot(acc.astype(BF16), v[:, sl]))

        b_end = bc_ref[c - 1:c, :]
        qd = (q * jnp.exp2(b)).astype(BF16)
        kd = (k * jnp.exp2(b_end - b)).astype(BF16)
        upd = [_dot_tn(v[:, hsl[h]], kd[:, hsl[h]]) for h in range(HEADS)]
        return intra, upd, qd, jnp.exp2(b_end)

    parts = [intra_chunk(ci) for ci in range(A_TILE_CHUNKS)]
    nw = nw_ref[...]
    sts = [state_ref[h] for h in range(HEADS)]
    for ci, (intra, upd, qd, s_decay) in enumerate(parts):
        rows = slice(ci * c, (ci + 1) * c)
        inter = [_dot_nt(qd[:, hsl[h]], sts[h].astype(BF16)) for h in range(HEADS)]
        sts = [sts[h] * s_decay[:, hsl[h]] + upd[h] for h in range(HEADS)]
        for h, sl in enumerate(hsl):
            o = intra[h] + inter[h]
            y = o * lax.rsqrt(jnp.mean(o * o, axis=-1, keepdims=True) + EPS) * nw
            gate = pa_ref[rows, 3 * bw + h * HEAD_DIM:3 * bw + (h + 1) * HEAD_DIM]
            o_ref[rows, sl] = (y * _silu(gate)).astype(BF16)
    for h in range(HEADS):
        state_ref[h] = sts[h]


def _hgrn2(pa, lbp, nw, batch, seq, layer):
    c = A_CHUNK
    tc = A_TILE_CHUNKS
    nt = seq // (c * tc)
    tril = jnp.asarray(np.tril(np.ones((c, c), np.float32)), dtype=BF16)
    lvl = jnp.asarray(_a_level_ids())
    const2 = lambda b, i: (0, 0)
    return pl.pallas_call(
        functools.partial(_hgrn2_kernel, layer=layer),
        grid=(batch, nt),
        in_specs=[pl.BlockSpec((tc * c, PA_WIDTH), lambda b, i: (b * nt + i, 0)),
                  pl.BlockSpec(lbp.shape, const2),
                  pl.BlockSpec((1, HEAD_DIM), const2),
                  pl.BlockSpec((c, c), const2),
                  pl.BlockSpec((c, c), const2)],
        out_specs=pl.BlockSpec((tc * c, BRANCH_WIDTH), lambda b, i: (b * nt + i, 0)),
        out_shape=jax.ShapeDtypeStruct((batch * seq, BRANCH_WIDTH), BF16),
        scratch_shapes=[pltpu.VMEM((HEADS, HEAD_DIM, HEAD_DIM), F32),
                        pltpu.VMEM((tc, c, BRANCH_WIDTH), F32),
                        pltpu.VMEM((tc, c + 2 * SUBLANES, BRANCH_WIDTH), F32)],
        compiler_params=pltpu.CompilerParams(
            dimension_semantics=("arbitrary", "arbitrary"), vmem_limit_bytes=VMEM_LIMIT),
        name="hgrn2",
    )(pa, lbp, nw, tril, lvl)


def _b_levels():
    return [1 << i for i in range(int(math.log2(B_CHUNK)))]


def _unit_lower_inverses(mats, ti, si):
    eye = (ti == si).astype(F32)
    ts = [eye] * len(mats)
    for s in _b_levels():
        shift = int(math.log2(2 * s))
        off = ((ti >> shift) == (si >> shift)) & ((ti & s) != 0) & ((si & s) == 0)
        a_off = [jnp.where(off, a, 0.0) for a in mats]
        if s == 1:
            ts = [t - ao for t, ao in zip(ts, a_off)]
        else:
            t16 = [t.astype(BF16) for t in ts]
            inner = [_dot(ao.astype(BF16), t) for ao, t in zip(a_off, t16)]
            yield
            outer = [_dot(t, x.astype(BF16)) for t, x in zip(t16, inner)]
            yield
            ts = [t - x for t, x in zip(ts, outer)]
    return ts


def _dn_prep_stages(seq_tile, pb_ref, halo_ref, cw_ref, alog_ref, dtb_ref, tril_ref,
                    u_ref, w_ref, qd_ref, kd_ref, qk_ref, ge_ref, xp_ref, qkv_ref):
    ts = B_TILE
    c = B_CHUNK
    pr = B_PAIR
    bw = BRANCH_WIDTH
    sl8 = SUBLANES

    keep = jnp.where(seq_tile == 0, 0.0, 1.0)
    half = ts // 2
    for s in range(B_CONV_CH // LANES):
        lanes = slice(s * LANES, (s + 1) * LANES)
        xp_ref[s, 0:sl8, :] = halo_ref[:, lanes] * keep
        xp_ref[s, sl8:sl8 + ts, :] = pb_ref[:, lanes]
    yield
    for s in range(B_CONV_CH // LANES):
        lanes = slice(s * LANES, (s + 1) * LANES)
        cw = [0.5 * cw_ref[j:j + 1, lanes] for j in range(B_CONV)]
        ld = {off: xp_ref[s, pl.ds(sl8 + off, half, stride=2), :] for off in range(1 - B_CONV, 2)}
        even = None
        odd = None
        for j in range(B_CONV):
            te = cw[j] * ld[j - (B_CONV - 1)]
            to = cw[j] * ld[j - (B_CONV - 1) + 1]
            even = te if even is None else even + te
            odd = to if odd is None else odd + to
        qkv_ref[s, pl.ds(0, half, stride=2), :] = even * jnp.tanh(even) + even
        qkv_ref[s, pl.ds(1, half, stride=2), :] = odd * jnp.tanh(odd) + odd
        yield

    small = pb_ref[:, B_CONV_CH + bw:B_CONV_CH + bw + SMALL_PAD]
    beta_all = _sigmoid(small)
    g_all = -jnp.exp(alog_ref[...]) * _softplus(small + dtb_ref[...])
    gcum = jnp.concatenate(
        [_cumsum_rows(tril_ref[...], g_all[p * pr:(p + 1) * pr, :]) for p in range(ts // pr)], axis=0)
    gcum_t = gcum.T
    for j in range(ts // c):
        ge_ref[0, j:j + 1, :] = jnp.exp(gcum[j * c + c - 1:j * c + c, :])

    ti = lax.broadcasted_iota(jnp.int32, (pr, pr), 0)
    si = lax.broadcasted_iota(jnp.int32, (pr, pr), 1)
    same = (ti >> int(math.log2(c))) == (si >> int(math.log2(c)))
    incl = same & (ti >= si)
    first_chunk = lax.broadcasted_iota(jnp.int32, (pr, 1), 0) < c
    yield

    all_probs = [(h, p) for h in range(HEADS) for p in range(ts // pr)]
    for first in range(0, len(all_probs), B_LOCKSTEP):
        probs = all_probs[first:first + B_LOCKSTEP]
        k16s, kb16s, q16s, decays, rhs = [], [], [], [], []
        for h, p in probs:
            sl = slice(h * HEAD_DIM, (h + 1) * HEAD_DIM)
            rows = slice(p * pr, (p + 1) * pr)
            q = qkv_ref[h, rows, :]
            k = qkv_ref[HEADS + h, rows, :]
            v = qkv_ref[2 * HEADS + h, rows, :]
            q = q * (lax.rsqrt(jnp.sum(q * q, axis=-1, keepdims=True) + EPS) * (HEAD_DIM ** -0.5))
            k = k * lax.rsqrt(jnp.sum(k * k, axis=-1, keepdims=True) + EPS)
            beta = beta_all[rows, h:h + 1]
            gc = gcum[rows, HEADS + h:HEADS + h + 1]
            gr = gcum_t[HEADS + h:HEADS + h + 1, rows]
            g_last = jnp.where(first_chunk, gc[c - 1:c, :], gc[pr - 1:pr, :])
            egc = jnp.exp(gc)
            kb = k * beta
            decays.append(jnp.where(incl, jnp.exp(jnp.minimum(gc - gr, 0.0)), 0.0))
            k16s.append(k.astype(BF16))
            kb16s.append(kb.astype(BF16))
            q16s.append(q.astype(BF16))
            rhs.append(jnp.concatenate([v * beta, kb * egc], axis=1).astype(BF16))
            qd_ref[rows, sl] = (q * egc).astype(BF16)
            kd_ref[rows, sl] = (k * jnp.exp(g_last - gc)).astype(BF16)
            yield

        kk = [_dot_nt(kb16, k16) for kb16, k16 in zip(kb16s, k16s)]
        yield
        qk = [_dot_nt(q16, k16) for q16, k16 in zip(q16s, k16s)]
        yield
        mats = [x * d for x, d in zip(kk, decays)]
        tinv = yield from _unit_lower_inverses(mats, ti, si)
        uw = [_dot(t.astype(BF16), r) for t, r in zip(tinv, rhs)]
        yield
        for i, (h, p) in enumerate(probs):
            sl = slice(h * HEAD_DIM, (h + 1) * HEAD_DIM)
            rows = slice(p * pr, (p + 1) * pr)
            u_ref[rows, sl] = uw[i][:, 0:HEAD_DIM]
            w_ref[rows, sl] = uw[i][:, HEAD_DIM:].astype(BF16)
            qk_ref[p, h] = (qk[i] * decays[i]).astype(BF16)


def _dn_scan_kernel(u_ref, w_ref, qd_ref, kd_ref, qk_ref, ge_ref, z_ref, nw_ref, o_ref, state_ref):
    c = B_CHUNK
    batch = u_ref.shape[0]

    @pl.when(pl.program_id(0) == 0)
    def _():
        state_ref[...] = jnp.zeros_like(state_ref)

    nw = nw_ref[...]
    chains = [(b, h) for b in range(batch) for h in range(HEADS)]
    hsl = lambda h: slice(h * HEAD_DIM, (h + 1) * HEAD_DIM)
    states = [state_ref[b * HEADS + h] for b, h in chains]
    for pi in range(B_SCAN_PAIRS):
        o_inter = [[] for _ in chains]
        v_new = [[] for _ in chains]
        for j in range(B_PAIR // c):
            rows = slice(pi * B_PAIR + j * c, pi * B_PAIR + (j + 1) * c)
            st16 = [st.astype(BF16) for st in states]
            lhs = [jnp.concatenate([w_ref[b, rows, hsl(h)], qd_ref[b, rows, hsl(h)]], axis=0)
                   for b, h in chains]
            prod = [_dot(x, s) for x, s in zip(lhs, st16)]
            vn16 = []
            for i, (b, h) in enumerate(chains):
                vn = (u_ref[b, rows, hsl(h)] - prod[i][0:c, :]).astype(BF16)
                vn16.append(vn)
                v_new[i].append(vn)
                o_inter[i].append(prod[i][c:, :])
            upd = [_dot_tn(kd_ref[b, rows, hsl(h)], vn) for (b, h), vn in zip(chains, vn16)]
            states = [st * ge_ref[b, pi, j:j + 1, HEADS + h:HEADS + h + 1] + x
                      for (b, h), st, x in zip(chains, states, upd)]
        intra = [_dot(qk_ref[b, pi, h], jnp.concatenate(v_new[i], axis=0)) for i, (b, h) in enumerate(chains)]
        prows = slice(pi * B_PAIR, (pi + 1) * B_PAIR)
        for i, (b, h) in enumerate(chains):
            o = jnp.concatenate(o_inter[i], axis=0) + intra[i]
            y = o * lax.rsqrt(jnp.mean(o * o, axis=-1, keepdims=True) + EPS) * nw
            o_ref[b, prows, hsl(h)] = (y * _silu(z_ref[b, prows, hsl(h)])).astype(BF16)
    for i, (b, h) in enumerate(chains):
        state_ref[b * HEADS + h] = states[i]


def _dn_scan(pb, u, w, qd, kd, qk, ge, nw, batch, seq):
    c = B_CHUNK
    pr = B_PAIR
    bw = BRANCH_WIDTH
    t = batch * seq
    const = lambda i: (0, 0)
    np_seq = seq // pr
    seq3 = lambda x: x.reshape(batch, seq, x.shape[-1])
    sp = B_SCAN_PAIRS
    blk3 = pl.BlockSpec((batch, sp * pr, bw), lambda i: (0, i, 0))
    y = pl.pallas_call(
        _dn_scan_kernel,
        grid=(np_seq // sp,),
        in_specs=[blk3, blk3, blk3, blk3,
                  pl.BlockSpec((batch, sp, HEADS, pr, pr), lambda i: (0, i, 0, 0, 0)),
                  pl.BlockSpec((batch, sp, pr // c, SMALL_PAD), lambda i: (0, i, 0, 0)),
                  pl.BlockSpec((batch, sp * pr, bw), lambda i: (0, i, B_CONV_CH // bw)),
                  pl.BlockSpec((1, HEAD_DIM), const)],
        out_specs=blk3,
        out_shape=jax.ShapeDtypeStruct((batch, seq, bw), BF16),
        scratch_shapes=[pltpu.VMEM((batch * HEADS, HEAD_DIM, HEAD_DIM), F32)],
        compiler_params=pltpu.CompilerParams(
            dimension_semantics=("arbitrary",), vmem_limit_bytes=VMEM_LIMIT),
        name="dn_scan",
    )(seq3(u), seq3(w), seq3(qd), seq3(kd),
      qk.reshape(batch, np_seq, HEADS, pr, pr),
      ge.reshape(batch, np_seq, pr // c, SMALL_PAD),
      seq3(pb), nw)
    return y.reshape(t, bw)


def _band_buckets():
    assert WINDOW == C_BLOCK
    r = np.arange(C_BLOCK)[:, None]
    c = np.arange(C_BLOCK)[None, :]
    dist = np.where(c > r, r + C_BLOCK - c, r - c)
    max_exact = N_BUCKETS // 2
    d_f = np.maximum(dist, 1).astype(np.float32)
    large = max_exact + (np.log(d_f / np.float32(max_exact)) / np.float32(math.log(MAX_DISTANCE / max_exact))
                         * np.float32(N_BUCKETS - max_exact)).astype(np.int32)
    large = np.minimum(large, N_BUCKETS - 1)
    return np.where(dist < max_exact, dist, large).astype(np.int32)


def _band_halves():
    r = np.arange(C_BLOCK)[:, None]
    c = np.arange(C_BLOCK)[None, :]
    return np.stack([c > r, c <= r]).astype(np.float32)


def _swa_stages(n, first_step, rb_ref, sink_ref, cur_ref, prev_ref, bucket_ref, half_ref, o_ref, bias_ref):
    blk = C_BLOCK
    bw = BRANCH_WIDTH

    @pl.when(first_step)
    def _():
        bucket = bucket_ref[...]
        for h in range(C_Q_HEADS):
            acc = jnp.zeros((blk, blk), F32)
            for bk in range(N_BUCKETS):
                acc = jnp.where(bucket == bk, rb_ref[bk, h], acc)
            bias_ref[h] = acc

    kv = jnp.concatenate([prev_ref[...], cur_ref[:, bw:bw + 2 * C_KV_WIDTH]], axis=0)
    rows = kv.shape[0]
    lane = lax.broadcasted_iota(jnp.int32, (rows, C_KV_WIDTH), 1)

    def halves(x, j):
        own = jnp.where((lane >= j * C_HEAD_DIM) & (lane < (j + 1) * C_HEAD_DIM), x, 0.0)
        other = pltpu.roll(own, C_HEAD_DIM, 1)
        pair = (own, other) if j == 0 else (other, own)
        return [t.astype(BF16) for t in pair]

    kz = [halves(kv[:, 0:C_KV_WIDTH], j) for j in range(C_KV_HEADS)]
    vz = [halves(kv[:, C_KV_WIDTH:], j) for j in range(C_KV_HEADS)]
    scale = C_HEAD_DIM ** -0.5
    group = C_Q_HEADS // C_KV_HEADS
    from_prev = (lax.broadcasted_iota(jnp.int32, (blk, blk), 1) > lax.broadcasted_iota(jnp.int32, (blk, blk), 0))
    yield

    qrows = lambda i: slice(i * blk, (i + 1) * blk)
    krows = lambda i: slice(i * blk, (i + 2) * blk)
    for first in range(0, C_TILE_BLOCKS, C_LOCKSTEP_BLOCKS):
        blocks = range(first, first + C_LOCKSTEP_BLOCKS)
        chains = [(i, h) for i in blocks for h in range(C_Q_HEADS)]
        q2 = {i: [(cur_ref[qrows(i), p * LANES:(p + 1) * LANES] * scale).astype(BF16)
                  for p in range(C_Q_HEADS // 2)] for i in blocks}
        both = [_dot_nt(q2[i][h // 2], kz[h // group][h % 2][krows(i), :]) for i, h in chains]
        yield
        logits = []
        for (i, h), lg in zip(chains, both):
            prev_part = lg[:, 0:blk]
            if i == 0:
                prev_part = prev_part + jnp.where(n > 0, 0.0, NEG_BIG)
            logits.append(jnp.where(from_prev, prev_part, lg[:, blk:]) + bias_ref[h])
        yield
        mx = [jnp.maximum(jnp.max(lg, axis=-1, keepdims=True), sink_ref[h])
              for (i, h), lg in zip(chains, logits)]
        yield
        pr = [jnp.exp(lg - m) for lg, m in zip(logits, mx)]
        den = [jnp.sum(x, axis=-1, keepdims=True) + jnp.exp(sink_ref[h] - m)
               for (i, h), x, m in zip(chains, pr, mx)]
        yield
        pr16 = [x.astype(BF16) for x in pr]
        pcat = [jnp.concatenate([x * half_ref[0], x * half_ref[1]], axis=1) for x in pr16]
        pv = [_dot(x, vz[h // group][h % 2][krows(i), :]) for (i, h), x in zip(chains, pcat)]
        yield
        for bi, i in enumerate(blocks):
            for p in range(C_Q_HEADS // 2):
                a, b = bi * C_Q_HEADS + 2 * p, bi * C_Q_HEADS + 2 * p + 1
                out = pv[a] / den[a] + pv[b] / den[b]
                gate = cur_ref[qrows(i), bw + 2 * C_KV_WIDTH + p * LANES:bw + 2 * C_KV_WIDTH + (p + 1) * LANES]
                o_ref[qrows(i), p * LANES:(p + 1) * LANES] = (out * _silu(gate)).astype(BF16)
        yield


N_DN_IN = 6
N_DN_OUT = 6
N_SWA_IN = 6


def _mix_bc_kernel(*refs, tiles_per_seq):
    dn_in = refs[0:N_DN_IN]
    swa_in = refs[N_DN_IN:N_DN_IN + N_SWA_IN]
    outs = refs[N_DN_IN + N_SWA_IN:N_DN_IN + N_SWA_IN + N_DN_OUT + 1]
    xp_ref, qkv_ref, bias_ref = refs[N_DN_IN + N_SWA_IN + N_DN_OUT + 1:]
    step = pl.program_id(0)
    seq_tile = step % tiles_per_seq
    jobs = [_dn_prep_stages(seq_tile, *dn_in, *outs[0:N_DN_OUT], xp_ref, qkv_ref),
            _swa_stages(seq_tile, step == 0, *swa_in, outs[N_DN_OUT], bias_ref)]
    while jobs:
        for job in list(jobs):
            try:
                next(job)
            except StopIteration:
                jobs.remove(job)


def _mix_bc(pb, pc, cw, alog, dtb, rel_bias, sinks, batch, seq):
    c = B_CHUNK
    ts = B_TILE
    pr = B_PAIR
    bw = BRANCH_WIDTH
    blk = C_BLOCK
    t = batch * seq
    n_tiles = t // ts
    tril = jnp.asarray(np.kron(np.eye(pr // c, dtype=np.float32), np.tril(np.ones((c, c), np.float32))),
                       dtype=BF16)
    bucket = jnp.asarray(_band_buckets())
    halves = jnp.asarray(_band_halves(), dtype=BF16)
    const = lambda i: (0, 0)
    row = lambda i: (i, 0)
    halo_blocks = ts // SUBLANES
    kv_col = BRANCH_WIDTH // (2 * C_KV_WIDTH)
    smem = pl.BlockSpec(memory_space=pltpu.SMEM)
    return pl.pallas_call(
        functools.partial(_mix_bc_kernel, tiles_per_seq=seq // ts),
        grid=(n_tiles,),
        in_specs=[pl.BlockSpec((ts, PB_WIDTH), row),
                  pl.BlockSpec((SUBLANES, B_CONV_CH), lambda i: (jnp.maximum(i * halo_blocks - 1, 0), 0)),
                  pl.BlockSpec(cw.shape, const),
                  pl.BlockSpec((1, SMALL_PAD), const),
                  pl.BlockSpec((1, SMALL_PAD), const),
                  pl.BlockSpec((pr, pr), const),
                  smem, smem,
                  pl.BlockSpec((ts, PC_WIDTH), row),
                  pl.BlockSpec((blk, 2 * C_KV_WIDTH), lambda i: (jnp.maximum(i * C_TILE_BLOCKS - 1, 0), kv_col)),
                  pl.BlockSpec((blk, blk), const),
                  pl.BlockSpec((2, blk, blk), lambda i: (0, 0, 0))],
        out_specs=[pl.BlockSpec((ts, bw), row),
                   pl.BlockSpec((ts, bw), row),
                   pl.BlockSpec((ts, bw), row),
                   pl.BlockSpec((ts, bw), row),
                   pl.BlockSpec((ts // pr, HEADS, pr, pr), lambda i: (i, 0, 0, 0)),
                   pl.BlockSpec((1, ts // c, SMALL_PAD), lambda i: (i, 0, 0)),
                   pl.BlockSpec((ts, bw), row)],
        out_shape=[jax.ShapeDtypeStruct((t, bw), F32),
                   jax.ShapeDtypeStruct((t, bw), BF16),
                   jax.ShapeDtypeStruct((t, bw), BF16),
                   jax.ShapeDtypeStruct((t, bw), BF16),
                   jax.ShapeDtypeStruct((t // pr, HEADS, pr, pr), BF16),
                   jax.ShapeDtypeStruct((n_tiles, ts // c, SMALL_PAD), F32),
                   jax.ShapeDtypeStruct((t, bw), BF16)],
        scratch_shapes=[pltpu.VMEM((B_CONV_CH // LANES, ts + SUBLANES, LANES), F32),
                        pltpu.VMEM((B_CONV_CH // LANES, ts, LANES), F32),
                        pltpu.VMEM((C_Q_HEADS, blk, blk), F32)],
        compiler_params=pltpu.CompilerParams(
            dimension_semantics=("arbitrary",), vmem_limit_bytes=VMEM_LIMIT),
        name="mix_bc",
    )(pb, pb, cw, alog, dtb, tril, rel_bias, sinks, pc, pc, bucket, halves)


def _lane_pad(v, offset):
    return jnp.zeros((1, SMALL_PAD), F32).at[0, offset:offset + v.shape[0]].set(v.astype(F32))


def kernel(x, norm_w, w_in, conv_w, a_log, dt_bias, lb_param, norm_a, norm_b, sinks, rel_bias,
           w_branch, w_out, final_norm):
    batch, seq, _ = x.shape
    depth = w_in.shape[0]
    x2 = x.reshape(batch * seq, D_MODEL)
    fn = final_norm.reshape(1, D_MODEL)
    w16 = w_in[:, :, :W_ABC].astype(BF16)
    wg16 = w_in[:, :, W_ABC:].astype(BF16)
    wbr = w_branch.astype(BF16)
    wo = w_out.astype(BF16)
    for l in range(depth):
        nw = norm_w[l].reshape(1, D_MODEL)
        pa, pb, pc = _inproj(x2, nw, w16, l)
        ya = _hgrn2(pa, lb_param, norm_a[l].reshape(1, HEAD_DIM), batch, seq, l)
        *dn, yc = _mix_bc(pb, pc, conv_w[l], _lane_pad(a_log[l], HEADS), _lane_pad(dt_bias[l], HEADS),
                          rel_bias, sinks[l], batch, seq)
        yb = _dn_scan(pb, *dn, norm_b[l].reshape(1, HEAD_DIM), batch, seq)
        x2 = _merge(x2, nw, wg16, ya, yb, yc, wbr, wo, fn, layer=l, final=(l == depth - 1))
    return x2.reshape(batch, seq, D_MODEL)
```

```python
import functools
import math

import numpy as np
import jax
import jax.numpy as jnp
from jax import lax
from jax.experimental import pallas as pl
from jax.experimental.pallas import tpu as pltpu

F32 = jnp.float32
BF16 = jnp.bfloat16

D_MODEL = 1024
BRANCH_WIDTH = D_MODEL // 2
N_BRANCHES = 3
EPS = 1e-6
HEADS = 4
HEAD_DIM = BRANCH_WIDTH // HEADS
B_CONV = 4
B_CONV_CH = 3 * BRANCH_WIDTH
C_Q_HEADS = 8
C_KV_HEADS = 2
C_HEAD_DIM = BRANCH_WIDTH // C_Q_HEADS
C_KV_WIDTH = C_KV_HEADS * C_HEAD_DIM
WINDOW = 128
C_BLOCK = 128
C_LOCKSTEP_BLOCKS = 2
N_BUCKETS = 32
MAX_DISTANCE = 128
LANES = 128
SUBLANES = 8
NEG_BIG = -1e30
LOG2E = math.log2(math.e)

A_CHUNK = 128
A_TILE_CHUNKS = 8
B_CHUNK = 64
B_PAIR = 2 * B_CHUNK
B_TILE = 512
C_TILE_BLOCKS = B_TILE // C_BLOCK
B_LOCKSTEP = 16
B_SCAN_PAIRS = 4
PROJ_ROWS = 512
SMALL_PAD = LANES
VMEM_LIMIT = 52 * 1024 * 1024

PA_WIDTH = 4 * BRANCH_WIDTH
PB_WIDTH = B_CONV_CH + BRANCH_WIDTH + SMALL_PAD
PC_WIDTH = 2 * BRANCH_WIDTH + 2 * C_KV_WIDTH
W_AB = PA_WIDTH + PB_WIDTH - (SMALL_PAD - 2 * HEADS)
W_ABC = W_AB + PC_WIDTH


def _dot(a, b):
    return jnp.dot(a, b, preferred_element_type=F32)


def _dot_nt(a, b):
    return lax.dot_general(a, b, (((1,), (1,)), ((), ())), preferred_element_type=F32)


def _dot_tn(a, b):
    return lax.dot_general(a, b, (((0,), (0,)), ((), ())), preferred_element_type=F32)


def _sigmoid(x):
    return 0.5 * jnp.tanh(0.5 * x) + 0.5


def _silu(x):
    h = 0.5 * x
    return h * jnp.tanh(h) + h


def _cumsum_rows(tril16, x):
    hi = x.astype(BF16)
    r1 = x - hi.astype(F32)
    mid = r1.astype(BF16)
    lo = (r1 - mid.astype(F32)).astype(BF16)
    return _dot(tril16, hi) + (_dot(tril16, mid) + _dot(tril16, lo))


def _softplus(x):
    return jnp.maximum(x, 0.0) + jnp.log(1.0 + jnp.exp(-jnp.abs(x)))


def _rms(x, w):
    return x * lax.rsqrt(jnp.mean(x * x, axis=-1, keepdims=True) + EPS) * w


def _resident(shape):
    return pl.BlockSpec(shape, lambda i: (0,) * len(shape), pipeline_mode=pl.Buffered(1))


def _layer_resident(shape, layer):
    return pl.BlockSpec((None,) + tuple(shape[1:]), lambda i: (layer,) + (0,) * (len(shape) - 1),
                        pipeline_mode=pl.Buffered(1))


def _inproj_kernel(x_ref, nw_ref, w_ref, pa_ref, pb_ref, pc_ref, wc_ref):
    @pl.when(pl.program_id(0) == 0)
    def _():
        wc_ref[...] = w_ref[:, W_AB:W_ABC]

    h = _rms(x_ref[...], nw_ref[...]).astype(BF16)
    for o_ref, src_ref, base in ((pa_ref, w_ref, 0), (pb_ref, w_ref, PA_WIDTH), (pc_ref, wc_ref, 0)):
        n = o_ref.shape[1]
        for j in range(0, n, BRANCH_WIDTH):
            wd = min(BRANCH_WIDTH, n - j)
            o_ref[:, j:j + wd] = _dot(h, src_ref[:, base + j:base + j + wd])


def _inproj(x2, nw, w, layer):
    t = x2.shape[0]
    tm = PROJ_ROWS
    return pl.pallas_call(
        _inproj_kernel,
        grid=(t // tm,),
        in_specs=[pl.BlockSpec((tm, D_MODEL), lambda i: (i, 0)),
                  _resident((1, D_MODEL)),
                  _layer_resident(w.shape, layer)],
        out_specs=[pl.BlockSpec((tm, PA_WIDTH), lambda i: (i, 0)),
                   pl.BlockSpec((tm, PB_WIDTH), lambda i: (i, 0)),
                   pl.BlockSpec((tm, PC_WIDTH), lambda i: (i, 0))],
        out_shape=[jax.ShapeDtypeStruct((t, PA_WIDTH), F32),
                   jax.ShapeDtypeStruct((t, PB_WIDTH), F32),
                   jax.ShapeDtypeStruct((t, PC_WIDTH), F32)],
        scratch_shapes=[pltpu.VMEM((D_MODEL, PC_WIDTH), BF16)],
        compiler_params=pltpu.CompilerParams(
            dimension_semantics=("arbitrary",), vmem_limit_bytes=VMEM_LIMIT),
        name="inproj",
    )(x2, nw, w)


def _merge_kernel(x_ref, nw_ref, w_ref, ya_ref, yb_ref, yc_ref, wbr_ref, wo_ref, fn_ref,
                  o_ref, wg_ref, *, final):
    @pl.when(pl.program_id(0) == 0)
    def _():
        wg_ref[...] = w_ref[:, W_ABC:]

    x = x_ref[...]
    h = _rms(x, nw_ref[...]).astype(BF16)
    merged = None
    for n, y_ref in enumerate((ya_ref, yb_ref, yc_ref)):
        gate = _sigmoid(_dot(h, wg_ref[:, n * D_MODEL:(n + 1) * D_MODEL]))
        term = gate * _dot(y_ref[...], wbr_ref[n].astype(BF16))
        merged = term if merged is None else merged + term
    out = x + _dot(merged.astype(BF16), wo_ref[...].astype(BF16))
    if final:
        out = _rms(out, fn_ref[...])
    o_ref[...] = out


def _merge(x2, nw, w, ya, yb, yc, wbr, wo, fn, layer, final):
    t = x2.shape[0]
    tm = PROJ_ROWS
    row = lambda i: (i, 0)
    return pl.pallas_call(
        functools.partial(_merge_kernel, final=final),
        grid=(t // tm,),
        in_specs=[pl.BlockSpec((tm, D_MODEL), row),
                  _resident((1, D_MODEL)),
                  _layer_resident(w.shape, layer),
                  pl.BlockSpec((tm, BRANCH_WIDTH), row),
                  pl.BlockSpec((tm, BRANCH_WIDTH), row),
                  pl.BlockSpec((tm, BRANCH_WIDTH), row),
                  _layer_resident(wbr.shape, layer),
                  _layer_resident(wo.shape, layer),
                  _resident((1, D_MODEL))],
        out_specs=pl.BlockSpec((tm, D_MODEL), row),
        out_shape=jax.ShapeDtypeStruct((t, D_MODEL), F32),
        scratch_shapes=[pltpu.VMEM((D_MODEL, N_BRANCHES * D_MODEL), BF16)],
        compiler_params=pltpu.CompilerParams(
            dimension_semantics=("arbitrary",), vmem_limit_bytes=VMEM_LIMIT),
        name="merge",
    )(x2, nw, w, ya, yb, yc, wbr, wo, fn)


def _a_levels():
    return [1 << i for i in range(int(math.log2(A_CHUNK)))]


def _a_level_ids():
    idx = np.arange(A_CHUNK)
    ids = np.where(idx[:, None] == idx[None, :], 0, -1).astype(np.int32)
    for i, m in enumerate(_a_levels()):
        blk = idx // (2 * m)
        upper = (idx & m) != 0
        ids[(blk[:, None] == blk[None, :]) & upper[:, None] & ~upper[None, :]] = i + 1
    return ids


def _hgrn2_kernel(pa_ref, lbp_ref, nw_ref, tril_ref, lvl_ref, o_ref,
                  state_ref, b_ref, gp_ref, *, layer):
    c = A_CHUNK
    bw = BRANCH_WIDTH

    @pl.when(pl.program_id(1) == 0)
    def _():
        state_ref[...] = jnp.zeros_like(state_ref)
        gp_ref[...] = jnp.zeros_like(gp_ref)

    lbp = lbp_ref[...]
    ex = jnp.exp(lbp - jnp.max(lbp, axis=0, keepdims=True))
    sm = ex / jnp.sum(ex, axis=0, keepdims=True)
    lb = jnp.zeros((1, bw), F32)
    for j in range(1, layer + 1):
        lb = lb + sm[j:j + 1, :]

    hsl = [slice(h * HEAD_DIM, (h + 1) * HEAD_DIM) for h in range(HEADS)]
    row = lax.broadcasted_iota(jnp.int32, (c, 1), 0)
    lvl = lvl_ref[...]
    t1 = jnp.log(lb)
    t2_lb = jnp.log1p(-lb)

    def intra_chunk(ci):
        rows = slice(ci * c, (ci + 1) * c)
        bc_ref = b_ref.at[ci]
        gc_ref = gp_ref.at[ci]
        q = _silu(pa_ref[rows, 0:bw])
        z = pa_ref[rows, bw:2 * bw]
        v = pa_ref[rows, 2 * bw:3 * bw].astype(BF16)
        log_sig = jnp.minimum(z, 0.0) - jnp.log(1.0 + jnp.exp(-jnp.abs(z)))
        sig_neg = 0.5 - 0.5 * jnp.tanh(0.5 * z)
        t2 = t2_lb + log_sig
        lf = jnp.maximum(t1, t2) + jnp.log(1.0 + jnp.exp(-jnp.abs(t1 - t2)))
        k = (1.0 - lb) * sig_neg

        lf2 = lf * LOG2E
        b = _cumsum_rows(tril_ref[...], lf2)
        bc_ref[...] = b
        gc_ref[SUBLANES:SUBLANES + c, :] = lf2

        intra = []
        for sl in hsl:
            qh = q[:, sl]
            kh = k[:, sl]
            lfh = lf2[:, sl]
            acc = jnp.where(lvl == 0, _dot_nt(qh.astype(BF16), kh.astype(BF16)), 0.0)
            for li, m in enumerate(_a_levels()):
                blocks = range(0, c, 2 * m)
                if m == 1:
                    neg = jnp.where((row & 1) == 1, lfh, 0.0)
                elif m == 2:
                    r4 = row & 3
                    g_next = gc_ref[SUBLANES + 1:SUBLANES + 1 + c, sl]
                    g_prev = gc_ref[SUBLANES - 1:SUBLANES - 1 + c, sl]
                    neg = jnp.where(r4 == 0, g_next,
                                    jnp.where(r4 == 1, 0.0, jnp.where(r4 == 2, lfh, lfh + g_prev)))
                elif m < SUBLANES:
                    neg = jnp.concatenate(
                        [-jnp.abs(bc_ref[lo:lo + 2 * m, sl] - bc_ref[lo + m - 1:lo + m, sl]) for lo in blocks],
                        axis=0)
                else:
                    pieces = []
                    for lo in blocks:
                        anchor = bc_ref[lo + m - 1:lo + m, sl]
                        pieces += [anchor - bc_ref[lo:lo + m, sl], bc_ref[lo + m:lo + 2 * m, sl] - anchor]
                    neg = jnp.concatenate(pieces, axis=0)
                if m < SUBLANES:
                    x = jnp.where((row & m) != 0, qh, kh)
                else:
                    x = jnp.concatenate(
                        [t for lo in blocks for t in (kh[lo:lo + m, :], qh[lo + m:lo + 2 * m, :])], axis=0)
                xt = (x * jnp.exp2(neg)).astype(BF16)
                acc = jnp.where(lvl == li + 1, _dot_nt(xt, xt), acc)
            intra.append(_dot(acc.astype(BF16), v[:, sl]))

        b_end = bc_ref[c - 1:c, :]
        qd = (q * jnp.exp2(b)).astype(BF16)
        kd = (k * jnp.exp2(b_end - b)).astype(BF16)
        upd = [_dot_tn(v[:, hsl[h]], kd[:, hsl[h]]) for h in range(HEADS)]
        return intra, upd, qd, jnp.exp2(b_end)

    parts = [intra_chunk(ci) for ci in range(A_TILE_CHUNKS)]
    nw = nw_ref[...]
    sts = [state_ref[h] for h in range(HEADS)]
    for ci, (intra, upd, qd, s_decay) in enumerate(parts):
        rows = slice(ci * c, (ci + 1) * c)
        inter = [_dot_nt(qd[:, hsl[h]], sts[h].astype(BF16)) for h in range(HEADS)]
        sts = [sts[h] * s_decay[:, hsl[h]] + upd[h] for h in range(HEADS)]
        for h, sl in enumerate(hsl):
            o = intra[h] + inter[h]
            y = o * lax.rsqrt(jnp.mean(o * o, axis=-1, keepdims=True) + EPS) * nw
            gate = pa_ref[rows, 3 * bw + h * HEAD_DIM:3 * bw + (h + 1) * HEAD_DIM]
            o_ref[rows, sl] = (y * _silu(gate)).astype(BF16)
    for h in range(HEADS):
        state_ref[h] = sts[h]


def _hgrn2(pa, lbp, nw, batch, seq, layer):
    c = A_CHUNK
    tc = A_TILE_CHUNKS
    nt = seq // (c * tc)
    tril = jnp.asarray(np.tril(np.ones((c, c), np.float32)), dtype=BF16)
    lvl = jnp.asarray(_a_level_ids())
    const2 = lambda b, i: (0, 0)
    return pl.pallas_call(
        functools.partial(_hgrn2_kernel, layer=layer),
        grid=(batch, nt),
        in_specs=[pl.BlockSpec((tc * c, PA_WIDTH), lambda b, i: (b * nt + i, 0)),
                  pl.BlockSpec(lbp.shape, const2),
                  pl.BlockSpec((1, HEAD_DIM), const2),
                  pl.BlockSpec((c, c), const2),
                  pl.BlockSpec((c, c), const2)],
        out_specs=pl.BlockSpec((tc * c, BRANCH_WIDTH), lambda b, i: (b * nt + i, 0)),
        out_shape=jax.ShapeDtypeStruct((batch * seq, BRANCH_WIDTH), BF16),
        scratch_shapes=[pltpu.VMEM((HEADS, HEAD_DIM, HEAD_DIM), F32),
                        pltpu.VMEM((tc, c, BRANCH_WIDTH), F32),
                        pltpu.VMEM((tc, c + 2 * SUBLANES, BRANCH_WIDTH), F32)],
        compiler_params=pltpu.CompilerParams(
            dimension_semantics=("arbitrary", "arbitrary"), vmem_limit_bytes=VMEM_LIMIT),
        name="hgrn2",
    )(pa, lbp, nw, tril, lvl)


def _b_levels():
    return [1 << i for i in range(int(math.log2(B_CHUNK)))]


def _unit_lower_inverses(mats, ti, si):
    eye = (ti == si).astype(F32)
    ts = [eye] * len(mats)
    for s in _b_levels():
        shift = int(math.log2(2 * s))
        off = ((ti >> shift) == (si >> shift)) & ((ti & s) != 0) & ((si & s) == 0)
        a_off = [jnp.where(off, a, 0.0) for a in mats]
        if s == 1:
            ts = [t - ao for t, ao in zip(ts, a_off)]
        else:
            t16 = [t.astype(BF16) for t in ts]
            inner = [_dot(ao.astype(BF16), t) for ao, t in zip(a_off, t16)]
            yield
            outer = [_dot(t, x.astype(BF16)) for t, x in zip(t16, inner)]
            yield
            ts = [t - x for t, x in zip(ts, outer)]
    return ts


def _dn_prep_stages(seq_tile, pb_ref, halo_ref, cw_ref, alog_ref, dtb_ref, tril_ref,
                    u_ref, w_ref, qd_ref, kd_ref, qk_ref, ge_ref, xp_ref, qkv_ref):
    ts = B_TILE
    c = B_CHUNK
    pr = B_PAIR
    bw = BRANCH_WIDTH
    sl8 = SUBLANES

    keep = jnp.where(seq_tile == 0, 0.0, 1.0)
    half = ts // 2
    for s in range(B_CONV_CH // LANES):
        lanes = slice(s * LANES, (s + 1) * LANES)
        xp_ref[s, 0:sl8, :] = halo_ref[:, lanes] * keep
        xp_ref[s, sl8:sl8 + ts, :] = pb_ref[:, lanes]
    yield
    for s in range(B_CONV_CH // LANES):
        lanes = slice(s * LANES, (s + 1) * LANES)
        cw = [0.5 * cw_ref[j:j + 1, lanes] for j in range(B_CONV)]
        ld = {off: xp_ref[s, pl.ds(sl8 + off, half, stride=2), :] for off in range(1 - B_CONV, 2)}
        even = None
        odd = None
        for j in range(B_CONV):
            te = cw[j] * ld[j - (B_CONV - 1)]
            to = cw[j] * ld[j - (B_CONV - 1) + 1]
            even = te if even is None else even + te
            odd = to if odd is None else odd + to
        qkv_ref[s, pl.ds(0, half, stride=2), :] = even * jnp.tanh(even) + even
        qkv_ref[s, pl.ds(1, half, stride=2), :] = odd * jnp.tanh(odd) + odd
        yield

    small = pb_ref[:, B_CONV_CH + bw:B_CONV_CH + bw + SMALL_PAD]
    beta_all = _sigmoid(small)
    g_all = -jnp.exp(alog_ref[...]) * _softplus(small + dtb_ref[...])
    gcum = jnp.concatenate(
        [_cumsum_rows(tril_ref[...], g_all[p * pr:(p + 1) * pr, :]) for p in range(ts // pr)], axis=0)
    gcum_t = gcum.T
    for j in range(ts // c):
        ge_ref[0, j:j + 1, :] = jnp.exp(gcum[j * c + c - 1:j * c + c, :])

    ti = lax.broadcasted_iota(jnp.int32, (pr, pr), 0)
    si = lax.broadcasted_iota(jnp.int32, (pr, pr), 1)
    same = (ti >> int(math.log2(c))) == (si >> int(math.log2(c)))
    incl = same & (ti >= si)
    first_chunk = lax.broadcasted_iota(jnp.int32, (pr, 1), 0) < c
    yield

    all_probs = [(h, p) for h in range(HEADS) for p in range(ts // pr)]
    for first in range(0, len(all_probs), B_LOCKSTEP):
        probs = all_probs[first:first + B_LOCKSTEP]
        k16s, kb16s, q16s, decays, rhs = [], [], [], [], []
        for h, p in probs:
            sl = slice(h * HEAD_DIM, (h + 1) * HEAD_DIM)
            rows = slice(p * pr, (p + 1) * pr)
            q = qkv_ref[h, rows, :]
            k = qkv_ref[HEADS + h, rows, :]
            v = qkv_ref[2 * HEADS + h, rows, :]
            q = q * (lax.rsqrt(jnp.sum(q * q, axis=-1, keepdims=True) + EPS) * (HEAD_DIM ** -0.5))
            k = k * lax.rsqrt(jnp.sum(k * k, axis=-1, keepdims=True) + EPS)
            beta = beta_all[rows, h:h + 1]
            gc = gcum[rows, HEADS + h:HEADS + h + 1]
            gr = gcum_t[HEADS + h:HEADS + h + 1, rows]
            g_last = jnp.where(first_chunk, gc[c - 1:c, :], gc[pr - 1:pr, :])
            egc = jnp.exp(gc)
            kb = k * beta
            decays.append(jnp.where(incl, jnp.exp(jnp.minimum(gc - gr, 0.0)), 0.0))
            k16s.append(k.astype(BF16))
            kb16s.append(kb.astype(BF16))
            q16s.append(q.astype(BF16))
            rhs.append(jnp.concatenate([v * beta, kb * egc], axis=1).astype(BF16))
            qd_ref[rows, sl] = (q * egc).astype(BF16)
            kd_ref[rows, sl] = (k * jnp.exp(g_last - gc)).astype(BF16)
            yield

        kk = [_dot_nt(kb16, k16) for kb16, k16 in zip(kb16s, k16s)]
        yield
        qk = [_dot_nt(q16, k16) for q16, k16 in zip(q16s, k16s)]
        yield
        mats = [x * d for x, d in zip(kk, decays)]
        tinv = yield from _unit_lower_inverses(mats, ti, si)
        uw = [_dot(t.astype(BF16), r) for t, r in zip(tinv, rhs)]
        yield
        for i, (h, p) in enumerate(probs):
            sl = slice(h * HEAD_DIM, (h + 1) * HEAD_DIM)
            rows = slice(p * pr, (p + 1) * pr)
            u_ref[rows, sl] = uw[i][:, 0:HEAD_DIM]
            w_ref[rows, sl] = uw[i][:, HEAD_DIM:].astype(BF16)
            qk_ref[p, h] = (qk[i] * decays[i]).astype(BF16)


def _dn_scan_kernel(u_ref, w_ref, qd_ref, kd_ref, qk_ref, ge_ref, z_ref, nw_ref, o_ref, state_ref):
    c = B_CHUNK
    batch = u_ref.shape[0]

    @pl.when(pl.program_id(0) == 0)
    def _():
        state_ref[...] = jnp.zeros_like(state_ref)

    nw = nw_ref[...]
    chains = [(b, h) for b in range(batch) for h in range(HEADS)]
    hsl = lambda h: slice(h * HEAD_DIM, (h + 1) * HEAD_DIM)
    states = [state_ref[b * HEADS + h] for b, h in chains]
    for pi in range(B_SCAN_PAIRS):
        o_inter = [[] for _ in chains]
        v_new = [[] for _ in chains]
        for j in range(B_PAIR // c):
            rows = slice(pi * B_PAIR + j * c, pi * B_PAIR + (j + 1) * c)
            st16 = [st.astype(BF16) for st in states]
            lhs = [jnp.concatenate([w_ref[b, rows, hsl(h)], qd_ref[b, rows, hsl(h)]], axis=0)
                   for b, h in chains]
            prod = [_dot(x, s) for x, s in zip(lhs, st16)]
            vn16 = []
            for i, (b, h) in enumerate(chains):
                vn = (u_ref[b, rows, hsl(h)] - prod[i][0:c, :]).astype(BF16)
                vn16.append(vn)
                v_new[i].append(vn)
                o_inter[i].append(prod[i][c:, :])
            upd = [_dot_tn(kd_ref[b, rows, hsl(h)], vn) for (b, h), vn in zip(chains, vn16)]
            states = [st * ge_ref[b, pi, j:j + 1, HEADS + h:HEADS + h + 1] + x
                      for (b, h), st, x in zip(chains, states, upd)]
        intra = [_dot(qk_ref[b, pi, h], jnp.concatenate(v_new[i], axis=0)) for i, (b, h) in enumerate(chains)]
        prows = slice(pi * B_PAIR, (pi + 1) * B_PAIR)
        for i, (b, h) in enumerate(chains):
            o = jnp.concatenate(o_inter[i], axis=0) + intra[i]
            y = o * lax.rsqrt(jnp.mean(o * o, axis=-1, keepdims=True) + EPS) * nw
            o_ref[b, prows, hsl(h)] = (y * _silu(z_ref[b, prows, hsl(h)])).astype(BF16)
    for i, (b, h) in enumerate(chains):
        state_ref[b * HEADS + h] = states[i]


def _dn_scan(pb, u, w, qd, kd, qk, ge, nw, batch, seq):
    c = B_CHUNK
    pr = B_PAIR
    bw = BRANCH_WIDTH
    t = batch * seq
    const = lambda i: (0, 0)
    np_seq = seq // pr
    seq3 = lambda x: x.reshape(batch, seq, x.shape[-1])
    sp = B_SCAN_PAIRS
    blk3 = pl.BlockSpec((batch, sp * pr, bw), lambda i: (0, i, 0))
    y = pl.pallas_call(
        _dn_scan_kernel,
        grid=(np_seq // sp,),
        in_specs=[blk3, blk3, blk3, blk3,
                  pl.BlockSpec((batch, sp, HEADS, pr, pr), lambda i: (0, i, 0, 0, 0)),
                  pl.BlockSpec((batch, sp, pr // c, SMALL_PAD), lambda i: (0, i, 0, 0)),
                  pl.BlockSpec((batch, sp * pr, bw), lambda i: (0, i, B_CONV_CH // bw)),
                  pl.BlockSpec((1, HEAD_DIM), const)],
        out_specs=blk3,
        out_shape=jax.ShapeDtypeStruct((batch, seq, bw), BF16),
        scratch_shapes=[pltpu.VMEM((batch * HEADS, HEAD_DIM, HEAD_DIM), F32)],
        compiler_params=pltpu.CompilerParams(
            dimension_semantics=("arbitrary",), vmem_limit_bytes=VMEM_LIMIT),
        name="dn_scan",
    )(seq3(u), seq3(w), seq3(qd), seq3(kd),
      qk.reshape(batch, np_seq, HEADS, pr, pr),
      ge.reshape(batch, np_seq, pr // c, SMALL_PAD),
      seq3(pb), nw)
    return y.reshape(t, bw)


def _band_buckets():
    assert WINDOW == C_BLOCK
    r = np.arange(C_BLOCK)[:, None]
    c = np.arange(C_BLOCK)[None, :]
    dist = np.where(c > r, r + C_BLOCK - c, r - c)
    max_exact = N_BUCKETS // 2
    d_f = np.maximum(dist, 1).astype(np.float32)
    large = max_exact + (np.log(d_f / np.float32(max_exact)) / np.float32(math.log(MAX_DISTANCE / max_exact))
                         * np.float32(N_BUCKETS - max_exact)).astype(np.int32)
    large = np.minimum(large, N_BUCKETS - 1)
    return np.where(dist < max_exact, dist, large).astype(np.int32)


def _band_halves():
    r = np.arange(C_BLOCK)[:, None]
    c = np.arange(C_BLOCK)[None, :]
    return np.stack([c > r, c <= r]).astype(np.float32)


def _swa_stages(n, first_step, rb_ref, sink_ref, cur_ref, prev_ref, bucket_ref, half_ref, o_ref, bias_ref):
    blk = C_BLOCK
    bw = BRANCH_WIDTH

    @pl.when(first_step)
    def _():
        bucket = bucket_ref[...]
        for h in range(C_Q_HEADS):
            acc = jnp.zeros((blk, blk), F32)
            for bk in range(N_BUCKETS):
                acc = jnp.where(bucket == bk, rb_ref[bk, h], acc)
            bias_ref[h] = acc

    kv = jnp.concatenate([prev_ref[...], cur_ref[:, bw:bw + 2 * C_KV_WIDTH]], axis=0)
    rows = kv.shape[0]
    lane = lax.broadcasted_iota(jnp.int32, (rows, C_KV_WIDTH), 1)

    def halves(x, j):
        own = jnp.where((lane >= j * C_HEAD_DIM) & (lane < (j + 1) * C_HEAD_DIM), x, 0.0)
        other = pltpu.roll(own, C_HEAD_DIM, 1)
        pair = (own, other) if j == 0 else (other, own)
        return [t.astype(BF16) for t in pair]

    kz = [halves(kv[:, 0:C_KV_WIDTH], j) for j in range(C_KV_HEADS)]
    vz = [halves(kv[:, C_KV_WIDTH:], j) for j in range(C_KV_HEADS)]
    scale = C_HEAD_DIM ** -0.5
    group = C_Q_HEADS // C_KV_HEADS
    from_prev = (lax.broadcasted_iota(jnp.int32, (blk, blk), 1) > lax.broadcasted_iota(jnp.int32, (blk, blk), 0))
    yield

    qrows = lambda i: slice(i * blk, (i + 1) * blk)
    krows = lambda i: slice(i * blk, (i + 2) * blk)
    for first in range(0, C_TILE_BLOCKS, C_LOCKSTEP_BLOCKS):
        blocks = range(first, first + C_LOCKSTEP_BLOCKS)
        chains = [(i, h) for i in blocks for h in range(C_Q_HEADS)]
        q2 = {i: [(cur_ref[qrows(i), p * LANES:(p + 1) * LANES] * scale).astype(BF16)
                  for p in range(C_Q_HEADS // 2)] for i in blocks}
        both = [_dot_nt(q2[i][h // 2], kz[h // group][h % 2][krows(i), :]) for i, h in chains]
        yield
        logits = []
        for (i, h), lg in zip(chains, both):
            prev_part = lg[:, 0:blk]
            if i == 0:
                prev_part = prev_part + jnp.where(n > 0, 0.0, NEG_BIG)
            logits.append(jnp.where(from_prev, prev_part, lg[:, blk:]) + bias_ref[h])
        yield
        mx = [jnp.maximum(jnp.max(lg, axis=-1, keepdims=True), sink_ref[h])
              for (i, h), lg in zip(chains, logits)]
        yield
        pr = [jnp.exp(lg - m) for lg, m in zip(logits, mx)]
        den = [jnp.sum(x, axis=-1, keepdims=True) + jnp.exp(sink_ref[h] - m)
               for (i, h), x, m in zip(chains, pr, mx)]
        yield
        pr16 = [x.astype(BF16) for x in pr]
        pcat = [jnp.concatenate([x * half_ref[0], x * half_ref[1]], axis=1) for x in pr16]
        pv = [_dot(x, vz[h // group][h % 2][krows(i), :]) for (i, h), x in zip(chains, pcat)]
        yield
        for bi, i in enumerate(blocks):
            for p in range(C_Q_HEADS // 2):
                a, b = bi * C_Q_HEADS + 2 * p, bi * C_Q_HEADS + 2 * p + 1
                out = pv[a] / den[a] + pv[b] / den[b]
                gate = cur_ref[qrows(i), bw + 2 * C_KV_WIDTH + p * LANES:bw + 2 * C_KV_WIDTH + (p + 1) * LANES]
                o_ref[qrows(i), p * LANES:(p + 1) * LANES] = (out * _silu(gate)).astype(BF16)
        yield


N_DN_IN = 6
N_DN_OUT = 6
N_SWA_IN = 6


def _mix_bc_kernel(*refs, tiles_per_seq):
    dn_in = refs[0:N_DN_IN]
    swa_in = refs[N_DN_IN:N_DN_IN + N_SWA_IN]
    outs = refs[N_DN_IN + N_SWA_IN:N_DN_IN + N_SWA_IN + N_DN_OUT + 1]
    xp_ref, qkv_ref, bias_ref = refs[N_DN_IN + N_SWA_IN + N_DN_OUT + 1:]
    step = pl.program_id(0)
    seq_tile = step % tiles_per_seq
    jobs = [_dn_prep_stages(seq_tile, *dn_in, *outs[0:N_DN_OUT], xp_ref, qkv_ref),
            _swa_stages(seq_tile, step == 0, *swa_in, outs[N_DN_OUT], bias_ref)]
    while jobs:
        for job in list(jobs):
            try:
                next(job)
            except StopIteration:
                jobs.remove(job)


def _mix_bc(pb, pc, cw, alog, dtb, rel_bias, sinks, batch, seq):
    c = B_CHUNK
    ts = B_TILE
    pr = B_PAIR
    bw = BRANCH_WIDTH
    blk = C_BLOCK
    t = batch * seq
    n_tiles = t // ts
    tril = jnp.asarray(np.kron(np.eye(pr // c, dtype=np.float32), np.tril(np.ones((c, c), np.float32))),
                       dtype=BF16)
    bucket = jnp.asarray(_band_buckets())
    halves = jnp.asarray(_band_halves(), dtype=BF16)
    const = lambda i: (0, 0)
    row = lambda i: (i, 0)
    halo_blocks = ts // SUBLANES
    kv_col = BRANCH_WIDTH // (2 * C_KV_WIDTH)
    smem = pl.BlockSpec(memory_space=pltpu.SMEM)
    return pl.pallas_call(
        functools.partial(_mix_bc_kernel, tiles_per_seq=seq // ts),
        grid=(n_tiles,),
        in_specs=[pl.BlockSpec((ts, PB_WIDTH), row),
                  pl.BlockSpec((SUBLANES, B_CONV_CH), lambda i: (jnp.maximum(i * halo_blocks - 1, 0), 0)),
                  pl.BlockSpec(cw.shape, const),
                  pl.BlockSpec((1, SMALL_PAD), const),
                  pl.BlockSpec((1, SMALL_PAD), const),
                  pl.BlockSpec((pr, pr), const),
                  smem, smem,
                  pl.BlockSpec((ts, PC_WIDTH), row),
                  pl.BlockSpec((blk, 2 * C_KV_WIDTH), lambda i: (jnp.maximum(i * C_TILE_BLOCKS - 1, 0), kv_col)),
                  pl.BlockSpec((blk, blk), const),
                  pl.BlockSpec((2, blk, blk), lambda i: (0, 0, 0))],
        out_specs=[pl.BlockSpec((ts, bw), row),
                   pl.BlockSpec((ts, bw), row),
                   pl.BlockSpec((ts, bw), row),
                   pl.BlockSpec((ts, bw), row),
                   pl.BlockSpec((ts // pr, HEADS, pr, pr), lambda i: (i, 0, 0, 0)),
                   pl.BlockSpec((1, ts // c, SMALL_PAD), lambda i: (i, 0, 0)),
                   pl.BlockSpec((ts, bw), row)],
        out_shape=[jax.ShapeDtypeStruct((t, bw), F32),
                   jax.ShapeDtypeStruct((t, bw), BF16),
                   jax.ShapeDtypeStruct((t, bw), BF16),
                   jax.ShapeDtypeStruct((t, bw), BF16),
                   jax.ShapeDtypeStruct((t // pr, HEADS, pr, pr), BF16),
                   jax.ShapeDtypeStruct((n_tiles, ts // c, SMALL_PAD), F32),
                   jax.ShapeDtypeStruct((t, bw), BF16)],
        scratch_shapes=[pltpu.VMEM((B_CONV_CH // LANES, ts + SUBLANES, LANES), F32),
                        pltpu.VMEM((B_CONV_CH // LANES, ts, LANES), F32),
                        pltpu.VMEM((C_Q_HEADS, blk, blk), F32)],
        compiler_params=pltpu.CompilerParams(
            dimension_semantics=("arbitrary",), vmem_limit_bytes=VMEM_LIMIT),
        name="mix_bc",
    )(pb, pb, cw, alog, dtb, tril, rel_bias, sinks, pc, pc, bucket, halves)


def _lane_pad(v, offset):
    return jnp.zeros((1, SMALL_PAD), F32).at[0, offset:offset + v.shape[0]].set(v.astype(F32))


def kernel(x, norm_w, w_in, conv_w, a_log, dt_bias, lb_param, norm_a, norm_b, sinks, rel_bias,
           w_branch, w_out, final_norm):
    batch, seq, _ = x.shape
    depth = w_in.shape[0]
    x2 = x.reshape(batch * seq, D_MODEL)
    fn = final_norm.reshape(1, D_MODEL)
    w16 = w_in.astype(BF16)
    wbr = w_branch
    wo = w_out
    for l in range(depth):
        nw = norm_w[l].reshape(1, D_MODEL)
        pa, pb, pc = _inproj(x2, nw, w16, l)
        ya = _hgrn2(pa, lb_param, norm_a[l].reshape(1, HEAD_DIM), batch, seq, l)
        *dn, yc = _mix_bc(pb, pc, conv_w[l], _lane_pad(a_log[l], HEADS), _lane_pad(dt_bias[l], HEADS),
                          rel_bias, sinks[l], batch, seq)
        yb = _dn_scan(pb, *dn, norm_b[l].reshape(1, HEAD_DIM), batch, seq)
        x2 = _merge(x2, nw, w16, ya, yb, yc, wbr, wo, fn, layer=l, final=(l == depth - 1))
    return x2.reshape(batch, seq, D_MODEL)
```

```python
import functools
import math

import numpy as np
import jax
import jax.numpy as jnp
from jax import lax
from jax.experimental import pallas as pl
from jax.experimental.pallas import tpu as pltpu

F32 = jnp.float32
BF16 = jnp.bfloat16

D_MODEL = 1024
BRANCH_WIDTH = D_MODEL // 2
N_BRANCHES = 3
EPS = 1e-6
HEADS = 4
HEAD_DIM = BRANCH_WIDTH // HEADS
B_CONV = 4
B_CONV_CH = 3 * BRANCH_WIDTH
C_Q_HEADS = 8
C_KV_HEADS = 2
C_HEAD_DIM = BRANCH_WIDTH // C_Q_HEADS
C_KV_WIDTH = C_KV_HEADS * C_HEAD_DIM
WINDOW = 128
C_BLOCK = 128
C_LOCKSTEP_BLOCKS = 2
N_BUCKETS = 32
MAX_DISTANCE = 128
LANES = 128
SUBLANES = 8
NEG_BIG = -1e30
LOG2E = math.log2(math.e)

A_CHUNK = 128
A_TILE_CHUNKS = 8
B_CHUNK = 64
B_PAIR = 2 * B_CHUNK
B_TILE = 512
C_TILE_BLOCKS = B_TILE // C_BLOCK
B_LOCKSTEP = 16
B_SCAN_PAIRS = 4
PROJ_ROWS = 512
SMALL_PAD = LANES
VMEM_LIMIT = 52 * 1024 * 1024

PA_WIDTH = 4 * BRANCH_WIDTH
PB_WIDTH = B_CONV_CH + BRANCH_WIDTH + SMALL_PAD
PC_WIDTH = 2 * BRANCH_WIDTH + 2 * C_KV_WIDTH
W_AB = PA_WIDTH + PB_WIDTH - (SMALL_PAD - 2 * HEADS)
W_ABC = W_AB + PC_WIDTH


def _dot(a, b):
    return jnp.dot(a, b, preferred_element_type=F32)


def _dot_nt(a, b):
    return lax.dot_general(a, b, (((1,), (1,)), ((), ())), preferred_element_type=F32)


def _dot_tn(a, b):
    return lax.dot_general(a, b, (((0,), (0,)), ((), ())), preferred_element_type=F32)


def _sigmoid(x):
    return 0.5 * jnp.tanh(0.5 * x) + 0.5


def _silu(x):
    h = 0.5 * x
    return h * jnp.tanh(h) + h


def _cumsum_rows(tril16, x):
    hi = x.astype(BF16)
    r1 = x - hi.astype(F32)
    mid = r1.astype(BF16)
    lo = (r1 - mid.astype(F32)).astype(BF16)
    return _dot(tril16, hi) + (_dot(tril16, mid) + _dot(tril16, lo))


def _softplus(x):
    return jnp.maximum(x, 0.0) + jnp.log(1.0 + jnp.exp(-jnp.abs(x)))


def _rms(x, w):
    return x * lax.rsqrt(jnp.mean(x * x, axis=-1, keepdims=True) + EPS) * w


def _resident(shape):
    return pl.BlockSpec(shape, lambda i: (0,) * len(shape), pipeline_mode=pl.Buffered(1))


def _layer_resident(shape, layer):
    return pl.BlockSpec((None,) + tuple(shape[1:]), lambda i: (layer,) + (0,) * (len(shape) - 1),
                        pipeline_mode=pl.Buffered(1))


def _inproj_kernel(x_ref, nw_ref, w_ref, pa_ref, pb_ref, pc_ref, wc_ref, *, layer):
    @pl.when(pl.program_id(0) == 0)
    def _():
        wc_ref[...] = w_ref[:, W_AB:W_ABC]

    h = _rms(x_ref[...], nw_ref[layer:layer + 1, :]).astype(BF16)
    for o_ref, src_ref, base in ((pa_ref, w_ref, 0), (pb_ref, w_ref, PA_WIDTH), (pc_ref, wc_ref, 0)):
        n = o_ref.shape[1]
        for j in range(0, n, BRANCH_WIDTH):
            wd = min(BRANCH_WIDTH, n - j)
            o_ref[:, j:j + wd] = _dot(h, src_ref[:, base + j:base + j + wd])


def _inproj(x2, nw, w, layer):
    t = x2.shape[0]
    tm = PROJ_ROWS
    return pl.pallas_call(
        functools.partial(_inproj_kernel, layer=layer),
        grid=(t // tm,),
        in_specs=[pl.BlockSpec((tm, D_MODEL), lambda i: (i, 0)),
                  _resident(nw.shape),
                  _layer_resident(w.shape, layer)],
        out_specs=[pl.BlockSpec((tm, PA_WIDTH), lambda i: (i, 0)),
                   pl.BlockSpec((tm, PB_WIDTH), lambda i: (i, 0)),
                   pl.BlockSpec((tm, PC_WIDTH), lambda i: (i, 0))],
        out_shape=[jax.ShapeDtypeStruct((t, PA_WIDTH), F32),
                   jax.ShapeDtypeStruct((t, PB_WIDTH), F32),
                   jax.ShapeDtypeStruct((t, PC_WIDTH), F32)],
        scratch_shapes=[pltpu.VMEM((D_MODEL, PC_WIDTH), BF16)],
        compiler_params=pltpu.CompilerParams(
            dimension_semantics=("arbitrary",), vmem_limit_bytes=VMEM_LIMIT),
        name="inproj",
    )(x2, nw, w)


def _merge_kernel(x_ref, nw_ref, w_ref, ya_ref, yb_ref, yc_ref, wbr_ref, wo_ref, fn_ref,
                  o_ref, wg_ref, *, layer, final):
    @pl.when(pl.program_id(0) == 0)
    def _():
        wg_ref[...] = w_ref[:, W_ABC:]

    x = x_ref[...]
    h = _rms(x, nw_ref[layer:layer + 1, :]).astype(BF16)
    merged = None
    for n, y_ref in enumerate((ya_ref, yb_ref, yc_ref)):
        gate = _sigmoid(_dot(h, wg_ref[:, n * D_MODEL:(n + 1) * D_MODEL]))
        term = gate * _dot(y_ref[...], wbr_ref[n].astype(BF16))
        merged = term if merged is None else merged + term
    out = x + _dot(merged.astype(BF16), wo_ref[...].astype(BF16))
    if final:
        out = _rms(out, fn_ref[...])
    o_ref[...] = out


def _merge(x2, nw, w, ya, yb, yc, wbr, wo, fn, layer, final):
    t = x2.shape[0]
    tm = PROJ_ROWS
    row = lambda i: (i, 0)
    return pl.pallas_call(
        functools.partial(_merge_kernel, layer=layer, final=final),
        grid=(t // tm,),
        in_specs=[pl.BlockSpec((tm, D_MODEL), row),
                  _resident(nw.shape),
                  _layer_resident(w.shape, layer),
                  pl.BlockSpec((tm, BRANCH_WIDTH), row),
                  pl.BlockSpec((tm, BRANCH_WIDTH), row),
                  pl.BlockSpec((tm, BRANCH_WIDTH), row),
                  _layer_resident(wbr.shape, layer),
                  _layer_resident(wo.shape, layer),
                  _resident((1, D_MODEL))],
        out_specs=pl.BlockSpec((tm, D_MODEL), row),
        out_shape=jax.ShapeDtypeStruct((t, D_MODEL), F32),
        scratch_shapes=[pltpu.VMEM((D_MODEL, N_BRANCHES * D_MODEL), BF16)],
        compiler_params=pltpu.CompilerParams(
            dimension_semantics=("arbitrary",), vmem_limit_bytes=VMEM_LIMIT),
        name="merge",
    )(x2, nw, w, ya, yb, yc, wbr, wo, fn)


def _a_levels():
    return [1 << i for i in range(int(math.log2(A_CHUNK)))]


def _a_level_ids():
    idx = np.arange(A_CHUNK)
    ids = np.where(idx[:, None] == idx[None, :], 0, -1).astype(np.int32)
    for i, m in enumerate(_a_levels()):
        blk = idx // (2 * m)
        upper = (idx & m) != 0
        ids[(blk[:, None] == blk[None, :]) & upper[:, None] & ~upper[None, :]] = i + 1
    return ids


def _hgrn2_kernel(pa_ref, lbp_ref, nw_ref, tril_ref, lvl_ref, o_ref,
                  state_ref, b_ref, gp_ref, *, layer):
    c = A_CHUNK
    bw = BRANCH_WIDTH

    @pl.when(pl.program_id(1) == 0)
    def _():
        state_ref[...] = jnp.zeros_like(state_ref)
        gp_ref[...] = jnp.zeros_like(gp_ref)

    lbp = lbp_ref[...]
    ex = jnp.exp(lbp - jnp.max(lbp, axis=0, keepdims=True))
    sm = ex / jnp.sum(ex, axis=0, keepdims=True)
    lb = jnp.zeros((1, bw), F32)
    for j in range(1, layer + 1):
        lb = lb + sm[j:j + 1, :]

    hsl = [slice(h * HEAD_DIM, (h + 1) * HEAD_DIM) for h in range(HEADS)]
    row = lax.broadcasted_iota(jnp.int32, (c, 1), 0)
    lvl = lvl_ref[...]
    t1 = jnp.log(lb)
    t2_lb = jnp.log1p(-lb)

    def intra_chunk(ci):
        rows = slice(ci * c, (ci + 1) * c)
        bc_ref = b_ref.at[ci]
        gc_ref = gp_ref.at[ci]
        q = _silu(pa_ref[rows, 0:bw])
        z = pa_ref[rows, bw:2 * bw]
        v = pa_ref[rows, 2 * bw:3 * bw].astype(BF16)
        log_sig = jnp.minimum(z, 0.0) - jnp.log(1.0 + jnp.exp(-jnp.abs(z)))
        sig_neg = 0.5 - 0.5 * jnp.tanh(0.5 * z)
        t2 = t2_lb + log_sig
        lf = jnp.maximum(t1, t2) + jnp.log(1.0 + jnp.exp(-jnp.abs(t1 - t2)))
        k = (1.0 - lb) * sig_neg

        lf2 = lf * LOG2E
        b = _cumsum_rows(tril_ref[...], lf2)
        bc_ref[...] = b
        gc_ref[SUBLANES:SUBLANES + c, :] = lf2

        intra = []
        for sl in hsl:
            qh = q[:, sl]
            kh = k[:, sl]
            lfh = lf2[:, sl]
            acc = jnp.where(lvl == 0, _dot_nt(qh.astype(BF16), kh.astype(BF16)), 0.0)
            for li, m in enumerate(_a_levels()):
                blocks = range(0, c, 2 * m)
                if m == 1:
                    neg = jnp.where((row & 1) == 1, lfh, 0.0)
                elif m == 2:
                    r4 = row & 3
                    g_next = gc_ref[SUBLANES + 1:SUBLANES + 1 + c, sl]
                    g_prev = gc_ref[SUBLANES - 1:SUBLANES - 1 + c, sl]
                    neg = jnp.where(r4 == 0, g_next,
                                    jnp.where(r4 == 1, 0.0, jnp.where(r4 == 2, lfh, lfh + g_prev)))
                elif m < SUBLANES:
                    neg = jnp.concatenate(
                        [-jnp.abs(bc_ref[lo:lo + 2 * m, sl] - bc_ref[lo + m - 1:lo + m, sl]) for lo in blocks],
                        axis=0)
                else:
                    pieces = []
                    for lo in blocks:
                        anchor = bc_ref[lo + m - 1:lo + m, sl]
                        pieces += [anchor - bc_ref[lo:lo + m, sl], bc_ref[lo + m:lo + 2 * m, sl] - anchor]
                    neg = jnp.concatenate(pieces, axis=0)
                if m < SUBLANES:
                    x = jnp.where((row & m) != 0, qh, kh)
                else:
                    x = jnp.concatenate(
                        [t for lo in blocks for t in (kh[lo:lo + m, :], qh[lo + m:lo + 2 * m, :])], axis=0)
                xt = (x * jnp.exp2(neg)).astype(BF16)
                acc = jnp.where(lvl == li + 1, _dot_nt(xt, xt), acc)
            intra.append(_dot(acc.astype(BF16), v[:, sl]))

        b_end = bc_ref[c - 1:c, :]
        qd = (q * jnp.exp2(b)).astype(BF16)
        kd = (k * jnp.exp2(b_end - b)).astype(BF16)
        upd = [_dot_tn(v[:, hsl[h]], kd[:, hsl[h]]) for h in range(HEADS)]
        return intra, upd, qd, jnp.exp2(b_end)

    parts = [intra_chunk(ci) for ci in range(A_TILE_CHUNKS)]
    nw = nw_ref[layer:layer + 1, :]
    sts = [state_ref[h] for h in range(HEADS)]
    for ci, (intra, upd, qd, s_decay) in enumerate(parts):
        rows = slice(ci * c, (ci + 1) * c)
        inter = [_dot_nt(qd[:, hsl[h]], sts[h].astype(BF16)) for h in range(HEADS)]
        sts = [sts[h] * s_decay[:, hsl[h]] + upd[h] for h in range(HEADS)]
        for h, sl in enumerate(hsl):
            o = intra[h] + inter[h]
            y = o * lax.rsqrt(jnp.mean(o * o, axis=-1, keepdims=True) + EPS) * nw
            gate = pa_ref[rows, 3 * bw + h * HEAD_DIM:3 * bw + (h + 1) * HEAD_DIM]
            o_ref[rows, sl] = (y * _silu(gate)).astype(BF16)
    for h in range(HEADS):
        state_ref[h] = sts[h]


def _hgrn2(pa, lbp, nw, batch, seq, layer):
    c = A_CHUNK
    tc = A_TILE_CHUNKS
    nt = seq // (c * tc)
    tril = jnp.asarray(np.tril(np.ones((c, c), np.float32)), dtype=BF16)
    lvl = jnp.asarray(_a_level_ids())
    const2 = lambda b, i: (0, 0)
    return pl.pallas_call(
        functools.partial(_hgrn2_kernel, layer=layer),
        grid=(batch, nt),
        in_specs=[pl.BlockSpec((tc * c, PA_WIDTH), lambda b, i: (b * nt + i, 0)),
                  pl.BlockSpec(lbp.shape, const2),
                  pl.BlockSpec(nw.shape, const2),
                  pl.BlockSpec((c, c), const2),
                  pl.BlockSpec((c, c), const2)],
        out_specs=pl.BlockSpec((tc * c, BRANCH_WIDTH), lambda b, i: (b * nt + i, 0)),
        out_shape=jax.ShapeDtypeStruct((batch * seq, BRANCH_WIDTH), BF16),
        scratch_shapes=[pltpu.VMEM((HEADS, HEAD_DIM, HEAD_DIM), F32),
                        pltpu.VMEM((tc, c, BRANCH_WIDTH), F32),
                        pltpu.VMEM((tc, c + 2 * SUBLANES, BRANCH_WIDTH), F32)],
        compiler_params=pltpu.CompilerParams(
            dimension_semantics=("arbitrary", "arbitrary"), vmem_limit_bytes=VMEM_LIMIT),
        name="hgrn2",
    )(pa, lbp, nw, tril, lvl)


def _b_levels():
    return [1 << i for i in range(int(math.log2(B_CHUNK)))]


def _unit_lower_inverses(mats, ti, si):
    eye = (ti == si).astype(F32)
    ts = [eye] * len(mats)
    for s in _b_levels():
        shift = int(math.log2(2 * s))
        off = ((ti >> shift) == (si >> shift)) & ((ti & s) != 0) & ((si & s) == 0)
        a_off = [jnp.where(off, a, 0.0) for a in mats]
        if s == 1:
            ts = [t - ao for t, ao in zip(ts, a_off)]
        else:
            t16 = [t.astype(BF16) for t in ts]
            inner = [_dot(ao.astype(BF16), t) for ao, t in zip(a_off, t16)]
            yield
            outer = [_dot(t, x.astype(BF16)) for t, x in zip(t16, inner)]
            yield
            ts = [t - x for t, x in zip(ts, outer)]
    return ts


def _dn_prep_stages(seq_tile, layer, pb_ref, halo_ref, cw_ref, alog_ref, dtb_ref, tril_ref,
                    u_ref, w_ref, qd_ref, kd_ref, qk_ref, ge_ref, xp_ref, qkv_ref):
    ts = B_TILE
    c = B_CHUNK
    pr = B_PAIR
    bw = BRANCH_WIDTH
    sl8 = SUBLANES

    keep = jnp.where(seq_tile == 0, 0.0, 1.0)
    half = ts // 2
    for s in range(B_CONV_CH // LANES):
        lanes = slice(s * LANES, (s + 1) * LANES)
        xp_ref[s, 0:sl8, :] = halo_ref[:, lanes] * keep
        xp_ref[s, sl8:sl8 + ts, :] = pb_ref[:, lanes]
    yield
    for s in range(B_CONV_CH // LANES):
        lanes = slice(s * LANES, (s + 1) * LANES)
        cw = [0.5 * cw_ref[j:j + 1, lanes] for j in range(B_CONV)]
        ld = {off: xp_ref[s, pl.ds(sl8 + off, half, stride=2), :] for off in range(1 - B_CONV, 2)}
        even = None
        odd = None
        for j in range(B_CONV):
            te = cw[j] * ld[j - (B_CONV - 1)]
            to = cw[j] * ld[j - (B_CONV - 1) + 1]
            even = te if even is None else even + te
            odd = to if odd is None else odd + to
        qkv_ref[s, pl.ds(0, half, stride=2), :] = even * jnp.tanh(even) + even
        qkv_ref[s, pl.ds(1, half, stride=2), :] = odd * jnp.tanh(odd) + odd
        yield

    small = pb_ref[:, B_CONV_CH + bw:B_CONV_CH + bw + SMALL_PAD]
    beta_all = _sigmoid(small)
    small_lane = lax.broadcasted_iota(jnp.int32, (1, SMALL_PAD), 1)
    alog = jnp.zeros((1, SMALL_PAD), F32)
    dtb = jnp.zeros((1, SMALL_PAD), F32)
    for h in range(HEADS):
        alog = jnp.where(small_lane == HEADS + h, alog_ref[layer, h], alog)
        dtb = jnp.where(small_lane == HEADS + h, dtb_ref[layer, h], dtb)
    g_all = -jnp.exp(alog) * _softplus(small + dtb)
    gcum = jnp.concatenate(
        [_cumsum_rows(tril_ref[...], g_all[p * pr:(p + 1) * pr, :]) for p in range(ts // pr)], axis=0)
    gcum_t = gcum.T
    for j in range(ts // c):
        ge_ref[0, j:j + 1, :] = jnp.exp(gcum[j * c + c - 1:j * c + c, :])

    ti = lax.broadcasted_iota(jnp.int32, (pr, pr), 0)
    si = lax.broadcasted_iota(jnp.int32, (pr, pr), 1)
    same = (ti >> int(math.log2(c))) == (si >> int(math.log2(c)))
    incl = same & (ti >= si)
    first_chunk = lax.broadcasted_iota(jnp.int32, (pr, 1), 0) < c
    yield

    all_probs = [(h, p) for h in range(HEADS) for p in range(ts // pr)]
    for first in range(0, len(all_probs), B_LOCKSTEP):
        probs = all_probs[first:first + B_LOCKSTEP]
        k16s, kb16s, q16s, decays, rhs = [], [], [], [], []
        for h, p in probs:
            sl = slice(h * HEAD_DIM, (h + 1) * HEAD_DIM)
            rows = slice(p * pr, (p + 1) * pr)
            q = qkv_ref[h, rows, :]
            k = qkv_ref[HEADS + h, rows, :]
            v = qkv_ref[2 * HEADS + h, rows, :]
            q = q * (lax.rsqrt(jnp.sum(q * q, axis=-1, keepdims=True) + EPS) * (HEAD_DIM ** -0.5))
            k = k * lax.rsqrt(jnp.sum(k * k, axis=-1, keepdims=True) + EPS)
            beta = beta_all[rows, h:h + 1]
            gc = gcum[rows, HEADS + h:HEADS + h + 1]
            gr = gcum_t[HEADS + h:HEADS + h + 1, rows]
            g_last = jnp.where(first_chunk, gc[c - 1:c, :], gc[pr - 1:pr, :])
            egc = jnp.exp(gc)
            kb = k * beta
            decays.append(jnp.where(incl, jnp.exp(jnp.minimum(gc - gr, 0.0)), 0.0))
            k16s.append(k.astype(BF16))
            kb16s.append(kb.astype(BF16))
            q16s.append(q.astype(BF16))
            rhs.append(jnp.concatenate([v * beta, kb * egc], axis=1).astype(BF16))
            qd_ref[rows, sl] = (q * egc).astype(BF16)
            kd_ref[rows, sl] = (k * jnp.exp(g_last - gc)).astype(BF16)
            yield

        kk = [_dot_nt(kb16, k16) for kb16, k16 in zip(kb16s, k16s)]
        yield
        qk = [_dot_nt(q16, k16) for q16, k16 in zip(q16s, k16s)]
        yield
        mats = [x * d for x, d in zip(kk, decays)]
        tinv = yield from _unit_lower_inverses(mats, ti, si)
        uw = [_dot(t.astype(BF16), r) for t, r in zip(tinv, rhs)]
        yield
        for i, (h, p) in enumerate(probs):
            sl = slice(h * HEAD_DIM, (h + 1) * HEAD_DIM)
            rows = slice(p * pr, (p + 1) * pr)
            u_ref[rows, sl] = uw[i][:, 0:HEAD_DIM]
            w_ref[rows, sl] = uw[i][:, HEAD_DIM:].astype(BF16)
            qk_ref[p, h] = (qk[i] * decays[i]).astype(BF16)


def _dn_scan_kernel(u_ref, w_ref, qd_ref, kd_ref, qk_ref, ge_ref, z_ref, nw_ref, o_ref, state_ref, *, layer):
    c = B_CHUNK
    batch = u_ref.shape[0]

    @pl.when(pl.program_id(0) == 0)
    def _():
        state_ref[...] = jnp.zeros_like(state_ref)

    nw = nw_ref[layer:layer + 1, :]
    chains = [(b, h) for b in range(batch) for h in range(HEADS)]
    hsl = lambda h: slice(h * HEAD_DIM, (h + 1) * HEAD_DIM)
    states = [state_ref[b * HEADS + h] for b, h in chains]
    for pi in range(B_SCAN_PAIRS):
        o_inter = [[] for _ in chains]
        v_new = [[] for _ in chains]
        for j in range(B_PAIR // c):
            rows = slice(pi * B_PAIR + j * c, pi * B_PAIR + (j + 1) * c)
            st16 = [st.astype(BF16) for st in states]
            lhs = [jnp.concatenate([w_ref[b, rows, hsl(h)], qd_ref[b, rows, hsl(h)]], axis=0)
                   for b, h in chains]
            prod = [_dot(x, s) for x, s in zip(lhs, st16)]
            vn16 = []
            for i, (b, h) in enumerate(chains):
                vn = (u_ref[b, rows, hsl(h)] - prod[i][0:c, :]).astype(BF16)
                vn16.append(vn)
                v_new[i].append(vn)
                o_inter[i].append(prod[i][c:, :])
            upd = [_dot_tn(kd_ref[b, rows, hsl(h)], vn) for (b, h), vn in zip(chains, vn16)]
            states = [st * ge_ref[b, pi, j:j + 1, HEADS + h:HEADS + h + 1] + x
                      for (b, h), st, x in zip(chains, states, upd)]
        intra = [_dot(qk_ref[b, pi, h], jnp.concatenate(v_new[i], axis=0)) for i, (b, h) in enumerate(chains)]
        prows = slice(pi * B_PAIR, (pi + 1) * B_PAIR)
        for i, (b, h) in enumerate(chains):
            o = jnp.concatenate(o_inter[i], axis=0) + intra[i]
            y = o * lax.rsqrt(jnp.mean(o * o, axis=-1, keepdims=True) + EPS) * nw
            o_ref[b, prows, hsl(h)] = (y * _silu(z_ref[b, prows, hsl(h)])).astype(BF16)
    for i, (b, h) in enumerate(chains):
        state_ref[b * HEADS + h] = states[i]


def _dn_scan(pb, u, w, qd, kd, qk, ge, nw, batch, seq, layer):
    c = B_CHUNK
    pr = B_PAIR
    bw = BRANCH_WIDTH
    t = batch * seq
    const = lambda i: (0, 0)
    np_seq = seq // pr
    seq3 = lambda x: x.reshape(batch, seq, x.shape[-1])
    sp = B_SCAN_PAIRS
    blk3 = pl.BlockSpec((batch, sp * pr, bw), lambda i: (0, i, 0))
    y = pl.pallas_call(
        functools.partial(_dn_scan_kernel, layer=layer),
        grid=(np_seq // sp,),
        in_specs=[blk3, blk3, blk3, blk3,
                  pl.BlockSpec((batch, sp, HEADS, pr, pr), lambda i: (0, i, 0, 0, 0)),
                  pl.BlockSpec((batch, sp, pr // c, SMALL_PAD), lambda i: (0, i, 0, 0)),
                  pl.BlockSpec((batch, sp * pr, bw), lambda i: (0, i, B_CONV_CH // bw)),
                  pl.BlockSpec(nw.shape, const)],
        out_specs=blk3,
        out_shape=jax.ShapeDtypeStruct((batch, seq, bw), BF16),
        scratch_shapes=[pltpu.VMEM((batch * HEADS, HEAD_DIM, HEAD_DIM), F32)],
        compiler_params=pltpu.CompilerParams(
            dimension_semantics=("arbitrary",), vmem_limit_bytes=VMEM_LIMIT),
        name="dn_scan",
    )(seq3(u), seq3(w), seq3(qd), seq3(kd),
      qk.reshape(batch, np_seq, HEADS, pr, pr),
      ge.reshape(batch, np_seq, pr // c, SMALL_PAD),
      seq3(pb), nw)
    return y.reshape(t, bw)


def _band_buckets():
    assert WINDOW == C_BLOCK
    r = np.arange(C_BLOCK)[:, None]
    c = np.arange(C_BLOCK)[None, :]
    dist = np.where(c > r, r + C_BLOCK - c, r - c)
    max_exact = N_BUCKETS // 2
    d_f = np.maximum(dist, 1).astype(np.float32)
    large = max_exact + (np.log(d_f / np.float32(max_exact)) / np.float32(math.log(MAX_DISTANCE / max_exact))
                         * np.float32(N_BUCKETS - max_exact)).astype(np.int32)
    large = np.minimum(large, N_BUCKETS - 1)
    return np.where(dist < max_exact, dist, large).astype(np.int32)


def _band_halves():
    r = np.arange(C_BLOCK)[:, None]
    c = np.arange(C_BLOCK)[None, :]
    return np.stack([c > r, c <= r]).astype(np.float32)


def _swa_stages(n, layer, first_step, rb_ref, sink_ref, cur_ref, prev_ref, bucket_ref, half_ref, o_ref,
                bias_ref):
    blk = C_BLOCK
    bw = BRANCH_WIDTH

    @pl.when(first_step)
    def _():
        bucket = bucket_ref[...]
        for h in range(C_Q_HEADS):
            acc = jnp.zeros((blk, blk), F32)
            for bk in range(N_BUCKETS):
                acc = jnp.where(bucket == bk, rb_ref[bk, h], acc)
            bias_ref[h] = acc

    kv = jnp.concatenate([prev_ref[...], cur_ref[:, bw:bw + 2 * C_KV_WIDTH]], axis=0)
    rows = kv.shape[0]
    lane = lax.broadcasted_iota(jnp.int32, (rows, C_KV_WIDTH), 1)

    def halves(x, j):
        own = jnp.where((lane >= j * C_HEAD_DIM) & (lane < (j + 1) * C_HEAD_DIM), x, 0.0)
        other = pltpu.roll(own, C_HEAD_DIM, 1)
        pair = (own, other) if j == 0 else (other, own)
        return [t.astype(BF16) for t in pair]

    kz = [halves(kv[:, 0:C_KV_WIDTH], j) for j in range(C_KV_HEADS)]
    vz = [halves(kv[:, C_KV_WIDTH:], j) for j in range(C_KV_HEADS)]
    scale = C_HEAD_DIM ** -0.5
    group = C_Q_HEADS // C_KV_HEADS
    from_prev = (lax.broadcasted_iota(jnp.int32, (blk, blk), 1) > lax.broadcasted_iota(jnp.int32, (blk, blk), 0))
    yield

    qrows = lambda i: slice(i * blk, (i + 1) * blk)
    krows = lambda i: slice(i * blk, (i + 2) * blk)
    for first in range(0, C_TILE_BLOCKS, C_LOCKSTEP_BLOCKS):
        blocks = range(first, first + C_LOCKSTEP_BLOCKS)
        chains = [(i, h) for i in blocks for h in range(C_Q_HEADS)]
        q2 = {i: [(cur_ref[qrows(i), p * LANES:(p + 1) * LANES] * scale).astype(BF16)
                  for p in range(C_Q_HEADS // 2)] for i in blocks}
        both = [_dot_nt(q2[i][h // 2], kz[h // group][h % 2][krows(i), :]) for i, h in chains]
        yield
        logits = []
        for (i, h), lg in zip(chains, both):
            prev_part = lg[:, 0:blk]
            if i == 0:
                prev_part = prev_part + jnp.where(n > 0, 0.0, NEG_BIG)
            logits.append(jnp.where(from_prev, prev_part, lg[:, blk:]) + bias_ref[h])
        yield
        mx = [jnp.maximum(jnp.max(lg, axis=-1, keepdims=True), sink_ref[layer, h])
              for (i, h), lg in zip(chains, logits)]
        yield
        pr = [jnp.exp(lg - m) for lg, m in zip(logits, mx)]
        den = [jnp.sum(x, axis=-1, keepdims=True) + jnp.exp(sink_ref[layer, h] - m)
               for (i, h), x, m in zip(chains, pr, mx)]
        yield
        pr16 = [x.astype(BF16) for x in pr]
        pcat = [jnp.concatenate([x * half_ref[0], x * half_ref[1]], axis=1) for x in pr16]
        pv = [_dot(x, vz[h // group][h % 2][krows(i), :]) for (i, h), x in zip(chains, pcat)]
        yield
        for bi, i in enumerate(blocks):
            for p in range(C_Q_HEADS // 2):
                a, b = bi * C_Q_HEADS + 2 * p, bi * C_Q_HEADS + 2 * p + 1
                out = pv[a] / den[a] + pv[b] / den[b]
                gate = cur_ref[qrows(i), bw + 2 * C_KV_WIDTH + p * LANES:bw + 2 * C_KV_WIDTH + (p + 1) * LANES]
                o_ref[qrows(i), p * LANES:(p + 1) * LANES] = (out * _silu(gate)).astype(BF16)
        yield


N_DN_IN = 6
N_DN_OUT = 6
N_SWA_IN = 6


def _mix_bc_kernel(*refs, tiles_per_seq, layer):
    dn_in = refs[0:N_DN_IN]
    swa_in = refs[N_DN_IN:N_DN_IN + N_SWA_IN]
    outs = refs[N_DN_IN + N_SWA_IN:N_DN_IN + N_SWA_IN + N_DN_OUT + 1]
    xp_ref, qkv_ref, bias_ref = refs[N_DN_IN + N_SWA_IN + N_DN_OUT + 1:]
    step = pl.program_id(0)
    seq_tile = step % tiles_per_seq
    jobs = [_dn_prep_stages(seq_tile, layer, *dn_in, *outs[0:N_DN_OUT], xp_ref, qkv_ref),
            _swa_stages(seq_tile, layer, step == 0, *swa_in, outs[N_DN_OUT], bias_ref)]
    while jobs:
        for job in list(jobs):
            try:
                next(job)
            except StopIteration:
                jobs.remove(job)


def _mix_bc(pb, pc, cw, alog, dtb, rel_bias, sinks, batch, seq, layer):
    c = B_CHUNK
    ts = B_TILE
    pr = B_PAIR
    bw = BRANCH_WIDTH
    blk = C_BLOCK
    t = batch * seq
    n_tiles = t // ts
    tril = jnp.asarray(np.kron(np.eye(pr // c, dtype=np.float32), np.tril(np.ones((c, c), np.float32))),
                       dtype=BF16)
    bucket = jnp.asarray(_band_buckets())
    halves = jnp.asarray(_band_halves(), dtype=BF16)
    const = lambda i: (0, 0)
    row = lambda i: (i, 0)
    halo_blocks = ts // SUBLANES
    kv_col = BRANCH_WIDTH // (2 * C_KV_WIDTH)
    smem = pl.BlockSpec(memory_space=pltpu.SMEM)
    return pl.pallas_call(
        functools.partial(_mix_bc_kernel, tiles_per_seq=seq // ts, layer=layer),
        grid=(n_tiles,),
        in_specs=[pl.BlockSpec((ts, PB_WIDTH), row),
                  pl.BlockSpec((SUBLANES, B_CONV_CH), lambda i: (jnp.maximum(i * halo_blocks - 1, 0), 0)),
                  pl.BlockSpec((None,) + cw.shape[1:], lambda i: (layer, 0, 0)),
                  smem, smem,
                  pl.BlockSpec((pr, pr), const),
                  smem, smem,
                  pl.BlockSpec((ts, PC_WIDTH), row),
                  pl.BlockSpec((blk, 2 * C_KV_WIDTH), lambda i: (jnp.maximum(i * C_TILE_BLOCKS - 1, 0), kv_col)),
                  pl.BlockSpec((blk, blk), const),
                  pl.BlockSpec((2, blk, blk), lambda i: (0, 0, 0))],
        out_specs=[pl.BlockSpec((ts, bw), row),
                   pl.BlockSpec((ts, bw), row),
                   pl.BlockSpec((ts, bw), row),
                   pl.BlockSpec((ts, bw), row),
                   pl.BlockSpec((ts // pr, HEADS, pr, pr), lambda i: (i, 0, 0, 0)),
                   pl.BlockSpec((1, ts // c, SMALL_PAD), lambda i: (i, 0, 0)),
                   pl.BlockSpec((ts, bw), row)],
        out_shape=[jax.ShapeDtypeStruct((t, bw), F32),
                   jax.ShapeDtypeStruct((t, bw), BF16),
                   jax.ShapeDtypeStruct((t, bw), BF16),
                   jax.ShapeDtypeStruct((t, bw), BF16),
                   jax.ShapeDtypeStruct((t // pr, HEADS, pr, pr), BF16),
                   jax.ShapeDtypeStruct((n_tiles, ts // c, SMALL_PAD), F32),
                   jax.ShapeDtypeStruct((t, bw), BF16)],
        scratch_shapes=[pltpu.VMEM((B_CONV_CH // LANES, ts + SUBLANES, LANES), F32),
                        pltpu.VMEM((B_CONV_CH // LANES, ts, LANES), F32),
                        pltpu.VMEM((C_Q_HEADS, blk, blk), F32)],
        compiler_params=pltpu.CompilerParams(
            dimension_semantics=("arbitrary",), vmem_limit_bytes=VMEM_LIMIT),
        name="mix_bc",
    )(pb, pb, cw, alog, dtb, tril, rel_bias, sinks, pc, pc, bucket, halves)


def kernel(x, norm_w, w_in, conv_w, a_log, dt_bias, lb_param, norm_a, norm_b, sinks, rel_bias,
           w_branch, w_out, final_norm):
    batch, seq, _ = x.shape
    depth = w_in.shape[0]
    x2 = x.reshape(batch * seq, D_MODEL)
    fn = final_norm.reshape(1, D_MODEL)
    w16 = w_in.astype(BF16)
    for l in range(depth):
        pa, pb, pc = _inproj(x2, norm_w, w16, l)
        ya = _hgrn2(pa, lb_param, norm_a, batch, seq, l)
        *dn, yc = _mix_bc(pb, pc, conv_w, a_log, dt_bias, rel_bias, sinks, batch, seq, l)
        yb = _dn_scan(pb, *dn, norm_b, batch, seq, l)
        x2 = _merge(x2, norm_w, w16, ya, yb, yc, w_branch, w_out, fn, layer=l, final=(l == depth - 1))
    return x2.reshape(batch, seq, D_MODEL)
```

```python
import functools
import math

import numpy as np
import jax
import jax.numpy as jnp
from jax import lax
from jax.experimental import pallas as pl
from jax.experimental.pallas import tpu as pltpu

F32 = jnp.float32
BF16 = jnp.bfloat16

D_MODEL = 1024
BRANCH_WIDTH = D_MODEL // 2
N_BRANCHES = 3
EPS = 1e-6
HEADS = 4
HEAD_DIM = BRANCH_WIDTH // HEADS
B_CONV = 4
B_CONV_CH = 3 * BRANCH_WIDTH
C_Q_HEADS = 8
C_KV_HEADS = 2
C_HEAD_DIM = BRANCH_WIDTH // C_Q_HEADS
C_KV_WIDTH = C_KV_HEADS * C_HEAD_DIM
WINDOW = 128
C_BLOCK = 128
C_LOCKSTEP_BLOCKS = 2
N_BUCKETS = 32
MAX_DISTANCE = 128
LANES = 128
SUBLANES = 8
NEG_BIG = -1e30
LOG2E = math.log2(math.e)

A_CHUNK = 128
A_TILE_CHUNKS = 8
B_CHUNK = 64
B_PAIR = 2 * B_CHUNK
B_TILE = 512
C_TILE_BLOCKS = B_TILE // C_BLOCK
B_LOCKSTEP = 16
B_SCAN_PAIRS = 4
PROJ_ROWS = 512
SMALL_PAD = LANES
VMEM_LIMIT = 52 * 1024 * 1024

PA_WIDTH = 4 * BRANCH_WIDTH
PB_WIDTH = B_CONV_CH + BRANCH_WIDTH + SMALL_PAD
PC_WIDTH = 2 * BRANCH_WIDTH + 2 * C_KV_WIDTH
W_AB = PA_WIDTH + PB_WIDTH - (SMALL_PAD - 2 * HEADS)
W_ABC = W_AB + PC_WIDTH
W_MIXER_BLOCK = -(-W_ABC // LANES) * LANES
W_GATE_BLOCK = W_ABC // LANES * LANES


def _dot(a, b):
    return jnp.dot(a, b, preferred_element_type=F32)


def _dot_nt(a, b):
    return lax.dot_general(a, b, (((1,), (1,)), ((), ())), preferred_element_type=F32)


def _dot_tn(a, b):
    return lax.dot_general(a, b, (((0,), (0,)), ((), ())), preferred_element_type=F32)


def _sigmoid(x):
    return 0.5 * jnp.tanh(0.5 * x) + 0.5


def _silu(x):
    h = 0.5 * x
    return h * jnp.tanh(h) + h


def _cumsum_rows(tril16, x):
    hi = x.astype(BF16)
    r1 = x - hi.astype(F32)
    mid = r1.astype(BF16)
    lo = (r1 - mid.astype(F32)).astype(BF16)
    return _dot(tril16, hi) + (_dot(tril16, mid) + _dot(tril16, lo))


def _softplus(x):
    return jnp.maximum(x, 0.0) + jnp.log(1.0 + jnp.exp(-jnp.abs(x)))


def _rms(x, w):
    return x * lax.rsqrt(jnp.mean(x * x, axis=-1, keepdims=True) + EPS) * w


def _resident(shape):
    return pl.BlockSpec(shape, lambda i: (0,) * len(shape), pipeline_mode=pl.Buffered(1))


def _layer_resident(shape, layer):
    return pl.BlockSpec((None,) + tuple(shape[1:]), lambda i: (layer,) + (0,) * (len(shape) - 1),
                        pipeline_mode=pl.Buffered(1))


def _layer_columns(shape, layer, width, index):
    return pl.BlockSpec((None, shape[1], width), lambda i: (layer, 0, index), pipeline_mode=pl.Buffered(1))


def _inproj_kernel(x_ref, nw_ref, w_ref, pa_ref, pb_ref, pc_ref, wc_ref, *, layer):
    @pl.when(pl.program_id(0) == 0)
    def _():
        wc_ref[...] = w_ref[:, W_AB:W_ABC]

    h = _rms(x_ref[...], nw_ref[layer:layer + 1, :]).astype(BF16)
    for o_ref, src_ref, base in ((pa_ref, w_ref, 0), (pb_ref, w_ref, PA_WIDTH), (pc_ref, wc_ref, 0)):
        n = o_ref.shape[1]
        for j in range(0, n, BRANCH_WIDTH):
            wd = min(BRANCH_WIDTH, n - j)
            o_ref[:, j:j + wd] = _dot(h, src_ref[:, base + j:base + j + wd])


def _inproj(x2, nw, w, layer):
    t = x2.shape[0]
    tm = PROJ_ROWS
    return pl.pallas_call(
        functools.partial(_inproj_kernel, layer=layer),
        grid=(t // tm,),
        in_specs=[pl.BlockSpec((tm, D_MODEL), lambda i: (i, 0)),
                  _resident(nw.shape),
                  _layer_columns(w.shape, layer, W_MIXER_BLOCK, 0)],
        out_specs=[pl.BlockSpec((tm, PA_WIDTH), lambda i: (i, 0)),
                   pl.BlockSpec((tm, PB_WIDTH), lambda i: (i, 0)),
                   pl.BlockSpec((tm, PC_WIDTH), lambda i: (i, 0))],
        out_shape=[jax.ShapeDtypeStruct((t, PA_WIDTH), F32),
                   jax.ShapeDtypeStruct((t, PB_WIDTH), F32),
                   jax.ShapeDtypeStruct((t, PC_WIDTH), F32)],
        scratch_shapes=[pltpu.VMEM((D_MODEL, PC_WIDTH), BF16)],
        compiler_params=pltpu.CompilerParams(
            dimension_semantics=("arbitrary",), vmem_limit_bytes=VMEM_LIMIT),
        name="inproj",
    )(x2, nw, w)


def _merge_kernel(x_ref, nw_ref, w_ref, ya_ref, yb_ref, yc_ref, wbr_ref, wo_ref, fn_ref,
                  o_ref, wg_ref, *, layer, final):
    @pl.when(pl.program_id(0) == 0)
    def _():
        lo = W_ABC - W_GATE_BLOCK
        wg_ref[...] = w_ref[:, lo:lo + N_BRANCHES * D_MODEL]

    x = x_ref[...]
    h = _rms(x, nw_ref[layer:layer + 1, :]).astype(BF16)
    merged = None
    for n, y_ref in enumerate((ya_ref, yb_ref, yc_ref)):
        gate = _sigmoid(_dot(h, wg_ref[:, n * D_MODEL:(n + 1) * D_MODEL]))
        term = gate * _dot(y_ref[...], wbr_ref[n].astype(BF16))
        merged = term if merged is None else merged + term
    out = x + _dot(merged.astype(BF16), wo_ref[...].astype(BF16))
    if final:
        out = _rms(out, fn_ref[...])
    o_ref[...] = out


def _merge(x2, nw, w, ya, yb, yc, wbr, wo, fn, layer, final):
    t = x2.shape[0]
    tm = PROJ_ROWS
    assert w.shape[2] == W_ABC + N_BRANCHES * D_MODEL <= 2 * W_GATE_BLOCK
    row = lambda i: (i, 0)
    return pl.pallas_call(
        functools.partial(_merge_kernel, layer=layer, final=final),
        grid=(t // tm,),
        in_specs=[pl.BlockSpec((tm, D_MODEL), row),
                  _resident(nw.shape),
                  _layer_columns(w.shape, layer, W_GATE_BLOCK, 1),
                  pl.BlockSpec((tm, BRANCH_WIDTH), row),
                  pl.BlockSpec((tm, BRANCH_WIDTH), row),
                  pl.BlockSpec((tm, BRANCH_WIDTH), row),
                  _layer_resident(wbr.shape, layer),
                  _layer_resident(wo.shape, layer),
                  _resident((1, D_MODEL))],
        out_specs=pl.BlockSpec((tm, D_MODEL), row),
        out_shape=jax.ShapeDtypeStruct((t, D_MODEL), F32),
        scratch_shapes=[pltpu.VMEM((D_MODEL, N_BRANCHES * D_MODEL), BF16)],
        compiler_params=pltpu.CompilerParams(
            dimension_semantics=("arbitrary",), vmem_limit_bytes=VMEM_LIMIT),
        name="merge",
    )(x2, nw, w, ya, yb, yc, wbr, wo, fn)


def _a_levels():
    return [1 << i for i in range(int(math.log2(A_CHUNK)))]


def _a_level_ids():
    idx = np.arange(A_CHUNK)
    ids = np.where(idx[:, None] == idx[None, :], 0, -1).astype(np.int32)
    for i, m in enumerate(_a_levels()):
        blk = idx // (2 * m)
        upper = (idx & m) != 0
        ids[(blk[:, None] == blk[None, :]) & upper[:, None] & ~upper[None, :]] = i + 1
    return ids


def _hgrn2_kernel(pa_ref, lbp_ref, nw_ref, tril_ref, lvl_ref, o_ref,
                  state_ref, b_ref, gp_ref, *, layer):
    c = A_CHUNK
    bw = BRANCH_WIDTH

    @pl.when(pl.program_id(1) == 0)
    def _():
        state_ref[...] = jnp.zeros_like(state_ref)
        gp_ref[...] = jnp.zeros_like(gp_ref)

    lbp = lbp_ref[...]
    ex = jnp.exp(lbp - jnp.max(lbp, axis=0, keepdims=True))
    sm = ex / jnp.sum(ex, axis=0, keepdims=True)
    lb = jnp.zeros((1, bw), F32)
    for j in range(1, layer + 1):
        lb = lb + sm[j:j + 1, :]

    hsl = [slice(h * HEAD_DIM, (h + 1) * HEAD_DIM) for h in range(HEADS)]
    row = lax.broadcasted_iota(jnp.int32, (c, 1), 0)
    lvl = lvl_ref[...]
    t1 = jnp.log(lb)
    t2_lb = jnp.log1p(-lb)

    def intra_chunk(ci):
        rows = slice(ci * c, (ci + 1) * c)
        bc_ref = b_ref.at[ci]
        gc_ref = gp_ref.at[ci]
        q = _silu(pa_ref[rows, 0:bw])
        z = pa_ref[rows, bw:2 * bw]
        v = pa_ref[rows, 2 * bw:3 * bw].astype(BF16)
        log_sig = jnp.minimum(z, 0.0) - jnp.log(1.0 + jnp.exp(-jnp.abs(z)))
        sig_neg = 0.5 - 0.5 * jnp.tanh(0.5 * z)
        t2 = t2_lb + log_sig
        lf = jnp.maximum(t1, t2) + jnp.log(1.0 + jnp.exp(-jnp.abs(t1 - t2)))
        k = (1.0 - lb) * sig_neg

        lf2 = lf * LOG2E
        b = _cumsum_rows(tril_ref[...], lf2)
        bc_ref[...] = b
        gc_ref[SUBLANES:SUBLANES + c, :] = lf2

        intra = []
        for sl in hsl:
            qh = q[:, sl]
            kh = k[:, sl]
            lfh = lf2[:, sl]
            acc = jnp.where(lvl == 0, _dot_nt(qh.astype(BF16), kh.astype(BF16)), 0.0)
            for li, m in enumerate(_a_levels()):
                blocks = range(0, c, 2 * m)
                if m == 1:
                    neg = jnp.where((row & 1) == 1, lfh, 0.0)
                elif m == 2:
                    r4 = row & 3
                    g_next = gc_ref[SUBLANES + 1:SUBLANES + 1 + c, sl]
                    g_prev = gc_ref[SUBLANES - 1:SUBLANES - 1 + c, sl]
                    neg = jnp.where(r4 == 0, g_next,
                                    jnp.where(r4 == 1, 0.0, jnp.where(r4 == 2, lfh, lfh + g_prev)))
                elif m < SUBLANES:
                    neg = jnp.concatenate(
                        [-jnp.abs(bc_ref[lo:lo + 2 * m, sl] - bc_ref[lo + m - 1:lo + m, sl]) for lo in blocks],
                        axis=0)
                else:
                    pieces = []
                    for lo in blocks:
                        anchor = bc_ref[lo + m - 1:lo + m, sl]
                        pieces += [anchor - bc_ref[lo:lo + m, sl], bc_ref[lo + m:lo + 2 * m, sl] - anchor]
                    neg = jnp.concatenate(pieces, axis=0)
                if m < SUBLANES:
                    x = jnp.where((row & m) != 0, qh, kh)
                else:
                    x = jnp.concatenate(
                        [t for lo in blocks for t in (kh[lo:lo + m, :], qh[lo + m:lo + 2 * m, :])], axis=0)
                xt = (x * jnp.exp2(neg)).astype(BF16)
                acc = jnp.where(lvl == li + 1, _dot_nt(xt, xt), acc)
            intra.append(_dot(acc.astype(BF16), v[:, sl]))

        b_end = bc_ref[c - 1:c, :]
        qd = (q * jnp.exp2(b)).astype(BF16)
        kd = (k * jnp.exp2(b_end - b)).astype(BF16)
        upd = [_dot_tn(v[:, hsl[h]], kd[:, hsl[h]]) for h in range(HEADS)]
        return intra, upd, qd, jnp.exp2(b_end)

    parts = [intra_chunk(ci) for ci in range(A_TILE_CHUNKS)]
    nw = nw_ref[layer:layer + 1, :]
    sts = [state_ref[h] for h in range(HEADS)]
    for ci, (intra, upd, qd, s_decay) in enumerate(parts):
        rows = slice(ci * c, (ci + 1) * c)
        inter = [_dot_nt(qd[:, hsl[h]], sts[h].astype(BF16)) for h in range(HEADS)]
        sts = [sts[h] * s_decay[:, hsl[h]] + upd[h] for h in range(HEADS)]
        for h, sl in enumerate(hsl):
            o = intra[h] + inter[h]
            y = o * lax.rsqrt(jnp.mean(o * o, axis=-1, keepdims=True) + EPS) * nw
            gate = pa_ref[rows, 3 * bw + h * HEAD_DIM:3 * bw + (h + 1) * HEAD_DIM]
            o_ref[rows, sl] = (y * _silu(gate)).astype(BF16)
    for h in range(HEADS):
        state_ref[h] = sts[h]


def _hgrn2(pa, lbp, nw, batch, seq, layer):
    c = A_CHUNK
    tc = A_TILE_CHUNKS
    nt = seq // (c * tc)
    tril = jnp.asarray(np.tril(np.ones((c, c), np.float32)), dtype=BF16)
    lvl = jnp.asarray(_a_level_ids())
    const2 = lambda b, i: (0, 0)
    return pl.pallas_call(
        functools.partial(_hgrn2_kernel, layer=layer),
        grid=(batch, nt),
        in_specs=[pl.BlockSpec((tc * c, PA_WIDTH), lambda b, i: (b * nt + i, 0)),
                  pl.BlockSpec(lbp.shape, const2),
                  pl.BlockSpec(nw.shape, const2),
                  pl.BlockSpec((c, c), const2),
                  pl.BlockSpec((c, c), const2)],
        out_specs=pl.BlockSpec((tc * c, BRANCH_WIDTH), lambda b, i: (b * nt + i, 0)),
        out_shape=jax.ShapeDtypeStruct((batch * seq, BRANCH_WIDTH), BF16),
        scratch_shapes=[pltpu.VMEM((HEADS, HEAD_DIM, HEAD_DIM), F32),
                        pltpu.VMEM((tc, c, BRANCH_WIDTH), F32),
                        pltpu.VMEM((tc, c + 2 * SUBLANES, BRANCH_WIDTH), F32)],
        compiler_params=pltpu.CompilerParams(
            dimension_semantics=("arbitrary", "arbitrary"), vmem_limit_bytes=VMEM_LIMIT),
        name="hgrn2",
    )(pa, lbp, nw, tril, lvl)


def _b_levels():
    return [1 << i for i in range(int(math.log2(B_CHUNK)))]


def _unit_lower_inverses(mats, ti, si):
    eye = (ti == si).astype(F32)
    ts = [eye] * len(mats)
    for s in _b_levels():
        shift = int(math.log2(2 * s))
        off = ((ti >> shift) == (si >> shift)) & ((ti & s) != 0) & ((si & s) == 0)
        a_off = [jnp.where(off, a, 0.0) for a in mats]
        if s == 1:
            ts = [t - ao for t, ao in zip(ts, a_off)]
        else:
            t16 = [t.astype(BF16) for t in ts]
            inner = [_dot(ao.astype(BF16), t) for ao, t in zip(a_off, t16)]
            yield
            outer = [_dot(t, x.astype(BF16)) for t, x in zip(t16, inner)]
            yield
            ts = [t - x for t, x in zip(ts, outer)]
    return ts


def _dn_prep_stages(seq_tile, layer, pb_ref, halo_ref, cw_ref, alog_ref, dtb_ref, tril_ref,
                    u_ref, w_ref, qd_ref, kd_ref, qk_ref, ge_ref, xp_ref, qkv_ref):
    ts = B_TILE
    c = B_CHUNK
    pr = B_PAIR
    bw = BRANCH_WIDTH
    sl8 = SUBLANES

    keep = jnp.where(seq_tile == 0, 0.0, 1.0)
    half = ts // 2
    for s in range(B_CONV_CH // LANES):
        lanes = slice(s * LANES, (s + 1) * LANES)
        xp_ref[s, 0:sl8, :] = halo_ref[:, lanes] * keep
        xp_ref[s, sl8:sl8 + ts, :] = pb_ref[:, lanes]
    yield
    for s in range(B_CONV_CH // LANES):
        lanes = slice(s * LANES, (s + 1) * LANES)
        cw = [0.5 * cw_ref[j:j + 1, lanes] for j in range(B_CONV)]
        ld = {off: xp_ref[s, pl.ds(sl8 + off, half, stride=2), :] for off in range(1 - B_CONV, 2)}
        even = None
        odd = None
        for j in range(B_CONV):
            te = cw[j] * ld[j - (B_CONV - 1)]
            to = cw[j] * ld[j - (B_CONV - 1) + 1]
            even = te if even is None else even + te
            odd = to if odd is None else odd + to
        qkv_ref[s, pl.ds(0, half, stride=2), :] = even * jnp.tanh(even) + even
        qkv_ref[s, pl.ds(1, half, stride=2), :] = odd * jnp.tanh(odd) + odd
        yield

    small = pb_ref[:, B_CONV_CH + bw:B_CONV_CH + bw + SMALL_PAD]
    beta_all = _sigmoid(small)
    small_lane = lax.broadcasted_iota(jnp.int32, (1, SMALL_PAD), 1)
    alog = jnp.zeros((1, SMALL_PAD), F32)
    dtb = jnp.zeros((1, SMALL_PAD), F32)
    for h in range(HEADS):
        alog = jnp.where(small_lane == HEADS + h, alog_ref[layer, h], alog)
        dtb = jnp.where(small_lane == HEADS + h, dtb_ref[layer, h], dtb)
    g_all = -jnp.exp(alog) * _softplus(small + dtb)
    gcum = jnp.concatenate(
        [_cumsum_rows(tril_ref[...], g_all[p * pr:(p + 1) * pr, :]) for p in range(ts // pr)], axis=0)
    gcum_t = gcum.T
    for j in range(ts // c):
        ge_ref[0, j:j + 1, :] = jnp.exp(gcum[j * c + c - 1:j * c + c, :])

    ti = lax.broadcasted_iota(jnp.int32, (pr, pr), 0)
    si = lax.broadcasted_iota(jnp.int32, (pr, pr), 1)
    same = (ti >> int(math.log2(c))) == (si >> int(math.log2(c)))
    incl = same & (ti >= si)
    first_chunk = lax.broadcasted_iota(jnp.int32, (pr, 1), 0) < c
    yield

    all_probs = [(h, p) for h in range(HEADS) for p in range(ts // pr)]
    for first in range(0, len(all_probs), B_LOCKSTEP):
        probs = all_probs[first:first + B_LOCKSTEP]
        k16s, kb16s, q16s, decays, rhs = [], [], [], [], []
        for h, p in probs:
            sl = slice(h * HEAD_DIM, (h + 1) * HEAD_DIM)
            rows = slice(p * pr, (p + 1) * pr)
            q = qkv_ref[h, rows, :]
            k = qkv_ref[HEADS + h, rows, :]
            v = qkv_ref[2 * HEADS + h, rows, :]
            q = q * (lax.rsqrt(jnp.sum(q * q, axis=-1, keepdims=True) + EPS) * (HEAD_DIM ** -0.5))
            k = k * lax.rsqrt(jnp.sum(k * k, axis=-1, keepdims=True) + EPS)
            beta = beta_all[rows, h:h + 1]
            gc = gcum[rows, HEADS + h:HEADS + h + 1]
            gr = gcum_t[HEADS + h:HEADS + h + 1, rows]
            g_last = jnp.where(first_chunk, gc[c - 1:c, :], gc[pr - 1:pr, :])
            egc = jnp.exp(gc)
            kb = k * beta
            decays.append(jnp.where(incl, jnp.exp(jnp.minimum(gc - gr, 0.0)), 0.0))
            k16s.append(k.astype(BF16))
            kb16s.append(kb.astype(BF16))
            q16s.append(q.astype(BF16))
            rhs.append(jnp.concatenate([v * beta, kb * egc], axis=1).astype(BF16))
            qd_ref[rows, sl] = (q * egc).astype(BF16)
            kd_ref[rows, sl] = (k * jnp.exp(g_last - gc)).astype(BF16)
            yield

        kk = [_dot_nt(kb16, k16) for kb16, k16 in zip(kb16s, k16s)]
        yield
        qk = [_dot_nt(q16, k16) for q16, k16 in zip(q16s, k16s)]
        yield
        mats = [x * d for x, d in zip(kk, decays)]
        tinv = yield from _unit_lower_inverses(mats, ti, si)
        uw = [_dot(t.astype(BF16), r) for t, r in zip(tinv, rhs)]
        yield
        for i, (h, p) in enumerate(probs):
            sl = slice(h * HEAD_DIM, (h + 1) * HEAD_DIM)
            rows = slice(p * pr, (p + 1) * pr)
            u_ref[rows, sl] = uw[i][:, 0:HEAD_DIM]
            w_ref[rows, sl] = uw[i][:, HEAD_DIM:].astype(BF16)
            qk_ref[p, h] = (qk[i] * decays[i]).astype(BF16)


def _dn_scan_kernel(u_ref, w_ref, qd_ref, kd_ref, qk_ref, ge_ref, z_ref, nw_ref, o_ref, state_ref, *, layer):
    c = B_CHUNK
    batch = u_ref.shape[0]

    @pl.when(pl.program_id(0) == 0)
    def _():
        state_ref[...] = jnp.zeros_like(state_ref)

    nw = nw_ref[layer:layer + 1, :]
    chains = [(b, h) for b in range(batch) for h in range(HEADS)]
    hsl = lambda h: slice(h * HEAD_DIM, (h + 1) * HEAD_DIM)
    states = [state_ref[b * HEADS + h] for b, h in chains]
    for pi in range(B_SCAN_PAIRS):
        o_inter = [[] for _ in chains]
        v_new = [[] for _ in chains]
        for j in range(B_PAIR // c):
            rows = slice(pi * B_PAIR + j * c, pi * B_PAIR + (j + 1) * c)
            st16 = [st.astype(BF16) for st in states]
            lhs = [jnp.concatenate([w_ref[b, rows, hsl(h)], qd_ref[b, rows, hsl(h)]], axis=0)
                   for b, h in chains]
            prod = [_dot(x, s) for x, s in zip(lhs, st16)]
            vn16 = []
            for i, (b, h) in enumerate(chains):
                vn = (u_ref[b, rows, hsl(h)] - prod[i][0:c, :]).astype(BF16)
                vn16.append(vn)
                v_new[i].append(vn)
                o_inter[i].append(prod[i][c:, :])
            upd = [_dot_tn(kd_ref[b, rows, hsl(h)], vn) for (b, h), vn in zip(chains, vn16)]
            states = [st * ge_ref[b, pi, j:j + 1, HEADS + h:HEADS + h + 1] + x
                      for (b, h), st, x in zip(chains, states, upd)]
        intra = [_dot(qk_ref[b, pi, h], jnp.concatenate(v_new[i], axis=0)) for i, (b, h) in enumerate(chains)]
        prows = slice(pi * B_PAIR, (pi + 1) * B_PAIR)
        for i, (b, h) in enumerate(chains):
            o = jnp.concatenate(o_inter[i], axis=0) + intra[i]
            y = o * lax.rsqrt(jnp.mean(o * o, axis=-1, keepdims=True) + EPS) * nw
            o_ref[b, prows, hsl(h)] = (y * _silu(z_ref[b, prows, hsl(h)])).astype(BF16)
    for i, (b, h) in enumerate(chains):
        state_ref[b * HEADS + h] = states[i]


def _dn_scan(pb, u, w, qd, kd, qk, ge, nw, batch, seq, layer):
    c = B_CHUNK
    pr = B_PAIR
    bw = BRANCH_WIDTH
    t = batch * seq
    const = lambda i: (0, 0)
    np_seq = seq // pr
    seq3 = lambda x: x.reshape(batch, seq, x.shape[-1])
    sp = B_SCAN_PAIRS
    blk3 = pl.BlockSpec((batch, sp * pr, bw), lambda i: (0, i, 0))
    y = pl.pallas_call(
        functools.partial(_dn_scan_kernel, layer=layer),
        grid=(np_seq // sp,),
        in_specs=[blk3, blk3, blk3, blk3,
                  pl.BlockSpec((batch, sp, HEADS, pr, pr), lambda i: (0, i, 0, 0, 0)),
                  pl.BlockSpec((batch, sp, pr // c, SMALL_PAD), lambda i: (0, i, 0, 0)),
                  pl.BlockSpec((batch, sp * pr, bw), lambda i: (0, i, B_CONV_CH // bw)),
                  pl.BlockSpec(nw.shape, const)],
        out_specs=blk3,
        out_shape=jax.ShapeDtypeStruct((batch, seq, bw), BF16),
        scratch_shapes=[pltpu.VMEM((batch * HEADS, HEAD_DIM, HEAD_DIM), F32)],
        compiler_params=pltpu.CompilerParams(
            dimension_semantics=("arbitrary",), vmem_limit_bytes=VMEM_LIMIT),
        name="dn_scan",
    )(seq3(u), seq3(w), seq3(qd), seq3(kd),
      qk.reshape(batch, np_seq, HEADS, pr, pr),
      ge.reshape(batch, np_seq, pr // c, SMALL_PAD),
      seq3(pb), nw)
    return y.reshape(t, bw)


def _band_buckets():
    assert WINDOW == C_BLOCK
    r = np.arange(C_BLOCK)[:, None]
    c = np.arange(C_BLOCK)[None, :]
    dist = np.where(c > r, r + C_BLOCK - c, r - c)
    max_exact = N_BUCKETS // 2
    d_f = np.maximum(dist, 1).astype(np.float32)
    large = max_exact + (np.log(d_f / np.float32(max_exact)) / np.float32(math.log(MAX_DISTANCE / max_exact))
                         * np.float32(N_BUCKETS - max_exact)).astype(np.int32)
    large = np.minimum(large, N_BUCKETS - 1)
    return np.where(dist < max_exact, dist, large).astype(np.int32)


def _band_halves():
    r = np.arange(C_BLOCK)[:, None]
    c = np.arange(C_BLOCK)[None, :]
    return np.stack([c > r, c <= r]).astype(np.float32)


def _swa_stages(n, layer, first_step, rb_ref, sink_ref, cur_ref, prev_ref, bucket_ref, half_ref, o_ref,
                bias_ref):
    blk = C_BLOCK
    bw = BRANCH_WIDTH

    @pl.when(first_step)
    def _():
        bucket = bucket_ref[...]
        for h in range(C_Q_HEADS):
            acc = jnp.zeros((blk, blk), F32)
            for bk in range(N_BUCKETS):
                acc = jnp.where(bucket == bk, rb_ref[bk, h], acc)
            bias_ref[h] = acc

    kv = jnp.concatenate([prev_ref[...], cur_ref[:, bw:bw + 2 * C_KV_WIDTH]], axis=0)
    rows = kv.shape[0]
    lane = lax.broadcasted_iota(jnp.int32, (rows, C_KV_WIDTH), 1)

    def halves(x, j):
        own = jnp.where((lane >= j * C_HEAD_DIM) & (lane < (j + 1) * C_HEAD_DIM), x, 0.0)
        other = pltpu.roll(own, C_HEAD_DIM, 1)
        pair = (own, other) if j == 0 else (other, own)
        return [t.astype(BF16) for t in pair]

    kz = [halves(kv[:, 0:C_KV_WIDTH], j) for j in range(C_KV_HEADS)]
    vz = [halves(kv[:, C_KV_WIDTH:], j) for j in range(C_KV_HEADS)]
    scale = C_HEAD_DIM ** -0.5
    group = C_Q_HEADS // C_KV_HEADS
    from_prev = (lax.broadcasted_iota(jnp.int32, (blk, blk), 1) > lax.broadcasted_iota(jnp.int32, (blk, blk), 0))
    yield

    qrows = lambda i: slice(i * blk, (i + 1) * blk)
    krows = lambda i: slice(i * blk, (i + 2) * blk)
    for first in range(0, C_TILE_BLOCKS, C_LOCKSTEP_BLOCKS):
        blocks = range(first, first + C_LOCKSTEP_BLOCKS)
        chains = [(i, h) for i in blocks for h in range(C_Q_HEADS)]
        q2 = {i: [(cur_ref[qrows(i), p * LANES:(p + 1) * LANES] * scale).astype(BF16)
                  for p in range(C_Q_HEADS // 2)] for i in blocks}
        both = [_dot_nt(q2[i][h // 2], kz[h // group][h % 2][krows(i), :]) for i, h in chains]
        yield
        logits = []
        for (i, h), lg in zip(chains, both):
            prev_part = lg[:, 0:blk]
            if i == 0:
                prev_part = prev_part + jnp.where(n > 0, 0.0, NEG_BIG)
            logits.append(jnp.where(from_prev, prev_part, lg[:, blk:]) + bias_ref[h])
        yield
        mx = [jnp.maximum(jnp.max(lg, axis=-1, keepdims=True), sink_ref[layer, h])
              for (i, h), lg in zip(chains, logits)]
        yield
        pr = [jnp.exp(lg - m) for lg, m in zip(logits, mx)]
        den = [jnp.sum(x, axis=-1, keepdims=True) + jnp.exp(sink_ref[layer, h] - m)
               for (i, h), x, m in zip(chains, pr, mx)]
        yield
        pr16 = [x.astype(BF16) for x in pr]
        pcat = [jnp.concatenate([x * half_ref[0], x * half_ref[1]], axis=1) for x in pr16]
        pv = [_dot(x, vz[h // group][h % 2][krows(i), :]) for (i, h), x in zip(chains, pcat)]
        yield
        for bi, i in enumerate(blocks):
            for p in range(C_Q_HEADS // 2):
                a, b = bi * C_Q_HEADS + 2 * p, bi * C_Q_HEADS + 2 * p + 1
                out = pv[a] / den[a] + pv[b] / den[b]
                gate = cur_ref[qrows(i), bw + 2 * C_KV_WIDTH + p * LANES:bw + 2 * C_KV_WIDTH + (p + 1) * LANES]
                o_ref[qrows(i), p * LANES:(p + 1) * LANES] = (out * _silu(gate)).astype(BF16)
        yield


N_DN_IN = 6
N_DN_OUT = 6
N_SWA_IN = 6


def _mix_bc_kernel(*refs, tiles_per_seq, layer):
    dn_in = refs[0:N_DN_IN]
    swa_in = refs[N_DN_IN:N_DN_IN + N_SWA_IN]
    outs = refs[N_DN_IN + N_SWA_IN:N_DN_IN + N_SWA_IN + N_DN_OUT + 1]
    xp_ref, qkv_ref, bias_ref = refs[N_DN_IN + N_SWA_IN + N_DN_OUT + 1:]
    step = pl.program_id(0)
    seq_tile = step % tiles_per_seq
    jobs = [_dn_prep_stages(seq_tile, layer, *dn_in, *outs[0:N_DN_OUT], xp_ref, qkv_ref),
            _swa_stages(seq_tile, layer, step == 0, *swa_in, outs[N_DN_OUT], bias_ref)]
    while jobs:
        for job in list(jobs):
            try:
                next(job)
            except StopIteration:
                jobs.remove(job)


def _mix_bc(pb, pc, cw, alog, dtb, rel_bias, sinks, batch, seq, layer):
    c = B_CHUNK
    ts = B_TILE
    pr = B_PAIR
    bw = BRANCH_WIDTH
    blk = C_BLOCK
    t = batch * seq
    n_tiles = t // ts
    tril = jnp.asarray(np.kron(np.eye(pr // c, dtype=np.float32), np.tril(np.ones((c, c), np.float32))),
                       dtype=BF16)
    bucket = jnp.asarray(_band_buckets())
    halves = jnp.asarray(_band_halves(), dtype=BF16)
    const = lambda i: (0, 0)
    row = lambda i: (i, 0)
    halo_blocks = ts // SUBLANES
    kv_col = BRANCH_WIDTH // (2 * C_KV_WIDTH)
    smem = pl.BlockSpec(memory_space=pltpu.SMEM)
    return pl.pallas_call(
        functools.partial(_mix_bc_kernel, tiles_per_seq=seq // ts, layer=layer),
        grid=(n_tiles,),
        in_specs=[pl.BlockSpec((ts, PB_WIDTH), row),
                  pl.BlockSpec((SUBLANES, B_CONV_CH), lambda i: (jnp.maximum(i * halo_blocks - 1, 0), 0)),
                  pl.BlockSpec((None,) + cw.shape[1:], lambda i: (layer, 0, 0)),
                  smem, smem,
                  pl.BlockSpec((pr, pr), const),
                  smem, smem,
                  pl.BlockSpec((ts, PC_WIDTH), row),
                  pl.BlockSpec((blk, 2 * C_KV_WIDTH), lambda i: (jnp.maximum(i * C_TILE_BLOCKS - 1, 0), kv_col)),
                  pl.BlockSpec((blk, blk), const),
                  pl.BlockSpec((2, blk, blk), lambda i: (0, 0, 0))],
        out_specs=[pl.BlockSpec((ts, bw), row),
                   pl.BlockSpec((ts, bw), row),
                   pl.BlockSpec((ts, bw), row),
                   pl.BlockSpec((ts, bw), row),
                   pl.BlockSpec((ts // pr, HEADS, pr, pr), lambda i: (i, 0, 0, 0)),
                   pl.BlockSpec((1, ts // c, SMALL_PAD), lambda i: (i, 0, 0)),
                   pl.BlockSpec((ts, bw), row)],
        out_shape=[jax.ShapeDtypeStruct((t, bw), F32),
                   jax.ShapeDtypeStruct((t, bw), BF16),
                   jax.ShapeDtypeStruct((t, bw), BF16),
                   jax.ShapeDtypeStruct((t, bw), BF16),
                   jax.ShapeDtypeStruct((t // pr, HEADS, pr, pr), BF16),
                   jax.ShapeDtypeStruct((n_tiles, ts // c, SMALL_PAD), F32),
                   jax.ShapeDtypeStruct((t, bw), BF16)],
        scratch_shapes=[pltpu.VMEM((B_CONV_CH // LANES, ts + SUBLANES, LANES), F32),
                        pltpu.VMEM((B_CONV_CH // LANES, ts, LANES), F32),
                        pltpu.VMEM((C_Q_HEADS, blk, blk), F32)],
        compiler_params=pltpu.CompilerParams(
            dimension_semantics=("arbitrary",), vmem_limit_bytes=VMEM_LIMIT),
        name="mix_bc",
    )(pb, pb, cw, alog, dtb, tril, rel_bias, sinks, pc, pc, bucket, halves)


def kernel(x, norm_w, w_in, conv_w, a_log, dt_bias, lb_param, norm_a, norm_b, sinks, rel_bias,
           w_branch, w_out, final_norm):
    batch, seq, _ = x.shape
    depth = w_in.shape[0]
    x2 = x.reshape(batch * seq, D_MODEL)
    fn = final_norm.reshape(1, D_MODEL)
    w16 = w_in.astype(BF16)
    for l in range(depth):
        pa, pb, pc = _inproj(x2, norm_w, w16, l)
        ya = _hgrn2(pa, lb_param, norm_a, batch, seq, l)
        *dn, yc = _mix_bc(pb, pc, conv_w, a_log, dt_bias, rel_bias, sinks, batch, seq, l)
        yb = _dn_scan(pb, *dn, norm_b, batch, seq, l)
        x2 = _merge(x2, norm_w, w16, ya, yb, yc, w_branch, w_out, fn, layer=l, final=(l == depth - 1))
    return x2.reshape(batch, seq, D_MODEL)
```

```python
import functools
import math

import numpy as np
import jax
import jax.numpy as jnp
from jax import lax
from jax.experimental import pallas as pl
from jax.experimental.pallas import tpu as pltpu

F32 = jnp.float32
BF16 = jnp.bfloat16

D_MODEL = 1024
BRANCH_WIDTH = D_MODEL // 2
N_BRANCHES = 3
EPS = 1e-6
HEADS = 4
HEAD_DIM = BRANCH_WIDTH // HEADS
B_CONV = 4
B_CONV_CH = 3 * BRANCH_WIDTH
C_Q_HEADS = 8
C_KV_HEADS = 2
C_HEAD_DIM = BRANCH_WIDTH // C_Q_HEADS
C_KV_WIDTH = C_KV_HEADS * C_HEAD_DIM
WINDOW = 128
C_BLOCK = 128
C_LOCKSTEP_BLOCKS = 2
N_BUCKETS = 32
MAX_DISTANCE = 128
LANES = 128
SUBLANES = 8
NEG_BIG = -1e30
LOG2E = math.log2(math.e)

A_CHUNK = 128
A_TILE_CHUNKS = 8
B_CHUNK = 64
B_PAIR = 2 * B_CHUNK
B_TILE = 512
C_TILE_BLOCKS = B_TILE // C_BLOCK
B_LOCKSTEP = 16
B_SCAN_PAIRS = 4
PROJ_ROWS = 512
SMALL_PAD = LANES
VMEM_LIMIT = 52 * 1024 * 1024

PA_WIDTH = 4 * BRANCH_WIDTH
PB_WIDTH = B_CONV_CH + BRANCH_WIDTH + SMALL_PAD
PC_WIDTH = 2 * BRANCH_WIDTH + 2 * C_KV_WIDTH
W_AB = PA_WIDTH + PB_WIDTH - (SMALL_PAD - 2 * HEADS)
W_ABC = W_AB + PC_WIDTH
W_MIXER_BLOCK = -(-W_ABC // LANES) * LANES
W_GATE_BLOCK = W_ABC // LANES * LANES
W_FETCH_RANGES = ((0, PA_WIDTH), (PA_WIDTH, PA_WIDTH + PB_WIDTH), (PA_WIDTH + PB_WIDTH, W_MIXER_BLOCK))


def _dot(a, b):
    return jnp.dot(a, b, preferred_element_type=F32)


def _dot_nt(a, b):
    return lax.dot_general(a, b, (((1,), (1,)), ((), ())), preferred_element_type=F32)


def _dot_tn(a, b):
    return lax.dot_general(a, b, (((0,), (0,)), ((), ())), preferred_element_type=F32)


def _sigmoid(x):
    return 0.5 * jnp.tanh(0.5 * x) + 0.5


def _silu(x):
    h = 0.5 * x
    return h * jnp.tanh(h) + h


def _cumsum_rows(tril16, x):
    hi = x.astype(BF16)
    r1 = x - hi.astype(F32)
    mid = r1.astype(BF16)
    lo = (r1 - mid.astype(F32)).astype(BF16)
    return _dot(tril16, hi) + (_dot(tril16, mid) + _dot(tril16, lo))


def _softplus(x):
    return jnp.maximum(x, 0.0) + jnp.log(1.0 + jnp.exp(-jnp.abs(x)))


def _rms(x, w):
    return x * lax.rsqrt(jnp.mean(x * x, axis=-1, keepdims=True) + EPS) * w


def _resident(shape):
    return pl.BlockSpec(shape, lambda i: (0,) * len(shape), pipeline_mode=pl.Buffered(1))


def _layer_resident(shape, layer):
    return pl.BlockSpec((None,) + tuple(shape[1:]), lambda i: (layer,) + (0,) * (len(shape) - 1),
                        pipeline_mode=pl.Buffered(1))


def _layer_columns(shape, layer, width, index):
    return pl.BlockSpec((None, shape[1], width), lambda i: (layer, 0, index), pipeline_mode=pl.Buffered(1))


def _inproj_kernel(x_ref, nw_ref, w_hbm, pa_ref, pb_ref, pc_ref, w_ref, wc_ref, sem, *, layer):
    first = pl.program_id(0) == 0
    copies = [pltpu.make_async_copy(w_hbm.at[layer, :, lo:hi], w_ref.at[:, lo:hi], sem.at[g])
              for g, (lo, hi) in enumerate(W_FETCH_RANGES)]

    def body(fetching):
        h = _rms(x_ref[...], nw_ref[layer:layer + 1, :]).astype(BF16)
        for g, (o_ref, src_ref, base) in enumerate(
                ((pa_ref, w_ref, 0), (pb_ref, w_ref, PA_WIDTH), (pc_ref, wc_ref, 0))):
            if fetching:
                copies[g].wait()
                if g == len(copies) - 1:
                    wc_ref[...] = w_ref[:, W_AB:W_ABC]
            n = o_ref.shape[1]
            for j in range(0, n, BRANCH_WIDTH):
                wd = min(BRANCH_WIDTH, n - j)
                o_ref[:, j:j + wd] = _dot(h, src_ref[:, base + j:base + j + wd])

    @pl.when(first)
    def _():
        for cp in copies:
            cp.start()
        body(True)

    @pl.when(jnp.logical_not(first))
    def _():
        body(False)


def _inproj(x2, nw, w, layer):
    t = x2.shape[0]
    tm = PROJ_ROWS
    return pl.pallas_call(
        functools.partial(_inproj_kernel, layer=layer),
        grid=(t // tm,),
        in_specs=[pl.BlockSpec((tm, D_MODEL), lambda i: (i, 0)),
                  _resident(nw.shape),
                  pl.BlockSpec(memory_space=pl.ANY)],
        out_specs=[pl.BlockSpec((tm, PA_WIDTH), lambda i: (i, 0)),
                   pl.BlockSpec((tm, PB_WIDTH), lambda i: (i, 0)),
                   pl.BlockSpec((tm, PC_WIDTH), lambda i: (i, 0))],
        out_shape=[jax.ShapeDtypeStruct((t, PA_WIDTH), F32),
                   jax.ShapeDtypeStruct((t, PB_WIDTH), F32),
                   jax.ShapeDtypeStruct((t, PC_WIDTH), F32)],
        scratch_shapes=[pltpu.VMEM((D_MODEL, W_MIXER_BLOCK), BF16),
                        pltpu.VMEM((D_MODEL, PC_WIDTH), BF16),
                        pltpu.SemaphoreType.DMA((len(W_FETCH_RANGES),))],
        compiler_params=pltpu.CompilerParams(
            dimension_semantics=("arbitrary",), vmem_limit_bytes=VMEM_LIMIT),
        name="inproj",
    )(x2, nw, w)


def _merge_kernel(x_ref, nw_ref, w_ref, ya_ref, yb_ref, yc_ref, wbr_hbm, wo_hbm, fn_ref,
                  o_ref, wg_ref, wbr_ref, wo_ref, sem, *, layer, final):
    first = pl.program_id(0) == 0
    fetch_br = pltpu.make_async_copy(wbr_hbm.at[layer], wbr_ref, sem.at[0])
    fetch_o = pltpu.make_async_copy(wo_hbm.at[layer], wo_ref, sem.at[1])

    def body(fetching):
        x = x_ref[...]
        h = _rms(x, nw_ref[layer:layer + 1, :]).astype(BF16)
        merged = None
        for n, y_ref in enumerate((ya_ref, yb_ref, yc_ref)):
            gate = _sigmoid(_dot(h, wg_ref[:, n * D_MODEL:(n + 1) * D_MODEL]))
            if fetching and n == 0:
                fetch_br.wait()
            term = gate * _dot(y_ref[...], wbr_ref[n].astype(BF16))
            merged = term if merged is None else merged + term
        if fetching:
            fetch_o.wait()
        out = x + _dot(merged.astype(BF16), wo_ref[...].astype(BF16))
        if final:
            out = _rms(out, fn_ref[...])
        o_ref[...] = out

    @pl.when(first)
    def _():
        fetch_br.start()
        fetch_o.start()
        lo = W_ABC - W_GATE_BLOCK
        wg_ref[...] = w_ref[:, lo:lo + N_BRANCHES * D_MODEL]
        body(True)

    @pl.when(jnp.logical_not(first))
    def _():
        body(False)


def _merge(x2, nw, w, ya, yb, yc, wbr, wo, fn, layer, final):
    t = x2.shape[0]
    tm = PROJ_ROWS
    assert w.shape[2] == W_ABC + N_BRANCHES * D_MODEL <= 2 * W_GATE_BLOCK
    row = lambda i: (i, 0)
    return pl.pallas_call(
        functools.partial(_merge_kernel, layer=layer, final=final),
        grid=(t // tm,),
        in_specs=[pl.BlockSpec((tm, D_MODEL), row),
                  _resident(nw.shape),
                  _layer_columns(w.shape, layer, W_GATE_BLOCK, 1),
                  pl.BlockSpec((tm, BRANCH_WIDTH), row),
                  pl.BlockSpec((tm, BRANCH_WIDTH), row),
                  pl.BlockSpec((tm, BRANCH_WIDTH), row),
                  pl.BlockSpec(memory_space=pl.ANY),
                  pl.BlockSpec(memory_space=pl.ANY),
                  _resident((1, D_MODEL))],
        out_specs=pl.BlockSpec((tm, D_MODEL), row),
        out_shape=jax.ShapeDtypeStruct((t, D_MODEL), F32),
        scratch_shapes=[pltpu.VMEM((D_MODEL, N_BRANCHES * D_MODEL), BF16),
                        pltpu.VMEM(wbr.shape[1:], wbr.dtype),
                        pltpu.VMEM(wo.shape[1:], wo.dtype),
                        pltpu.SemaphoreType.DMA((2,))],
        compiler_params=pltpu.CompilerParams(
            dimension_semantics=("arbitrary",), vmem_limit_bytes=VMEM_LIMIT),
        name="merge",
    )(x2, nw, w, ya, yb, yc, wbr, wo, fn)


def _a_levels():
    return [1 << i for i in range(int(math.log2(A_CHUNK)))]


def _a_level_ids():
    idx = np.arange(A_CHUNK)
    ids = np.where(idx[:, None] == idx[None, :], 0, -1).astype(np.int32)
    for i, m in enumerate(_a_levels()):
        blk = idx // (2 * m)
        upper = (idx & m) != 0
        ids[(blk[:, None] == blk[None, :]) & upper[:, None] & ~upper[None, :]] = i + 1
    return ids


def _hgrn2_kernel(pa_ref, lbp_ref, nw_ref, tril_ref, lvl_ref, o_ref,
                  state_ref, b_ref, gp_ref, *, layer):
    c = A_CHUNK
    bw = BRANCH_WIDTH

    @pl.when(pl.program_id(1) == 0)
    def _():
        state_ref[...] = jnp.zeros_like(state_ref)
        gp_ref[...] = jnp.zeros_like(gp_ref)

    lbp = lbp_ref[...]
    ex = jnp.exp(lbp - jnp.max(lbp, axis=0, keepdims=True))
    sm = ex / jnp.sum(ex, axis=0, keepdims=True)
    lb = jnp.zeros((1, bw), F32)
    for j in range(1, layer + 1):
        lb = lb + sm[j:j + 1, :]

    hsl = [slice(h * HEAD_DIM, (h + 1) * HEAD_DIM) for h in range(HEADS)]
    row = lax.broadcasted_iota(jnp.int32, (c, 1), 0)
    lvl = lvl_ref[...]
    t1 = jnp.log(lb)
    t2_lb = jnp.log1p(-lb)

    def intra_chunk(ci):
        rows = slice(ci * c, (ci + 1) * c)
        bc_ref = b_ref.at[ci]
        gc_ref = gp_ref.at[ci]
        q = _silu(pa_ref[rows, 0:bw])
        z = pa_ref[rows, bw:2 * bw]
        v = pa_ref[rows, 2 * bw:3 * bw].astype(BF16)
        log_sig = jnp.minimum(z, 0.0) - jnp.log(1.0 + jnp.exp(-jnp.abs(z)))
        sig_neg = 0.5 - 0.5 * jnp.tanh(0.5 * z)
        t2 = t2_lb + log_sig
        lf = jnp.maximum(t1, t2) + jnp.log(1.0 + jnp.exp(-jnp.abs(t1 - t2)))
        k = (1.0 - lb) * sig_neg

        lf2 = lf * LOG2E
        b = _cumsum_rows(tril_ref[...], lf2)
        bc_ref[...] = b
        gc_ref[SUBLANES:SUBLANES + c, :] = lf2

        intra = []
        for sl in hsl:
            qh = q[:, sl]
            kh = k[:, sl]
            lfh = lf2[:, sl]
            acc = jnp.where(lvl == 0, _dot_nt(qh.astype(BF16), kh.astype(BF16)), 0.0)
            for li, m in enumerate(_a_levels()):
                blocks = range(0, c, 2 * m)
                if m == 1:
                    neg = jnp.where((row & 1) == 1, lfh, 0.0)
                elif m == 2:
                    r4 = row & 3
                    g_next = gc_ref[SUBLANES + 1:SUBLANES + 1 + c, sl]
                    g_prev = gc_ref[SUBLANES - 1:SUBLANES - 1 + c, sl]
                    neg = jnp.where(r4 == 0, g_next,
                                    jnp.where(r4 == 1, 0.0, jnp.where(r4 == 2, lfh, lfh + g_prev)))
                elif m < SUBLANES:
                    neg = jnp.concatenate(
                        [-jnp.abs(bc_ref[lo:lo + 2 * m, sl] - bc_ref[lo + m - 1:lo + m, sl]) for lo in blocks],
                        axis=0)
                else:
                    pieces = []
                    for lo in blocks:
                        anchor = bc_ref[lo + m - 1:lo + m, sl]
                        pieces += [anchor - bc_ref[lo:lo + m, sl], bc_ref[lo + m:lo + 2 * m, sl] - anchor]
                    neg = jnp.concatenate(pieces, axis=0)
                if m < SUBLANES:
                    x = jnp.where((row & m) != 0, qh, kh)
                else:
                    x = jnp.concatenate(
                        [t for lo in blocks for t in (kh[lo:lo + m, :], qh[lo + m:lo + 2 * m, :])], axis=0)
                xt = (x * jnp.exp2(neg)).astype(BF16)
                acc = jnp.where(lvl == li + 1, _dot_nt(xt, xt), acc)
            intra.append(_dot(acc.astype(BF16), v[:, sl]))

        b_end = bc_ref[c - 1:c, :]
        qd = (q * jnp.exp2(b)).astype(BF16)
        kd = (k * jnp.exp2(b_end - b)).astype(BF16)
        upd = [_dot_tn(v[:, hsl[h]], kd[:, hsl[h]]) for h in range(HEADS)]
        return intra, upd, qd, jnp.exp2(b_end)

    parts = [intra_chunk(ci) for ci in range(A_TILE_CHUNKS)]
    nw = nw_ref[layer:layer + 1, :]
    sts = [state_ref[h] for h in range(HEADS)]
    for ci, (intra, upd, qd, s_decay) in enumerate(parts):
        rows = slice(ci * c, (ci + 1) * c)
        inter = [_dot_nt(qd[:, hsl[h]], sts[h].astype(BF16)) for h in range(HEADS)]
        sts = [sts[h] * s_decay[:, hsl[h]] + upd[h] for h in range(HEADS)]
        for h, sl in enumerate(hsl):
            o = intra[h] + inter[h]
            y = o * lax.rsqrt(jnp.mean(o * o, axis=-1, keepdims=True) + EPS) * nw
            gate = pa_ref[rows, 3 * bw + h * HEAD_DIM:3 * bw + (h + 1) * HEAD_DIM]
            o_ref[rows, sl] = (y * _silu(gate)).astype(BF16)
    for h in range(HEADS):
        state_ref[h] = sts[h]


def _hgrn2(pa, lbp, nw, batch, seq, layer):
    c = A_CHUNK
    tc = A_TILE_CHUNKS
    nt = seq // (c * tc)
    tril = jnp.asarray(np.tril(np.ones((c, c), np.float32)), dtype=BF16)
    lvl = jnp.asarray(_a_level_ids())
    const2 = lambda b, i: (0, 0)
    return pl.pallas_call(
        functools.partial(_hgrn2_kernel, layer=layer),
        grid=(batch, nt),
        in_specs=[pl.BlockSpec((tc * c, PA_WIDTH), lambda b, i: (b * nt + i, 0)),
                  pl.BlockSpec(lbp.shape, const2),
                  pl.BlockSpec(nw.shape, const2),
                  pl.BlockSpec((c, c), const2),
                  pl.BlockSpec((c, c), const2)],
        out_specs=pl.BlockSpec((tc * c, BRANCH_WIDTH), lambda b, i: (b * nt + i, 0)),
        out_shape=jax.ShapeDtypeStruct((batch * seq, BRANCH_WIDTH), BF16),
        scratch_shapes=[pltpu.VMEM((HEADS, HEAD_DIM, HEAD_DIM), F32),
                        pltpu.VMEM((tc, c, BRANCH_WIDTH), F32),
                        pltpu.VMEM((tc, c + 2 * SUBLANES, BRANCH_WIDTH), F32)],
        compiler_params=pltpu.CompilerParams(
            dimension_semantics=("arbitrary", "arbitrary"), vmem_limit_bytes=VMEM_LIMIT),
        name="hgrn2",
    )(pa, lbp, nw, tril, lvl)


def _b_levels():
    return [1 << i for i in range(int(math.log2(B_CHUNK)))]


def _unit_lower_inverses(mats, ti, si):
    eye = (ti == si).astype(F32)
    ts = [eye] * len(mats)
    for s in _b_levels():
        shift = int(math.log2(2 * s))
        off = ((ti >> shift) == (si >> shift)) & ((ti & s) != 0) & ((si & s) == 0)
        a_off = [jnp.where(off, a, 0.0) for a in mats]
        if s == 1:
            ts = [t - ao for t, ao in zip(ts, a_off)]
        else:
            t16 = [t.astype(BF16) for t in ts]
            inner = [_dot(ao.astype(BF16), t) for ao, t in zip(a_off, t16)]
            yield
            outer = [_dot(t, x.astype(BF16)) for t, x in zip(t16, inner)]
            yield
            ts = [t - x for t, x in zip(ts, outer)]
    return ts


def _dn_prep_stages(seq_tile, layer, pb_ref, halo_ref, cw_ref, alog_ref, dtb_ref, tril_ref,
                    u_ref, w_ref, qd_ref, kd_ref, qk_ref, ge_ref, xp_ref, qkv_ref):
    ts = B_TILE
    c = B_CHUNK
    pr = B_PAIR
    bw = BRANCH_WIDTH
    sl8 = SUBLANES

    keep = jnp.where(seq_tile == 0, 0.0, 1.0)
    half = ts // 2
    for s in range(B_CONV_CH // LANES):
        lanes = slice(s * LANES, (s + 1) * LANES)
        xp_ref[s, 0:sl8, :] = halo_ref[:, lanes] * keep
        xp_ref[s, sl8:sl8 + ts, :] = pb_ref[:, lanes]
    yield
    for s in range(B_CONV_CH // LANES):
        lanes = slice(s * LANES, (s + 1) * LANES)
        cw = [0.5 * cw_ref[j:j + 1, lanes] for j in range(B_CONV)]
        ld = {off: xp_ref[s, pl.ds(sl8 + off, half, stride=2), :] for off in range(1 - B_CONV, 2)}
        even = None
        odd = None
        for j in range(B_CONV):
            te = cw[j] * ld[j - (B_CONV - 1)]
            to = cw[j] * ld[j - (B_CONV - 1) + 1]
            even = te if even is None else even + te
            odd = to if odd is None else odd + to
        qkv_ref[s, pl.ds(0, half, stride=2), :] = even * jnp.tanh(even) + even
        qkv_ref[s, pl.ds(1, half, stride=2), :] = odd * jnp.tanh(odd) + odd
        yield

    small = pb_ref[:, B_CONV_CH + bw:B_CONV_CH + bw + SMALL_PAD]
    beta_all = _sigmoid(small)
    small_lane = lax.broadcasted_iota(jnp.int32, (1, SMALL_PAD), 1)
    alog = jnp.zeros((1, SMALL_PAD), F32)
    dtb = jnp.zeros((1, SMALL_PAD), F32)
    for h in range(HEADS):
        alog = jnp.where(small_lane == HEADS + h, alog_ref[layer, h], alog)
        dtb = jnp.where(small_lane == HEADS + h, dtb_ref[layer, h], dtb)
    g_all = -jnp.exp(alog) * _softplus(small + dtb)
    gcum = jnp.concatenate(
        [_cumsum_rows(tril_ref[...], g_all[p * pr:(p + 1) * pr, :]) for p in range(ts // pr)], axis=0)
    gcum_t = gcum.T
    for j in range(ts // c):
        ge_ref[0, j:j + 1, :] = jnp.exp(gcum[j * c + c - 1:j * c + c, :])

    ti = lax.broadcasted_iota(jnp.int32, (pr, pr), 0)
    si = lax.broadcasted_iota(jnp.int32, (pr, pr), 1)
    same = (ti >> int(math.log2(c))) == (si >> int(math.log2(c)))
    incl = same & (ti >= si)
    first_chunk = lax.broadcasted_iota(jnp.int32, (pr, 1), 0) < c
    yield

    all_probs = [(h, p) for h in range(HEADS) for p in range(ts // pr)]
    for first in range(0, len(all_probs), B_LOCKSTEP):
        probs = all_probs[first:first + B_LOCKSTEP]
        k16s, kb16s, q16s, decays, rhs = [], [], [], [], []
        for h, p in probs:
            sl = slice(h * HEAD_DIM, (h + 1) * HEAD_DIM)
            rows = slice(p * pr, (p + 1) * pr)
            q = qkv_ref[h, rows, :]
            k = qkv_ref[HEADS + h, rows, :]
            v = qkv_ref[2 * HEADS + h, rows, :]
            q = q * (lax.rsqrt(jnp.sum(q * q, axis=-1, keepdims=True) + EPS) * (HEAD_DIM ** -0.5))
            k = k * lax.rsqrt(jnp.sum(k * k, axis=-1, keepdims=True) + EPS)
            beta = beta_all[rows, h:h + 1]
            gc = gcum[rows, HEADS + h:HEADS + h + 1]
            gr = gcum_t[HEADS + h:HEADS + h + 1, rows]
            g_last = jnp.where(first_chunk, gc[c - 1:c, :], gc[pr - 1:pr, :])
            egc = jnp.exp(gc)
            kb = k * beta
            decays.append(jnp.where(incl, jnp.exp(jnp.minimum(gc - gr, 0.0)), 0.0))
            k16s.append(k.astype(BF16))
            kb16s.append(kb.astype(BF16))
            q16s.append(q.astype(BF16))
            rhs.append(jnp.concatenate([v * beta, kb * egc], axis=1).astype(BF16))
            qd_ref[rows, sl] = (q * egc).astype(BF16)
            kd_ref[rows, sl] = (k * jnp.exp(g_last - gc)).astype(BF16)
            yield

        kk = [_dot_nt(kb16, k16) for kb16, k16 in zip(kb16s, k16s)]
        yield
        qk = [_dot_nt(q16, k16) for q16, k16 in zip(q16s, k16s)]
        yield
        mats = [x * d for x, d in zip(kk, decays)]
        tinv = yield from _unit_lower_inverses(mats, ti, si)
        uw = [_dot(t.astype(BF16), r) for t, r in zip(tinv, rhs)]
        yield
        for i, (h, p) in enumerate(probs):
            sl = slice(h * HEAD_DIM, (h + 1) * HEAD_DIM)
            rows = slice(p * pr, (p + 1) * pr)
            u_ref[rows, sl] = uw[i][:, 0:HEAD_DIM]
            w_ref[rows, sl] = uw[i][:, HEAD_DIM:].astype(BF16)
            qk_ref[p, h] = (qk[i] * decays[i]).astype(BF16)


def _dn_scan_kernel(u_ref, w_ref, qd_ref, kd_ref, qk_ref, ge_ref, z_ref, nw_ref, o_ref, state_ref, *, layer):
    c = B_CHUNK
    batch = u_ref.shape[0]

    @pl.when(pl.program_id(0) == 0)
    def _():
        state_ref[...] = jnp.zeros_like(state_ref)

    nw = nw_ref[layer:layer + 1, :]
    chains = [(b, h) for b in range(batch) for h in range(HEADS)]
    hsl = lambda h: slice(h * HEAD_DIM, (h + 1) * HEAD_DIM)
    states = [state_ref[b * HEADS + h] for b, h in chains]
    for pi in range(B_SCAN_PAIRS):
        o_inter = [[] for _ in chains]
        v_new = [[] for _ in chains]
        for j in range(B_PAIR // c):
            rows = slice(pi * B_PAIR + j * c, pi * B_PAIR + (j + 1) * c)
            st16 = [st.astype(BF16) for st in states]
            lhs = [jnp.concatenate([w_ref[b, rows, hsl(h)], qd_ref[b, rows, hsl(h)]], axis=0)
                   for b, h in chains]
            prod = [_dot(x, s) for x, s in zip(lhs, st16)]
            vn16 = []
            for i, (b, h) in enumerate(chains):
                vn = (u_ref[b, rows, hsl(h)] - prod[i][0:c, :]).astype(BF16)
                vn16.append(vn)
                v_new[i].append(vn)
                o_inter[i].append(prod[i][c:, :])
            upd = [_dot_tn(kd_ref[b, rows, hsl(h)], vn) for (b, h), vn in zip(chains, vn16)]
            states = [st * ge_ref[b, pi, j:j + 1, HEADS + h:HEADS + h + 1] + x
                      for (b, h), st, x in zip(chains, states, upd)]
        intra = [_dot(qk_ref[b, pi, h], jnp.concatenate(v_new[i], axis=0)) for i, (b, h) in enumerate(chains)]
        prows = slice(pi * B_PAIR, (pi + 1) * B_PAIR)
        for i, (b, h) in enumerate(chains):
            o = jnp.concatenate(o_inter[i], axis=0) + intra[i]
            y = o * lax.rsqrt(jnp.mean(o * o, axis=-1, keepdims=True) + EPS) * nw
            o_ref[b, prows, hsl(h)] = (y * _silu(z_ref[b, prows, hsl(h)])).astype(BF16)
    for i, (b, h) in enumerate(chains):
        state_ref[b * HEADS + h] = states[i]


def _dn_scan(pb, u, w, qd, kd, qk, ge, nw, batch, seq, layer):
    c = B_CHUNK
    pr = B_PAIR
    bw = BRANCH_WIDTH
    t = batch * seq
    const = lambda i: (0, 0)
    np_seq = seq // pr
    seq3 = lambda x: x.reshape(batch, seq, x.shape[-1])
    sp = B_SCAN_PAIRS
    blk3 = pl.BlockSpec((batch, sp * pr, bw), lambda i: (0, i, 0))
    y = pl.pallas_call(
        functools.partial(_dn_scan_kernel, layer=layer),
        grid=(np_seq // sp,),
        in_specs=[blk3, blk3, blk3, blk3,
                  pl.BlockSpec((batch, sp, HEADS, pr, pr), lambda i: (0, i, 0, 0, 0)),
                  pl.BlockSpec((batch, sp, pr // c, SMALL_PAD), lambda i: (0, i, 0, 0)),
                  pl.BlockSpec((batch, sp * pr, bw), lambda i: (0, i, B_CONV_CH // bw)),
                  pl.BlockSpec(nw.shape, const)],
        out_specs=blk3,
        out_shape=jax.ShapeDtypeStruct((batch, seq, bw), BF16),
        scratch_shapes=[pltpu.VMEM((batch * HEADS, HEAD_DIM, HEAD_DIM), F32)],
        compiler_params=pltpu.CompilerParams(
            dimension_semantics=("arbitrary",), vmem_limit_bytes=VMEM_LIMIT),
        name="dn_scan",
    )(seq3(u), seq3(w), seq3(qd), seq3(kd),
      qk.reshape(batch, np_seq, HEADS, pr, pr),
      ge.reshape(batch, np_seq, pr // c, SMALL_PAD),
      seq3(pb), nw)
    return y.reshape(t, bw)


def _band_buckets():
    assert WINDOW == C_BLOCK
    r = np.arange(C_BLOCK)[:, None]
    c = np.arange(C_BLOCK)[None, :]
    dist = np.where(c > r, r + C_BLOCK - c, r - c)
    max_exact = N_BUCKETS // 2
    d_f = np.maximum(dist, 1).astype(np.float32)
    large = max_exact + (np.log(d_f / np.float32(max_exact)) / np.float32(math.log(MAX_DISTANCE / max_exact))
                         * np.float32(N_BUCKETS - max_exact)).astype(np.int32)
    large = np.minimum(large, N_BUCKETS - 1)
    return np.where(dist < max_exact, dist, large).astype(np.int32)


def _band_halves():
    r = np.arange(C_BLOCK)[:, None]
    c = np.arange(C_BLOCK)[None, :]
    return np.stack([c > r, c <= r]).astype(np.float32)


def _swa_stages(n, layer, first_step, rb_ref, sink_ref, cur_ref, prev_ref, bucket_ref, half_ref, o_ref,
                bias_ref):
    blk = C_BLOCK
    bw = BRANCH_WIDTH

    @pl.when(first_step)
    def _():
        bucket = bucket_ref[...]
        for h in range(C_Q_HEADS):
            acc = jnp.zeros((blk, blk), F32)
            for bk in range(N_BUCKETS):
                acc = jnp.where(bucket == bk, rb_ref[bk, h], acc)
            bias_ref[h] = acc

    kv = jnp.concatenate([prev_ref[...], cur_ref[:, bw:bw + 2 * C_KV_WIDTH]], axis=0)
    rows = kv.shape[0]
    lane = lax.broadcasted_iota(jnp.int32, (rows, C_KV_WIDTH), 1)

    def halves(x, j):
        own = jnp.where((lane >= j * C_HEAD_DIM) & (lane < (j + 1) * C_HEAD_DIM), x, 0.0)
        other = pltpu.roll(own, C_HEAD_DIM, 1)
        pair = (own, other) if j == 0 else (other, own)
        return [t.astype(BF16) for t in pair]

    kz = [halves(kv[:, 0:C_KV_WIDTH], j) for j in range(C_KV_HEADS)]
    vz = [halves(kv[:, C_KV_WIDTH:], j) for j in range(C_KV_HEADS)]
    scale = C_HEAD_DIM ** -0.5
    group = C_Q_HEADS // C_KV_HEADS
    from_prev = (lax.broadcasted_iota(jnp.int32, (blk, blk), 1) > lax.broadcasted_iota(jnp.int32, (blk, blk), 0))
    yield

    qrows = lambda i: slice(i * blk, (i + 1) * blk)
    krows = lambda i: slice(i * blk, (i + 2) * blk)
    for first in range(0, C_TILE_BLOCKS, C_LOCKSTEP_BLOCKS):
        blocks = range(first, first + C_LOCKSTEP_BLOCKS)
        chains = [(i, h) for i in blocks for h in range(C_Q_HEADS)]
        q2 = {i: [(cur_ref[qrows(i), p * LANES:(p + 1) * LANES] * scale).astype(BF16)
                  for p in range(C_Q_HEADS // 2)] for i in blocks}
        both = [_dot_nt(q2[i][h // 2], kz[h // group][h % 2][krows(i), :]) for i, h in chains]
        yield
        logits = []
        for (i, h), lg in zip(chains, both):
            prev_part = lg[:, 0:blk]
            if i == 0:
                prev_part = prev_part + jnp.where(n > 0, 0.0, NEG_BIG)
            logits.append(jnp.where(from_prev, prev_part, lg[:, blk:]) + bias_ref[h])
        yield
        mx = [jnp.maximum(jnp.max(lg, axis=-1, keepdims=True), sink_ref[layer, h])
              for (i, h), lg in zip(chains, logits)]
        yield
        pr = [jnp.exp(lg - m) for lg, m in zip(logits, mx)]
        den = [jnp.sum(x, axis=-1, keepdims=True) + jnp.exp(sink_ref[layer, h] - m)
               for (i, h), x, m in zip(chains, pr, mx)]
        yield
        pr16 = [x.astype(BF16) for x in pr]
        pcat = [jnp.concatenate([x * half_ref[0], x * half_ref[1]], axis=1) for x in pr16]
        pv = [_dot(x, vz[h // group][h % 2][krows(i), :]) for (i, h), x in zip(chains, pcat)]
        yield
        for bi, i in enumerate(blocks):
            for p in range(C_Q_HEADS // 2):
                a, b = bi * C_Q_HEADS + 2 * p, bi * C_Q_HEADS + 2 * p + 1
                out = pv[a] / den[a] + pv[b] / den[b]
                gate = cur_ref[qrows(i), bw + 2 * C_KV_WIDTH + p * LANES:bw + 2 * C_KV_WIDTH + (p + 1) * LANES]
                o_ref[qrows(i), p * LANES:(p + 1) * LANES] = (out * _silu(gate)).astype(BF16)
        yield


N_DN_IN = 6
N_DN_OUT = 6
N_SWA_IN = 6


def _mix_bc_kernel(*refs, tiles_per_seq, layer):
    dn_in = refs[0:N_DN_IN]
    swa_in = refs[N_DN_IN:N_DN_IN + N_SWA_IN]
    outs = refs[N_DN_IN + N_SWA_IN:N_DN_IN + N_SWA_IN + N_DN_OUT + 1]
    xp_ref, qkv_ref, bias_ref = refs[N_DN_IN + N_SWA_IN + N_DN_OUT + 1:]
    step = pl.program_id(0)
    seq_tile = step % tiles_per_seq
    jobs = [_dn_prep_stages(seq_tile, layer, *dn_in, *outs[0:N_DN_OUT], xp_ref, qkv_ref),
            _swa_stages(seq_tile, layer, step == 0, *swa_in, outs[N_DN_OUT], bias_ref)]
    while jobs:
        for job in list(jobs):
            try:
                next(job)
            except StopIteration:
                jobs.remove(job)


def _mix_bc(pb, pc, cw, alog, dtb, rel_bias, sinks, batch, seq, layer):
    c = B_CHUNK
    ts = B_TILE
    pr = B_PAIR
    bw = BRANCH_WIDTH
    blk = C_BLOCK
    t = batch * seq
    n_tiles = t // ts
    tril = jnp.asarray(np.kron(np.eye(pr // c, dtype=np.float32), np.tril(np.ones((c, c), np.float32))),
                       dtype=BF16)
    bucket = jnp.asarray(_band_buckets())
    halves = jnp.asarray(_band_halves(), dtype=BF16)
    const = lambda i: (0, 0)
    row = lambda i: (i, 0)
    halo_blocks = ts // SUBLANES
    kv_col = BRANCH_WIDTH // (2 * C_KV_WIDTH)
    smem = pl.BlockSpec(memory_space=pltpu.SMEM)
    return pl.pallas_call(
        functools.partial(_mix_bc_kernel, tiles_per_seq=seq // ts, layer=layer),
        grid=(n_tiles,),
        in_specs=[pl.BlockSpec((ts, PB_WIDTH), row),
                  pl.BlockSpec((SUBLANES, B_CONV_CH), lambda i: (jnp.maximum(i * halo_blocks - 1, 0), 0)),
                  pl.BlockSpec((None,) + cw.shape[1:], lambda i: (layer, 0, 0)),
                  smem, smem,
                  pl.BlockSpec((pr, pr), const),
                  smem, smem,
                  pl.BlockSpec((ts, PC_WIDTH), row),
                  pl.BlockSpec((blk, 2 * C_KV_WIDTH), lambda i: (jnp.maximum(i * C_TILE_BLOCKS - 1, 0), kv_col)),
                  pl.BlockSpec((blk, blk), const),
                  pl.BlockSpec((2, blk, blk), lambda i: (0, 0, 0))],
        out_specs=[pl.BlockSpec((ts, bw), row),
                   pl.BlockSpec((ts, bw), row),
                   pl.BlockSpec((ts, bw), row),
                   pl.BlockSpec((ts, bw), row),
                   pl.BlockSpec((ts // pr, HEADS, pr, pr), lambda i: (i, 0, 0, 0)),
                   pl.BlockSpec((1, ts // c, SMALL_PAD), lambda i: (i, 0, 0)),
                   pl.BlockSpec((ts, bw), row)],
        out_shape=[jax.ShapeDtypeStruct((t, bw), F32),
                   jax.ShapeDtypeStruct((t, bw), BF16),
                   jax.ShapeDtypeStruct((t, bw), BF16),
                   jax.ShapeDtypeStruct((t, bw), BF16),
                   jax.ShapeDtypeStruct((t // pr, HEADS, pr, pr), BF16),
                   jax.ShapeDtypeStruct((n_tiles, ts // c, SMALL_PAD), F32),
                   jax.ShapeDtypeStruct((t, bw), BF16)],
        scratch_shapes=[pltpu.VMEM((B_CONV_CH // LANES, ts + SUBLANES, LANES), F32),
                        pltpu.VMEM((B_CONV_CH // LANES, ts, LANES), F32),
                        pltpu.VMEM((C_Q_HEADS, blk, blk), F32)],
        compiler_params=pltpu.CompilerParams(
            dimension_semantics=("arbitrary",), vmem_limit_bytes=VMEM_LIMIT),
        name="mix_bc",
    )(pb, pb, cw, alog, dtb, tril, rel_bias, sinks, pc, pc, bucket, halves)


def kernel(x, norm_w, w_in, conv_w, a_log, dt_bias, lb_param, norm_a, norm_b, sinks, rel_bias,
           w_branch, w_out, final_norm):
    batch, seq, _ = x.shape
    depth = w_in.shape[0]
    x2 = x.reshape(batch * seq, D_MODEL)
    fn = final_norm.reshape(1, D_MODEL)
    w16 = w_in.astype(BF16)
    for l in range(depth):
        pa, pb, pc = _inproj(x2, norm_w, w16, l)
        ya = _hgrn2(pa, lb_param, norm_a, batch, seq, l)
        *dn, yc = _mix_bc(pb, pc, conv_w, a_log, dt_bias, rel_bias, sinks, batch, seq, l)
        yb = _dn_scan(pb, *dn, norm_b, batch, seq, l)
        x2 = _merge(x2, norm_w, w16, ya, yb, yc, w_branch, w_out, fn, layer=l, final=(l == depth - 1))
    return x2.reshape(batch, seq, D_MODEL)
```
